```python
import jax
import jax.numpy as jnp
from jax import lax
import numpy as np

D_MODEL = 1024
BATCH = 4
SEQ = 4096
DEPTH = 2

GRID_W = 64
CTX_LEN = 256
HEAD_DIM = 64
CONV_CH = D_MODEL // 2
CONV_WIDTH = 31
B_HEADS = (D_MODEL // 2) // HEAD_DIM
B_KV_HEADS = 2
Q_BLOCK = 128
ROPE_THETA = 10000.0
NA_HEADS = D_MODEL // HEAD_DIM
WIN_R = 8
WIN_C = 16
FFN_DIM = 2816
N_EXPERTS = 8
TOP_K = 2
EXPERT_DIM = 3584
N_EVEN = (DEPTH + 1) // 2
N_ODD = DEPTH // 2
ALPHA = (2 * DEPTH) ** 0.25
BETA = (8 * DEPTH) ** -0.25
LN_EPS = 1e-5
RMS_EPS = 1e-6
A_COLS = 2 * CONV_CH
Q_COLS = B_HEADS * HEAD_DIM
KV_COLS = B_KV_HEADS * HEAD_DIM
IN_AB = A_COLS + Q_COLS + 2 * KV_COLS
OUT_AB = CONV_CH + Q_COLS
NA_WIDTH = NA_HEADS * HEAD_DIM

kernel_name = 'hybrid_conv_gqa_natten_moe_diffusion_block'


def layer_norm(x, g, b):
    xf = x.astype(jnp.float32)
    xc = xf - jnp.mean(xf, axis=-1, keepdims=True)
    var = jnp.mean(xc * xc, axis=-1, keepdims=True)
    return (xc * lax.rsqrt(var + LN_EPS) * g + b).astype(x.dtype)


def rms_norm(x, g):
    xf = x.astype(jnp.float32)
    y = xf * lax.rsqrt(jnp.mean(xf * xf, axis=-1, keepdims=True) + RMS_EPS)
    return (y * g).astype(x.dtype)


def rope_axis(x, pos):
    d = x.shape[-1]
    half = d // 2
    inv = ROPE_THETA ** (-jnp.arange(half, dtype=jnp.float32) * 2.0 / d)
    ang = pos.astype(jnp.float32)[:, None] * inv[None, :]
    cos = jnp.cos(ang)[:, None, :]
    sin = jnp.sin(ang)[:, None, :]
    xf = x.astype(jnp.float32)
    x1, x2 = xf[..., :half], xf[..., half:]
    return jnp.concatenate([x1 * cos - x2 * sin, x2 * cos + x1 * sin], axis=-1).astype(x.dtype)


def rope_2d(x, rows, cols):
    h = x.shape[-1] // 2
    return jnp.concatenate([rope_axis(x[..., :h], rows), rope_axis(x[..., h:], cols)], axis=-1)


def conformer_conv(p, conv_w, conv_g, conv_b):
    u = p[..., :CONV_CH] * jax.nn.sigmoid(p[..., CONV_CH:])
    u = lax.conv_general_dilated(
        u, conv_w[:, None, :].astype(u.dtype), window_strides=(1,),
        padding=[(CONV_WIDTH // 2, CONV_WIDTH // 2)],
        dimension_numbers=('NWC', 'WIO', 'NWC'), feature_group_count=CONV_CH)
    return jax.nn.silu(layer_norm(u, conv_g, conv_b))


def gqa_attend(qb, keys, vals):
    bsz, nq = qb.shape[0], qb.shape[1]
    g = B_HEADS // B_KV_HEADS
    q5 = qb.reshape(bsz, nq, B_KV_HEADS, g, HEAD_DIM)
    s = jnp.einsum('bqkgd,bskd->bkgqs', q5, keys).astype(jnp.float32)
    p = jax.nn.softmax(s, axis=-1).astype(vals.dtype)
    o = jnp.einsum('bkgqs,bskd->bqkgd', p, vals)
    return o.reshape(bsz, nq, B_HEADS * HEAD_DIM)


def mixer_ab(h, hc, w_in, conv_w, conv_g, conv_b, q_g, k_g, w_out, rows, cols, update_ctx):
    bsz, t, _ = h.shape
    n_ctx = hc.shape[1]
    scale = HEAD_DIM ** -0.5
    p = h @ w_in
    a = conformer_conv(p[..., :A_COLS], conv_w, conv_g, conv_b)
    pb = p[..., A_COLS:]
    q = rms_norm(pb[..., :Q_COLS].reshape(bsz, t, B_HEADS, HEAD_DIM), q_g)
    k = rms_norm(pb[..., Q_COLS:Q_COLS + KV_COLS].reshape(bsz, t, B_KV_HEADS, HEAD_DIM), k_g)
    v = pb[..., Q_COLS + KV_COLS:].reshape(bsz, t, B_KV_HEADS, HEAD_DIM)
    q = rope_2d(q, rows, cols) * scale
    k = rope_2d(k, rows, cols)
    if update_ctx:
        pc = hc @ w_in
        pc_kv = pc[..., A_COLS + Q_COLS:]
    else:
        pc_kv = hc @ w_in[:, A_COLS + Q_COLS:]
    kc = rms_norm(pc_kv[..., :KV_COLS].reshape(bsz, n_ctx, B_KV_HEADS, HEAD_DIM), k_g)
    vc = pc_kv[..., KV_COLS:].reshape(bsz, n_ctx, B_KV_HEADS, HEAD_DIM)
    keys = jnp.concatenate([kc, k], axis=1)
    vals = jnp.concatenate([vc, v], axis=1)
    qs = q.reshape(bsz, t // Q_BLOCK, Q_BLOCK, B_HEADS, HEAD_DIM).transpose(1, 0, 2, 3, 4)
    o = lax.map(lambda qb: gqa_attend(qb, keys, vals), qs)
    o = o.transpose(1, 0, 2, 3).reshape(bsz, t, Q_COLS)
    y = jnp.concatenate([a, o], axis=-1) @ w_out
    if update_ctx:
        ac = conformer_conv(pc[..., :A_COLS], conv_g=conv_g, conv_w=conv_w, conv_b=conv_b)
        qc = rms_norm(pc[..., A_COLS:A_COLS + Q_COLS].reshape(bsz, n_ctx, B_HEADS, HEAD_DIM), q_g) * scale
        oc = gqa_attend(qc, kc, vc)
        yc = jnp.concatenate([ac, oc], axis=-1) @ w_out
    else:
        yc = None
    return y, yc


def mixer_na(h, hc, w_qkv, rpb, w_out, update_ctx):
    bsz, t, _ = h.shape
    n_ctx = hc.shape[1]
    n_rows = t // GRID_W
    kr = min(WIN_R, n_rows)
    scale = HEAD_DIM ** -0.5
    qkv = (h @ w_qkv).reshape(bsz, t, 3, NA_HEADS, HEAD_DIM)
    qg = (qkv[:, :, 0] * scale).reshape(bsz, n_rows, GRID_W, NA_HEADS, HEAD_DIM)
    kg = qkv[:, :, 1].reshape(bsz, n_rows, GRID_W, NA_HEADS, HEAD_DIM)
    vg = qkv[:, :, 2].reshape(bsz, n_rows, GRID_W, NA_HEADS, HEAD_DIM)
    if update_ctx:
        qkvc = (hc @ w_qkv).reshape(bsz, n_ctx, 3, NA_HEADS, HEAD_DIM)
        kc, vc = qkvc[:, :, 1], qkvc[:, :, 2]
    else:
        kvc = (hc @ w_qkv[:, NA_WIDTH:]).reshape(bsz, n_ctx, 2, NA_HEADS, HEAD_DIM)
        kc, vc = kvc[:, :, 0], kvc[:, :, 1]
    row_start = jnp.clip(jnp.arange(n_rows) - kr // 2, 0, n_rows - kr)
    col_start = jnp.clip(jnp.arange(GRID_W) - WIN_C // 2, 0, GRID_W - WIN_C)
    col_idx = col_start[:, None] + jnp.arange(WIN_C)[None, :]
    dc_idx = col_idx - jnp.arange(GRID_W)[:, None] + WIN_C - 1
    bias_c = rpb[:, :, dc_idx]

    def row_block(args):
        q_r, r = args
        r0 = row_start[r]
        k_rows = lax.dynamic_slice_in_dim(kg, r0, kr, axis=1)
        v_rows = lax.dynamic_slice_in_dim(vg, r0, kr, axis=1)
        k_win = k_rows[:, :, col_idx]
        v_win = v_rows[:, :, col_idx]
        dr_idx = r0 + jnp.arange(kr) - r + WIN_R - 1
        bias = bias_c[:, dr_idx].transpose(0, 2, 1, 3)
        s_win = jnp.einsum('bqhd,biqjhd->bhqij', q_r, k_win).astype(jnp.float32) + bias[None].astype(jnp.float32)
        s_ctx = jnp.einsum('bqhd,bshd->bhqs', q_r, kc).astype(jnp.float32)
        s = jnp.concatenate([s_win.reshape(bsz, NA_HEADS, GRID_W, kr * WIN_C), s_ctx], axis=-1)
        p = jax.nn.softmax(s, axis=-1).astype(vg.dtype)
        p_win = p[..., :kr * WIN_C].reshape(bsz, NA_HEADS, GRID_W, kr, WIN_C)
        p_ctx = p[..., kr * WIN_C:]
        return (jnp.einsum('bhqij,biqjhd->bqhd', p_win, v_win)
                + jnp.einsum('bhqs,bshd->bqhd', p_ctx, vc))

    o = lax.map(row_block, (qg.transpose(1, 0, 2, 3, 4), jnp.arange(n_rows)))
    o = o.transpose(1, 0, 2, 3, 4).reshape(bsz, t, NA_WIDTH)
    y = o @ w_out
    if update_ctx:
        qc = qkvc[:, :, 0] * scale
        sc = jnp.einsum('bqhd,bshd->bhqs', qc, kc).astype(jnp.float32)
        pc = jax.nn.softmax(sc, axis=-1).astype(vc.dtype)
        oc = jnp.einsum('bhqs,bshd->bqhd', pc, vc).reshape(bsz, n_ctx, NA_WIDTH)
        yc = oc @ w_out
    else:
        yc = None
    return y, yc


def swiglu(h, w_gate, w_up, w_down):
    return (jax.nn.silu(h @ w_gate) * (h @ w_up)) @ w_down


def moe_swiglu(h, w_router, w_gate, w_up, w_down):
    lead = h.shape[:-1]
    xt = h.reshape(-1, h.shape[-1])
    logits = (xt @ w_router).astype(jnp.float32)
    top_v, top_i = lax.top_k(logits, TOP_K)
    top_w = jax.nn.softmax(top_v, axis=-1)
    gates = jnp.einsum('nk,nke->ne', top_w, jax.nn.one_hot(top_i, N_EXPERTS, dtype=jnp.float32)).astype(h.dtype)
    y = jnp.zeros_like(xt)
    for e in range(N_EXPERTS):
        y = y + gates[:, e:e + 1] * swiglu(xt, w_gate[e], w_up[e], w_down[e])
    return y.reshape(*lead, -1)


def setup_inputs(seed: int = 0) -> dict:
    key = jax.random.key(seed)
    ks = jax.random.split(key, 25)

    def nrm(k, shape, s):
        return jax.random.normal(k, shape, jnp.float32) * s

    d = D_MODEL
    return {
        'x': nrm(ks[0], (BATCH, SEQ, d), 1.0),
        'c': nrm(ks[1], (BATCH, d), 1.0),
        'ctx': nrm(ks[2], (BATCH, CTX_LEN, d), 1.0),
        'c_ctx': nrm(ks[3], (d,), 1.0),
        'w_mod': nrm(ks[4], (DEPTH, d, 6 * d), 0.5 * d ** -0.5),
        'b_mod': nrm(ks[5], (DEPTH, 6 * d), 0.01),
        'ln_g': 1.0 + nrm(ks[6], (DEPTH, 2, d), 0.02),
        'ln_b': nrm(ks[7], (DEPTH, 2, d), 0.02),
        'ab_w_in': nrm(ks[8], (N_EVEN, d, IN_AB), d ** -0.5),
        'ab_conv_w': nrm(ks[9], (N_EVEN, CONV_WIDTH, CONV_CH), CONV_WIDTH ** -0.5),
        'ab_conv_g': 1.0 + nrm(ks[10], (N_EVEN, CONV_CH), 0.02),
        'ab_conv_b': nrm(ks[11], (N_EVEN, CONV_CH), 0.02),
        'ab_q_g': 1.0 + nrm(ks[12], (N_EVEN, HEAD_DIM), 0.02),
        'ab_k_g': 1.0 + nrm(ks[13], (N_EVEN, HEAD_DIM), 0.02),
        'ab_w_out': nrm(ks[14], (N_EVEN, OUT_AB, d), BETA * OUT_AB ** -0.5),
        'ffn_w_gate': nrm(ks[15], (N_EVEN, d, FFN_DIM), d ** -0.5),
        'ffn_w_up': nrm(ks[16], (N_EVEN, d, FFN_DIM), d ** -0.5),
        'ffn_w_down': nrm(ks[17], (N_EVEN, FFN_DIM, d), BETA * FFN_DIM ** -0.5),
        'na_w_qkv': nrm(ks[18], (N_ODD, d, 3 * NA_WIDTH), d ** -0.5),
        'na_rpb': nrm(ks[19], (N_ODD, NA_HEADS, 2 * WIN_R - 1, 2 * WIN_C - 1), 0.1),
        'na_w_out': nrm(ks[20], (N_ODD, NA_WIDTH, d), BETA * NA_WIDTH ** -0.5),
        'moe_w_router': nrm(ks[21], (N_ODD, d, N_EXPERTS), d ** -0.5),
        'moe_w_gate': nrm(ks[22], (N_ODD, N_EXPERTS, d, EXPERT_DIM), d ** -0.5),
        'moe_w_up': nrm(ks[23], (N_ODD, N_EXPERTS, d, EXPERT_DIM), d ** -0.5),
        'moe_w_down': nrm(ks[24], (N_ODD, N_EXPERTS, EXPERT_DIM, d), BETA * EXPERT_DIM ** -0.5),
    }


def reference(x, c, ctx, c_ctx, w_mod, b_mod, ln_g, ln_b, ab_w_in, ab_conv_w, ab_conv_g, ab_conv_b,
              ab_q_g, ab_k_g, ab_w_out, ffn_w_gate, ffn_w_up, ffn_w_down, na_w_qkv, na_rpb, na_w_out,
              moe_w_router, moe_w_gate, moe_w_up, moe_w_down):
    t = x.shape[1]
    pos = jnp.arange(t)
    rows = pos // GRID_W
    cols = pos % GRID_W
    for i in range(DEPTH):
        j = i // 2
        update_ctx = i < DEPTH - 1
        mod = (jax.nn.silu(c) @ w_mod[i] + b_mod[i])[:, None, :]
        sh_m, sc_m, g_m, sh_f, sc_f, g_f = jnp.split(mod, 6, axis=-1)
        modc = jax.nn.silu(c_ctx) @ w_mod[i] + b_mod[i]
        shc_m, scc_m, gc_m, shc_f, scc_f, gc_f = jnp.split(modc, 6, axis=-1)
        h = x * (1.0 + sc_m) + sh_m
        hc = ctx * (1.0 + scc_m) + shc_m
        if i % 2 == 0:
            y, yc = mixer_ab(h, hc, ab_w_in[j], ab_conv_w[j], ab_conv_g[j], ab_conv_b[j],
                             ab_q_g[j], ab_k_g[j], ab_w_out[j], rows, cols, update_ctx)
        else:
            y, yc = mixer_na(h, hc, na_w_qkv[j], na_rpb[j], na_w_out[j], update_ctx)
        x = layer_norm(ALPHA * x + g_m * y, ln_g[i, 0], ln_b[i, 0])
        if update_ctx:
            ctx = layer_norm(ALPHA * ctx + gc_m * yc, ln_g[i, 0], ln_b[i, 0])
        h = x * (1.0 + sc_f) + sh_f
        if i % 2 == 0:
            y = swiglu(h, ffn_w_gate[j], ffn_w_up[j], ffn_w_down[j])
        else:
            y = moe_swiglu(h, moe_w_router[j], moe_w_gate[j], moe_w_up[j], moe_w_down[j])
        x = layer_norm(ALPHA * x + g_f * y, ln_g[i, 1], ln_b[i, 1])
        if update_ctx:
            hc = ctx * (1.0 + scc_f) + shc_f
            if i % 2 == 0:
                yc = swiglu(hc, ffn_w_gate[j], ffn_w_up[j], ffn_w_down[j])
            else:
                yc = moe_swiglu(hc, moe_w_router[j], moe_w_gate[j], moe_w_up[j], moe_w_down[j])
            ctx = layer_norm(ALPHA * ctx + gc_f * yc, ln_g[i, 1], ln_b[i, 1])
    return x
```

```python
import functools

import numpy as np
import jax
import jax.numpy as jnp
from jax import lax
from jax.experimental import pallas as pl
from jax.experimental.pallas import tpu as pltpu

F32 = jnp.float32
BF16 = jnp.bfloat16

D_MODEL = 1024
GRID_W = 64
HEAD_DIM = 64
CONV_CH = 512
CONV_WIDTH = 31
CONV_HALO = 16
Q_COLS = 512
KV_COLS = 128
A_COLS = 2 * CONV_CH
ROPE_THETA = 10000.0
WIN_R = 8
WIN_C = 16
N_EXPERTS = 8
DEPTH = 2
ALPHA = (2 * DEPTH) ** 0.25
LN_EPS = 1e-5
RMS_EPS = 1e-6
ATT_SCALE = HEAD_DIM ** -0.5
MASK_VALUE = -1e30

LANES = 128
VMEM_LIMIT = 56 * 1024 * 1024

ATT_TQ = 256
NA_ROWS = 4
NA_KROWS = 12
MOE_TM = 512
MOE_TF = 1792
GATHER_ROWS = 256


def _cparams(*sem):
    return pltpu.CompilerParams(dimension_semantics=sem, vmem_limit_bytes=VMEM_LIMIT)


def _layer_norm(z, g, b):
    mu = jnp.mean(z, axis=-1, keepdims=True)
    zc = z - mu
    var = jnp.mean(zc * zc, axis=-1, keepdims=True)
    return zc * lax.rsqrt(var + LN_EPS) * g + b


def _vec_spec(k, tm, rows_per_batch):
    if rows_per_batch is None:
        return pl.BlockSpec((1, 1, D_MODEL), lambda i, *_: (4 * 6 + k, 0, 0))
    return pl.BlockSpec((1, 1, D_MODEL), lambda i, *_: ((i * tm // rows_per_batch) * 6 + k, 0, 0))


def _full_spec(shape):
    nd = len(shape)
    return pl.BlockSpec(shape, lambda *_: (0,) * nd)


def _mod_kernel(c_ref, w_ref, b_ref, o_ref):
    c = c_ref[...]
    s = c * jax.nn.sigmoid(c)
    o_ref[...] = jnp.dot(s.astype(BF16), w_ref[...].astype(BF16), preferred_element_type=F32) + b_ref[...]


def _modulation(cc, w_mod, b_mod):
    n = 6 * D_MODEL
    tn = D_MODEL
    return pl.pallas_call(
        _mod_kernel,
        grid=(DEPTH, n // tn),
        in_specs=[pl.BlockSpec((8, D_MODEL), lambda l, j: (0, 0)),
                  pl.BlockSpec((None, D_MODEL, tn), lambda l, j: (l, 0, j)),
                  pl.BlockSpec((None, 1, tn), lambda l, j: (l, 0, j))],
        out_specs=pl.BlockSpec((None, 8, tn), lambda l, j: (l, 0, j)),
        out_shape=jax.ShapeDtypeStruct((DEPTH, 8, n), F32),
        compiler_params=_cparams("parallel", "parallel"),
        name="modulation",
    )(cc, w_mod, b_mod.reshape(DEPTH, 1, n))


def _inproj0_kernel(x_ref, sc_ref, sh_ref, w_ref, oa_ref, oq_ref):
    h = x_ref[...] * (1.0 + sc_ref[0]) + sh_ref[0]
    o = jnp.dot(h.astype(BF16), w_ref[...], preferred_element_type=F32)
    oa_ref[...] = o[:, :A_COLS]
    oq_ref[...] = o[:, A_COLS:]


def _inproj0(x2d, vecs, w, rows_per_batch, tm):
    r = x2d.shape[0]
    nq = w.shape[1] - A_COLS
    return pl.pallas_call(
        _inproj0_kernel,
        grid=(r // tm,),
        in_specs=[pl.BlockSpec((tm, D_MODEL), lambda i: (i, 0)),
                  _vec_spec(1, tm, rows_per_batch), _vec_spec(0, tm, rows_per_batch),
                  _full_spec(w.shape)],
        out_specs=[pl.BlockSpec((tm, A_COLS), lambda i: (i, 0)),
                   pl.BlockSpec((tm, nq), lambda i: (i, 0))],
        out_shape=[jax.ShapeDtypeStruct((r, A_COLS), F32), jax.ShapeDtypeStruct((r, nq), F32)],
        compiler_params=_cparams("parallel"),
        name="inproj0",
    )(x2d, vecs, vecs, w)


def _conv_kernel(pm_ref, pp_ref, pn_ref, w_ref, g_ref, b_ref, o_ref, u_ref, *, tt, tx):
    i = pl.program_id(0)

    def glu(p):
        return p[:, :CONV_CH] * jax.nn.sigmoid(p[:, CONV_CH:])

    first = (i * tt) % tx == 0
    last = ((i + 1) * tt) % tx == 0
    u_ref[0:CONV_HALO, :] = jnp.where(first, 0.0, glu(pp_ref[...]))
    u_ref[CONV_HALO:CONV_HALO + tt, :] = glu(pm_ref[...])
    u_ref[CONV_HALO + tt:2 * CONV_HALO + tt, :] = jnp.where(last, 0.0, glu(pn_ref[...]))
    ch = 32
    win = ch + 2 * CONV_HALO
    base = CONV_HALO - CONV_WIDTH // 2

    def body(c, carry):
        r0 = pl.multiple_of(c * ch, ch)
        w = u_ref[pl.ds(r0, win), :]
        acc = jnp.zeros((ch, CONV_CH), F32)
        for s in range(8):
            ws = w if s == 0 else pltpu.roll(w, win - s, axis=0)
            for a in range(win // 8):
                k = 8 * a + s - base
                if 0 <= k < CONV_WIDTH and 8 * a + ch <= win - s:
                    acc = acc + ws[8 * a:8 * a + ch] * w_ref[pl.ds(k, 1), :]
        y = _layer_norm(acc, g_ref[...], b_ref[...])
        y = y * jax.nn.sigmoid(y)
        o_ref[pl.ds(r0, ch), :] = y.astype(BF16)
        return carry

    lax.fori_loop(0, tt // ch, body, 0)


def _conformer_conv(pa, conv_w, conv_g, conv_b, tx, tt):
    r = pa.shape[0]
    hb = tt // CONV_HALO
    nhb = r // CONV_HALO
    return pl.pallas_call(
        functools.partial(_conv_kernel, tt=tt, tx=tx),
        grid=(r // tt,),
        in_specs=[pl.BlockSpec((tt, A_COLS), lambda i: (i, 0)),
                  pl.BlockSpec((CONV_HALO, A_COLS), lambda i: (jnp.maximum(i * hb - 1, 0), 0)),
                  pl.BlockSpec((CONV_HALO, A_COLS), lambda i: (jnp.minimum((i + 1) * hb, nhb - 1), 0)),
                  _full_spec((CONV_WIDTH, CONV_CH)), _full_spec((1, CONV_CH)), _full_spec((1, CONV_CH))],
        out_specs=pl.BlockSpec((tt, CONV_CH), lambda i: (i, 0)),
        out_shape=jax.ShapeDtypeStruct((r, CONV_CH), BF16),
        scratch_shapes=[pltpu.VMEM((tt + 2 * CONV_HALO, CONV_CH), F32)],
        compiler_params=_cparams("parallel"),
        name="conformer_conv",
    )(pa, pa, pa, conv_w, conv_g.reshape(1, CONV_CH), conv_b.reshape(1, CONV_CH))


def _prep_kernel(p_ref, cos_ref, sin_ref, qg_ref, kg_ref, qt_ref, k_ref, vt_ref, *, tq):
    x = p_ref[...]
    cos = cos_ref[...]
    sin = sin_ref[...]

    def norm_rope(xh, g):
        ms = jnp.mean(xh * xh, axis=0, keepdims=True)
        y = xh * lax.rsqrt(ms + RMS_EPS) * g
        sw = jnp.concatenate([y[16:32], y[0:16], y[48:64], y[32:48]], axis=0)
        return y * cos + sw * sin

    zeros = jnp.zeros((HEAD_DIM, tq), BF16)
    for p in range(Q_COLS // LANES):
        xp = x[:, LANES * p:LANES * (p + 1)].T
        for half in range(2):
            h = 2 * p + half
            j, g = h // 4, h % 4
            r = (norm_rope(xp[HEAD_DIM * half:HEAD_DIM * (half + 1)], qg_ref[...]) * ATT_SCALE).astype(BF16)
            qt_ref[j, HEAD_DIM * j:HEAD_DIM * (j + 1), g * tq:(g + 1) * tq] = r
            qt_ref[j, HEAD_DIM * (1 - j):HEAD_DIM * (2 - j), g * tq:(g + 1) * tq] = zeros
    xk = x[:, Q_COLS:Q_COLS + KV_COLS].T
    k0 = norm_rope(xk[0:HEAD_DIM], kg_ref[...])
    k1 = norm_rope(xk[HEAD_DIM:2 * HEAD_DIM], kg_ref[...])
    k_ref[...] = jnp.concatenate([k0, k1], axis=0).T.astype(BF16)
    xv = x[:, Q_COLS + KV_COLS:].T.astype(BF16)
    vt_ref[0] = xv[0:HEAD_DIM]
    vt_ref[1] = xv[HEAD_DIM:2 * HEAD_DIM]


def _prep_qkv(pq, cos_t, sin_t, q_g, k_g, bx, tx):
    tq = ATT_TQ
    nq = tx // tq
    return pl.pallas_call(
        functools.partial(_prep_kernel, tq=tq),
        grid=(bx, nq),
        in_specs=[pl.BlockSpec((tq, Q_COLS + 2 * KV_COLS), lambda b, t: (b * nq + t, 0)),
                  pl.BlockSpec((HEAD_DIM, tq), lambda b, t: (0, t)),
                  pl.BlockSpec((HEAD_DIM, tq), lambda b, t: (0, t)),
                  _full_spec((HEAD_DIM, 1)), _full_spec((HEAD_DIM, 1))],
        out_specs=[pl.BlockSpec((None, None, 2, LANES, 4 * tq), lambda b, t: (b, t, 0, 0, 0)),
                   pl.BlockSpec((None, None, tq, LANES), lambda b, t: (b, t, 0, 0)),
                   pl.BlockSpec((None, 2, None, HEAD_DIM, tq), lambda b, t: (b, 0, t, 0, 0))],
        out_shape=[jax.ShapeDtypeStruct((bx, nq, 2, LANES, 4 * tq), BF16),
                   jax.ShapeDtypeStruct((bx, nq, tq, LANES), BF16),
                   jax.ShapeDtypeStruct((bx, 2, nq, HEAD_DIM, tq), BF16)],
        compiler_params=_cparams("parallel", "parallel"),
        name="prep_qkv",
    )(pq, cos_t, sin_t, q_g.reshape(HEAD_DIM, 1), k_g.reshape(HEAD_DIM, 1))


def _rope_tables(t):
    pos = np.arange(t)
    half = HEAD_DIM // 4
    inv = ROPE_THETA ** (-jnp.arange(half, dtype=F32) * 2.0 / (HEAD_DIM // 2))
    dd = np.arange(HEAD_DIM)
    part_pos = np.where((dd // (HEAD_DIM // 2))[:, None] == 0, (pos // GRID_W)[None, :], (pos % GRID_W)[None, :])
    ang = jnp.asarray(part_pos, F32) * inv[dd % half][:, None]
    sign = jnp.asarray(np.where((dd % (HEAD_DIM // 2)) < half, -1.0, 1.0)[:, None], F32)
    return jnp.cos(ang), jnp.sin(ang) * sign


def _attn_kernel(qt_ref, k_ref, vt_ref, o_ref, acc_ref, *, nk, tq):
    qt = qt_ref[...]
    ncol = 4 * tq
    acc_ref[...] = jnp.zeros_like(acc_ref)

    def body(c, carry):
        m, l = carry
        s = jnp.dot(k_ref[c], qt, preferred_element_type=F32)
        m_new = jnp.maximum(m, jnp.max(s, axis=0, keepdims=True))
        alpha = jnp.exp(m - m_new)
        p = jnp.exp(s - m_new)
        l = alpha * l + jnp.sum(p, axis=0, keepdims=True)
        pv = jnp.dot(vt_ref[c], p.astype(BF16), preferred_element_type=F32)
        acc_ref[...] = acc_ref[...] * alpha + pv
        return m_new, l

    m0 = jnp.full((1, ncol), -jnp.inf, F32)
    l0 = jnp.zeros((1, ncol), F32)
    _, l = lax.fori_loop(0, nk, body, (m0, l0))
    o = acc_ref[...] * (1.0 / l)
    for pp in range(2):
        blk = jnp.concatenate([o[:, (2 * pp) * tq:(2 * pp + 1) * tq],
                               o[:, (2 * pp + 1) * tq:(2 * pp + 2) * tq]], axis=0)
        o_ref[:, LANES * pp:LANES * (pp + 1)] = blk.T.astype(BF16)


def _gqa_attention(qt, k, vt, tx):
    bx, nq = qt.shape[0], qt.shape[1]
    nk = k.shape[1]
    tq = ATT_TQ
    return pl.pallas_call(
        functools.partial(_attn_kernel, nk=nk, tq=tq),
        grid=(bx, 2, nq),
        in_specs=[pl.BlockSpec((None, None, None, LANES, 4 * tq), lambda b, j, t: (b, t, j, 0, 0)),
                  pl.BlockSpec((None, nk, tq, LANES), lambda b, j, t: (b, 0, 0, 0)),
                  pl.BlockSpec((None, None, nk, HEAD_DIM, tq), lambda b, j, t: (b, j, 0, 0, 0))],
        out_specs=pl.BlockSpec((tq, 2 * LANES), lambda b, j, t: (b * nq + t, j)),
        out_shape=jax.ShapeDtypeStruct((bx * tx, Q_COLS), BF16),
        scratch_shapes=[pltpu.VMEM((HEAD_DIM, 4 * tq), F32)],
        compiler_params=_cparams("parallel", "parallel", "parallel"),
        name="gqa_attention",
    )(qt, k, vt)


def _outproj_ln_kernel(*refs, n_lhs):
    lhs = refs[:n_lhs]
    w_ref, x_ref, gate_ref, lng_ref, lnb_ref, o_ref = refs[n_lhs:]
    y = None
    off = 0
    for r in lhs:
        kk = r.shape[1]
        t = jnp.dot(r[...], w_ref[off:off + kk, :], preferred_element_type=F32)
        off += kk
        y = t if y is None else y + t
    z = ALPHA * x_ref[...] + gate_ref[0] * y
    o_ref[...] = _layer_norm(z, lng_ref[...], lnb_ref[...])


def _outproj_ln(lhs, w, x2d, vecs, gate_k, ln_g, ln_b, rows_per_batch, tm):
    r = x2d.shape[0]
    return pl.pallas_call(
        functools.partial(_outproj_ln_kernel, n_lhs=len(lhs)),
        grid=(r // tm,),
        in_specs=[pl.BlockSpec((tm, a.shape[1]), lambda i: (i, 0)) for a in lhs]
        + [_full_spec(w.shape), pl.BlockSpec((tm, D_MODEL), lambda i: (i, 0)),
           _vec_spec(gate_k, tm, rows_per_batch), _full_spec((1, D_MODEL)), _full_spec((1, D_MODEL))],
        out_specs=pl.BlockSpec((tm, D_MODEL), lambda i: (i, 0)),
        out_shape=jax.ShapeDtypeStruct((r, D_MODEL), F32),
        compiler_params=_cparams("parallel"),
        name="outproj_ln",
    )(*lhs, w, x2d, vecs, ln_g.reshape(1, D_MODEL), ln_b.reshape(1, D_MODEL))


def _ffn_kernel(x_ref, sc_ref, sh_ref, gate_ref, wg_ref, wu_ref, wd_ref, lng_ref, lnb_ref, o_ref, hb_ref, acc_ref):
    j = pl.program_id(1)

    @pl.when(j == 0)
    def _():
        hb_ref[...] = (x_ref[...] * (1.0 + sc_ref[0]) + sh_ref[0]).astype(BF16)
        acc_ref[...] = jnp.zeros_like(acc_ref)

    hb = hb_ref[...]
    hg = jnp.dot(hb, wg_ref[...], preferred_element_type=F32)
    hu = jnp.dot(hb, wu_ref[...], preferred_element_type=F32)
    h1 = (hg * jax.nn.sigmoid(hg) * hu).astype(BF16)
    acc_ref[...] += jnp.dot(h1, wd_ref[...], preferred_element_type=F32)

    @pl.when(j == pl.num_programs(1) - 1)
    def _():
        z = ALPHA * x_ref[...] + gate_ref[0] * acc_ref[...]
        o_ref[...] = _layer_norm(z, lng_ref[...], lnb_ref[...])


def _ffn_ln(x2d, vecs, wg, wu, wd, ln_g, ln_b, rows_per_batch, tm, tf):
    r = x2d.shape[0]
    f = wg.shape[1]
    return pl.pallas_call(
        _ffn_kernel,
        grid=(r // tm, f // tf),
        in_specs=[pl.BlockSpec((tm, D_MODEL), lambda i, j: (i, 0)),
                  _vec_spec(4, tm, rows_per_batch), _vec_spec(3, tm, rows_per_batch),
                  _vec_spec(5, tm, rows_per_batch),
                  pl.BlockSpec((D_MODEL, tf), lambda i, j: (0, j)),
                  pl.BlockSpec((D_MODEL, tf), lambda i, j: (0, j)),
                  pl.BlockSpec((tf, D_MODEL), lambda i, j: (j, 0)),
                  _full_spec((1, D_MODEL)), _full_spec((1, D_MODEL))],
        out_specs=pl.BlockSpec((tm, D_MODEL), lambda i, j: (i, 0)),
        out_shape=jax.ShapeDtypeStruct((r, D_MODEL), F32),
        scratch_shapes=[pltpu.VMEM((tm, D_MODEL), BF16), pltpu.VMEM((tm, D_MODEL), F32)],
        compiler_params=_cparams("parallel", "arbitrary"),
        name="ffn_ln",
    )(x2d, vecs, vecs, vecs, wg, wu, wd, ln_g.reshape(1, D_MODEL), ln_b.reshape(1, D_MODEL))


def _inproj1_kernel(x_ref, sc_ref, sh_ref, w_ref, o_ref, *, n_off):
    h = x_ref[...] * (1.0 + sc_ref[0]) + sh_ref[0]
    o = jnp.dot(h.astype(BF16), w_ref[...], preferred_element_type=F32)
    scale = jnp.where(pl.program_id(1) + n_off == 0, ATT_SCALE, 1.0)
    o_ref[...] = (o * scale).astype(BF16)


def _inproj1(x2d, vecs, w, rows_per_batch, tm, n_off):
    r = x2d.shape[0]
    nn = 3 - n_off
    return pl.pallas_call(
        functools.partial(_inproj1_kernel, n_off=n_off),
        grid=(r // tm, nn),
        in_specs=[pl.BlockSpec((tm, D_MODEL), lambda i, n: (i, 0)),
                  _vec_spec(1, tm, rows_per_batch), _vec_spec(0, tm, rows_per_batch),
                  pl.BlockSpec((D_MODEL, D_MODEL), lambda i, n: (0, n + n_off))],
        out_specs=pl.BlockSpec((None, tm, D_MODEL), lambda i, n: (n, i, 0)),
        out_shape=jax.ShapeDtypeStruct((nn, r, D_MODEL), BF16),
        compiler_params=_cparams("parallel", "parallel"),
        name="inproj1",
    )(x2d, vecs, vecs, w)


def _na_bias_indices():
    nq = NA_ROWS * GRID_W
    nkeys = NA_KROWS * GRID_W
    n_rows = GRID_W
    dr = np.zeros((3, nq, nkeys), np.int32)
    dc = np.zeros((3, nq, nkeys), np.int32)
    ok = np.zeros((3, nq, nkeys), bool)
    for var, r0 in enumerate((0, NA_ROWS, n_rows - NA_ROWS)):
        start = int(np.clip(r0 - WIN_R // 2, 0, n_rows - NA_KROWS))
        r = r0 + np.arange(nq) // GRID_W
        c = np.arange(nq) % GRID_W
        kr = start + np.arange(nkeys) // GRID_W
        kc = np.arange(nkeys) % GRID_W
        rs = np.clip(r - WIN_R // 2, 0, n_rows - WIN_R)
        cs = np.clip(c - WIN_C // 2, 0, GRID_W - WIN_C)
        in_r = (kr[None, :] >= rs[:, None]) & (kr[None, :] < rs[:, None] + WIN_R)
        in_c = (kc[None, :] >= cs[:, None]) & (kc[None, :] < cs[:, None] + WIN_C)
        ok[var] = in_r & in_c
        dr[var] = np.clip(kr[None, :] - r[:, None] + WIN_R - 1, 0, 2 * WIN_R - 2)
        dc[var] = np.clip(kc[None, :] - c[:, None] + WIN_C - 1, 0, 2 * WIN_C - 2)
    return dr, dc, ok


def _na_kernel(q_ref, k0_ref, k1_ref, k2_ref, kc_ref, v0_ref, v1_ref, v2_ref, vc_ref, bias_ref, o_ref):
    q = q_ref[...]
    kk = jnp.concatenate([k0_ref[...], k1_ref[...], k2_ref[...], kc_ref[...]], axis=0)
    vv = jnp.concatenate([v0_ref[...], v1_ref[...], v2_ref[...], vc_ref[...]], axis=0)
    nctx = kc_ref.shape[0]
    lane = lax.broadcasted_iota(jnp.int32, (1, LANES), 1)
    out = jnp.zeros(q.shape, F32)
    for e in range(2):
        sel = (lane // HEAD_DIM == e).astype(BF16)
        s = lax.dot_general(q * sel, kk, (((1,), (1,)), ((), ())), preferred_element_type=F32)
        s = s + jnp.concatenate([bias_ref[e], jnp.zeros((q.shape[0], nctx), F32)], axis=1)
        m = jnp.max(s, axis=1, keepdims=True)
        p = jnp.exp(s - m)
        l = jnp.sum(p, axis=1, keepdims=True)
        out = out + jnp.dot(p.astype(BF16), vv * sel, preferred_element_type=F32) * (1.0 / l)
    o_ref[...] = out.astype(BF16)


def _neighbourhood_attention(qkv, kvc, bias, bsz, t):
    nq = NA_ROWS * GRID_W
    nrb = t // nq
    nhp = D_MODEL // LANES
    nctx = kvc.shape[1] // bsz
    nwin = NA_KROWS // NA_ROWS

    def kspec(which, d):
        return pl.BlockSpec((None, nq, LANES),
                            lambda hp, b, rb: (which, b * nrb + jnp.clip(rb - 1, 0, nrb - nwin) + d, hp))

    def cspec(which):
        return pl.BlockSpec((None, nctx, LANES), lambda hp, b, rb: (which, b, hp))

    def var(rb):
        return jnp.where(rb == 0, 0, jnp.where(rb == nrb - 1, 2, 1))

    return pl.pallas_call(
        _na_kernel,
        grid=(nhp, bsz, nrb),
        in_specs=[pl.BlockSpec((None, nq, LANES), lambda hp, b, rb: (0, b * nrb + rb, hp)),
                  kspec(1, 0), kspec(1, 1), kspec(1, 2), cspec(0),
                  kspec(2, 0), kspec(2, 1), kspec(2, 2), cspec(1),
                  pl.BlockSpec((None, None, 2, nq, NA_KROWS * GRID_W), lambda hp, b, rb: (var(rb), hp, 0, 0, 0))],
        out_specs=pl.BlockSpec((nq, LANES), lambda hp, b, rb: (b * nrb + rb, hp)),
        out_shape=jax.ShapeDtypeStruct((bsz * t, D_MODEL), BF16),
        compiler_params=_cparams("parallel", "parallel", "parallel"),
        name="neighbourhood_attention",
    )(qkv, qkv, qkv, qkv, kvc, qkv, qkv, qkv, kvc, bias)


def _router_kernel(x_ref, sc_ref, sh_ref, wr_ref, h_ref, r_ref):
    h = x_ref[...] * (1.0 + sc_ref[0]) + sh_ref[0]
    h_ref[...] = h
    hi = h.astype(BF16)
    lo = (h - hi.astype(F32)).astype(BF16)
    w = wr_ref[...]
    whi = w.astype(BF16)
    wlo = (w - whi.astype(F32)).astype(BF16)
    lg = (jnp.dot(hi, whi, preferred_element_type=F32)
          + (jnp.dot(hi, wlo, preferred_element_type=F32) + jnp.dot(lo, whi, preferred_element_type=F32)))
    lane = lax.broadcasted_iota(jnp.int32, lg.shape, 1).astype(F32)
    lg = jnp.where(lane < N_EXPERTS, lg, -jnp.inf)
    v1 = jnp.max(lg, axis=1, keepdims=True)
    i1 = jnp.min(jnp.where(lg == v1, lane, float(LANES)), axis=1, keepdims=True)
    lg2 = jnp.where(lane == i1, -jnp.inf, lg)
    v2 = jnp.max(lg2, axis=1, keepdims=True)
    i2 = jnp.min(jnp.where(lg2 == v2, lane, float(LANES)), axis=1, keepdims=True)
    e = jnp.exp(v2 - v1)
    w1 = 1.0 / (1.0 + e)
    w2 = e / (1.0 + e)
    r_ref[...] = jnp.where(lane == 0, i1, jnp.where(lane == 1, i2, jnp.where(lane == 2, w1,
                                                                              jnp.where(lane == 3, w2, 0.0))))


def _router(x2d, vecs, w_router, rows_per_batch, tm):
    r = x2d.shape[0]
    wr = jnp.pad(w_router, ((0, 0), (0, LANES - N_EXPERTS)))
    return pl.pallas_call(
        _router_kernel,
        grid=(r // tm,),
        in_specs=[pl.BlockSpec((tm, D_MODEL), lambda i: (i, 0)),
                  _vec_spec(4, tm, rows_per_batch), _vec_spec(3, tm, rows_per_batch),
                  _full_spec((D_MODEL, LANES))],
        out_specs=[pl.BlockSpec((tm, D_MODEL), lambda i: (i, 0)), pl.BlockSpec((tm, LANES), lambda i: (i, 0))],
        out_shape=[jax.ShapeDtypeStruct((r, D_MODEL), F32), jax.ShapeDtypeStruct((r, LANES), F32)],
        compiler_params=_cparams("parallel"),
        name="router",
    )(x2d, vecs, vecs, wr)


def _row_copy(src_hbm, idx_ref, r, dst_ref, sem):
    return pltpu.make_async_copy(src_hbm.at[pl.ds(idx_ref[0, r], 1), :], dst_ref.at[pl.ds(r, 1), :], sem)


def _gather_rows_into(src_hbm, idx_ref, dst_ref, sem, n):
    def issue(r, carry):
        _row_copy(src_hbm, idx_ref, r, dst_ref, sem).start()
        return carry

    lax.fori_loop(0, n, issue, 0)


def _wait_rows(src_hbm, idx_ref, dst_ref, sem, n):
    def wait(r, carry):
        _row_copy(src_hbm, idx_ref, r, dst_ref, sem).wait()
        return carry

    lax.fori_loop(0, n, wait, 0)


def _gather_kernel(idx_ref, src_hbm, o_ref, sem):
    n = o_ref.shape[0]
    _gather_rows_into(src_hbm, idx_ref, o_ref, sem, n)
    _wait_rows(src_hbm, idx_ref, o_ref, sem, n)


def _gather_rows(src, idx):
    p = idx.shape[0]
    g = GATHER_ROWS
    return pl.pallas_call(
        _gather_kernel,
        grid=(p // g,),
        in_specs=[pl.BlockSpec((None, 1, g), lambda i: (i, 0, 0), memory_space=pltpu.SMEM),
                  pl.BlockSpec(memory_space=pl.ANY)],
        out_specs=pl.BlockSpec((g, src.shape[1]), lambda i: (i, 0)),
        out_shape=jax.ShapeDtypeStruct((p, src.shape[1]), src.dtype),
        scratch_shapes=[pltpu.SemaphoreType.DMA(())],
        compiler_params=_cparams("arbitrary"),
        name="gather_rows",
    )(idx.reshape(p // g, 1, g), src)


def _moe_ffn_kernel(te_ref, nu_ref, xs_ref, wrow_ref, wg_ref, wu_ref, wd_ref, o_ref, hb_ref, acc_ref):
    t = pl.program_id(0)
    j = pl.program_id(1)
    nj = pl.num_programs(1)
    used = t < nu_ref[0]

    @pl.when(jnp.logical_and(used, j == 0))
    def _():
        hb_ref[...] = xs_ref[...].astype(BF16)
        acc_ref[...] = jnp.zeros_like(acc_ref)

    @pl.when(used)
    def _():
        hb = hb_ref[...]
        hg = jnp.dot(hb, wg_ref[...], preferred_element_type=F32)
        hu = jnp.dot(hb, wu_ref[...], preferred_element_type=F32)
        h1 = (hg * jax.nn.sigmoid(hg) * hu).astype(BF16)
        acc_ref[...] += jnp.dot(h1, wd_ref[...], preferred_element_type=F32)

    @pl.when(jnp.logical_and(used, j == nj - 1))
    def _():
        o_ref[...] = acc_ref[...] * wrow_ref[...]

    @pl.when(jnp.logical_and(jnp.logical_not(used), j == nj - 1))
    def _():
        o_ref[...] = jnp.zeros_like(o_ref)


def _moe_ffn(xs, wrow, tile_expert, n_used, wg, wu, wd):
    p = xs.shape[0]
    tm, tf = MOE_TM, MOE_TF
    f = wg.shape[2]

    def jj(t, j, nu):
        return jnp.where(t < nu[0], j, 0)

    grid_spec = pltpu.PrefetchScalarGridSpec(
        num_scalar_prefetch=2,
        grid=(p // tm, f // tf),
        in_specs=[pl.BlockSpec((tm, D_MODEL), lambda t, j, te, nu: (t, 0)),
                  pl.BlockSpec((tm, 1), lambda t, j, te, nu: (t, 0)),
                  pl.BlockSpec((None, D_MODEL, tf), lambda t, j, te, nu: (te[t], 0, jj(t, j, nu))),
                  pl.BlockSpec((None, D_MODEL, tf), lambda t, j, te, nu: (te[t], 0, jj(t, j, nu))),
                  pl.BlockSpec((None, tf, D_MODEL), lambda t, j, te, nu: (te[t], jj(t, j, nu), 0))],
        out_specs=pl.BlockSpec((tm, D_MODEL), lambda t, j, te, nu: (t, 0)),
        scratch_shapes=[pltpu.VMEM((tm, D_MODEL), BF16), pltpu.VMEM((tm, D_MODEL), F32)],
    )
    return pl.pallas_call(
        _moe_ffn_kernel,
        grid_spec=grid_spec,
        out_shape=jax.ShapeDtypeStruct((p, D_MODEL), F32),
        compiler_params=_cparams("arbitrary", "arbitrary"),
        name="moe_ffn",
    )(tile_expert, n_used, xs, wrow, wg, wu, wd)


def _combine_ln_kernel(p1_ref, p2_ref, ys_hbm, x_ref, gate_ref, lng_ref, lnb_ref, o_ref, y1_ref, y2_ref, sem):
    n = x_ref.shape[0]
    _gather_rows_into(ys_hbm, p1_ref, y1_ref, sem.at[0], n)
    _gather_rows_into(ys_hbm, p2_ref, y2_ref, sem.at[1], n)
    _wait_rows(ys_hbm, p1_ref, y1_ref, sem.at[0], n)
    _wait_rows(ys_hbm, p2_ref, y2_ref, sem.at[1], n)
    y = y1_ref[...] + y2_ref[...]
    z = ALPHA * x_ref[...] + gate_ref[0] * y
    o_ref[...] = _layer_norm(z, lng_ref[...], lnb_ref[...])


def _combine_ln(ys, pos1, pos2, x2d, vecs, ln_g, ln_b, rows_per_batch):
    r = x2d.shape[0]
    g = GATHER_ROWS
    idx_spec = pl.BlockSpec((None, 1, g), lambda i: (i, 0, 0), memory_space=pltpu.SMEM)
    return pl.pallas_call(
        _combine_ln_kernel,
        grid=(r // g,),
        in_specs=[idx_spec, idx_spec, pl.BlockSpec(memory_space=pl.ANY),
                  pl.BlockSpec((g, D_MODEL), lambda i: (i, 0)),
                  _vec_spec(5, g, rows_per_batch), _full_spec((1, D_MODEL)), _full_spec((1, D_MODEL))],
        out_specs=pl.BlockSpec((g, D_MODEL), lambda i: (i, 0)),
        out_shape=jax.ShapeDtypeStruct((r, D_MODEL), F32),
        scratch_shapes=[pltpu.VMEM((g, D_MODEL), F32), pltpu.VMEM((g, D_MODEL), F32),
                        pltpu.SemaphoreType.DMA((2,))],
        compiler_params=_cparams("arbitrary"),
        name="combine_ln",
    )(pos1.reshape(r // g, 1, g), pos2.reshape(r // g, 1, g), ys, x2d, vecs,
      ln_g.reshape(1, D_MODEL), ln_b.reshape(1, D_MODEL))


def _routing_plan(route, tm):
    n = route.shape[0]
    e = jnp.concatenate([route[:, 0], route[:, 1]]).astype(jnp.int32)
    w = jnp.concatenate([route[:, 2], route[:, 3]])
    tok = jnp.concatenate([jnp.arange(n, dtype=jnp.int32)] * 2)
    onehot = (e[:, None] == jnp.arange(N_EXPERTS, dtype=jnp.int32)[None, :]).astype(jnp.int32)
    csum = jnp.cumsum(onehot, axis=0)
    rank = jnp.sum(csum * onehot, axis=1) - 1
    counts = csum[-1]
    padded = ((counts + tm - 1) // tm) * tm
    ends = jnp.cumsum(padded)
    starts = ends - padded
    pos = jnp.sum(starts[None, :] * onehot, axis=1) + rank
    p = 2 * n + N_EXPERTS * tm
    src = jnp.zeros((p,), jnp.int32).at[pos].set(tok)
    wrow = jnp.zeros((p,), F32).at[pos].set(w)
    tile_start = jnp.arange(p // tm, dtype=jnp.int32) * tm
    tile_expert = jnp.minimum(jnp.sum((tile_start[:, None] >= ends[None, :]).astype(jnp.int32), axis=1),
                              N_EXPERTS - 1).astype(jnp.int32)
    n_used = (ends[-1] // tm).astype(jnp.int32).reshape(1)
    return src, wrow.reshape(p, 1), tile_expert, n_used, pos[:n], pos[n:]


def kernel(x, c, ctx, c_ctx, w_mod, b_mod, ln_g, ln_b, ab_w_in, ab_conv_w, ab_conv_g, ab_conv_b, ab_q_g, ab_k_g,
           ab_w_out, ffn_w_gate, ffn_w_up, ffn_w_down, na_w_qkv, na_rpb, na_w_out, moe_w_router, moe_w_gate,
           moe_w_up, moe_w_down):
    bsz, t, d = x.shape
    n_ctx = ctx.shape[1]
    n = bsz * t
    nc = bsz * n_ctx
    x2 = x.reshape(n, d)
    c2 = ctx.reshape(nc, d)

    cc = jnp.concatenate([c, c_ctx[None, :], jnp.zeros((8 - bsz - 1, d), F32)], axis=0)
    mod = _modulation(cc, w_mod, b_mod)
    vec0 = mod[0].reshape(8 * 6, 1, d)
    vec1 = mod[1].reshape(8 * 6, 1, d)

    w_in = ab_w_in[0].astype(BF16)
    w_out = ab_w_out[0].astype(BF16)
    pa, pq = _inproj0(x2, vec0, w_in, t, 512)
    pac, pqc = _inproj0(c2, vec0, w_in, None, 512)
    a = _conformer_conv(pa, ab_conv_w[0], ab_conv_g[0], ab_conv_b[0], t, 512)
    ac = _conformer_conv(pac, ab_conv_w[0], ab_conv_g[0], ab_conv_b[0], n_ctx, n_ctx)
    cos_t, sin_t = _rope_tables(t)
    qt, k, vt = _prep_qkv(pq, cos_t, sin_t, ab_q_g[0], ab_k_g[0], bsz, t)
    ones = jnp.ones((HEAD_DIM, n_ctx), F32)
    qtc, kc, vtc = _prep_qkv(pqc, ones, jnp.zeros_like(ones), ab_q_g[0], ab_k_g[0], bsz, n_ctx)
    o = _gqa_attention(qt, jnp.concatenate([kc, k], axis=1), jnp.concatenate([vtc, vt], axis=2), t)
    oc = _gqa_attention(qtc, kc, vtc, n_ctx)
    x2 = _outproj_ln([a, o], w_out, x2, vec0, 2, ln_g[0, 0], ln_b[0, 0], t, 512)
    c2 = _outproj_ln([ac, oc], w_out, c2, vec0, 2, ln_g[0, 0], ln_b[0, 0], None, 512)
    wg = ffn_w_gate[0].astype(BF16)
    wu = ffn_w_up[0].astype(BF16)
    wd = ffn_w_down[0].astype(BF16)
    x2 = _ffn_ln(x2, vec0, wg, wu, wd, ln_g[0, 1], ln_b[0, 1], t, 512, 1408)
    c2 = _ffn_ln(c2, vec0, wg, wu, wd, ln_g[0, 1], ln_b[0, 1], None, 512, 1408)

    w_qkv = na_w_qkv[0].astype(BF16)
    qkv = _inproj1(x2, vec1, w_qkv, t, 512, 0)
    kvc = _inproj1(c2, vec1, w_qkv, None, 512, 1)
    dr, dc, ok = _na_bias_indices()
    bias = jnp.where(ok[None], na_rpb[0][:, dr, dc], MASK_VALUE)
    nh = bias.shape[0]
    bias = bias.reshape(nh // 2, 2, 3, bias.shape[2], bias.shape[3]).transpose(2, 0, 1, 3, 4)
    o = _neighbourhood_attention(qkv, kvc, bias, bsz, t)
    x2 = _outproj_ln([o], na_w_out[0].astype(BF16), x2, vec1, 2, ln_g[1, 0], ln_b[1, 0], t, 512)

    h, route = _router(x2, vec1, moe_w_router[0], t, 512)
    src, wrow, tile_expert, n_used, pos1, pos2 = _routing_plan(route, MOE_TM)
    xs = _gather_rows(h, src)
    ys = _moe_ffn(xs, wrow, tile_expert, n_used, moe_w_gate[0].astype(BF16), moe_w_up[0].astype(BF16),
                  moe_w_down[0].astype(BF16))
    x2 = _combine_ln(ys, pos1, pos2, x2, vec1, ln_g[1, 1], ln_b[1, 1], t)
    return x2.reshape(bsz, t, d)
```

```python
import functools

import numpy as np
import jax
import jax.numpy as jnp
from jax import lax
from jax.experimental import pallas as pl
from jax.experimental.pallas import tpu as pltpu

F32 = jnp.float32
BF16 = jnp.bfloat16

D_MODEL = 1024
GRID_W = 64
HEAD_DIM = 64
CONV_CH = 512
CONV_WIDTH = 31
CONV_HALO = 16
Q_COLS = 512
KV_COLS = 128
A_COLS = 2 * CONV_CH
ROPE_THETA = 10000.0
WIN_R = 8
WIN_C = 16
N_EXPERTS = 8
DEPTH = 2
ALPHA = (2 * DEPTH) ** 0.25
LN_EPS = 1e-5
RMS_EPS = 1e-6
ATT_SCALE = HEAD_DIM ** -0.5
LOG2E = 1.4426950408889634
MASK_VALUE = -1e30

LANES = 128
ROW_TILES = D_MODEL // LANES
VMEM_LIMIT = 56 * 1024 * 1024

ATT_TQ = 256
VT_ROWS = HEAD_DIM + 16
NA_ROWS = 4
NA_KROWS = 12
MOE_TM = 512
MOE_TF = 1792
GATHER_ROWS = 256


def _cparams(*sem):
    return pltpu.CompilerParams(dimension_semantics=sem, vmem_limit_bytes=VMEM_LIMIT)


def _layer_norm(z, g, b):
    mu = jnp.mean(z, axis=-1, keepdims=True)
    zc = z - mu
    var = jnp.mean(zc * zc, axis=-1, keepdims=True)
    return zc * lax.rsqrt(var + LN_EPS) * g + b


def _vec_spec(k, tm, rows_per_batch):
    if rows_per_batch is None:
        return pl.BlockSpec((1, 1, D_MODEL), lambda i, *_: (4 * 6 + k, 0, 0))
    return pl.BlockSpec((1, 1, D_MODEL), lambda i, *_: ((i * tm // rows_per_batch) * 6 + k, 0, 0))


def _full_spec(shape):
    nd = len(shape)
    return pl.BlockSpec(shape, lambda *_: (0,) * nd)


def _mod_kernel(c_ref, w_ref, b_ref, o_ref):
    c = c_ref[...]
    s = c * jax.nn.sigmoid(c)
    o_ref[...] = jnp.dot(s.astype(BF16), w_ref[...].astype(BF16), preferred_element_type=F32) + b_ref[...]


def _modulation(cc, w_mod, b_mod):
    n = 6 * D_MODEL
    tn = D_MODEL
    return pl.pallas_call(
        _mod_kernel,
        grid=(DEPTH, n // tn),
        in_specs=[pl.BlockSpec((8, D_MODEL), lambda l, j: (0, 0)),
                  pl.BlockSpec((None, D_MODEL, tn), lambda l, j: (l, 0, j)),
                  pl.BlockSpec((None, 1, tn), lambda l, j: (l, 0, j))],
        out_specs=pl.BlockSpec((None, 8, tn), lambda l, j: (l, 0, j)),
        out_shape=jax.ShapeDtypeStruct((DEPTH, 8, n), F32),
        compiler_params=_cparams("parallel", "parallel"),
        name="modulation",
    )(cc, w_mod, b_mod.reshape(DEPTH, 1, n))


def _inproj0_kernel(x_ref, sc_ref, sh_ref, w_ref, oa_ref, oq_ref):
    h = x_ref[...] * (1.0 + sc_ref[0]) + sh_ref[0]
    o = jnp.dot(h.astype(BF16), w_ref[...], preferred_element_type=F32)
    oa_ref[...] = o[:, :A_COLS]
    oq_ref[...] = o[:, A_COLS:]


def _inproj0(x2d, vecs, w, rows_per_batch, tm):
    r = x2d.shape[0]
    nq = w.shape[1] - A_COLS
    return pl.pallas_call(
        _inproj0_kernel,
        grid=(r // tm,),
        in_specs=[pl.BlockSpec((tm, D_MODEL), lambda i: (i, 0)),
                  _vec_spec(1, tm, rows_per_batch), _vec_spec(0, tm, rows_per_batch),
                  _full_spec(w.shape)],
        out_specs=[pl.BlockSpec((tm, A_COLS), lambda i: (i, 0)),
                   pl.BlockSpec((tm, nq), lambda i: (i, 0))],
        out_shape=[jax.ShapeDtypeStruct((r, A_COLS), F32), jax.ShapeDtypeStruct((r, nq), F32)],
        compiler_params=_cparams("parallel"),
        name="inproj0",
    )(x2d, vecs, vecs, w)


def _conv_kernel(pm_ref, pp_ref, pn_ref, w_ref, g_ref, b_ref, o_ref, u_ref, *, tt, tx):
    i = pl.program_id(0)

    def glu(p):
        return p[:, :CONV_CH] * jax.nn.sigmoid(p[:, CONV_CH:])

    first = (i * tt) % tx == 0
    last = ((i + 1) * tt) % tx == 0
    u_ref[0:CONV_HALO, :] = jnp.where(first, 0.0, glu(pp_ref[...]))
    u_ref[CONV_HALO:CONV_HALO + tt, :] = glu(pm_ref[...])
    u_ref[CONV_HALO + tt:2 * CONV_HALO + tt, :] = jnp.where(last, 0.0, glu(pn_ref[...]))
    ch = 32
    win = ch + 2 * CONV_HALO
    base = CONV_HALO - CONV_WIDTH // 2

    def body(c, carry):
        r0 = pl.multiple_of(c * ch, ch)
        w = u_ref[pl.ds(r0, win), :]
        acc = jnp.zeros((ch, CONV_CH), F32)
        for s in range(8):
            ws = w if s == 0 else pltpu.roll(w, win - s, axis=0)
            for a in range(win // 8):
                k = 8 * a + s - base
                if 0 <= k < CONV_WIDTH and 8 * a + ch <= win - s:
                    acc = acc + ws[8 * a:8 * a + ch] * w_ref[pl.ds(k, 1), :]
        y = _layer_norm(acc, g_ref[...], b_ref[...])
        y = y * jax.nn.sigmoid(y)
        o_ref[pl.ds(r0, ch), :] = y.astype(BF16)
        return carry

    lax.fori_loop(0, tt // ch, body, 0)


def _conformer_conv(pa, conv_w, conv_g, conv_b, tx, tt):
    r = pa.shape[0]
    hb = tt // CONV_HALO
    nhb = r // CONV_HALO
    return pl.pallas_call(
        functools.partial(_conv_kernel, tt=tt, tx=tx),
        grid=(r // tt,),
        in_specs=[pl.BlockSpec((tt, A_COLS), lambda i: (i, 0)),
                  pl.BlockSpec((CONV_HALO, A_COLS), lambda i: (jnp.maximum(i * hb - 1, 0), 0)),
                  pl.BlockSpec((CONV_HALO, A_COLS), lambda i: (jnp.minimum((i + 1) * hb, nhb - 1), 0)),
                  _full_spec((CONV_WIDTH, CONV_CH)), _full_spec((1, CONV_CH)), _full_spec((1, CONV_CH))],
        out_specs=pl.BlockSpec((tt, CONV_CH), lambda i: (i, 0)),
        out_shape=jax.ShapeDtypeStruct((r, CONV_CH), BF16),
        scratch_shapes=[pltpu.VMEM((tt + 2 * CONV_HALO, CONV_CH), F32)],
        compiler_params=_cparams("parallel"),
        name="conformer_conv",
    )(pa, pa, pa, conv_w, conv_g.reshape(1, CONV_CH), conv_b.reshape(1, CONV_CH))


def _prep_kernel(p_ref, cos_ref, sin_ref, qg_ref, kg_ref, qt_ref, k_ref, vt_ref, *, tq):
    x = p_ref[...]
    cos = cos_ref[...]
    sin = sin_ref[...]

    def norm_rope(xh, g):
        ms = jnp.mean(xh * xh, axis=0, keepdims=True)
        y = xh * lax.rsqrt(ms + RMS_EPS) * g
        sw = jnp.concatenate([y[16:32], y[0:16], y[48:64], y[32:48]], axis=0)
        return y * cos + sw * sin

    zeros = jnp.zeros((HEAD_DIM, tq), BF16)
    for p in range(Q_COLS // LANES):
        xp = x[:, LANES * p:LANES * (p + 1)].T
        for half in range(2):
            h = 2 * p + half
            j, g = h // 4, h % 4
            r = (norm_rope(xp[HEAD_DIM * half:HEAD_DIM * (half + 1)], qg_ref[...]) * (ATT_SCALE * LOG2E)).astype(BF16)
            qt_ref[j, HEAD_DIM * j:HEAD_DIM * (j + 1), g * tq:(g + 1) * tq] = r
            qt_ref[j, HEAD_DIM * (1 - j):HEAD_DIM * (2 - j), g * tq:(g + 1) * tq] = zeros
    xk = x[:, Q_COLS:Q_COLS + KV_COLS].T
    k0 = norm_rope(xk[0:HEAD_DIM], kg_ref[...])
    k1 = norm_rope(xk[HEAD_DIM:2 * HEAD_DIM], kg_ref[...])
    k_ref[...] = jnp.concatenate([k0, k1], axis=0).T.astype(BF16)
    xv = x[:, Q_COLS + KV_COLS:].T.astype(BF16)
    ones = jnp.ones((VT_ROWS - HEAD_DIM, tq), BF16)
    for j in range(2):
        vt_ref[j, 0:HEAD_DIM, :] = xv[HEAD_DIM * j:HEAD_DIM * (j + 1)]
        vt_ref[j, HEAD_DIM:VT_ROWS, :] = ones


def _prep_qkv(pq, cos_t, sin_t, q_g, k_g, bx, tx):
    tq = ATT_TQ
    nq = tx // tq
    return pl.pallas_call(
        functools.partial(_prep_kernel, tq=tq),
        grid=(bx, nq),
        in_specs=[pl.BlockSpec((tq, Q_COLS + 2 * KV_COLS), lambda b, t: (b * nq + t, 0)),
                  pl.BlockSpec((HEAD_DIM, tq), lambda b, t: (0, t)),
                  pl.BlockSpec((HEAD_DIM, tq), lambda b, t: (0, t)),
                  _full_spec((HEAD_DIM, 1)), _full_spec((HEAD_DIM, 1))],
        out_specs=[pl.BlockSpec((None, None, 2, LANES, 4 * tq), lambda b, t: (b, t, 0, 0, 0)),
                   pl.BlockSpec((None, None, tq, LANES), lambda b, t: (b, t, 0, 0)),
                   pl.BlockSpec((None, 2, None, VT_ROWS, tq), lambda b, t: (b, 0, t, 0, 0))],
        out_shape=[jax.ShapeDtypeStruct((bx, nq, 2, LANES, 4 * tq), BF16),
                   jax.ShapeDtypeStruct((bx, nq, tq, LANES), BF16),
                   jax.ShapeDtypeStruct((bx, 2, nq, VT_ROWS, tq), BF16)],
        compiler_params=_cparams("parallel", "parallel"),
        name="prep_qkv",
    )(pq, cos_t, sin_t, q_g.reshape(HEAD_DIM, 1), k_g.reshape(HEAD_DIM, 1))


def _rope_tables(t):
    pos = np.arange(t)
    half = HEAD_DIM // 4
    inv = ROPE_THETA ** (-jnp.arange(half, dtype=F32) * 2.0 / (HEAD_DIM // 2))
    dd = np.arange(HEAD_DIM)
    part_pos = np.where((dd // (HEAD_DIM // 2))[:, None] == 0, (pos // GRID_W)[None, :], (pos % GRID_W)[None, :])
    ang = jnp.asarray(part_pos, F32) * inv[dd % half][:, None]
    sign = jnp.asarray(np.where((dd % (HEAD_DIM // 2)) < half, -1.0, 1.0)[:, None], F32)
    return jnp.cos(ang), jnp.sin(ang) * sign


def _attn_kernel(qt_ref, k_ref, vt_ref, o_ref, acc_ref, *, nk, tq):
    qt = qt_ref[...]
    ncol = 4 * tq
    acc_ref[...] = jnp.zeros_like(acc_ref)

    def scores(c):
        return jnp.dot(k_ref[c], qt, preferred_element_type=F32)

    s = scores(0)
    m_prev = jnp.full((1, ncol), -jnp.inf, F32)
    m = jnp.max(s, axis=0, keepdims=True)
    for c in range(nk):
        s_next = scores(c + 1) if c + 1 < nk else None
        alpha = jnp.exp2(m_prev - m)
        p = jnp.exp2((s - m).astype(BF16))
        pv = jnp.dot(vt_ref[c], p, preferred_element_type=F32)
        acc_ref[...] = acc_ref[...] * alpha + pv
        if s_next is not None:
            m_prev = m
            m = jnp.maximum(m, jnp.max(s_next, axis=0, keepdims=True))
            s = s_next
    o = acc_ref[0:HEAD_DIM, :] * (1.0 / acc_ref[HEAD_DIM:HEAD_DIM + 1, :])
    for pp in range(2):
        blk = jnp.concatenate([o[:, (2 * pp) * tq:(2 * pp + 1) * tq],
                               o[:, (2 * pp + 1) * tq:(2 * pp + 2) * tq]], axis=0)
        o_ref[:, LANES * pp:LANES * (pp + 1)] = blk.T.astype(BF16)


def _gqa_attention(qt, k, vt, tx):
    bx, nq = qt.shape[0], qt.shape[1]
    nk = k.shape[1]
    tq = ATT_TQ
    return pl.pallas_call(
        functools.partial(_attn_kernel, nk=nk, tq=tq),
        grid=(bx, 2, nq),
        in_specs=[pl.BlockSpec((None, None, None, LANES, 4 * tq), lambda b, j, t: (b, t, j, 0, 0)),
                  pl.BlockSpec((None, nk, tq, LANES), lambda b, j, t: (b, 0, 0, 0)),
                  pl.BlockSpec((None, None, nk, VT_ROWS, tq), lambda b, j, t: (b, j, 0, 0, 0))],
        out_specs=pl.BlockSpec((tq, 2 * LANES), lambda b, j, t: (b * nq + t, j)),
        out_shape=jax.ShapeDtypeStruct((bx * tx, Q_COLS), BF16),
        scratch_shapes=[pltpu.VMEM((VT_ROWS, 4 * tq), F32)],
        compiler_params=_cparams("parallel", "parallel", "parallel"),
        name="gqa_attention",
    )(qt, k, vt)


def _outproj_ln_kernel(*refs, n_lhs):
    lhs = refs[:n_lhs]
    w_ref, x_ref, gate_ref, lng_ref, lnb_ref, o_ref = refs[n_lhs:]
    y = None
    off = 0
    for r in lhs:
        kk = r.shape[1]
        t = jnp.dot(r[...], w_ref[off:off + kk, :], preferred_element_type=F32)
        off += kk
        y = t if y is None else y + t
    z = ALPHA * x_ref[...] + gate_ref[0] * y
    o_ref[...] = _layer_norm(z, lng_ref[...], lnb_ref[...])


def _outproj_ln(lhs, w, x2d, vecs, gate_k, ln_g, ln_b, rows_per_batch, tm):
    r = x2d.shape[0]
    return pl.pallas_call(
        functools.partial(_outproj_ln_kernel, n_lhs=len(lhs)),
        grid=(r // tm,),
        in_specs=[pl.BlockSpec((tm, a.shape[1]), lambda i: (i, 0)) for a in lhs]
        + [_full_spec(w.shape), pl.BlockSpec((tm, D_MODEL), lambda i: (i, 0)),
           _vec_spec(gate_k, tm, rows_per_batch), _full_spec((1, D_MODEL)), _full_spec((1, D_MODEL))],
        out_specs=pl.BlockSpec((tm, D_MODEL), lambda i: (i, 0)),
        out_shape=jax.ShapeDtypeStruct((r, D_MODEL), F32),
        compiler_params=_cparams("parallel"),
        name="outproj_ln",
    )(*lhs, w, x2d, vecs, ln_g.reshape(1, D_MODEL), ln_b.reshape(1, D_MODEL))


def _ffn_kernel(x_ref, sc_ref, sh_ref, gate_ref, wg_ref, wu_ref, wd_ref, lng_ref, lnb_ref, o_ref, hb_ref, acc_ref):
    j = pl.program_id(1)

    @pl.when(j == 0)
    def _():
        hb_ref[...] = (x_ref[...] * (1.0 + sc_ref[0]) + sh_ref[0]).astype(BF16)
        acc_ref[...] = jnp.zeros_like(acc_ref)

    hb = hb_ref[...]
    hg = jnp.dot(hb, wg_ref[...], preferred_element_type=F32)
    hu = jnp.dot(hb, wu_ref[...], preferred_element_type=F32)
    h1 = (hg * jax.nn.sigmoid(hg) * hu).astype(BF16)
    acc_ref[...] += jnp.dot(h1, wd_ref[...], preferred_element_type=F32)

    @pl.when(j == pl.num_programs(1) - 1)
    def _():
        z = ALPHA * x_ref[...] + gate_ref[0] * acc_ref[...]
        o_ref[...] = _layer_norm(z, lng_ref[...], lnb_ref[...])


def _ffn_ln(x2d, vecs, wg, wu, wd, ln_g, ln_b, rows_per_batch, tm, tf):
    r = x2d.shape[0]
    f = wg.shape[1]
    return pl.pallas_call(
        _ffn_kernel,
        grid=(r // tm, f // tf),
        in_specs=[pl.BlockSpec((tm, D_MODEL), lambda i, j: (i, 0)),
                  _vec_spec(4, tm, rows_per_batch), _vec_spec(3, tm, rows_per_batch),
                  _vec_spec(5, tm, rows_per_batch),
                  pl.BlockSpec((D_MODEL, tf), lambda i, j: (0, j)),
                  pl.BlockSpec((D_MODEL, tf), lambda i, j: (0, j)),
                  pl.BlockSpec((tf, D_MODEL), lambda i, j: (j, 0)),
                  _full_spec((1, D_MODEL)), _full_spec((1, D_MODEL))],
        out_specs=pl.BlockSpec((tm, D_MODEL), lambda i, j: (i, 0)),
        out_shape=jax.ShapeDtypeStruct((r, D_MODEL), F32),
        scratch_shapes=[pltpu.VMEM((tm, D_MODEL), BF16), pltpu.VMEM((tm, D_MODEL), F32)],
        compiler_params=_cparams("parallel", "arbitrary"),
        name="ffn_ln",
    )(x2d, vecs, vecs, vecs, wg, wu, wd, ln_g.reshape(1, D_MODEL), ln_b.reshape(1, D_MODEL))


def _inproj1_kernel(x_ref, sc_ref, sh_ref, w_ref, o_ref, *, n_off):
    h = x_ref[...] * (1.0 + sc_ref[0]) + sh_ref[0]
    o = jnp.dot(h.astype(BF16), w_ref[...], preferred_element_type=F32)
    scale = jnp.where(pl.program_id(1) + n_off == 0, ATT_SCALE, 1.0)
    o_ref[...] = (o * scale).astype(BF16)


def _inproj1(x2d, vecs, w, rows_per_batch, tm, n_off):
    r = x2d.shape[0]
    nn = 3 - n_off
    return pl.pallas_call(
        functools.partial(_inproj1_kernel, n_off=n_off),
        grid=(r // tm, nn),
        in_specs=[pl.BlockSpec((tm, D_MODEL), lambda i, n: (i, 0)),
                  _vec_spec(1, tm, rows_per_batch), _vec_spec(0, tm, rows_per_batch),
                  pl.BlockSpec((D_MODEL, D_MODEL), lambda i, n: (0, n + n_off))],
        out_specs=pl.BlockSpec((None, tm, D_MODEL), lambda i, n: (n, i, 0)),
        out_shape=jax.ShapeDtypeStruct((nn, r, D_MODEL), BF16),
        compiler_params=_cparams("parallel", "parallel"),
        name="inproj1",
    )(x2d, vecs, vecs, w)


def _na_bias(rpb, n_rows):
    nh = rpb.shape[0]
    c = np.arange(GRID_W)
    cs = np.clip(c - WIN_C // 2, 0, GRID_W - WIN_C)
    in_c = (c[None, :] >= cs[:, None]) & (c[None, :] < cs[:, None] + WIN_C)
    dc = c[None, :] - c[:, None] + WIN_C - 1
    pick = ((dc[None] == np.arange(2 * WIN_C - 1)[:, None, None]) & in_c[None]).astype(np.float32)
    cols = jnp.einsum("hrd,dck->hrck", rpb, jnp.asarray(pick), precision=lax.Precision.HIGHEST)
    cols = jnp.where(jnp.asarray(in_c)[None, None], cols, MASK_VALUE)
    masked = jnp.full((nh, GRID_W, GRID_W), MASK_VALUE, F32)
    tables = []
    for r0 in (0, NA_ROWS, n_rows - NA_ROWS):
        start = int(np.clip(r0 - WIN_R // 2, 0, n_rows - NA_KROWS))
        per_q = []
        for ri in range(NA_ROWS):
            r = r0 + ri
            rs = int(np.clip(r - WIN_R // 2, 0, n_rows - WIN_R))
            per_k = []
            for ki in range(NA_KROWS):
                kr = start + ki
                per_k.append(cols[:, kr - r + WIN_R - 1] if rs <= kr < rs + WIN_R else masked)
            per_q.append(jnp.stack(per_k, axis=2))
        tables.append(jnp.stack(per_q, axis=1))
    bias = jnp.stack(tables, axis=0)
    return bias.reshape(3, nh // 2, 2, NA_ROWS * GRID_W, NA_KROWS * GRID_W)


def _na_kernel(q_ref, k0_ref, k1_ref, k2_ref, kc_ref, v0_ref, v1_ref, v2_ref, vc_ref, bias_ref, o_ref):
    q = q_ref[...]
    kk = jnp.concatenate([k0_ref[...], k1_ref[...], k2_ref[...], kc_ref[...]], axis=0)
    vv = jnp.concatenate([v0_ref[...], v1_ref[...], v2_ref[...], vc_ref[...]], axis=0)
    nctx = kc_ref.shape[0]
    lane = lax.broadcasted_iota(jnp.int32, (1, LANES), 1)
    out = jnp.zeros(q.shape, F32)
    for e in range(2):
        sel = (lane // HEAD_DIM == e).astype(BF16)
        s = lax.dot_general(q * sel, kk, (((1,), (1,)), ((), ())), preferred_element_type=F32)
        s = s + jnp.concatenate([bias_ref[e], jnp.zeros((q.shape[0], nctx), F32)], axis=1)
        m = jnp.max(s, axis=1, keepdims=True)
        p = jnp.exp(s - m)
        l = jnp.sum(p, axis=1, keepdims=True)
        out = out + jnp.dot(p.astype(BF16), vv * sel, preferred_element_type=F32) * (1.0 / l)
    o_ref[...] = out.astype(BF16)


def _neighbourhood_attention(qkv, kvc, bias, bsz, t):
    nq = NA_ROWS * GRID_W
    nrb = t // nq
    nhp = D_MODEL // LANES
    nctx = kvc.shape[1] // bsz
    nwin = NA_KROWS // NA_ROWS

    def kspec(which, d):
        return pl.BlockSpec((None, nq, LANES),
                            lambda hp, b, rb: (which, b * nrb + jnp.clip(rb - 1, 0, nrb - nwin) + d, hp))

    def cspec(which):
        return pl.BlockSpec((None, nctx, LANES), lambda hp, b, rb: (which, b, hp))

    def var(rb):
        return jnp.where(rb == 0, 0, jnp.where(rb == nrb - 1, 2, 1))

    return pl.pallas_call(
        _na_kernel,
        grid=(nhp, bsz, nrb),
        in_specs=[pl.BlockSpec((None, nq, LANES), lambda hp, b, rb: (0, b * nrb + rb, hp)),
                  kspec(1, 0), kspec(1, 1), kspec(1, 2), cspec(0),
                  kspec(2, 0), kspec(2, 1), kspec(2, 2), cspec(1),
                  pl.BlockSpec((None, None, 2, nq, NA_KROWS * GRID_W), lambda hp, b, rb: (var(rb), hp, 0, 0, 0))],
        out_specs=pl.BlockSpec((nq, LANES), lambda hp, b, rb: (b * nrb + rb, hp)),
        out_shape=jax.ShapeDtypeStruct((bsz * t, D_MODEL), BF16),
        compiler_params=_cparams("parallel", "parallel", "parallel"),
        name="neighbourhood_attention",
    )(qkv, qkv, qkv, qkv, kvc, qkv, qkv, qkv, kvc, bias)


def _store_row_tiles(dst_ref, val):
    rows = val.shape[0]
    for lt in range(ROW_TILES):
        dst_ref[pl.ds(lt, rows, stride=ROW_TILES), :] = val[:, lt * LANES:(lt + 1) * LANES]


def _load_row_tiles(src_ref, lt, rows):
    return src_ref[pl.ds(lt, rows, stride=ROW_TILES), :]


def _router_kernel(x_ref, sc_ref, sh_ref, wr_ref, h_ref, r_ref):
    h = x_ref[...] * (1.0 + sc_ref[0]) + sh_ref[0]
    _store_row_tiles(h_ref, h)
    hi = h.astype(BF16)
    lo = (h - hi.astype(F32)).astype(BF16)
    w = wr_ref[...]
    whi = w.astype(BF16)
    wlo = (w - whi.astype(F32)).astype(BF16)
    lg = (jnp.dot(hi, whi, preferred_element_type=F32)
          + (jnp.dot(hi, wlo, preferred_element_type=F32) + jnp.dot(lo, whi, preferred_element_type=F32)))
    lane = lax.broadcasted_iota(jnp.int32, lg.shape, 1).astype(F32)
    lg = jnp.where(lane < N_EXPERTS, lg, -jnp.inf)
    v1 = jnp.max(lg, axis=1, keepdims=True)
    i1 = jnp.min(jnp.where(lg == v1, lane, float(LANES)), axis=1, keepdims=True)
    lg2 = jnp.where(lane == i1, -jnp.inf, lg)
    v2 = jnp.max(lg2, axis=1, keepdims=True)
    i2 = jnp.min(jnp.where(lg2 == v2, lane, float(LANES)), axis=1, keepdims=True)
    e = jnp.exp(v2 - v1)
    w1 = 1.0 / (1.0 + e)
    w2 = e / (1.0 + e)
    r_ref[...] = jnp.where(lane == 0, i1, jnp.where(lane == 1, i2, jnp.where(lane == 2, w1,
                                                                              jnp.where(lane == 3, w2, 0.0))))


def _router(x2d, vecs, w_router, rows_per_batch, tm):
    r = x2d.shape[0]
    wr = jnp.pad(w_router, ((0, 0), (0, LANES - N_EXPERTS)))
    return pl.pallas_call(
        _router_kernel,
        grid=(r // tm,),
        in_specs=[pl.BlockSpec((tm, D_MODEL), lambda i: (i, 0)),
                  _vec_spec(4, tm, rows_per_batch), _vec_spec(3, tm, rows_per_batch),
                  _full_spec((D_MODEL, LANES))],
        out_specs=[pl.BlockSpec((tm * ROW_TILES, LANES), lambda i: (i, 0)),
                   pl.BlockSpec((tm, LANES), lambda i: (i, 0))],
        out_shape=[jax.ShapeDtypeStruct((r * ROW_TILES, LANES), F32), jax.ShapeDtypeStruct((r, LANES), F32)],
        compiler_params=_cparams("parallel"),
        name="router",
    )(x2d, vecs, vecs, wr)


def _row_copy(src_hbm, row, r, dst_ref, sem):
    def tile(i):
        start = i * ROW_TILES
        return pl.ds(start if isinstance(i, int) else pl.multiple_of(start, ROW_TILES), ROW_TILES)

    return pltpu.make_async_copy(src_hbm.at[tile(row), :], dst_ref.at[tile(r), :], sem)


def _start_row_gather(src_hbm, idx_ref, dst_ref, sem, n):
    def issue(r, carry):
        _row_copy(src_hbm, idx_ref[0, r], r, dst_ref, sem).start()
        return carry

    lax.fori_loop(0, n, issue, 0, unroll=8)


def _wait_row_gather(src_hbm, dst_ref, sem, n):
    def wait(r, carry):
        _row_copy(src_hbm, 0, r, dst_ref, sem).wait()
        return carry

    lax.fori_loop(0, n, wait, 0, unroll=8)


def _gather_kernel(idx_ref, src_hbm, o_ref, sem):
    n = o_ref.shape[0] // ROW_TILES
    _start_row_gather(src_hbm, idx_ref, o_ref, sem, n)
    _wait_row_gather(src_hbm, o_ref, sem, n)


def _gather_rows(src, idx):
    p = idx.shape[0]
    g = GATHER_ROWS
    return pl.pallas_call(
        _gather_kernel,
        grid=(p // g,),
        in_specs=[pl.BlockSpec((None, 1, g), lambda i: (i, 0, 0), memory_space=pltpu.SMEM),
                  pl.BlockSpec(memory_space=pl.ANY)],
        out_specs=pl.BlockSpec((g * ROW_TILES, LANES), lambda i: (i, 0)),
        out_shape=jax.ShapeDtypeStruct((p * ROW_TILES, LANES), src.dtype),
        scratch_shapes=[pltpu.SemaphoreType.DMA(())],
        compiler_params=_cparams("arbitrary"),
        name="gather_rows",
    )(idx.reshape(p // g, 1, g), src)


def _moe_ffn_kernel(te_ref, nu_ref, xs_ref, wg_ref, wu_ref, wd_ref, o_ref, hb_ref, acc_ref):
    t = pl.program_id(0)
    j = pl.program_id(1)
    nj = pl.num_programs(1)
    used = t < nu_ref[0]
    tm = hb_ref.shape[0]

    @pl.when(jnp.logical_and(used, j == 0))
    def _():
        for lt in range(ROW_TILES):
            hb_ref[:, lt * LANES:(lt + 1) * LANES] = _load_row_tiles(xs_ref, lt, tm).astype(BF16)
        acc_ref[...] = jnp.zeros_like(acc_ref)

    @pl.when(used)
    def _():
        hb = hb_ref[...]
        hg = jnp.dot(hb, wg_ref[...], preferred_element_type=F32)
        hu = jnp.dot(hb, wu_ref[...], preferred_element_type=F32)
        h1 = (hg * jax.nn.sigmoid(hg) * hu).astype(BF16)
        acc_ref[...] += jnp.dot(h1, wd_ref[...], preferred_element_type=F32)

    @pl.when(jnp.logical_and(used, j == nj - 1))
    def _():
        _store_row_tiles(o_ref, acc_ref[...])

    @pl.when(jnp.logical_and(jnp.logical_not(used), j == nj - 1))
    def _():
        o_ref[...] = jnp.zeros_like(o_ref)


def _moe_ffn(xs, tile_expert, n_used, wg, wu, wd):
    p = xs.shape[0] // ROW_TILES
    tm, tf = MOE_TM, MOE_TF
    f = wg.shape[2]

    def jj(t, j, nu):
        return jnp.where(t < nu[0], j, 0)

    grid_spec = pltpu.PrefetchScalarGridSpec(
        num_scalar_prefetch=2,
        grid=(p // tm, f // tf),
        in_specs=[pl.BlockSpec((tm * ROW_TILES, LANES), lambda t, j, te, nu: (t, 0)),
                  pl.BlockSpec((None, D_MODEL, tf), lambda t, j, te, nu: (te[t], 0, jj(t, j, nu))),
                  pl.BlockSpec((None, D_MODEL, tf), lambda t, j, te, nu: (te[t], 0, jj(t, j, nu))),
                  pl.BlockSpec((None, tf, D_MODEL), lambda t, j, te, nu: (te[t], jj(t, j, nu), 0))],
        out_specs=pl.BlockSpec((tm * ROW_TILES, LANES), lambda t, j, te, nu: (t, 0)),
        scratch_shapes=[pltpu.VMEM((tm, D_MODEL), BF16), pltpu.VMEM((tm, D_MODEL), F32)],
    )
    return pl.pallas_call(
        _moe_ffn_kernel,
        grid_spec=grid_spec,
        out_shape=jax.ShapeDtypeStruct((p * ROW_TILES, LANES), F32),
        compiler_params=_cparams("arbitrary", "arbitrary"),
        name="moe_ffn",
    )(tile_expert, n_used, xs, wg, wu, wd)


def _combine_ln_kernel(p1_ref, p2_ref, ys_hbm, route_ref, x_ref, gate_ref, lng_ref, lnb_ref, o_ref,
                       y1_ref, y2_ref, sem):
    n = x_ref.shape[0]
    _start_row_gather(ys_hbm, p1_ref, y1_ref, sem.at[0], n)
    _start_row_gather(ys_hbm, p2_ref, y2_ref, sem.at[1], n)
    _wait_row_gather(ys_hbm, y1_ref, sem.at[0], n)
    _wait_row_gather(ys_hbm, y2_ref, sem.at[1], n)
    w1 = route_ref[:, 2:3]
    w2 = route_ref[:, 3:4]
    y = jnp.concatenate([w1 * _load_row_tiles(y1_ref, lt, n) + w2 * _load_row_tiles(y2_ref, lt, n)
                         for lt in range(ROW_TILES)], axis=1)
    z = ALPHA * x_ref[...] + gate_ref[0] * y
    o_ref[...] = _layer_norm(z, lng_ref[...], lnb_ref[...])


def _combine_ln(ys, pos1, pos2, route, x2d, vecs, ln_g, ln_b, rows_per_batch):
    r = x2d.shape[0]
    g = GATHER_ROWS
    idx_spec = pl.BlockSpec((None, 1, g), lambda i: (i, 0, 0), memory_space=pltpu.SMEM)
    return pl.pallas_call(
        _combine_ln_kernel,
        grid=(r // g,),
        in_specs=[idx_spec, idx_spec, pl.BlockSpec(memory_space=pl.ANY),
                  pl.BlockSpec((g, LANES), lambda i: (i, 0)),
                  pl.BlockSpec((g, D_MODEL), lambda i: (i, 0)),
                  _vec_spec(5, g, rows_per_batch), _full_spec((1, D_MODEL)), _full_spec((1, D_MODEL))],
        out_specs=pl.BlockSpec((g, D_MODEL), lambda i: (i, 0)),
        out_shape=jax.ShapeDtypeStruct((r, D_MODEL), F32),
        scratch_shapes=[pltpu.VMEM((g * ROW_TILES, LANES), F32), pltpu.VMEM((g * ROW_TILES, LANES), F32),
                        pltpu.SemaphoreType.DMA((2,))],
        compiler_params=_cparams("arbitrary"),
        name="combine_ln",
    )(pos1.reshape(r // g, 1, g), pos2.reshape(r // g, 1, g), ys, route, x2d, vecs,
      ln_g.reshape(1, D_MODEL), ln_b.reshape(1, D_MODEL))


def _routing_plan(route, tm):
    n = route.shape[0]
    e = jnp.concatenate([route[:, 0], route[:, 1]]).astype(jnp.int32)
    tok = jnp.concatenate([jnp.arange(n, dtype=jnp.int32)] * 2)
    onehot = (e[:, None] == jnp.arange(N_EXPERTS, dtype=jnp.int32)[None, :]).astype(jnp.int32)
    csum = jnp.cumsum(onehot, axis=0)
    rank = jnp.sum(csum * onehot, axis=1) - 1
    counts = csum[-1]
    padded = ((counts + tm - 1) // tm) * tm
    ends = jnp.cumsum(padded)
    starts = ends - padded
    pos = jnp.sum(starts[None, :] * onehot, axis=1) + rank
    p = 2 * n + N_EXPERTS * tm
    src = jnp.zeros((p,), jnp.int32).at[pos].set(tok)
    tile_start = jnp.arange(p // tm, dtype=jnp.int32) * tm
    tile_expert = jnp.minimum(jnp.sum((tile_start[:, None] >= ends[None, :]).astype(jnp.int32), axis=1),
                              N_EXPERTS - 1).astype(jnp.int32)
    n_used = (ends[-1] // tm).astype(jnp.int32).reshape(1)
    return src, tile_expert, n_used, pos[:n], pos[n:]


def kernel(x, c, ctx, c_ctx, w_mod, b_mod, ln_g, ln_b, ab_w_in, ab_conv_w, ab_conv_g, ab_conv_b, ab_q_g, ab_k_g,
           ab_w_out, ffn_w_gate, ffn_w_up, ffn_w_down, na_w_qkv, na_rpb, na_w_out, moe_w_router, moe_w_gate,
           moe_w_up, moe_w_down):
    bsz, t, d = x.shape
    n_ctx = ctx.shape[1]
    n = bsz * t
    nc = bsz * n_ctx
    x2 = x.reshape(n, d)
    c2 = ctx.reshape(nc, d)

    cc = jnp.concatenate([c, c_ctx[None, :], jnp.zeros((8 - bsz - 1, d), F32)], axis=0)
    mod = _modulation(cc, w_mod, b_mod)
    vec0 = mod[0].reshape(8 * 6, 1, d)
    vec1 = mod[1].reshape(8 * 6, 1, d)

    w_in = ab_w_in[0].astype(BF16)
    w_out = ab_w_out[0].astype(BF16)
    pa, pq = _inproj0(x2, vec0, w_in, t, 512)
    pac, pqc = _inproj0(c2, vec0, w_in, None, 512)
    a = _conformer_conv(pa, ab_conv_w[0], ab_conv_g[0], ab_conv_b[0], t, 512)
    ac = _conformer_conv(pac, ab_conv_w[0], ab_conv_g[0], ab_conv_b[0], n_ctx, n_ctx)
    cos_t, sin_t = _rope_tables(t)
    qt, k, vt = _prep_qkv(pq, cos_t, sin_t, ab_q_g[0], ab_k_g[0], bsz, t)
    ones = jnp.ones((HEAD_DIM, n_ctx), F32)
    qtc, kc, vtc = _prep_qkv(pqc, ones, jnp.zeros_like(ones), ab_q_g[0], ab_k_g[0], bsz, n_ctx)
    o = _gqa_attention(qt, jnp.concatenate([kc, k], axis=1), jnp.concatenate([vtc, vt], axis=2), t)
    oc = _gqa_attention(qtc, kc, vtc, n_ctx)
    x2 = _outproj_ln([a, o], w_out, x2, vec0, 2, ln_g[0, 0], ln_b[0, 0], t, 512)
    c2 = _outproj_ln([ac, oc], w_out, c2, vec0, 2, ln_g[0, 0], ln_b[0, 0], None, 512)
    wg = ffn_w_gate[0].astype(BF16)
    wu = ffn_w_up[0].astype(BF16)
    wd = ffn_w_down[0].astype(BF16)
    x2 = _ffn_ln(x2, vec0, wg, wu, wd, ln_g[0, 1], ln_b[0, 1], t, 512, 1408)
    c2 = _ffn_ln(c2, vec0, wg, wu, wd, ln_g[0, 1], ln_b[0, 1], None, 512, 1408)

    w_qkv = na_w_qkv[0].astype(BF16)
    qkv = _inproj1(x2, vec1, w_qkv, t, 512, 0)
    kvc = _inproj1(c2, vec1, w_qkv, None, 512, 1)
    o = _neighbourhood_attention(qkv, kvc, _na_bias(na_rpb[0], t // GRID_W), bsz, t)
    x2 = _outproj_ln([o], na_w_out[0].astype(BF16), x2, vec1, 2, ln_g[1, 0], ln_b[1, 0], t, 512)

    h, route = _router(x2, vec1, moe_w_router[0], t, 512)
    src, tile_expert, n_used, pos1, pos2 = _routing_plan(route, MOE_TM)
    xs = _gather_rows(h, src)
    ys = _moe_ffn(xs, tile_expert, n_used, moe_w_gate[0].astype(BF16), moe_w_up[0].astype(BF16),
                  moe_w_down[0].astype(BF16))
    x2 = _combine_ln(ys, pos1, pos2, route, x2, vec1, ln_g[1, 1], ln_b[1, 1], t)
    return x2.reshape(bsz, t, d)
```

```python
import functools

import numpy as np
import jax
import jax.numpy as jnp
from jax import lax
from jax.experimental import pallas as pl
from jax.experimental.pallas import tpu as pltpu

F32 = jnp.float32
BF16 = jnp.bfloat16

D_MODEL = 1024
GRID_W = 64
HEAD_DIM = 64
CONV_CH = 512
CONV_WIDTH = 31
CONV_HALO = 16
Q_COLS = 512
KV_COLS = 128
A_COLS = 2 * CONV_CH
ROPE_THETA = 10000.0
WIN_R = 8
WIN_C = 16
N_EXPERTS = 8
DEPTH = 2
ALPHA = (2 * DEPTH) ** 0.25
LN_EPS = 1e-5
RMS_EPS = 1e-6
ATT_SCALE = HEAD_DIM ** -0.5
LOG2E = 1.4426950408889634
MASK_VALUE = -1e30

LANES = 128
ROW_TILES = D_MODEL // LANES
VMEM_LIMIT = 56 * 1024 * 1024

ATT_TQ = 256
VT_ROWS = HEAD_DIM + 16
NA_ROWS = 4
NA_KROWS = 12
NA_PAIRS = 4
MOE_TM = 512
MOE_TF = 1792
GATHER_ROWS = 256


def _cparams(*sem):
    return pltpu.CompilerParams(dimension_semantics=sem, vmem_limit_bytes=VMEM_LIMIT)


def _layer_norm(z, g, b):
    mu = jnp.mean(z, axis=-1, keepdims=True)
    zc = z - mu
    var = jnp.mean(zc * zc, axis=-1, keepdims=True)
    return zc * lax.rsqrt(var + LN_EPS) * g + b


def _vec_spec(k, tm, rows_per_batch):
    if rows_per_batch is None:
        return pl.BlockSpec((1, 1, D_MODEL), lambda i, *_: (4 * 6 + k, 0, 0))
    return pl.BlockSpec((1, 1, D_MODEL), lambda i, *_: ((i * tm // rows_per_batch) * 6 + k, 0, 0))


def _full_spec(shape):
    nd = len(shape)
    return pl.BlockSpec(shape, lambda *_: (0,) * nd)


def _mod_kernel(c_ref, w_ref, b_ref, o_ref):
    c = c_ref[...]
    s = c * jax.nn.sigmoid(c)
    o_ref[...] = jnp.dot(s.astype(BF16), w_ref[...].astype(BF16), preferred_element_type=F32) + b_ref[...]


def _modulation(cc, w_mod, b_mod):
    n = 6 * D_MODEL
    tn = D_MODEL
    return pl.pallas_call(
        _mod_kernel,
        grid=(DEPTH, n // tn),
        in_specs=[pl.BlockSpec((8, D_MODEL), lambda l, j: (0, 0)),
                  pl.BlockSpec((None, D_MODEL, tn), lambda l, j: (l, 0, j)),
                  pl.BlockSpec((None, 1, tn), lambda l, j: (l, 0, j))],
        out_specs=pl.BlockSpec((None, 8, tn), lambda l, j: (l, 0, j)),
        out_shape=jax.ShapeDtypeStruct((DEPTH, 8, n), F32),
        compiler_params=_cparams("parallel", "parallel"),
        name="modulation",
    )(cc, w_mod, b_mod.reshape(DEPTH, 1, n))


def _inproj0_kernel(x_ref, sc_ref, sh_ref, w_ref, oa_ref, oq_ref):
    h = x_ref[...] * (1.0 + sc_ref[0]) + sh_ref[0]
    o = jnp.dot(h.astype(BF16), w_ref[...], preferred_element_type=F32)
    oa_ref[...] = o[:, :A_COLS]
    oq_ref[...] = o[:, A_COLS:]


def _inproj0(x2d, vecs, w, rows_per_batch, tm):
    r = x2d.shape[0]
    nq = w.shape[1] - A_COLS
    return pl.pallas_call(
        _inproj0_kernel,
        grid=(r // tm,),
        in_specs=[pl.BlockSpec((tm, D_MODEL), lambda i: (i, 0)),
                  _vec_spec(1, tm, rows_per_batch), _vec_spec(0, tm, rows_per_batch),
                  _full_spec(w.shape)],
        out_specs=[pl.BlockSpec((tm, A_COLS), lambda i: (i, 0)),
                   pl.BlockSpec((tm, nq), lambda i: (i, 0))],
        out_shape=[jax.ShapeDtypeStruct((r, A_COLS), F32), jax.ShapeDtypeStruct((r, nq), F32)],
        compiler_params=_cparams("parallel"),
        name="inproj0",
    )(x2d, vecs, vecs, w)


def _conv_kernel(pm_ref, pp_ref, pn_ref, w_ref, g_ref, b_ref, o_ref, u_ref, *, tt, tx):
    i = pl.program_id(0)

    def glu(p):
        return p[:, :CONV_CH] * jax.nn.sigmoid(p[:, CONV_CH:])

    first = (i * tt) % tx == 0
    last = ((i + 1) * tt) % tx == 0
    u_ref[0:CONV_HALO, :] = jnp.where(first, 0.0, glu(pp_ref[...]))
    u_ref[CONV_HALO:CONV_HALO + tt, :] = glu(pm_ref[...])
    u_ref[CONV_HALO + tt:2 * CONV_HALO + tt, :] = jnp.where(last, 0.0, glu(pn_ref[...]))
    ch = 32
    win = ch + 2 * CONV_HALO
    base = CONV_HALO - CONV_WIDTH // 2

    def body(c, carry):
        r0 = pl.multiple_of(c * ch, ch)
        w = u_ref[pl.ds(r0, win), :]
        acc = jnp.zeros((ch, CONV_CH), F32)
        for s in range(8):
            ws = w if s == 0 else pltpu.roll(w, win - s, axis=0)
            for a in range(win // 8):
                k = 8 * a + s - base
                if 0 <= k < CONV_WIDTH and 8 * a + ch <= win - s:
                    acc = acc + ws[8 * a:8 * a + ch] * w_ref[pl.ds(k, 1), :]
        y = _layer_norm(acc, g_ref[...], b_ref[...])
        y = y * jax.nn.sigmoid(y)
        o_ref[pl.ds(r0, ch), :] = y.astype(BF16)
        return carry

    lax.fori_loop(0, tt // ch, body, 0)


def _conformer_conv(pa, conv_w, conv_g, conv_b, tx, tt):
    r = pa.shape[0]
    hb = tt // CONV_HALO
    nhb = r // CONV_HALO
    return pl.pallas_call(
        functools.partial(_conv_kernel, tt=tt, tx=tx),
        grid=(r // tt,),
        in_specs=[pl.BlockSpec((tt, A_COLS), lambda i: (i, 0)),
                  pl.BlockSpec((CONV_HALO, A_COLS), lambda i: (jnp.maximum(i * hb - 1, 0), 0)),
                  pl.BlockSpec((CONV_HALO, A_COLS), lambda i: (jnp.minimum((i + 1) * hb, nhb - 1), 0)),
                  _full_spec((CONV_WIDTH, CONV_CH)), _full_spec((1, CONV_CH)), _full_spec((1, CONV_CH))],
        out_specs=pl.BlockSpec((tt, CONV_CH), lambda i: (i, 0)),
        out_shape=jax.ShapeDtypeStruct((r, CONV_CH), BF16),
        scratch_shapes=[pltpu.VMEM((tt + 2 * CONV_HALO, CONV_CH), F32)],
        compiler_params=_cparams("parallel"),
        name="conformer_conv",
    )(pa, pa, pa, conv_w, conv_g.reshape(1, CONV_CH), conv_b.reshape(1, CONV_CH))


def _prep_kernel(p_ref, cos_ref, sin_ref, qg_ref, kg_ref, qt_ref, k_ref, vt_ref, *, tq):
    x = p_ref[...]
    cos = cos_ref[...]
    sin = sin_ref[...]

    def norm_rope(xh, g):
        ms = jnp.mean(xh * xh, axis=0, keepdims=True)
        y = xh * lax.rsqrt(ms + RMS_EPS) * g
        sw = jnp.concatenate([y[16:32], y[0:16], y[48:64], y[32:48]], axis=0)
        return y * cos + sw * sin

    zeros = jnp.zeros((HEAD_DIM, tq), BF16)
    for p in range(Q_COLS // LANES):
        xp = x[:, LANES * p:LANES * (p + 1)].T
        for half in range(2):
            h = 2 * p + half
            j, g = h // 4, h % 4
            r = (norm_rope(xp[HEAD_DIM * half:HEAD_DIM * (half + 1)], qg_ref[...]) * (ATT_SCALE * LOG2E)).astype(BF16)
            qt_ref[j, HEAD_DIM * j:HEAD_DIM * (j + 1), g * tq:(g + 1) * tq] = r
            qt_ref[j, HEAD_DIM * (1 - j):HEAD_DIM * (2 - j), g * tq:(g + 1) * tq] = zeros
    xk = x[:, Q_COLS:Q_COLS + KV_COLS].T
    k0 = norm_rope(xk[0:HEAD_DIM], kg_ref[...])
    k1 = norm_rope(xk[HEAD_DIM:2 * HEAD_DIM], kg_ref[...])
    k_ref[...] = jnp.concatenate([k0, k1], axis=0).T.astype(BF16)
    xv = x[:, Q_COLS + KV_COLS:].T.astype(BF16)
    ones = jnp.ones((VT_ROWS - HEAD_DIM, tq), BF16)
    for j in range(2):
        vt_ref[j, 0:HEAD_DIM, :] = xv[HEAD_DIM * j:HEAD_DIM * (j + 1)]
        vt_ref[j, HEAD_DIM:VT_ROWS, :] = ones


def _prep_qkv(pq, cos_t, sin_t, q_g, k_g, bx, tx):
    tq = ATT_TQ
    nq = tx // tq
    return pl.pallas_call(
        functools.partial(_prep_kernel, tq=tq),
        grid=(bx, nq),
        in_specs=[pl.BlockSpec((tq, Q_COLS + 2 * KV_COLS), lambda b, t: (b * nq + t, 0)),
                  pl.BlockSpec((HEAD_DIM, tq), lambda b, t: (0, t)),
                  pl.BlockSpec((HEAD_DIM, tq), lambda b, t: (0, t)),
                  _full_spec((HEAD_DIM, 1)), _full_spec((HEAD_DIM, 1))],
        out_specs=[pl.BlockSpec((None, None, 2, LANES, 4 * tq), lambda b, t: (b, t, 0, 0, 0)),
                   pl.BlockSpec((None, None, tq, LANES), lambda b, t: (b, t, 0, 0)),
                   pl.BlockSpec((None, 2, None, VT_ROWS, tq), lambda b, t: (b, 0, t, 0, 0))],
        out_shape=[jax.ShapeDtypeStruct((bx, nq, 2, LANES, 4 * tq), BF16),
                   jax.ShapeDtypeStruct((bx, nq, tq, LANES), BF16),
                   jax.ShapeDtypeStruct((bx, 2, nq, VT_ROWS, tq), BF16)],
        compiler_params=_cparams("parallel", "parallel"),
        name="prep_qkv",
    )(pq, cos_t, sin_t, q_g.reshape(HEAD_DIM, 1), k_g.reshape(HEAD_DIM, 1))


def _rope_tables(t):
    pos = np.arange(t)
    half = HEAD_DIM // 4
    inv = ROPE_THETA ** (-jnp.arange(half, dtype=F32) * 2.0 / (HEAD_DIM // 2))
    dd = np.arange(HEAD_DIM)
    part_pos = np.where((dd // (HEAD_DIM // 2))[:, None] == 0, (pos // GRID_W)[None, :], (pos % GRID_W)[None, :])
    ang = jnp.asarray(part_pos, F32) * inv[dd % half][:, None]
    sign = jnp.asarray(np.where((dd % (HEAD_DIM // 2)) < half, -1.0, 1.0)[:, None], F32)
    return jnp.cos(ang), jnp.sin(ang) * sign


def _attn_kernel(qt_ref, k_ref, vt_ref, o_ref, acc_ref, *, nk, tq):
    qt = qt_ref[...]
    ncol = 4 * tq
    acc_ref[...] = jnp.zeros_like(acc_ref)

    def scores(c):
        return jnp.dot(k_ref[c], qt, preferred_element_type=F32)

    s = scores(0)
    m_prev = jnp.full((1, ncol), -jnp.inf, F32)
    m = jnp.max(s, axis=0, keepdims=True)
    for c in range(nk):
        s_next = scores(c + 1) if c + 1 < nk else None
        alpha = jnp.exp2(m_prev - m)
        p = jnp.exp2((s - m).astype(BF16))
        pv = jnp.dot(vt_ref[c], p, preferred_element_type=F32)
        acc_ref[...] = acc_ref[...] * alpha + pv
        if s_next is not None:
            m_prev = m
            m = jnp.maximum(m, jnp.max(s_next, axis=0, keepdims=True))
            s = s_next
    o = acc_ref[0:HEAD_DIM, :] * (1.0 / acc_ref[HEAD_DIM:HEAD_DIM + 1, :])
    for pp in range(2):
        blk = jnp.concatenate([o[:, (2 * pp) * tq:(2 * pp + 1) * tq],
                               o[:, (2 * pp + 1) * tq:(2 * pp + 2) * tq]], axis=0)
        o_ref[:, LANES * pp:LANES * (pp + 1)] = blk.T.astype(BF16)


def _gqa_attention(qt, k, vt, tx):
    bx, nq = qt.shape[0], qt.shape[1]
    nk = k.shape[1]
    tq = ATT_TQ
    return pl.pallas_call(
        functools.partial(_attn_kernel, nk=nk, tq=tq),
        grid=(bx, 2, nq),
        in_specs=[pl.BlockSpec((None, None, None, LANES, 4 * tq), lambda b, j, t: (b, t, j, 0, 0)),
                  pl.BlockSpec((None, nk, tq, LANES), lambda b, j, t: (b, 0, 0, 0)),
                  pl.BlockSpec((None, None, nk, VT_ROWS, tq), lambda b, j, t: (b, j, 0, 0, 0))],
        out_specs=pl.BlockSpec((tq, 2 * LANES), lambda b, j, t: (b * nq + t, j)),
        out_shape=jax.ShapeDtypeStruct((bx * tx, Q_COLS), BF16),
        scratch_shapes=[pltpu.VMEM((VT_ROWS, 4 * tq), F32)],
        compiler_params=_cparams("parallel", "parallel", "parallel"),
        name="gqa_attention",
    )(qt, k, vt)


def _outproj_ln_kernel(*refs, n_lhs):
    lhs = refs[:n_lhs]
    w_ref, x_ref, gate_ref, lng_ref, lnb_ref, o_ref = refs[n_lhs:]
    y = None
    off = 0
    for r in lhs:
        kk = r.shape[1]
        t = jnp.dot(r[...], w_ref[off:off + kk, :], preferred_element_type=F32)
        off += kk
        y = t if y is None else y + t
    z = ALPHA * x_ref[...] + gate_ref[0] * y
    o_ref[...] = _layer_norm(z, lng_ref[...], lnb_ref[...])


def _outproj_ln(lhs, w, x2d, vecs, gate_k, ln_g, ln_b, rows_per_batch, tm):
    r = x2d.shape[0]
    return pl.pallas_call(
        functools.partial(_outproj_ln_kernel, n_lhs=len(lhs)),
        grid=(r // tm,),
        in_specs=[pl.BlockSpec((tm, a.shape[1]), lambda i: (i, 0)) for a in lhs]
        + [_full_spec(w.shape), pl.BlockSpec((tm, D_MODEL), lambda i: (i, 0)),
           _vec_spec(gate_k, tm, rows_per_batch), _full_spec((1, D_MODEL)), _full_spec((1, D_MODEL))],
        out_specs=pl.BlockSpec((tm, D_MODEL), lambda i: (i, 0)),
        out_shape=jax.ShapeDtypeStruct((r, D_MODEL), F32),
        compiler_params=_cparams("parallel"),
        name="outproj_ln",
    )(*lhs, w, x2d, vecs, ln_g.reshape(1, D_MODEL), ln_b.reshape(1, D_MODEL))


def _ffn_kernel(x_ref, sc_ref, sh_ref, gate_ref, wg_ref, wu_ref, wd_ref, lng_ref, lnb_ref, o_ref, hb_ref, acc_ref):
    j = pl.program_id(1)

    @pl.when(j == 0)
    def _():
        hb_ref[...] = (x_ref[...] * (1.0 + sc_ref[0]) + sh_ref[0]).astype(BF16)
        acc_ref[...] = jnp.zeros_like(acc_ref)

    hb = hb_ref[...]
    hg = jnp.dot(hb, wg_ref[...], preferred_element_type=F32)
    hu = jnp.dot(hb, wu_ref[...], preferred_element_type=F32)
    h1 = (hg * jax.nn.sigmoid(hg) * hu).astype(BF16)
    acc_ref[...] += jnp.dot(h1, wd_ref[...], preferred_element_type=F32)

    @pl.when(j == pl.num_programs(1) - 1)
    def _():
        z = ALPHA * x_ref[...] + gate_ref[0] * acc_ref[...]
        o_ref[...] = _layer_norm(z, lng_ref[...], lnb_ref[...])


def _ffn_ln(x2d, vecs, wg, wu, wd, ln_g, ln_b, rows_per_batch, tm, tf):
    r = x2d.shape[0]
    f = wg.shape[1]
    return pl.pallas_call(
        _ffn_kernel,
        grid=(r // tm, f // tf),
        in_specs=[pl.BlockSpec((tm, D_MODEL), lambda i, j: (i, 0)),
                  _vec_spec(4, tm, rows_per_batch), _vec_spec(3, tm, rows_per_batch),
                  _vec_spec(5, tm, rows_per_batch),
                  pl.BlockSpec((D_MODEL, tf), lambda i, j: (0, j)),
                  pl.BlockSpec((D_MODEL, tf), lambda i, j: (0, j)),
                  pl.BlockSpec((tf, D_MODEL), lambda i, j: (j, 0)),
                  _full_spec((1, D_MODEL)), _full_spec((1, D_MODEL))],
        out_specs=pl.BlockSpec((tm, D_MODEL), lambda i, j: (i, 0)),
        out_shape=jax.ShapeDtypeStruct((r, D_MODEL), F32),
        scratch_shapes=[pltpu.VMEM((tm, D_MODEL), BF16), pltpu.VMEM((tm, D_MODEL), F32)],
        compiler_params=_cparams("parallel", "arbitrary"),
        name="ffn_ln",
    )(x2d, vecs, vecs, vecs, wg, wu, wd, ln_g.reshape(1, D_MODEL), ln_b.reshape(1, D_MODEL))


def _inproj1_kernel(x_ref, sc_ref, sh_ref, w_ref, *o_refs, with_q):
    h = x_ref[...] * (1.0 + sc_ref[0]) + sh_ref[0]
    o = jnp.dot(h.astype(BF16), w_ref[...], preferred_element_type=F32)
    if with_q:
        qt_ref, k_ref, vt_ref = o_refs
        qt_ref[...] = (o[:, :D_MODEL] * (ATT_SCALE * LOG2E)).T.astype(BF16)
    else:
        k_ref, vt_ref = o_refs
    nk = o.shape[1] - 2 * D_MODEL
    k_ref[...] = o[:, nk:nk + D_MODEL].astype(BF16)
    vt_ref[...] = o[:, nk + D_MODEL:].T.astype(BF16)


def _inproj1(x2d, vecs, w, rows_per_batch, tm, with_q):
    r = x2d.shape[0]
    nat = pl.BlockSpec((tm, D_MODEL), lambda i: (i, 0))
    tr = pl.BlockSpec((D_MODEL, tm), lambda i: (0, i))
    nat_shape = jax.ShapeDtypeStruct((r, D_MODEL), BF16)
    tr_shape = jax.ShapeDtypeStruct((D_MODEL, r), BF16)
    return pl.pallas_call(
        functools.partial(_inproj1_kernel, with_q=with_q),
        grid=(r // tm,),
        in_specs=[pl.BlockSpec((tm, D_MODEL), lambda i: (i, 0)),
                  _vec_spec(1, tm, rows_per_batch), _vec_spec(0, tm, rows_per_batch),
                  _full_spec(w.shape)],
        out_specs=([tr] if with_q else []) + [nat, tr],
        out_shape=([tr_shape] if with_q else []) + [nat_shape, tr_shape],
        compiler_params=_cparams("parallel"),
        name="inproj1",
    )(x2d, vecs, vecs, w)


def _na_bias(rpb, n_rows, n_ctx):
    nh = rpb.shape[0]
    c = np.arange(GRID_W)
    cs = np.clip(c - WIN_C // 2, 0, GRID_W - WIN_C)
    in_c = (c[None, :] >= cs[:, None]) & (c[None, :] < cs[:, None] + WIN_C)
    dc = c[None, :] - c[:, None] + WIN_C - 1
    pick = ((dc[None] == np.arange(2 * WIN_C - 1)[:, None, None]) & in_c[None]).astype(np.float32)
    cols = jnp.einsum("hrd,dck->hrck", rpb, jnp.asarray(pick), precision=lax.Precision.HIGHEST)
    cols = jnp.where(jnp.asarray(in_c)[None, None], cols * LOG2E, MASK_VALUE)
    n_dr = 2 * WIN_R - 1
    cols = jnp.concatenate([cols, jnp.full((nh, 1, GRID_W, GRID_W), MASK_VALUE, F32)], axis=1)
    pick_dr = np.full((3, NA_ROWS, NA_KROWS), n_dr, np.int32)
    for case, r0 in enumerate((0, NA_ROWS, n_rows - NA_ROWS)):
        start = int(np.clip(r0 - WIN_R // 2, 0, n_rows - NA_KROWS))
        for ri in range(NA_ROWS):
            r = r0 + ri
            rs = int(np.clip(r - WIN_R // 2, 0, n_rows - WIN_R))
            for ki in range(NA_KROWS):
                kr = start + ki
                if rs <= kr < rs + WIN_R:
                    pick_dr[case, ri, ki] = kr - r + WIN_R - 1
    bias = jnp.take(cols, jnp.asarray(pick_dr.reshape(-1)), axis=1)
    bias = bias.reshape(nh // (2 * NA_PAIRS), NA_PAIRS, 2, 3, NA_ROWS, NA_KROWS, GRID_W, GRID_W)
    bias = bias.transpose(3, 0, 1, 5, 7, 2, 4, 6)
    bias = bias.reshape(3, nh // (2 * NA_PAIRS), NA_PAIRS, NA_KROWS * GRID_W, 2 * NA_ROWS * GRID_W)
    return jnp.pad(bias, ((0, 0), (0, 0), (0, 0), (0, n_ctx), (0, 0)))


def _na_kernel(qt_ref, k0_ref, k1_ref, k2_ref, kc_ref, v0_ref, v1_ref, v2_ref, vc_ref, bias_ref, o_ref):
    qt = qt_ref[...]
    kk = jnp.concatenate([k0_ref[...], k1_ref[...], k2_ref[...], kc_ref[...]], axis=0)
    vt = jnp.concatenate([v0_ref[...], v1_ref[...], v2_ref[...], vc_ref[...]], axis=1)
    nq = qt.shape[1]
    zeros = jnp.zeros((HEAD_DIM, nq), BF16)
    ones = jnp.ones((VT_ROWS - HEAD_DIM, kk.shape[0]), BF16)

    def scores(pp):
        q2 = qt[LANES * pp:LANES * (pp + 1)]
        qcat = jnp.concatenate([jnp.concatenate([q2[0:HEAD_DIM], zeros], axis=0),
                                jnp.concatenate([zeros, q2[HEAD_DIM:]], axis=0)], axis=1)
        return jnp.dot(kk[:, LANES * pp:LANES * (pp + 1)], qcat, preferred_element_type=F32) + bias_ref[pp]

    s_all = [scores(pp) for pp in range(NA_PAIRS)]
    for pp in range(NA_PAIRS):
        s = s_all[pp]
        m = jnp.max(s, axis=0, keepdims=True)
        p = jnp.exp2((s - m).astype(BF16))
        vext = jnp.concatenate([vt[LANES * pp:LANES * (pp + 1)], ones], axis=0)
        pv = jnp.dot(vext, p, preferred_element_type=F32)
        o0 = pv[0:HEAD_DIM, 0:nq] * (1.0 / pv[LANES:LANES + 1, 0:nq])
        o1 = pv[HEAD_DIM:LANES, nq:] * (1.0 / pv[LANES:LANES + 1, nq:])
        o_ref[:, LANES * pp:LANES * (pp + 1)] = jnp.concatenate([o0, o1], axis=0).T.astype(BF16)


def _neighbourhood_attention(qt, k, vt, kc, vct, bias, bsz, t):
    nq = NA_ROWS * GRID_W
    nrb = t // nq
    wl = NA_PAIRS * LANES
    ngrp = D_MODEL // wl
    nctx = kc.shape[0] // bsz
    nwin = NA_KROWS // NA_ROWS

    def first_kblock(b, rb):
        return b * nrb + jnp.clip(rb - 1, 0, nrb - nwin)

    def kspec(d):
        return pl.BlockSpec((nq, wl), lambda hp, b, rb: (first_kblock(b, rb) + d, hp))

    def vspec(d):
        return pl.BlockSpec((wl, nq), lambda hp, b, rb: (hp, first_kblock(b, rb) + d))

    def case(rb):
        return jnp.where(rb == 0, 0, jnp.where(rb == nrb - 1, 2, 1))

    return pl.pallas_call(
        _na_kernel,
        grid=(ngrp, bsz, nrb),
        in_specs=[pl.BlockSpec((wl, nq), lambda hp, b, rb: (hp, b * nrb + rb)),
                  kspec(0), kspec(1), kspec(2), pl.BlockSpec((nctx, wl), lambda hp, b, rb: (b, hp)),
                  vspec(0), vspec(1), vspec(2), pl.BlockSpec((wl, nctx), lambda hp, b, rb: (hp, b)),
                  pl.BlockSpec((None, None, NA_PAIRS, NA_KROWS * GRID_W + nctx, 2 * nq),
                               lambda hp, b, rb: (case(rb), hp, 0, 0, 0))],
        out_specs=pl.BlockSpec((nq, wl), lambda hp, b, rb: (b * nrb + rb, hp)),
        out_shape=jax.ShapeDtypeStruct((bsz * t, D_MODEL), BF16),
        compiler_params=_cparams("parallel", "parallel", "parallel"),
        name="neighbourhood_attention",
    )(qt, k, k, k, kc, vt, vt, vt, vct, bias)


def _store_row_tiles(dst_ref, val):
    rows = val.shape[0]
    for lt in range(ROW_TILES):
        dst_ref[pl.ds(lt, rows, stride=ROW_TILES), :] = val[:, lt * LANES:(lt + 1) * LANES]


def _load_row_tiles(src_ref, lt, rows):
    return src_ref[pl.ds(lt, rows, stride=ROW_TILES), :]


def _router_kernel(x_ref, sc_ref, sh_ref, wr_ref, h_ref, r_ref):
    h = x_ref[...] * (1.0 + sc_ref[0]) + sh_ref[0]
    _store_row_tiles(h_ref, h)
    hi = h.astype(BF16)
    lo = (h - hi.astype(F32)).astype(BF16)
    w = wr_ref[...]
    whi = w.astype(BF16)
    wlo = (w - whi.astype(F32)).astype(BF16)
    lg = (jnp.dot(hi, whi, preferred_element_type=F32)
          + (jnp.dot(hi, wlo, preferred_element_type=F32) + jnp.dot(lo, whi, preferred_element_type=F32)))
    lane = lax.broadcasted_iota(jnp.int32, lg.shape, 1).astype(F32)
    lg = jnp.where(lane < N_EXPERTS, lg, -jnp.inf)
    v1 = jnp.max(lg, axis=1, keepdims=True)
    i1 = jnp.min(jnp.where(lg == v1, lane, float(LANES)), axis=1, keepdims=True)
    lg2 = jnp.where(lane == i1, -jnp.inf, lg)
    v2 = jnp.max(lg2, axis=1, keepdims=True)
    i2 = jnp.min(jnp.where(lg2 == v2, lane, float(LANES)), axis=1, keepdims=True)
    e = jnp.exp(v2 - v1)
    w1 = 1.0 / (1.0 + e)
    w2 = e / (1.0 + e)
    r_ref[...] = jnp.where(lane == 0, i1, jnp.where(lane == 1, i2, jnp.where(lane == 2, w1,
                                                                              jnp.where(lane == 3, w2, 0.0))))


def _router(x2d, vecs, w_router, rows_per_batch, tm):
    r = x2d.shape[0]
    wr = jnp.pad(w_router, ((0, 0), (0, LANES - N_EXPERTS)))
    return pl.pallas_call(
        _router_kernel,
        grid=(r // tm,),
        in_specs=[pl.BlockSpec((tm, D_MODEL), lambda i: (i, 0)),
                  _vec_spec(4, tm, rows_per_batch), _vec_spec(3, tm, rows_per_batch),
                  _full_spec((D_MODEL, LANES))],
        out_specs=[pl.BlockSpec((tm * ROW_TILES, LANES), lambda i: (i, 0)),
                   pl.BlockSpec((tm, LANES), lambda i: (i, 0))],
        out_shape=[jax.ShapeDtypeStruct((r * ROW_TILES, LANES), F32), jax.ShapeDtypeStruct((r, LANES), F32)],
        compiler_params=_cparams("parallel"),
        name="router",
    )(x2d, vecs, vecs, wr)


def _row_copy(src_hbm, row, r, dst_ref, sem):
    def tile(i):
        start = i * ROW_TILES
        return pl.ds(start if isinstance(i, int) else pl.multiple_of(start, ROW_TILES), ROW_TILES)

    return pltpu.make_async_copy(src_hbm.at[tile(row), :], dst_ref.at[tile(r), :], sem)


def _start_row_gather(src_hbm, idx_ref, dst_ref, sem, n):
    def issue(i, carry):
        for u in range(2):
            r = 2 * i + u
            _row_copy(src_hbm, idx_ref[0, r], r, dst_ref, sem).start(priority=u)
        return carry

    lax.fori_loop(0, n // 2, issue, 0, unroll=4)


def _wait_row_gather(src_hbm, dst_ref, sem, n):
    def wait(r, carry):
        _row_copy(src_hbm, 0, r, dst_ref, sem).wait()
        return carry

    lax.fori_loop(0, n, wait, 0, unroll=8)


def _gather_kernel(idx_ref, src_hbm, o_ref, sem):
    n = o_ref.shape[0] // ROW_TILES
    _start_row_gather(src_hbm, idx_ref, o_ref, sem, n)
    _wait_row_gather(src_hbm, o_ref, sem, n)


def _gather_rows(src, idx):
    p = idx.shape[0]
    g = GATHER_ROWS
    return pl.pallas_call(
        _gather_kernel,
        grid=(p // g,),
        in_specs=[pl.BlockSpec((None, 1, g), lambda i: (i, 0, 0), memory_space=pltpu.SMEM),
                  pl.BlockSpec(memory_space=pl.ANY)],
        out_specs=pl.BlockSpec((g * ROW_TILES, LANES), lambda i: (i, 0)),
        out_shape=jax.ShapeDtypeStruct((p * ROW_TILES, LANES), src.dtype),
        scratch_shapes=[pltpu.SemaphoreType.DMA(())],
        compiler_params=_cparams("arbitrary"),
        name="gather_rows",
    )(idx.reshape(p // g, 1, g), src)


def _moe_ffn_kernel(te_ref, nu_ref, xs_ref, wg_ref, wu_ref, wd_ref, o_ref, hb_ref, acc_ref):
    t = pl.program_id(0)
    j = pl.program_id(1)
    nj = pl.num_programs(1)
    used = t < nu_ref[0]
    tm = hb_ref.shape[0]

    @pl.when(jnp.logical_and(used, j == 0))
    def _():
        for lt in range(ROW_TILES):
            hb_ref[:, lt * LANES:(lt + 1) * LANES] = _load_row_tiles(xs_ref, lt, tm).astype(BF16)
        acc_ref[...] = jnp.zeros_like(acc_ref)

    @pl.when(used)
    def _():
        hb = hb_ref[...]
        hg = jnp.dot(hb, wg_ref[...], preferred_element_type=F32)
        hu = jnp.dot(hb, wu_ref[...], preferred_element_type=F32)
        h1 = (hg * jax.nn.sigmoid(hg) * hu).astype(BF16)
        acc_ref[...] += jnp.dot(h1, wd_ref[...], preferred_element_type=F32)

    @pl.when(jnp.logical_and(used, j == nj - 1))
    def _():
        _store_row_tiles(o_ref, acc_ref[...])

    @pl.when(jnp.logical_and(jnp.logical_not(used), j == nj - 1))
    def _():
        o_ref[...] = jnp.zeros_like(o_ref)


def _moe_ffn(xs, tile_expert, n_used, wg, wu, wd):
    p = xs.shape[0] // ROW_TILES
    tm, tf = MOE_TM, MOE_TF
    f = wg.shape[2]

    def jj(t, j, nu):
        return jnp.where(t < nu[0], j, 0)

    grid_spec = pltpu.PrefetchScalarGridSpec(
        num_scalar_prefetch=2,
        grid=(p // tm, f // tf),
        in_specs=[pl.BlockSpec((tm * ROW_TILES, LANES), lambda t, j, te, nu: (t, 0)),
                  pl.BlockSpec((None, D_MODEL, tf), lambda t, j, te, nu: (te[t], 0, jj(t, j, nu))),
                  pl.BlockSpec((None, D_MODEL, tf), lambda t, j, te, nu: (te[t], 0, jj(t, j, nu))),
                  pl.BlockSpec((None, tf, D_MODEL), lambda t, j, te, nu: (te[t], jj(t, j, nu), 0))],
        out_specs=pl.BlockSpec((tm * ROW_TILES, LANES), lambda t, j, te, nu: (t, 0)),
        scratch_shapes=[pltpu.VMEM((tm, D_MODEL), BF16), pltpu.VMEM((tm, D_MODEL), F32)],
    )
    return pl.pallas_call(
        _moe_ffn_kernel,
        grid_spec=grid_spec,
        out_shape=jax.ShapeDtypeStruct((p * ROW_TILES, LANES), F32),
        compiler_params=_cparams("arbitrary", "arbitrary"),
        name="moe_ffn",
    )(tile_expert, n_used, xs, wg, wu, wd)


def _combine_ln_kernel(p1_ref, p2_ref, ys_hbm, route_ref, x_ref, gate_ref, lng_ref, lnb_ref, o_ref,
                       y1_ref, y2_ref, sem):
    n = x_ref.shape[0]
    _start_row_gather(ys_hbm, p1_ref, y1_ref, sem.at[0], n)
    _start_row_gather(ys_hbm, p2_ref, y2_ref, sem.at[1], n)
    _wait_row_gather(ys_hbm, y1_ref, sem.at[0], n)
    _wait_row_gather(ys_hbm, y2_ref, sem.at[1], n)
    w1 = route_ref[:, 2:3]
    w2 = route_ref[:, 3:4]
    y = jnp.concatenate([w1 * _load_row_tiles(y1_ref, lt, n) + w2 * _load_row_tiles(y2_ref, lt, n)
                         for lt in range(ROW_TILES)], axis=1)
    z = ALPHA * x_ref[...] + gate_ref[0] * y
    o_ref[...] = _layer_norm(z, lng_ref[...], lnb_ref[...])


def _combine_ln(ys, pos1, pos2, route, x2d, vecs, ln_g, ln_b, rows_per_batch):
    r = x2d.shape[0]
    g = GATHER_ROWS
    idx_spec = pl.BlockSpec((None, 1, g), lambda i: (i, 0, 0), memory_space=pltpu.SMEM)
    return pl.pallas_call(
        _combine_ln_kernel,
        grid=(r // g,),
        in_specs=[idx_spec, idx_spec, pl.BlockSpec(memory_space=pl.ANY),
                  pl.BlockSpec((g, LANES), lambda i: (i, 0)),
                  pl.BlockSpec((g, D_MODEL), lambda i: (i, 0)),
                  _vec_spec(5, g, rows_per_batch), _full_spec((1, D_MODEL)), _full_spec((1, D_MODEL))],
        out_specs=pl.BlockSpec((g, D_MODEL), lambda i: (i, 0)),
        out_shape=jax.ShapeDtypeStruct((r, D_MODEL), F32),
        scratch_shapes=[pltpu.VMEM((g * ROW_TILES, LANES), F32), pltpu.VMEM((g * ROW_TILES, LANES), F32),
                        pltpu.SemaphoreType.DMA((2,))],
        compiler_params=_cparams("arbitrary"),
        name="combine_ln",
    )(pos1.reshape(r // g, 1, g), pos2.reshape(r // g, 1, g), ys, route, x2d, vecs,
      ln_g.reshape(1, D_MODEL), ln_b.reshape(1, D_MODEL))


def _routing_plan(route, tm):
    n = route.shape[0]
    e = jnp.concatenate([route[:, 0], route[:, 1]]).astype(jnp.int32)
    tok = jnp.concatenate([jnp.arange(n, dtype=jnp.int32)] * 2)
    onehot = (e[:, None] == jnp.arange(N_EXPERTS, dtype=jnp.int32)[None, :]).astype(jnp.int32)
    csum = jnp.cumsum(onehot, axis=0)
    rank = jnp.sum(csum * onehot, axis=1) - 1
    counts = csum[-1]
    padded = ((counts + tm - 1) // tm) * tm
    ends = jnp.cumsum(padded)
    starts = ends - padded
    pos = jnp.sum(starts[None, :] * onehot, axis=1) + rank
    p = 2 * n + N_EXPERTS * tm
    src = jnp.zeros((p,), jnp.int32).at[pos].set(tok)
    tile_start = jnp.arange(p // tm, dtype=jnp.int32) * tm
    tile_expert = jnp.minimum(jnp.sum((tile_start[:, None] >= ends[None, :]).astype(jnp.int32), axis=1),
                              N_EXPERTS - 1).astype(jnp.int32)
    n_used = (ends[-1] // tm).astype(jnp.int32).reshape(1)
    return src, tile_expert, n_used, pos[:n], pos[n:]


def kernel(x, c, ctx, c_ctx, w_mod, b_mod, ln_g, ln_b, ab_w_in, ab_conv_w, ab_conv_g, ab_conv_b, ab_q_g, ab_k_g,
           ab_w_out, ffn_w_gate, ffn_w_up, ffn_w_down, na_w_qkv, na_rpb, na_w_out, moe_w_router, moe_w_gate,
           moe_w_up, moe_w_down):
    bsz, t, d = x.shape
    n_ctx = ctx.shape[1]
    n = bsz * t
    nc = bsz * n_ctx
    x2 = x.reshape(n, d)
    c2 = ctx.reshape(nc, d)

    cc = jnp.concatenate([c, c_ctx[None, :], jnp.zeros((8 - bsz - 1, d), F32)], axis=0)
    mod = _modulation(cc, w_mod, b_mod)
    vec0 = mod[0].reshape(8 * 6, 1, d)
    vec1 = mod[1].reshape(8 * 6, 1, d)

    w_in = ab_w_in[0].astype(BF16)
    w_out = ab_w_out[0].astype(BF16)
    pa, pq = _inproj0(x2, vec0, w_in, t, 512)
    pac, pqc = _inproj0(c2, vec0, w_in, None, 512)
    a = _conformer_conv(pa, ab_conv_w[0], ab_conv_g[0], ab_conv_b[0], t, 512)
    ac = _conformer_conv(pac, ab_conv_w[0], ab_conv_g[0], ab_conv_b[0], n_ctx, n_ctx)
    cos_t, sin_t = _rope_tables(t)
    qt, k, vt = _prep_qkv(pq, cos_t, sin_t, ab_q_g[0], ab_k_g[0], bsz, t)
    ones = jnp.ones((HEAD_DIM, n_ctx), F32)
    qtc, kc, vtc = _prep_qkv(pqc, ones, jnp.zeros_like(ones), ab_q_g[0], ab_k_g[0], bsz, n_ctx)
    o = _gqa_attention(qt, jnp.concatenate([kc, k], axis=1), jnp.concatenate([vtc, vt], axis=2), t)
    oc = _gqa_attention(qtc, kc, vtc, n_ctx)
    x2 = _outproj_ln([a, o], w_out, x2, vec0, 2, ln_g[0, 0], ln_b[0, 0], t, 512)
    c2 = _outproj_ln([ac, oc], w_out, c2, vec0, 2, ln_g[0, 0], ln_b[0, 0], None, 512)
    wg = ffn_w_gate[0].astype(BF16)
    wu = ffn_w_up[0].astype(BF16)
    wd = ffn_w_down[0].astype(BF16)
    x2 = _ffn_ln(x2, vec0, wg, wu, wd, ln_g[0, 1], ln_b[0, 1], t, 512, 1408)
    c2 = _ffn_ln(c2, vec0, wg, wu, wd, ln_g[0, 1], ln_b[0, 1], None, 512, 1408)

    w_qkv = na_w_qkv[0].astype(BF16)
    qt1, k1, vt1 = _inproj1(x2, vec1, w_qkv, t, 512, True)
    kc1, vct1 = _inproj1(c2, vec1, w_qkv[:, d:], None, 512, False)
    o = _neighbourhood_attention(qt1, k1, vt1, kc1, vct1, _na_bias(na_rpb[0], t // GRID_W, n_ctx), bsz, t)
    x2 = _outproj_ln([o], na_w_out[0].astype(BF16), x2, vec1, 2, ln_g[1, 0], ln_b[1, 0], t, 512)

    h, route = _router(x2, vec1, moe_w_router[0], t, 512)
    src, tile_expert, n_used, pos1, pos2 = _routing_plan(route, MOE_TM)
    xs = _gather_rows(h, src)
    ys = _moe_ffn(xs, tile_expert, n_used, moe_w_gate[0].astype(BF16), moe_w_up[0].astype(BF16),
                  moe_w_down[0].astype(BF16))
    x2 = _combine_ln(ys, pos1, pos2, route, x2, vec1, ln_g[1, 1], ln_b[1, 1], t)
    return x2.reshape(bsz, t, d)
```

```python
import functools

import numpy as np
import jax
import jax.numpy as jnp
from jax import lax
from jax.experimental import pallas as pl
from jax.experimental.pallas import tpu as pltpu

F32 = jnp.float32
BF16 = jnp.bfloat16

D_MODEL = 1024
GRID_W = 64
HEAD_DIM = 64
CONV_CH = 512
CONV_WIDTH = 31
CONV_HALO = 16
Q_COLS = 512
KV_COLS = 128
A_COLS = 2 * CONV_CH
ROPE_THETA = 10000.0
WIN_R = 8
WIN_C = 16
N_EXPERTS = 8
DEPTH = 2
ALPHA = (2 * DEPTH) ** 0.25
LN_EPS = 1e-5
RMS_EPS = 1e-6
ATT_SCALE = HEAD_DIM ** -0.5
LOG2E = 1.4426950408889634
MASK_VALUE = -1e30

LANES = 128
ROW_TILES = D_MODEL // LANES
VMEM_LIMIT = 56 * 1024 * 1024

ATT_TQ = 256
ATT_SK = 256
ATT_CW = 256
VT_ROWS = HEAD_DIM + 16
NA_ROWS = 4
NA_KROWS = 12
NA_PAIRS = 4
MOE_TM = 512
MOE_TF = 1792
GATHER_ROWS = 256


def _cparams(*sem):
    return pltpu.CompilerParams(dimension_semantics=sem, vmem_limit_bytes=VMEM_LIMIT)


def _layer_norm(z, g, b):
    mu = jnp.mean(z, axis=-1, keepdims=True)
    zc = z - mu
    var = jnp.mean(zc * zc, axis=-1, keepdims=True)
    return zc * lax.rsqrt(var + LN_EPS) * g + b


def _vec_spec(k, tm, rows_per_batch):
    if rows_per_batch is None:
        return pl.BlockSpec((1, 1, D_MODEL), lambda i, *_: (4 * 6 + k, 0, 0))
    return pl.BlockSpec((1, 1, D_MODEL), lambda i, *_: ((i * tm // rows_per_batch) * 6 + k, 0, 0))


def _full_spec(shape):
    nd = len(shape)
    return pl.BlockSpec(shape, lambda *_: (0,) * nd)


def _mod_kernel(c_ref, w_ref, b_ref, o_ref):
    c = c_ref[...]
    s = c * jax.nn.sigmoid(c)
    o_ref[...] = jnp.dot(s.astype(BF16), w_ref[...].astype(BF16), preferred_element_type=F32) + b_ref[...]


def _modulation(cc, w_mod, b_mod):
    n = 6 * D_MODEL
    tn = D_MODEL
    return pl.pallas_call(
        _mod_kernel,
        grid=(DEPTH, n // tn),
        in_specs=[pl.BlockSpec((8, D_MODEL), lambda l, j: (0, 0)),
                  pl.BlockSpec((None, D_MODEL, tn), lambda l, j: (l, 0, j)),
                  pl.BlockSpec((None, 1, tn), lambda l, j: (l, 0, j))],
        out_specs=pl.BlockSpec((None, 8, tn), lambda l, j: (l, 0, j)),
        out_shape=jax.ShapeDtypeStruct((DEPTH, 8, n), F32),
        compiler_params=_cparams("parallel", "parallel"),
        name="modulation",
    )(cc, w_mod, b_mod.reshape(DEPTH, 1, n))


def _inproj0_kernel(x_ref, sc_ref, sh_ref, w_ref, oa_ref, oq_ref):
    h = x_ref[...] * (1.0 + sc_ref[0]) + sh_ref[0]
    o = jnp.dot(h.astype(BF16), w_ref[...], preferred_element_type=F32)
    oa_ref[...] = o[:, :A_COLS]
    oq_ref[...] = o[:, A_COLS:]


def _inproj0(x2d, vecs, w, rows_per_batch, tm):
    r = x2d.shape[0]
    nq = w.shape[1] - A_COLS
    return pl.pallas_call(
        _inproj0_kernel,
        grid=(r // tm,),
        in_specs=[pl.BlockSpec((tm, D_MODEL), lambda i: (i, 0)),
                  _vec_spec(1, tm, rows_per_batch), _vec_spec(0, tm, rows_per_batch),
                  _full_spec(w.shape)],
        out_specs=[pl.BlockSpec((tm, A_COLS), lambda i: (i, 0)),
                   pl.BlockSpec((tm, nq), lambda i: (i, 0))],
        out_shape=[jax.ShapeDtypeStruct((r, A_COLS), F32), jax.ShapeDtypeStruct((r, nq), F32)],
        compiler_params=_cparams("parallel"),
        name="inproj0",
    )(x2d, vecs, vecs, w)


def _conv_kernel(pm_ref, pp_ref, pn_ref, w_ref, g_ref, b_ref, o_ref, u_ref, *, tt, tx):
    i = pl.program_id(0)

    def sigmoid(v):
        return 0.5 * jnp.tanh(0.5 * v) + 0.5

    def glu(p):
        return p[:, :CONV_CH] * sigmoid(p[:, CONV_CH:])

    first = (i * tt) % tx == 0
    last = ((i + 1) * tt) % tx == 0
    u_ref[0:CONV_HALO, :] = jnp.where(first, 0.0, glu(pp_ref[...]))
    u_ref[CONV_HALO:CONV_HALO + tt, :] = glu(pm_ref[...])
    u_ref[CONV_HALO + tt:2 * CONV_HALO + tt, :] = jnp.where(last, 0.0, glu(pn_ref[...]))
    ch = 32
    win = ch + 2 * CONV_HALO
    base = CONV_HALO - CONV_WIDTH // 2

    def body(c, carry):
        r0 = pl.multiple_of(c * ch, ch)
        w = u_ref[pl.ds(r0, win), :]
        acc = jnp.zeros((ch, CONV_CH), F32)
        for s in range(8):
            ws = w if s == 0 else pltpu.roll(w, win - s, axis=0)
            for a in range(win // 8):
                k = 8 * a + s - base
                if 0 <= k < CONV_WIDTH and 8 * a + ch <= win - s:
                    acc = acc + ws[8 * a:8 * a + ch] * w_ref[pl.ds(k, 1), :]
        y = _layer_norm(acc, g_ref[...], b_ref[...])
        y = y * sigmoid(y)
        o_ref[pl.ds(r0, ch), :] = y.astype(BF16)
        return carry

    lax.fori_loop(0, tt // ch, body, 0)


def _conformer_conv(pa, conv_w, conv_g, conv_b, tx, tt):
    r = pa.shape[0]
    hb = tt // CONV_HALO
    nhb = r // CONV_HALO
    return pl.pallas_call(
        functools.partial(_conv_kernel, tt=tt, tx=tx),
        grid=(r // tt,),
        in_specs=[pl.BlockSpec((tt, A_COLS), lambda i: (i, 0)),
                  pl.BlockSpec((CONV_HALO, A_COLS), lambda i: (jnp.maximum(i * hb - 1, 0), 0)),
                  pl.BlockSpec((CONV_HALO, A_COLS), lambda i: (jnp.minimum((i + 1) * hb, nhb - 1), 0)),
                  _full_spec((CONV_WIDTH, CONV_CH)), _full_spec((1, CONV_CH)), _full_spec((1, CONV_CH))],
        out_specs=pl.BlockSpec((tt, CONV_CH), lambda i: (i, 0)),
        out_shape=jax.ShapeDtypeStruct((r, CONV_CH), BF16),
        scratch_shapes=[pltpu.VMEM((tt + 2 * CONV_HALO, CONV_CH), F32)],
        compiler_params=_cparams("parallel"),
        name="conformer_conv",
    )(pa, pa, pa, conv_w, conv_g.reshape(1, CONV_CH), conv_b.reshape(1, CONV_CH))


def _prep_kernel(p_ref, cos_ref, sin_ref, qg_ref, kg_ref, qt_ref, k_ref, vt_ref, *, tq):
    x = p_ref[...]
    cos = cos_ref[...]
    sin = sin_ref[...]

    def norm_rope(xh, g):
        ms = jnp.mean(xh * xh, axis=0, keepdims=True)
        y = xh * lax.rsqrt(ms + RMS_EPS) * g
        sw = jnp.concatenate([y[16:32], y[0:16], y[48:64], y[32:48]], axis=0)
        return y * cos + sw * sin

    zeros = jnp.zeros((HEAD_DIM, tq), BF16)
    for p in range(Q_COLS // LANES):
        xp = x[:, LANES * p:LANES * (p + 1)].T
        for half in range(2):
            h = 2 * p + half
            j, g = h // 4, h % 4
            r = (norm_rope(xp[HEAD_DIM * half:HEAD_DIM * (half + 1)], qg_ref[...]) * (ATT_SCALE * LOG2E)).astype(BF16)
            qt_ref[j, HEAD_DIM * j:HEAD_DIM * (j + 1), g * tq:(g + 1) * tq] = r
            qt_ref[j, HEAD_DIM * (1 - j):HEAD_DIM * (2 - j), g * tq:(g + 1) * tq] = zeros
    xk = x[:, Q_COLS:Q_COLS + KV_COLS].T
    k0 = norm_rope(xk[0:HEAD_DIM], kg_ref[...])
    k1 = norm_rope(xk[HEAD_DIM:2 * HEAD_DIM], kg_ref[...])
    k_ref[...] = jnp.concatenate([k0, k1], axis=0).T.astype(BF16)
    xv = x[:, Q_COLS + KV_COLS:].T.astype(BF16)
    ones = jnp.ones((VT_ROWS - HEAD_DIM, tq), BF16)
    for j in range(2):
        vt_ref[j, 0:HEAD_DIM, :] = xv[HEAD_DIM * j:HEAD_DIM * (j + 1)]
        vt_ref[j, HEAD_DIM:VT_ROWS, :] = ones


def _prep_qkv(pq, cos_t, sin_t, q_g, k_g, bx, tx):
    tq = ATT_TQ
    nq = tx // tq
    return pl.pallas_call(
        functools.partial(_prep_kernel, tq=tq),
        grid=(bx, nq),
        in_specs=[pl.BlockSpec((tq, Q_COLS + 2 * KV_COLS), lambda b, t: (b * nq + t, 0)),
                  pl.BlockSpec((HEAD_DIM, tq), lambda b, t: (0, t)),
                  pl.BlockSpec((HEAD_DIM, tq), lambda b, t: (0, t)),
                  _full_spec((HEAD_DIM, 1)), _full_spec((HEAD_DIM, 1))],
        out_specs=[pl.BlockSpec((None, None, 2, LANES, 4 * tq), lambda b, t: (b, t, 0, 0, 0)),
                   pl.BlockSpec((None, None, tq, LANES), lambda b, t: (b, t, 0, 0)),
                   pl.BlockSpec((None, 2, None, VT_ROWS, tq), lambda b, t: (b, 0, t, 0, 0))],
        out_shape=[jax.ShapeDtypeStruct((bx, nq, 2, LANES, 4 * tq), BF16),
                   jax.ShapeDtypeStruct((bx, nq, tq, LANES), BF16),
                   jax.ShapeDtypeStruct((bx, 2, nq, VT_ROWS, tq), BF16)],
        compiler_params=_cparams("parallel", "parallel"),
        name="prep_qkv",
    )(pq, cos_t, sin_t, q_g.reshape(HEAD_DIM, 1), k_g.reshape(HEAD_DIM, 1))


def _rope_tables(t):
    pos = np.arange(t)
    half = HEAD_DIM // 4
    inv = ROPE_THETA ** (-jnp.arange(half, dtype=F32) * 2.0 / (HEAD_DIM // 2))
    dd = np.arange(HEAD_DIM)
    part_pos = np.where((dd // (HEAD_DIM // 2))[:, None] == 0, (pos // GRID_W)[None, :], (pos % GRID_W)[None, :])
    ang = jnp.asarray(part_pos, F32) * inv[dd % half][:, None]
    sign = jnp.asarray(np.where((dd % (HEAD_DIM // 2)) < half, -1.0, 1.0)[:, None], F32)
    return jnp.cos(ang), jnp.sin(ang) * sign


def _attn_kernel(qt_ref, k_ref, vt_ref, o_ref, acc_ref, *, nk, tq):
    qt = qt_ref[...]
    ncol = 4 * tq
    acc_ref[...] = jnp.zeros_like(acc_ref)

    cw = ATT_CW
    nstrip = ncol // cw

    sk = ATT_SK
    per = tq // sk
    nsub = nk * per

    def scores(i, n):
        c, u = divmod(i, per)
        return jnp.dot(k_ref[c, sk * u:sk * (u + 1), :], qt[:, cw * n:cw * (n + 1)], preferred_element_type=F32)

    s = [scores(0, n) for n in range(nstrip)]
    m_prev = [jnp.full((1, cw), -jnp.inf, F32)] * nstrip
    m = [jnp.max(sn, axis=0, keepdims=True) for sn in s]
    for i in range(nsub):
        c, u = divmod(i, per)
        vt = vt_ref[c, :, sk * u:sk * (u + 1)]
        for n in range(nstrip):
            s_next = scores(i + 1, n) if i + 1 < nsub else None
            alpha = jnp.exp2(m_prev[n] - m[n])
            p = jnp.exp2((s[n] - m[n]).astype(BF16))
            pv = jnp.dot(vt, p, preferred_element_type=F32)
            acc_ref[:, cw * n:cw * (n + 1)] = acc_ref[:, cw * n:cw * (n + 1)] * alpha + pv
            if s_next is not None:
                m_prev[n] = m[n]
                m[n] = jnp.maximum(m[n], jnp.max(s_next, axis=0, keepdims=True))
                s[n] = s_next
    o = acc_ref[0:HEAD_DIM, :] * (1.0 / acc_ref[HEAD_DIM:HEAD_DIM + 1, :])
    for pp in range(2):
        blk = jnp.concatenate([o[:, (2 * pp) * tq:(2 * pp + 1) * tq],
                               o[:, (2 * pp + 1) * tq:(2 * pp + 2) * tq]], axis=0)
        o_ref[:, LANES * pp:LANES * (pp + 1)] = blk.T.astype(BF16)


def _gqa_attention(qt, k, vt, tx):
    bx, nq = qt.shape[0], qt.shape[1]
    nk = k.shape[1]
    tq = ATT_TQ
    return pl.pallas_call(
        functools.partial(_attn_kernel, nk=nk, tq=tq),
        grid=(bx, 2, nq),
        in_specs=[pl.BlockSpec((None, None, None, LANES, 4 * tq), lambda b, j, t: (b, t, j, 0, 0)),
                  pl.BlockSpec((None, nk, tq, LANES), lambda b, j, t: (b, 0, 0, 0)),
                  pl.BlockSpec((None, None, nk, VT_ROWS, tq), lambda b, j, t: (b, j, 0, 0, 0))],
        out_specs=pl.BlockSpec((tq, 2 * LANES), lambda b, j, t: (b * nq + t, j)),
        out_shape=jax.ShapeDtypeStruct((bx * tx, Q_COLS), BF16),
        scratch_shapes=[pltpu.VMEM((VT_ROWS, 4 * tq), F32)],
        compiler_params=_cparams("parallel", "parallel", "parallel"),
        name="gqa_attention",
    )(qt, k, vt)


def _outproj_ln_kernel(*refs, n_lhs):
    lhs = refs[:n_lhs]
    w_ref, x_ref, gate_ref, lng_ref, lnb_ref, o_ref = refs[n_lhs:]
    y = None
    off = 0
    for r in lhs:
        kk = r.shape[1]
        t = jnp.dot(r[...], w_ref[off:off + kk, :], preferred_element_type=F32)
        off += kk
        y = t if y is None else y + t
    z = ALPHA * x_ref[...] + gate_ref[0] * y
    o_ref[...] = _layer_norm(z, lng_ref[...], lnb_ref[...])


def _outproj_ln(lhs, w, x2d, vecs, gate_k, ln_g, ln_b, rows_per_batch, tm):
    r = x2d.shape[0]
    return pl.pallas_call(
        functools.partial(_outproj_ln_kernel, n_lhs=len(lhs)),
        grid=(r // tm,),
        in_specs=[pl.BlockSpec((tm, a.shape[1]), lambda i: (i, 0)) for a in lhs]
        + [_full_spec(w.shape), pl.BlockSpec((tm, D_MODEL), lambda i: (i, 0)),
           _vec_spec(gate_k, tm, rows_per_batch), _full_spec((1, D_MODEL)), _full_spec((1, D_MODEL))],
        out_specs=pl.BlockSpec((tm, D_MODEL), lambda i: (i, 0)),
        out_shape=jax.ShapeDtypeStruct((r, D_MODEL), F32),
        compiler_params=_cparams("parallel"),
        name="outproj_ln",
    )(*lhs, w, x2d, vecs, ln_g.reshape(1, D_MODEL), ln_b.reshape(1, D_MODEL))


def _ffn_kernel(x_ref, sc_ref, sh_ref, gate_ref, wg_ref, wu_ref, wd_ref, lng_ref, lnb_ref, o_ref, hb_ref, acc_ref):
    j = pl.program_id(1)

    @pl.when(j == 0)
    def _():
        hb_ref[...] = (x_ref[...] * (1.0 + sc_ref[0]) + sh_ref[0]).astype(BF16)
        acc_ref[...] = jnp.zeros_like(acc_ref)

    hb = hb_ref[...]
    hg = jnp.dot(hb, wg_ref[...], preferred_element_type=F32)
    hu = jnp.dot(hb, wu_ref[...], preferred_element_type=F32)
    h1 = (hg * jax.nn.sigmoid(hg) * hu).astype(BF16)
    acc_ref[...] += jnp.dot(h1, wd_ref[...], preferred_element_type=F32)

    @pl.when(j == pl.num_programs(1) - 1)
    def _():
        z = ALPHA * x_ref[...] + gate_ref[0] * acc_ref[...]
        o_ref[...] = _layer_norm(z, lng_ref[...], lnb_ref[...])


def _ffn_ln(x2d, vecs, wg, wu, wd, ln_g, ln_b, rows_per_batch, tm, tf):
    r = x2d.shape[0]
    f = wg.shape[1]
    return pl.pallas_call(
        _ffn_kernel,
        grid=(r // tm, f // tf),
        in_specs=[pl.BlockSpec((tm, D_MODEL), lambda i, j: (i, 0)),
                  _vec_spec(4, tm, rows_per_batch), _vec_spec(3, tm, rows_per_batch),
                  _vec_spec(5, tm, rows_per_batch),
                  pl.BlockSpec((D_MODEL, tf), lambda i, j: (0, j)),
                  pl.BlockSpec((D_MODEL, tf), lambda i, j: (0, j)),
                  pl.BlockSpec((tf, D_MODEL), lambda i, j: (j, 0)),
                  _full_spec((1, D_MODEL)), _full_spec((1, D_MODEL))],
        out_specs=pl.BlockSpec((tm, D_MODEL), lambda i, j: (i, 0)),
        out_shape=jax.ShapeDtypeStruct((r, D_MODEL), F32),
        scratch_shapes=[pltpu.VMEM((tm, D_MODEL), BF16), pltpu.VMEM((tm, D_MODEL), F32)],
        compiler_params=_cparams("parallel", "arbitrary"),
        name="ffn_ln",
    )(x2d, vecs, vecs, vecs, wg, wu, wd, ln_g.reshape(1, D_MODEL), ln_b.reshape(1, D_MODEL))


def _inproj1_kernel(x_ref, sc_ref, sh_ref, w_ref, *o_refs, with_q):
    h = x_ref[...] * (1.0 + sc_ref[0]) + sh_ref[0]
    o = jnp.dot(h.astype(BF16), w_ref[...], preferred_element_type=F32)
    if with_q:
        qt_ref, k_ref, vt_ref = o_refs
        qt_ref[...] = (o[:, :D_MODEL] * (ATT_SCALE * LOG2E)).T.astype(BF16)
    else:
        k_ref, vt_ref = o_refs
    nk = o.shape[1] - 2 * D_MODEL
    k_ref[...] = o[:, nk:nk + D_MODEL].astype(BF16)
    vt_ref[...] = o[:, nk + D_MODEL:].T.astype(BF16)


def _inproj1(x2d, vecs, w, rows_per_batch, tm, with_q):
    r = x2d.shape[0]
    nat = pl.BlockSpec((tm, D_MODEL), lambda i: (i, 0))
    tr = pl.BlockSpec((D_MODEL, tm), lambda i: (0, i))
    nat_shape = jax.ShapeDtypeStruct((r, D_MODEL), BF16)
    tr_shape = jax.ShapeDtypeStruct((D_MODEL, r), BF16)
    return pl.pallas_call(
        functools.partial(_inproj1_kernel, with_q=with_q),
        grid=(r // tm,),
        in_specs=[pl.BlockSpec((tm, D_MODEL), lambda i: (i, 0)),
                  _vec_spec(1, tm, rows_per_batch), _vec_spec(0, tm, rows_per_batch),
                  _full_spec(w.shape)],
        out_specs=([tr] if with_q else []) + [nat, tr],
        out_shape=([tr_shape] if with_q else []) + [nat_shape, tr_shape],
        compiler_params=_cparams("parallel"),
        name="inproj1",
    )(x2d, vecs, vecs, w)


def _na_bias(rpb, n_rows, n_ctx):
    nh = rpb.shape[0]
    c = np.arange(GRID_W)
    cs = np.clip(c - WIN_C // 2, 0, GRID_W - WIN_C)
    in_c = (c[None, :] >= cs[:, None]) & (c[None, :] < cs[:, None] + WIN_C)
    dc = c[None, :] - c[:, None] + WIN_C - 1
    pick = ((dc[None] == np.arange(2 * WIN_C - 1)[:, None, None]) & in_c[None]).astype(np.float32)
    cols = jnp.einsum("hrd,dkc->hrkc", rpb, jnp.asarray(pick.transpose(0, 2, 1)), precision=lax.Precision.HIGHEST)
    cols = jnp.where(jnp.asarray(in_c.T)[None, None], cols * LOG2E, MASK_VALUE)
    n_dr = 2 * WIN_R - 1
    cols = jnp.concatenate([cols, jnp.full((nh, 1, GRID_W, GRID_W), MASK_VALUE, F32)], axis=1)
    pick_dr = np.full((3, NA_ROWS, NA_KROWS), n_dr, np.int32)
    for case, r0 in enumerate((0, NA_ROWS, n_rows - NA_ROWS)):
        start = int(np.clip(r0 - WIN_R // 2, 0, n_rows - NA_KROWS))
        for ri in range(NA_ROWS):
            r = r0 + ri
            rs = int(np.clip(r - WIN_R // 2, 0, n_rows - WIN_R))
            for ki in range(NA_KROWS):
                kr = start + ki
                if rs <= kr < rs + WIN_R:
                    pick_dr[case, ri, ki] = kr - r + WIN_R - 1
    bias = jnp.take(cols, jnp.asarray(pick_dr.reshape(-1)), axis=1)
    bias = bias.reshape(nh // (2 * NA_PAIRS), NA_PAIRS, 2, 3, NA_ROWS, NA_KROWS, GRID_W, GRID_W)
    bias = bias.transpose(3, 0, 1, 5, 6, 2, 4, 7)
    bias = bias.reshape(3, nh // (2 * NA_PAIRS), NA_PAIRS, NA_KROWS * GRID_W, 2 * NA_ROWS * GRID_W)
    return jnp.pad(bias, ((0, 0), (0, 0), (0, 0), (0, n_ctx), (0, 0)))


def _na_kernel(qt_ref, k0_ref, k1_ref, k2_ref, kc_ref, v0_ref, v1_ref, v2_ref, vc_ref, bias_ref, o_ref):
    qt = qt_ref[...]
    kk = jnp.concatenate([k0_ref[...], k1_ref[...], k2_ref[...], kc_ref[...]], axis=0)
    vt = jnp.concatenate([v0_ref[...], v1_ref[...], v2_ref[...], vc_ref[...]], axis=1)
    nq = qt.shape[1]
    zeros = jnp.zeros((HEAD_DIM, nq), BF16)
    ones = jnp.ones((VT_ROWS - HEAD_DIM, kk.shape[0]), BF16)

    def scores(pp):
        q2 = qt[LANES * pp:LANES * (pp + 1)]
        qcat = jnp.concatenate([jnp.concatenate([q2[0:HEAD_DIM], zeros], axis=0),
                                jnp.concatenate([zeros, q2[HEAD_DIM:]], axis=0)], axis=1)
        return jnp.dot(kk[:, LANES * pp:LANES * (pp + 1)], qcat, preferred_element_type=F32) + bias_ref[pp]

    s_all = [scores(pp) for pp in range(NA_PAIRS)]
    for pp in range(NA_PAIRS):
        s = s_all[pp]
        m = jnp.max(s, axis=0, keepdims=True)
        p = jnp.exp2((s - m).astype(BF16))
        vext = jnp.concatenate([vt[LANES * pp:LANES * (pp + 1)], ones], axis=0)
        pv = jnp.dot(vext, p, preferred_element_type=F32)
        o0 = pv[0:HEAD_DIM, 0:nq] * (1.0 / pv[LANES:LANES + 1, 0:nq])
        o1 = pv[HEAD_DIM:LANES, nq:] * (1.0 / pv[LANES:LANES + 1, nq:])
        o_ref[:, LANES * pp:LANES * (pp + 1)] = jnp.concatenate([o0, o1], axis=0).T.astype(BF16)


def _neighbourhood_attention(qt, k, vt, kc, vct, bias, bsz, t):
    nq = NA_ROWS * GRID_W
    nrb = t // nq
    wl = NA_PAIRS * LANES
    ngrp = D_MODEL // wl
    nctx = kc.shape[0] // bsz
    nwin = NA_KROWS // NA_ROWS

    def first_kblock(b, rb):
        return b * nrb + jnp.clip(rb - 1, 0, nrb - nwin)

    def kspec(d):
        return pl.BlockSpec((nq, wl), lambda hp, b, rb: (first_kblock(b, rb) + d, hp))

    def vspec(d):
        return pl.BlockSpec((wl, nq), lambda hp, b, rb: (hp, first_kblock(b, rb) + d))

    def case(rb):
        return jnp.where(rb == 0, 0, jnp.where(rb == nrb - 1, 2, 1))

    return pl.pallas_call(
        _na_kernel,
        grid=(ngrp, bsz, nrb),
        in_specs=[pl.BlockSpec((wl, nq), lambda hp, b, rb: (hp, b * nrb + rb)),
                  kspec(0), kspec(1), kspec(2), pl.BlockSpec((nctx, wl), lambda hp, b, rb: (b, hp)),
                  vspec(0), vspec(1), vspec(2), pl.BlockSpec((wl, nctx), lambda hp, b, rb: (hp, b)),
                  pl.BlockSpec((None, None, NA_PAIRS, NA_KROWS * GRID_W + nctx, 2 * nq),
                               lambda hp, b, rb: (case(rb), hp, 0, 0, 0))],
        out_specs=pl.BlockSpec((nq, wl), lambda hp, b, rb: (b * nrb + rb, hp)),
        out_shape=jax.ShapeDtypeStruct((bsz * t, D_MODEL), BF16),
        compiler_params=_cparams("parallel", "parallel", "parallel"),
        name="neighbourhood_attention",
    )(qt, k, k, k, kc, vt, vt, vt, vct, bias)


def _store_row_tiles(dst_ref, val):
    rows = val.shape[0]
    for lt in range(ROW_TILES):
        dst_ref[pl.ds(lt, rows, stride=ROW_TILES), :] = val[:, lt * LANES:(lt + 1) * LANES]


def _load_row_tiles(src_ref, lt, rows):
    return src_ref[pl.ds(lt, rows, stride=ROW_TILES), :]


def _router_kernel(x_ref, sc_ref, sh_ref, wr_ref, h_ref, r_ref):
    h = x_ref[...] * (1.0 + sc_ref[0]) + sh_ref[0]
    _store_row_tiles(h_ref, h)
    hi = h.astype(BF16)
    lo = (h - hi.astype(F32)).astype(BF16)
    w = wr_ref[...]
    whi = w.astype(BF16)
    wlo = (w - whi.astype(F32)).astype(BF16)
    lg = (jnp.dot(hi, whi, preferred_element_type=F32)
          + (jnp.dot(hi, wlo, preferred_element_type=F32) + jnp.dot(lo, whi, preferred_element_type=F32)))
    lane = lax.broadcasted_iota(jnp.int32, lg.shape, 1).astype(F32)
    lg = jnp.where(lane < N_EXPERTS, lg, -jnp.inf)
    v1 = jnp.max(lg, axis=1, keepdims=True)
    i1 = jnp.min(jnp.where(lg == v1, lane, float(LANES)), axis=1, keepdims=True)
    lg2 = jnp.where(lane == i1, -jnp.inf, lg)
    v2 = jnp.max(lg2, axis=1, keepdims=True)
    i2 = jnp.min(jnp.where(lg2 == v2, lane, float(LANES)), axis=1, keepdims=True)
    e = jnp.exp(v2 - v1)
    w1 = 1.0 / (1.0 + e)
    w2 = e / (1.0 + e)
    r_ref[...] = jnp.where(lane == 0, i1, jnp.where(lane == 1, i2, jnp.where(lane == 2, w1,
                                                                              jnp.where(lane == 3, w2, 0.0))))


def _router(x2d, vecs, w_router, rows_per_batch, tm):
    r = x2d.shape[0]
    wr = jnp.pad(w_router, ((0, 0), (0, LANES - N_EXPERTS)))
    return pl.pallas_call(
        _router_kernel,
        grid=(r // tm,),
        in_specs=[pl.BlockSpec((tm, D_MODEL), lambda i: (i, 0)),
                  _vec_spec(4, tm, rows_per_batch), _vec_spec(3, tm, rows_per_batch),
                  _full_spec((D_MODEL, LANES))],
        out_specs=[pl.BlockSpec((tm * ROW_TILES, LANES), lambda i: (i, 0)),
                   pl.BlockSpec((tm, LANES), lambda i: (i, 0))],
        out_shape=[jax.ShapeDtypeStruct((r * ROW_TILES, LANES), F32), jax.ShapeDtypeStruct((r, LANES), F32)],
        compiler_params=_cparams("parallel"),
        name="router",
    )(x2d, vecs, vecs, wr)


def _row_copy(src_hbm, row, r, dst_ref, sem):
    def tile(i):
        start = i * ROW_TILES
        return pl.ds(start if isinstance(i, int) else pl.multiple_of(start, ROW_TILES), ROW_TILES)

    return pltpu.make_async_copy(src_hbm.at[tile(row), :], dst_ref.at[tile(r), :], sem)


def _start_row_gather(src_hbm, idx_ref, dst_ref, sem, n):
    def issue(i, carry):
        for u in range(2):
            r = 2 * i + u
            _row_copy(src_hbm, idx_ref[0, r], r, dst_ref, sem).start(priority=u)
        return carry

    lax.fori_loop(0, n // 2, issue, 0, unroll=4)


def _wait_row_gather(src_hbm, dst_ref, sem, n):
    def wait(r, carry):
        _row_copy(src_hbm, 0, r, dst_ref, sem).wait()
        return carry

    lax.fori_loop(0, n, wait, 0, unroll=8)


def _moe_ffn_kernel(te_ref, nu_ref, idx0_ref, idxn_ref, h_hbm, wg_ref, wu_ref, wd_ref, o_ref,
                    xbuf_ref, hb_ref, acc_ref, sem):
    t = pl.program_id(0)
    j = pl.program_id(1)
    nt = pl.num_programs(0)
    nj = pl.num_programs(1)
    used = t < nu_ref[0]
    tm = hb_ref.shape[0]
    slot = t % 2

    @pl.when(j == 0)
    def _():
        @pl.when(t == 0)
        def _():
            _start_row_gather(h_hbm, idx0_ref, xbuf_ref.at[0], sem.at[0], tm)

        @pl.when(t + 1 < nt)
        def _():
            _start_row_gather(h_hbm, idxn_ref, xbuf_ref.at[1 - slot], sem.at[1 - slot], tm)

        _wait_row_gather(h_hbm, xbuf_ref.at[slot], sem.at[slot], tm)

    @pl.when(jnp.logical_and(used, j == 0))
    def _():
        for lt in range(ROW_TILES):
            hb_ref[:, lt * LANES:(lt + 1) * LANES] = _load_row_tiles(xbuf_ref.at[slot], lt, tm).astype(BF16)
        acc_ref[...] = jnp.zeros_like(acc_ref)

    @pl.when(used)
    def _():
        hb = hb_ref[...]
        hg = jnp.dot(hb, wg_ref[...], preferred_element_type=F32)
        hu = jnp.dot(hb, wu_ref[...], preferred_element_type=F32)
        h1 = (hg * jax.nn.sigmoid(hg) * hu).astype(BF16)
        acc_ref[...] += jnp.dot(h1, wd_ref[...], preferred_element_type=F32)

    @pl.when(jnp.logical_and(used, j == nj - 1))
    def _():
        _store_row_tiles(o_ref, acc_ref[...])

    @pl.when(jnp.logical_and(jnp.logical_not(used), j == nj - 1))
    def _():
        o_ref[...] = jnp.zeros_like(o_ref)


def _moe_ffn(h, src, tile_expert, n_used, wg, wu, wd):
    p = src.shape[0]
    tm, tf = MOE_TM, MOE_TF
    nt = p // tm
    f = wg.shape[2]

    def jj(t, j, nu):
        return jnp.where(t < nu[0], j, 0)

    grid_spec = pltpu.PrefetchScalarGridSpec(
        num_scalar_prefetch=2,
        grid=(nt, f // tf),
        in_specs=[pl.BlockSpec((None, 1, tm), lambda t, j, te, nu: (0, 0, 0), memory_space=pltpu.SMEM),
                  pl.BlockSpec((None, 1, tm), lambda t, j, te, nu: (jnp.minimum(t + 1, nt - 1), 0, 0),
                               memory_space=pltpu.SMEM),
                  pl.BlockSpec(memory_space=pl.ANY),
                  pl.BlockSpec((None, D_MODEL, tf), lambda t, j, te, nu: (te[t], 0, jj(t, j, nu))),
                  pl.BlockSpec((None, D_MODEL, tf), lambda t, j, te, nu: (te[t], 0, jj(t, j, nu))),
                  pl.BlockSpec((None, tf, D_MODEL), lambda t, j, te, nu: (te[t], jj(t, j, nu), 0))],
        out_specs=pl.BlockSpec((tm * ROW_TILES, LANES), lambda t, j, te, nu: (t, 0)),
        scratch_shapes=[pltpu.VMEM((2, tm * ROW_TILES, LANES), F32), pltpu.VMEM((tm, D_MODEL), BF16),
                        pltpu.VMEM((tm, D_MODEL), F32), pltpu.SemaphoreType.DMA((2,))],
    )
    idx = src.reshape(nt, 1, tm)
    return pl.pallas_call(
        _moe_ffn_kernel,
        grid_spec=grid_spec,
        out_shape=jax.ShapeDtypeStruct((p * ROW_TILES, LANES), F32),
        compiler_params=_cparams("arbitrary", "arbitrary"),
        name="moe_ffn",
    )(tile_expert, n_used, idx, idx, h, wg, wu, wd)


def _combine_ln_kernel(p1_ref, p2_ref, ys_hbm, route_ref, x_ref, gate_ref, lng_ref, lnb_ref, o_ref,
                       y1_ref, y2_ref, sem):
    n = x_ref.shape[0]
    _start_row_gather(ys_hbm, p1_ref, y1_ref, sem.at[0], n)
    _start_row_gather(ys_hbm, p2_ref, y2_ref, sem.at[1], n)
    _wait_row_gather(ys_hbm, y1_ref, sem.at[0], n)
    _wait_row_gather(ys_hbm, y2_ref, sem.at[1], n)
    w1 = route_ref[:, 2:3]
    w2 = route_ref[:, 3:4]
    y = jnp.concatenate([w1 * _load_row_tiles(y1_ref, lt, n) + w2 * _load_row_tiles(y2_ref, lt, n)
                         for lt in range(ROW_TILES)], axis=1)
    z = ALPHA * x_ref[...] + gate_ref[0] * y
    o_ref[...] = _layer_norm(z, lng_ref[...], lnb_ref[...])


def _combine_ln(ys, pos1, pos2, route, x2d, vecs, ln_g, ln_b, rows_per_batch):
    r = x2d.shape[0]
    g = GATHER_ROWS
    idx_spec = pl.BlockSpec((None, 1, g), lambda i: (i, 0, 0), memory_space=pltpu.SMEM)
    return pl.pallas_call(
        _combine_ln_kernel,
        grid=(r // g,),
        in_specs=[idx_spec, idx_spec, pl.BlockSpec(memory_space=pl.ANY),
                  pl.BlockSpec((g, LANES), lambda i: (i, 0)),
                  pl.BlockSpec((g, D_MODEL), lambda i: (i, 0)),
                  _vec_spec(5, g, rows_per_batch), _full_spec((1, D_MODEL)), _full_spec((1, D_MODEL))],
        out_specs=pl.BlockSpec((g, D_MODEL), lambda i: (i, 0)),
        out_shape=jax.ShapeDtypeStruct((r, D_MODEL), F32),
        scratch_shapes=[pltpu.VMEM((g * ROW_TILES, LANES), F32), pltpu.VMEM((g * ROW_TILES, LANES), F32),
                        pltpu.SemaphoreType.DMA((2,))],
        compiler_params=_cparams("arbitrary"),
        name="combine_ln",
    )(pos1.reshape(r // g, 1, g), pos2.reshape(r // g, 1, g), ys, route, x2d, vecs,
      ln_g.reshape(1, D_MODEL), ln_b.reshape(1, D_MODEL))


def _routing_plan(route, tm):
    n = route.shape[0]
    e = jnp.concatenate([route[:, 0], route[:, 1]]).astype(jnp.int32)
    tok = jnp.concatenate([jnp.arange(n, dtype=jnp.int32)] * 2)
    onehot = (e[:, None] == jnp.arange(N_EXPERTS, dtype=jnp.int32)[None, :]).astype(jnp.int32)
    csum = jnp.cumsum(onehot, axis=0)
    rank = jnp.sum(csum * onehot, axis=1) - 1
    counts = csum[-1]
    padded = ((counts + tm - 1) // tm) * tm
    ends = jnp.cumsum(padded)
    starts = ends - padded
    pos = jnp.sum(starts[None, :] * onehot, axis=1) + rank
    p = 2 * n + N_EXPERTS * tm
    src = jnp.zeros((p,), jnp.int32).at[pos].set(tok)
    tile_start = jnp.arange(p // tm, dtype=jnp.int32) * tm
    tile_expert = jnp.minimum(jnp.sum((tile_start[:, None] >= ends[None, :]).astype(jnp.int32), axis=1),
                              N_EXPERTS - 1).astype(jnp.int32)
    n_used = (ends[-1] // tm).astype(jnp.int32).reshape(1)
    return src, tile_expert, n_used, pos[:n], pos[n:]


def kernel(x, c, ctx, c_ctx, w_mod, b_mod, ln_g, ln_b, ab_w_in, ab_conv_w, ab_conv_g, ab_conv_b, ab_q_g, ab_k_g,
           ab_w_out, ffn_w_gate, ffn_w_up, ffn_w_down, na_w_qkv, na_rpb, na_w_out, moe_w_router, moe_w_gate,
           moe_w_up, moe_w_down):
    bsz, t, d = x.shape
    n_ctx = ctx.shape[1]
    n = bsz * t
    nc = bsz * n_ctx
    x2 = x.reshape(n, d)
    c2 = ctx.reshape(nc, d)

    cc = jnp.concatenate([c, c_ctx[None, :], jnp.zeros((8 - bsz - 1, d), F32)], axis=0)
    mod = _modulation(cc, w_mod, b_mod)
    vec0 = mod[0].reshape(8 * 6, 1, d)
    vec1 = mod[1].reshape(8 * 6, 1, d)

    w_in = ab_w_in[0].astype(BF16)
    w_out = ab_w_out[0].astype(BF16)
    pa, pq = _inproj0(x2, vec0, w_in, t, 512)
    pac, pqc = _inproj0(c2, vec0, w_in, None, 512)
    a = _conformer_conv(pa, ab_conv_w[0], ab_conv_g[0], ab_conv_b[0], t, 512)
    ac = _conformer_conv(pac, ab_conv_w[0], ab_conv_g[0], ab_conv_b[0], n_ctx, n_ctx)
    cos_t, sin_t = _rope_tables(t)
    qt, k, vt = _prep_qkv(pq, cos_t, sin_t, ab_q_g[0], ab_k_g[0], bsz, t)
    ones = jnp.ones((HEAD_DIM, n_ctx), F32)
    qtc, kc, vtc = _prep_qkv(pqc, ones, jnp.zeros_like(ones), ab_q_g[0], ab_k_g[0], bsz, n_ctx)
    o = _gqa_attention(qt, jnp.concatenate([kc, k], axis=1), jnp.concatenate([vtc, vt], axis=2), t)
    oc = _gqa_attention(qtc, kc, vtc, n_ctx)
    x2 = _outproj_ln([a, o], w_out, x2, vec0, 2, ln_g[0, 0], ln_b[0, 0], t, 512)
    c2 = _outproj_ln([ac, oc], w_out, c2, vec0, 2, ln_g[0, 0], ln_b[0, 0], None, 512)
    wg = ffn_w_gate[0].astype(BF16)
    wu = ffn_w_up[0].astype(BF16)
    wd = ffn_w_down[0].astype(BF16)
    x2 = _ffn_ln(x2, vec0, wg, wu, wd, ln_g[0, 1], ln_b[0, 1], t, 512, 1408)
    c2 = _ffn_ln(c2, vec0, wg, wu, wd, ln_g[0, 1], ln_b[0, 1], None, 512, 1408)

    w_qkv = na_w_qkv[0].astype(BF16)
    qt1, k1, vt1 = _inproj1(x2, vec1, w_qkv, t, 512, True)
    kc1, vct1 = _inproj1(c2, vec1, w_qkv[:, d:], None, 512, False)
    o = _neighbourhood_attention(qt1, k1, vt1, kc1, vct1, _na_bias(na_rpb[0], t // GRID_W, n_ctx), bsz, t)
    x2 = _outproj_ln([o], na_w_out[0].astype(BF16), x2, vec1, 2, ln_g[1, 0], ln_b[1, 0], t, 512)

    h, route = _router(x2, vec1, moe_w_router[0], t, 512)
    src, tile_expert, n_used, pos1, pos2 = _routing_plan(route, MOE_TM)
    ys = _moe_ffn(h, src, tile_expert, n_used, moe_w_gate[0].astype(BF16), moe_w_up[0].astype(BF16),
                  moe_w_down[0].astype(BF16))
    x2 = _combine_ln(ys, pos1, pos2, route, x2, vec1, ln_g[1, 1], ln_b[1, 1], t)
    return x2.reshape(bsz, t, d)
```

```python
import functools

import numpy as np
import jax
import jax.numpy as jnp
from jax import lax
from jax.experimental import pallas as pl
from jax.experimental.pallas import tpu as pltpu

F32 = jnp.float32
BF16 = jnp.bfloat16

D_MODEL = 1024
GRID_W = 64
HEAD_DIM = 64
CONV_CH = 512
CONV_WIDTH = 31
CONV_HALO = 16
Q_COLS = 512
KV_COLS = 128
A_COLS = 2 * CONV_CH
ROPE_THETA = 10000.0
WIN_R = 8
WIN_C = 16
N_EXPERTS = 8
DEPTH = 2
ALPHA = (2 * DEPTH) ** 0.25
LN_EPS = 1e-5
RMS_EPS = 1e-6
ATT_SCALE = HEAD_DIM ** -0.5
LOG2E = 1.4426950408889634
MASK_VALUE = -1e30

LANES = 128
ROW_TILES = D_MODEL // LANES
VMEM_LIMIT = 56 * 1024 * 1024

ATT_TQ = 256
ATT_SK = 256
ATT_CW = 256
VT_ROWS = HEAD_DIM + 16
NA_ROWS = 4
NA_KROWS = 12
NA_PAIRS = 4
MOE_TM = 512
MOE_TF = 1792
GATHER_ROWS = 256


def _cparams(*sem):
    return pltpu.CompilerParams(dimension_semantics=sem, vmem_limit_bytes=VMEM_LIMIT)


def _layer_norm(z, g, b):
    mu = jnp.mean(z, axis=-1, keepdims=True)
    zc = z - mu
    var = jnp.mean(zc * zc, axis=-1, keepdims=True)
    return zc * lax.rsqrt(var + LN_EPS) * g + b


def _vec_spec(k, tm, rows_per_batch):
    if rows_per_batch is None:
        return pl.BlockSpec((1, 1, D_MODEL), lambda i, *_: (4 * 6 + k, 0, 0))
    return pl.BlockSpec((1, 1, D_MODEL), lambda i, *_: ((i * tm // rows_per_batch) * 6 + k, 0, 0))


def _full_spec(shape):
    nd = len(shape)
    return pl.BlockSpec(shape, lambda *_: (0,) * nd)


def _mod_kernel(c_ref, w_ref, b_ref, o_ref):
    c = c_ref[...]
    s = c * jax.nn.sigmoid(c)
    o_ref[...] = jnp.dot(s.astype(BF16), w_ref[...].astype(BF16), preferred_element_type=F32) + b_ref[...]


def _modulation(cc, w_mod, b_mod):
    n = 6 * D_MODEL
    tn = D_MODEL
    return pl.pallas_call(
        _mod_kernel,
        grid=(DEPTH, n // tn),
        in_specs=[pl.BlockSpec((8, D_MODEL), lambda l, j: (0, 0)),
                  pl.BlockSpec((None, D_MODEL, tn), lambda l, j: (l, 0, j)),
                  pl.BlockSpec((None, 1, tn), lambda l, j: (l, 0, j))],
        out_specs=pl.BlockSpec((None, 8, tn), lambda l, j: (l, 0, j)),
        out_shape=jax.ShapeDtypeStruct((DEPTH, 8, n), F32),
        compiler_params=_cparams("parallel", "parallel"),
        name="modulation",
    )(cc, w_mod, b_mod.reshape(DEPTH, 1, n))


def _inproj0_kernel(x_ref, sc_ref, sh_ref, w_ref, oa_ref, oq_ref):
    h = x_ref[...] * (1.0 + sc_ref[0]) + sh_ref[0]
    o = jnp.dot(h.astype(BF16), w_ref[...], preferred_element_type=F32)
    oa_ref[...] = o[:, :A_COLS]
    oq_ref[...] = o[:, A_COLS:]


def _inproj0(x2d, vecs, w, rows_per_batch, tm):
    r = x2d.shape[0]
    nq = w.shape[1] - A_COLS
    return pl.pallas_call(
        _inproj0_kernel,
        grid=(r // tm,),
        in_specs=[pl.BlockSpec((tm, D_MODEL), lambda i: (i, 0)),
                  _vec_spec(1, tm, rows_per_batch), _vec_spec(0, tm, rows_per_batch),
                  _full_spec(w.shape)],
        out_specs=[pl.BlockSpec((tm, A_COLS), lambda i: (i, 0)),
                   pl.BlockSpec((tm, nq), lambda i: (i, 0))],
        out_shape=[jax.ShapeDtypeStruct((r, A_COLS), F32), jax.ShapeDtypeStruct((r, nq), F32)],
        compiler_params=_cparams("parallel"),
        name="inproj0",
    )(x2d, vecs, vecs, w)


def _conv_kernel(pm_ref, pp_ref, pn_ref, w_ref, g_ref, b_ref, o_ref, u_ref, us_ref, cv_ref, *, tt, tx):
    i = pl.program_id(0)

    def sigmoid(v):
        return 0.5 * jnp.tanh(0.5 * v) + 0.5

    def glu(p):
        return p[:, :CONV_CH] * sigmoid(p[:, CONV_CH:])

    first = (i * tt) % tx == 0
    last = ((i + 1) * tt) % tx == 0
    u_ref[0:CONV_HALO, :] = jnp.where(first, 0.0, glu(pp_ref[...]))
    u_ref[CONV_HALO:CONV_HALO + tt, :] = glu(pm_ref[...])
    u_ref[CONV_HALO + tt:2 * CONV_HALO + tt, :] = jnp.where(last, 0.0, glu(pn_ref[...]))
    u_ref[2 * CONV_HALO + tt:, :] = jnp.zeros((8, CONV_CH), F32)
    ch = 32
    nrow = tt + 2 * CONV_HALO
    base = CONV_HALO - CONV_WIDTH // 2

    def shift_body(c, carry):
        r0 = pl.multiple_of(c * ch, ch)
        w = u_ref[pl.ds(r0, ch + 8), :]
        for s in range(1, 8):
            us_ref[s - 1, pl.ds(r0, ch), :] = pltpu.roll(w, ch + 8 - s, axis=0)[0:ch]
        return carry

    lax.fori_loop(0, nrow // ch, shift_body, 0)

    def body(c, carry):
        r0 = pl.multiple_of(c * ch, ch)
        acc = jnp.zeros((ch, CONV_CH), F32)
        for k in range(CONV_WIDTH):
            a, s = divmod(k + base, 8)
            src = u_ref if s == 0 else us_ref.at[s - 1]
            acc = acc + src[pl.ds(r0 + 8 * a, ch), :] * w_ref[pl.ds(k, 1), :]
        cv_ref[pl.ds(r0, ch), :] = acc
        return carry

    lax.fori_loop(0, tt // ch, body, 0)
    y = _layer_norm(cv_ref[...], g_ref[...], b_ref[...])
    o_ref[...] = (y * sigmoid(y)).astype(BF16)


def _conformer_conv(pa, conv_w, conv_g, conv_b, tx, tt):
    r = pa.shape[0]
    hb = tt // CONV_HALO
    nhb = r // CONV_HALO
    return pl.pallas_call(
        functools.partial(_conv_kernel, tt=tt, tx=tx),
        grid=(r // tt,),
        in_specs=[pl.BlockSpec((tt, A_COLS), lambda i: (i, 0)),
                  pl.BlockSpec((CONV_HALO, A_COLS), lambda i: (jnp.maximum(i * hb - 1, 0), 0)),
                  pl.BlockSpec((CONV_HALO, A_COLS), lambda i: (jnp.minimum((i + 1) * hb, nhb - 1), 0)),
                  _full_spec((CONV_WIDTH, CONV_CH)), _full_spec((1, CONV_CH)), _full_spec((1, CONV_CH))],
        out_specs=pl.BlockSpec((tt, CONV_CH), lambda i: (i, 0)),
        out_shape=jax.ShapeDtypeStruct((r, CONV_CH), BF16),
        scratch_shapes=[pltpu.VMEM((tt + 2 * CONV_HALO + 8, CONV_CH), F32),
                        pltpu.VMEM((7, tt + 2 * CONV_HALO, CONV_CH), F32),
                        pltpu.VMEM((tt, CONV_CH), F32)],
        compiler_params=_cparams("parallel"),
        name="conformer_conv",
    )(pa, pa, pa, conv_w, conv_g.reshape(1, CONV_CH), conv_b.reshape(1, CONV_CH))


def _prep_kernel(p_ref, cos_ref, sin_ref, qg_ref, kg_ref, qt_ref, k_ref, vt_ref, *, tq):
    x = p_ref[...]
    cos = cos_ref[...]
    sin = sin_ref[...]

    def norm_rope(xh, g):
        ms = jnp.mean(xh * xh, axis=0, keepdims=True)
        y = xh * lax.rsqrt(ms + RMS_EPS) * g
        sw = jnp.concatenate([y[16:32], y[0:16], y[48:64], y[32:48]], axis=0)
        return y * cos + sw * sin

    zeros = jnp.zeros((HEAD_DIM, tq), BF16)
    for p in range(Q_COLS // LANES):
        xp = x[:, LANES * p:LANES * (p + 1)].T
        for half in range(2):
            h = 2 * p + half
            j, g = h // 4, h % 4
            r = (norm_rope(xp[HEAD_DIM * half:HEAD_DIM * (half + 1)], qg_ref[...]) * (ATT_SCALE * LOG2E)).astype(BF16)
            qt_ref[j, HEAD_DIM * j:HEAD_DIM * (j + 1), g * tq:(g + 1) * tq] = r
            qt_ref[j, HEAD_DIM * (1 - j):HEAD_DIM * (2 - j), g * tq:(g + 1) * tq] = zeros
    xk = x[:, Q_COLS:Q_COLS + KV_COLS].T
    k0 = norm_rope(xk[0:HEAD_DIM], kg_ref[...])
    k1 = norm_rope(xk[HEAD_DIM:2 * HEAD_DIM], kg_ref[...])
    k_ref[...] = jnp.concatenate([k0, k1], axis=0).T.astype(BF16)
    xv = x[:, Q_COLS + KV_COLS:].T.astype(BF16)
    ones = jnp.ones((VT_ROWS - HEAD_DIM, tq), BF16)
    for j in range(2):
        vt_ref[j, 0:HEAD_DIM, :] = xv[HEAD_DIM * j:HEAD_DIM * (j + 1)]
        vt_ref[j, HEAD_DIM:VT_ROWS, :] = ones


def _prep_qkv(pq, cos_t, sin_t, q_g, k_g, bx, tx):
    tq = ATT_TQ
    nq = tx // tq
    return pl.pallas_call(
        functools.partial(_prep_kernel, tq=tq),
        grid=(bx, nq),
        in_specs=[pl.BlockSpec((tq, Q_COLS + 2 * KV_COLS), lambda b, t: (b * nq + t, 0)),
                  pl.BlockSpec((HEAD_DIM, tq), lambda b, t: (0, t)),
                  pl.BlockSpec((HEAD_DIM, tq), lambda b, t: (0, t)),
                  _full_spec((HEAD_DIM, 1)), _full_spec((HEAD_DIM, 1))],
        out_specs=[pl.BlockSpec((None, None, 2, LANES, 4 * tq), lambda b, t: (b, t, 0, 0, 0)),
                   pl.BlockSpec((None, None, tq, LANES), lambda b, t: (b, t, 0, 0)),
                   pl.BlockSpec((None, 2, None, VT_ROWS, tq), lambda b, t: (b, 0, t, 0, 0))],
        out_shape=[jax.ShapeDtypeStruct((bx, nq, 2, LANES, 4 * tq), BF16),
                   jax.ShapeDtypeStruct((bx, nq, tq, LANES), BF16),
                   jax.ShapeDtypeStruct((bx, 2, nq, VT_ROWS, tq), BF16)],
        compiler_params=_cparams("parallel", "parallel"),
        name="prep_qkv",
    )(pq, cos_t, sin_t, q_g.reshape(HEAD_DIM, 1), k_g.reshape(HEAD_DIM, 1))


def _rope_tables(t):
    pos = np.arange(t)
    half = HEAD_DIM // 4
    inv = ROPE_THETA ** (-jnp.arange(half, dtype=F32) * 2.0 / (HEAD_DIM // 2))
    dd = np.arange(HEAD_DIM)
    part_pos = np.where((dd // (HEAD_DIM // 2))[:, None] == 0, (pos // GRID_W)[None, :], (pos % GRID_W)[None, :])
    ang = jnp.asarray(part_pos, F32) * inv[dd % half][:, None]
    sign = jnp.asarray(np.where((dd % (HEAD_DIM // 2)) < half, -1.0, 1.0)[:, None], F32)
    return jnp.cos(ang), jnp.sin(ang) * sign


def _attn_kernel(qt_ref, k_ref, vt_ref, o_ref, acc_ref, *, nk, tq):
    qt = qt_ref[...]
    ncol = 4 * tq
    acc_ref[...] = jnp.zeros_like(acc_ref)

    cw = ATT_CW
    nstrip = ncol // cw

    sk = ATT_SK
    per = tq // sk
    nsub = nk * per

    def scores(i, n):
        c, u = divmod(i, per)
        return jnp.dot(k_ref[c, sk * u:sk * (u + 1), :], qt[:, cw * n:cw * (n + 1)], preferred_element_type=F32)

    s = [scores(0, n) for n in range(nstrip)]
    m_prev = [jnp.full((1, cw), -jnp.inf, F32)] * nstrip
    m = [jnp.max(sn, axis=0, keepdims=True) for sn in s]
    for i in range(nsub):
        c, u = divmod(i, per)
        vt = vt_ref[c, :, sk * u:sk * (u + 1)]
        for n in range(nstrip):
            s_next = scores(i + 1, n) if i + 1 < nsub else None
            alpha = jnp.exp2(m_prev[n] - m[n])
            p = jnp.exp2((s[n] - m[n]).astype(BF16))
            pv = jnp.dot(vt, p, preferred_element_type=F32)
            acc_ref[:, cw * n:cw * (n + 1)] = acc_ref[:, cw * n:cw * (n + 1)] * alpha + pv
            if s_next is not None:
                m_prev[n] = m[n]
                m[n] = jnp.maximum(m[n], jnp.max(s_next, axis=0, keepdims=True))
                s[n] = s_next
    o = acc_ref[0:HEAD_DIM, :] * (1.0 / acc_ref[HEAD_DIM:HEAD_DIM + 1, :])
    for pp in range(2):
        blk = jnp.concatenate([o[:, (2 * pp) * tq:(2 * pp + 1) * tq],
                               o[:, (2 * pp + 1) * tq:(2 * pp + 2) * tq]], axis=0)
        o_ref[:, LANES * pp:LANES * (pp + 1)] = blk.T.astype(BF16)


def _gqa_attention(qt, k, vt, tx):
    bx, nq = qt.shape[0], qt.shape[1]
    nk = k.shape[1]
    tq = ATT_TQ
    return pl.pallas_call(
        functools.partial(_attn_kernel, nk=nk, tq=tq),
        grid=(bx, 2, nq),
        in_specs=[pl.BlockSpec((None, None, None, LANES, 4 * tq), lambda b, j, t: (b, t, j, 0, 0)),
                  pl.BlockSpec((None, nk, tq, LANES), lambda b, j, t: (b, 0, 0, 0)),
                  pl.BlockSpec((None, None, nk, VT_ROWS, tq), lambda b, j, t: (b, j, 0, 0, 0))],
        out_specs=pl.BlockSpec((tq, 2 * LANES), lambda b, j, t: (b * nq + t, j)),
        out_shape=jax.ShapeDtypeStruct((bx * tx, Q_COLS), BF16),
        scratch_shapes=[pltpu.VMEM((VT_ROWS, 4 * tq), F32)],
        compiler_params=_cparams("parallel", "parallel", "parallel"),
        name="gqa_attention",
    )(qt, k, vt)


def _outproj_ln_kernel(*refs, n_lhs):
    lhs = refs[:n_lhs]
    w_ref, x_ref, gate_ref, lng_ref, lnb_ref, o_ref = refs[n_lhs:]
    y = None
    off = 0
    for r in lhs:
        kk = r.shape[1]
        t = jnp.dot(r[...], w_ref[off:off + kk, :], preferred_element_type=F32)
        off += kk
        y = t if y is None else y + t
    z = ALPHA * x_ref[...] + gate_ref[0] * y
    o_ref[...] = _layer_norm(z, lng_ref[...], lnb_ref[...])


def _outproj_ln(lhs, w, x2d, vecs, gate_k, ln_g, ln_b, rows_per_batch, tm):
    r = x2d.shape[0]
    return pl.pallas_call(
        functools.partial(_outproj_ln_kernel, n_lhs=len(lhs)),
        grid=(r // tm,),
        in_specs=[pl.BlockSpec((tm, a.shape[1]), lambda i: (i, 0)) for a in lhs]
        + [_full_spec(w.shape), pl.BlockSpec((tm, D_MODEL), lambda i: (i, 0)),
           _vec_spec(gate_k, tm, rows_per_batch), _full_spec((1, D_MODEL)), _full_spec((1, D_MODEL))],
        out_specs=pl.BlockSpec((tm, D_MODEL), lambda i: (i, 0)),
        out_shape=jax.ShapeDtypeStruct((r, D_MODEL), F32),
        compiler_params=_cparams("parallel"),
        name="outproj_ln",
    )(*lhs, w, x2d, vecs, ln_g.reshape(1, D_MODEL), ln_b.reshape(1, D_MODEL))


def _ffn_kernel(x_ref, sc_ref, sh_ref, gate_ref, wg_ref, wu_ref, wd_ref, lng_ref, lnb_ref, o_ref, hb_ref, acc_ref):
    j = pl.program_id(1)

    @pl.when(j == 0)
    def _():
        hb_ref[...] = (x_ref[...] * (1.0 + sc_ref[0]) + sh_ref[0]).astype(BF16)
        acc_ref[...] = jnp.zeros_like(acc_ref)

    hb = hb_ref[...]
    hg = jnp.dot(hb, wg_ref[...], preferred_element_type=F32)
    hu = jnp.dot(hb, wu_ref[...], preferred_element_type=F32)
    h1 = (hg * jax.nn.sigmoid(hg) * hu).astype(BF16)
    acc_ref[...] += jnp.dot(h1, wd_ref[...], preferred_element_type=F32)

    @pl.when(j == pl.num_programs(1) - 1)
    def _():
        z = ALPHA * x_ref[...] + gate_ref[0] * acc_ref[...]
        o_ref[...] = _layer_norm(z, lng_ref[...], lnb_ref[...])


def _ffn_ln(x2d, vecs, wg, wu, wd, ln_g, ln_b, rows_per_batch, tm, tf):
    r = x2d.shape[0]
    f = wg.shape[1]
    return pl.pallas_call(
        _ffn_kernel,
        grid=(r // tm, f // tf),
        in_specs=[pl.BlockSpec((tm, D_MODEL), lambda i, j: (i, 0)),
                  _vec_spec(4, tm, rows_per_batch), _vec_spec(3, tm, rows_per_batch),
                  _vec_spec(5, tm, rows_per_batch),
                  pl.BlockSpec((D_MODEL, tf), lambda i, j: (0, j)),
                  pl.BlockSpec((D_MODEL, tf), lambda i, j: (0, j)),
                  pl.BlockSpec((tf, D_MODEL), lambda i, j: (j, 0)),
                  _full_spec((1, D_MODEL)), _full_spec((1, D_MODEL))],
        out_specs=pl.BlockSpec((tm, D_MODEL), lambda i, j: (i, 0)),
        out_shape=jax.ShapeDtypeStruct((r, D_MODEL), F32),
        scratch_shapes=[pltpu.VMEM((tm, D_MODEL), BF16), pltpu.VMEM((tm, D_MODEL), F32)],
        compiler_params=_cparams("parallel", "arbitrary"),
        name="ffn_ln",
    )(x2d, vecs, vecs, vecs, wg, wu, wd, ln_g.reshape(1, D_MODEL), ln_b.reshape(1, D_MODEL))


def _inproj1_kernel(x_ref, sc_ref, sh_ref, w_ref, *o_refs, with_q):
    h = x_ref[...] * (1.0 + sc_ref[0]) + sh_ref[0]
    o = jnp.dot(h.astype(BF16), w_ref[...], preferred_element_type=F32)
    if with_q:
        qt_ref, k_ref, vt_ref = o_refs
        qt_ref[...] = (o[:, :D_MODEL] * (ATT_SCALE * LOG2E)).T.astype(BF16)
    else:
        k_ref, vt_ref = o_refs
    nk = o.shape[1] - 2 * D_MODEL
    k_ref[...] = o[:, nk:nk + D_MODEL].astype(BF16)
    vt_ref[...] = o[:, nk + D_MODEL:].T.astype(BF16)


def _inproj1(x2d, vecs, w, rows_per_batch, tm, with_q):
    r = x2d.shape[0]
    nat = pl.BlockSpec((tm, D_MODEL), lambda i: (i, 0))
    tr = pl.BlockSpec((D_MODEL, tm), lambda i: (0, i))
    nat_shape = jax.ShapeDtypeStruct((r, D_MODEL), BF16)
    tr_shape = jax.ShapeDtypeStruct((D_MODEL, r), BF16)
    return pl.pallas_call(
        functools.partial(_inproj1_kernel, with_q=with_q),
        grid=(r // tm,),
        in_specs=[pl.BlockSpec((tm, D_MODEL), lambda i: (i, 0)),
                  _vec_spec(1, tm, rows_per_batch), _vec_spec(0, tm, rows_per_batch),
                  _full_spec(w.shape)],
        out_specs=([tr] if with_q else []) + [nat, tr],
        out_shape=([tr_shape] if with_q else []) + [nat_shape, tr_shape],
        compiler_params=_cparams("parallel"),
        name="inproj1",
    )(x2d, vecs, vecs, w)


def _na_bias(rpb, n_rows, n_ctx):
    nh = rpb.shape[0]
    c = np.arange(GRID_W)
    cs = np.clip(c - WIN_C // 2, 0, GRID_W - WIN_C)
    in_c = (c[None, :] >= cs[:, None]) & (c[None, :] < cs[:, None] + WIN_C)
    dc = c[None, :] - c[:, None] + WIN_C - 1
    pick = ((dc[None] == np.arange(2 * WIN_C - 1)[:, None, None]) & in_c[None]).astype(np.float32)
    cols = jnp.einsum("hrd,dkc->hrkc", rpb, jnp.asarray(pick.transpose(0, 2, 1)), precision=lax.Precision.HIGHEST)
    cols = jnp.where(jnp.asarray(in_c.T)[None, None], cols * LOG2E, MASK_VALUE)
    n_dr = 2 * WIN_R - 1
    cols = jnp.concatenate([cols, jnp.full((nh, 1, GRID_W, GRID_W), MASK_VALUE, F32)], axis=1)
    pick_dr = np.full((3, NA_ROWS, NA_KROWS), n_dr, np.int32)
    for case, r0 in enumerate((0, NA_ROWS, n_rows - NA_ROWS)):
        start = int(np.clip(r0 - WIN_R // 2, 0, n_rows - NA_KROWS))
        for ri in range(NA_ROWS):
            r = r0 + ri
            rs = int(np.clip(r - WIN_R // 2, 0, n_rows - WIN_R))
            for ki in range(NA_KROWS):
                kr = start + ki
                if rs <= kr < rs + WIN_R:
                    pick_dr[case, ri, ki] = kr - r + WIN_R - 1
    npair = nh // 2
    nkeys = NA_KROWS * GRID_W
    out = pl.pallas_call(
        functools.partial(_na_bias_kernel, nkeys=nkeys),
        grid_spec=pltpu.PrefetchScalarGridSpec(
            num_scalar_prefetch=1,
            grid=(3, npair),
            in_specs=[pl.BlockSpec((2, n_dr + 1, GRID_W, GRID_W), lambda case, p, dr: (p, 0, 0, 0))],
            out_specs=pl.BlockSpec((None, None, nkeys + n_ctx, 2 * NA_ROWS * GRID_W),
                                   lambda case, p, dr: (case, p, 0, 0)),
        ),
        out_shape=jax.ShapeDtypeStruct((3, npair, nkeys + n_ctx, 2 * NA_ROWS * GRID_W), F32),
        compiler_params=_cparams("parallel", "parallel"),
        name="na_bias",
    )(jnp.asarray(pick_dr.reshape(-1)), cols)
    return out.reshape(3, npair // NA_PAIRS, NA_PAIRS, nkeys + n_ctx, 2 * NA_ROWS * GRID_W)


def _na_bias_kernel(dr_ref, cols_ref, o_ref, *, nkeys):
    case = pl.program_id(0)
    for ki in range(NA_KROWS):
        pieces = []
        for e in range(2):
            for ri in range(NA_ROWS):
                d = dr_ref[(case * NA_ROWS + ri) * NA_KROWS + ki]
                pieces.append(cols_ref[e, d])
        o_ref[GRID_W * ki:GRID_W * (ki + 1), :] = jnp.concatenate(pieces, axis=1)
    o_ref[nkeys:, :] = jnp.zeros((o_ref.shape[0] - nkeys, o_ref.shape[1]), F32)


def _na_kernel(qt_ref, k0_ref, k1_ref, k2_ref, kc_ref, v0_ref, v1_ref, v2_ref, vc_ref, bias_ref, o_ref):
    qt = qt_ref[...]
    kk = jnp.concatenate([k0_ref[...], k1_ref[...], k2_ref[...], kc_ref[...]], axis=0)
    vt = jnp.concatenate([v0_ref[...], v1_ref[...], v2_ref[...], vc_ref[...]], axis=1)
    nq = qt.shape[1]
    zeros = jnp.zeros((HEAD_DIM, nq), BF16)
    ones = jnp.ones((VT_ROWS - HEAD_DIM, kk.shape[0]), BF16)

    def scores(pp):
        q2 = qt[LANES * pp:LANES * (pp + 1)]
        qcat = jnp.concatenate([jnp.concatenate([q2[0:HEAD_DIM], zeros], axis=0),
                                jnp.concatenate([zeros, q2[HEAD_DIM:]], axis=0)], axis=1)
        return jnp.dot(kk[:, LANES * pp:LANES * (pp + 1)], qcat, preferred_element_type=F32) + bias_ref[pp]

    s_all = [scores(pp) for pp in range(NA_PAIRS)]
    for pp in range(NA_PAIRS):
        s = s_all[pp]
        m = jnp.max(s, axis=0, keepdims=True)
        p = jnp.exp2((s - m).astype(BF16))
        vext = jnp.concatenate([vt[LANES * pp:LANES * (pp + 1)], ones], axis=0)
        pv = jnp.dot(vext, p, preferred_element_type=F32)
        o0 = pv[0:HEAD_DIM, 0:nq] * (1.0 / pv[LANES:LANES + 1, 0:nq])
        o1 = pv[HEAD_DIM:LANES, nq:] * (1.0 / pv[LANES:LANES + 1, nq:])
        o_ref[:, LANES * pp:LANES * (pp + 1)] = jnp.concatenate([o0, o1], axis=0).T.astype(BF16)


def _neighbourhood_attention(qt, k, vt, kc, vct, bias, bsz, t):
    nq = NA_ROWS * GRID_W
    nrb = t // nq
    wl = NA_PAIRS * LANES
    ngrp = D_MODEL // wl
    nctx = kc.shape[0] // bsz
    nwin = NA_KROWS // NA_ROWS

    def first_kblock(b, rb):
        return b * nrb + jnp.clip(rb - 1, 0, nrb - nwin)

    def kspec(d):
        return pl.BlockSpec((nq, wl), lambda hp, b, rb: (first_kblock(b, rb) + d, hp))

    def vspec(d):
        return pl.BlockSpec((wl, nq), lambda hp, b, rb: (hp, first_kblock(b, rb) + d))

    def case(rb):
        return jnp.where(rb == 0, 0, jnp.where(rb == nrb - 1, 2, 1))

    return pl.pallas_call(
        _na_kernel,
        grid=(ngrp, bsz, nrb),
        in_specs=[pl.BlockSpec((wl, nq), lambda hp, b, rb: (hp, b * nrb + rb)),
                  kspec(0), kspec(1), kspec(2), pl.BlockSpec((nctx, wl), lambda hp, b, rb: (b, hp)),
                  vspec(0), vspec(1), vspec(2), pl.BlockSpec((wl, nctx), lambda hp, b, rb: (hp, b)),
                  pl.BlockSpec((None, None, NA_PAIRS, NA_KROWS * GRID_W + nctx, 2 * nq),
                               lambda hp, b, rb: (case(rb), hp, 0, 0, 0))],
        out_specs=pl.BlockSpec((nq, wl), lambda hp, b, rb: (b * nrb + rb, hp)),
        out_shape=jax.ShapeDtypeStruct((bsz * t, D_MODEL), BF16),
        compiler_params=_cparams("parallel", "parallel", "parallel"),
        name="neighbourhood_attention",
    )(qt, k, k, k, kc, vt, vt, vt, vct, bias)


def _store_row_tiles(dst_ref, val):
    rows = val.shape[0]
    for lt in range(ROW_TILES):
        dst_ref[pl.ds(lt, rows, stride=ROW_TILES), :] = val[:, lt * LANES:(lt + 1) * LANES]


def _load_row_tiles(src_ref, lt, rows):
    return src_ref[pl.ds(lt, rows, stride=ROW_TILES), :]


def _router_kernel(x_ref, sc_ref, sh_ref, wr_ref, h_ref, r_ref):
    h = x_ref[...] * (1.0 + sc_ref[0]) + sh_ref[0]
    _store_row_tiles(h_ref, h)
    hi = h.astype(BF16)
    lo = (h - hi.astype(F32)).astype(BF16)
    w = wr_ref[...]
    whi = w.astype(BF16)
    wlo = (w - whi.astype(F32)).astype(BF16)
    lg = (jnp.dot(hi, whi, preferred_element_type=F32)
          + (jnp.dot(hi, wlo, preferred_element_type=F32) + jnp.dot(lo, whi, preferred_element_type=F32)))
    lane = lax.broadcasted_iota(jnp.int32, lg.shape, 1).astype(F32)
    lg = jnp.where(lane < N_EXPERTS, lg, -jnp.inf)
    v1 = jnp.max(lg, axis=1, keepdims=True)
    i1 = jnp.min(jnp.where(lg == v1, lane, float(LANES)), axis=1, keepdims=True)
    lg2 = jnp.where(lane == i1, -jnp.inf, lg)
    v2 = jnp.max(lg2, axis=1, keepdims=True)
    i2 = jnp.min(jnp.where(lg2 == v2, lane, float(LANES)), axis=1, keepdims=True)
    e = jnp.exp(v2 - v1)
    w1 = 1.0 / (1.0 + e)
    w2 = e / (1.0 + e)
    r_ref[...] = jnp.where(lane == 0, i1, jnp.where(lane == 1, i2, jnp.where(lane == 2, w1,
                                                                              jnp.where(lane == 3, w2, 0.0))))


def _router(x2d, vecs, w_router, rows_per_batch, tm):
    r = x2d.shape[0]
    wr = jnp.pad(w_router, ((0, 0), (0, LANES - N_EXPERTS)))
    return pl.pallas_call(
        _router_kernel,
        grid=(r // tm,),
        in_specs=[pl.BlockSpec((tm, D_MODEL), lambda i: (i, 0)),
                  _vec_spec(4, tm, rows_per_batch), _vec_spec(3, tm, rows_per_batch),
                  _full_spec((D_MODEL, LANES))],
        out_specs=[pl.BlockSpec((tm * ROW_TILES, LANES), lambda i: (i, 0)),
                   pl.BlockSpec((tm, LANES), lambda i: (i, 0))],
        out_shape=[jax.ShapeDtypeStruct((r * ROW_TILES, LANES), F32), jax.ShapeDtypeStruct((r, LANES), F32)],
        compiler_params=_cparams("parallel"),
        name="router",
    )(x2d, vecs, vecs, wr)


def _row_copy(src_hbm, row, r, dst_ref, sem):
    def tile(i):
        start = i * ROW_TILES
        return pl.ds(start if isinstance(i, int) else pl.multiple_of(start, ROW_TILES), ROW_TILES)

    return pltpu.make_async_copy(src_hbm.at[tile(row), :], dst_ref.at[tile(r), :], sem)


def _start_row_gather(src_hbm, idx_ref, dst_ref, sem, n):
    def issue(i, carry):
        for u in range(2):
            r = 2 * i + u
            _row_copy(src_hbm, idx_ref[0, r], r, dst_ref, sem).start(priority=u)
        return carry

    lax.fori_loop(0, n // 2, issue, 0, unroll=4)


def _wait_row_gather(src_hbm, dst_ref, sem, n):
    def wait(r, carry):
        _row_copy(src_hbm, 0, r, dst_ref, sem).wait()
        return carry

    lax.fori_loop(0, n, wait, 0, unroll=8)


def _moe_ffn_kernel(te_ref, nu_ref, idx0_ref, idxn_ref, h_hbm, wg_ref, wu_ref, wd_ref, o_ref,
                    xbuf_ref, hb_ref, acc_ref, sem):
    t = pl.program_id(0)
    j = pl.program_id(1)
    nt = pl.num_programs(0)
    nj = pl.num_programs(1)
    used = t < nu_ref[0]
    tm = hb_ref.shape[0]
    slot = t % 2

    @pl.when(j == 0)
    def _():
        @pl.when(t == 0)
        def _():
            _start_row_gather(h_hbm, idx0_ref, xbuf_ref.at[0], sem.at[0], tm)

        @pl.when(t + 1 < nt)
        def _():
            _start_row_gather(h_hbm, idxn_ref, xbuf_ref.at[1 - slot], sem.at[1 - slot], tm)

        _wait_row_gather(h_hbm, xbuf_ref.at[slot], sem.at[slot], tm)

    @pl.when(jnp.logical_and(used, j == 0))
    def _():
        for lt in range(ROW_TILES):
            hb_ref[:, lt * LANES:(lt + 1) * LANES] = _load_row_tiles(xbuf_ref.at[slot], lt, tm).astype(BF16)
        acc_ref[...] = jnp.zeros_like(acc_ref)

    @pl.when(used)
    def _():
        hb = hb_ref[...]
        hg = jnp.dot(hb, wg_ref[...], preferred_element_type=F32)
        hu = jnp.dot(hb, wu_ref[...], preferred_element_type=F32)
        h1 = (hg * jax.nn.sigmoid(hg) * hu).astype(BF16)
        acc_ref[...] += jnp.dot(h1, wd_ref[...], preferred_element_type=F32)

    @pl.when(jnp.logical_and(used, j == nj - 1))
    def _():
        _store_row_tiles(o_ref, acc_ref[...])

    @pl.when(jnp.logical_and(jnp.logical_not(used), j == nj - 1))
    def _():
        o_ref[...] = jnp.zeros_like(o_ref)


def _moe_ffn(h, src, tile_expert, n_used, wg, wu, wd):
    p = src.shape[0]
    tm, tf = MOE_TM, MOE_TF
    nt = p // tm
    f = wg.shape[2]

    def jj(t, j, nu):
        return jnp.where(t < nu[0], j, 0)

    grid_spec = pltpu.PrefetchScalarGridSpec(
        num_scalar_prefetch=2,
        grid=(nt, f // tf),
        in_specs=[pl.BlockSpec((None, 1, tm), lambda t, j, te, nu: (0, 0, 0), memory_space=pltpu.SMEM),
                  pl.BlockSpec((None, 1, tm), lambda t, j, te, nu: (jnp.minimum(t + 1, nt - 1), 0, 0),
                               memory_space=pltpu.SMEM),
                  pl.BlockSpec(memory_space=pl.ANY),
                  pl.BlockSpec((None, D_MODEL, tf), lambda t, j, te, nu: (te[t], 0, jj(t, j, nu))),
                  pl.BlockSpec((None, D_MODEL, tf), lambda t, j, te, nu: (te[t], 0, jj(t, j, nu))),
                  pl.BlockSpec((None, tf, D_MODEL), lambda t, j, te, nu: (te[t], jj(t, j, nu), 0))],
        out_specs=pl.BlockSpec((tm * ROW_TILES, LANES), lambda t, j, te, nu: (t, 0)),
        scratch_shapes=[pltpu.VMEM((2, tm * ROW_TILES, LANES), F32), pltpu.VMEM((tm, D_MODEL), BF16),
                        pltpu.VMEM((tm, D_MODEL), F32), pltpu.SemaphoreType.DMA((2,))],
    )
    idx = src.reshape(nt, 1, tm)
    return pl.pallas_call(
        _moe_ffn_kernel,
        grid_spec=grid_spec,
        out_shape=jax.ShapeDtypeStruct((p * ROW_TILES, LANES), F32),
        compiler_params=_cparams("arbitrary", "arbitrary"),
        name="moe_ffn",
    )(tile_expert, n_used, idx, idx, h, wg, wu, wd)


def _combine_ln_kernel(p1_ref, p2_ref, ys_hbm, route_ref, x_ref, gate_ref, lng_ref, lnb_ref, o_ref,
                       y1_ref, y2_ref, sem):
    n = x_ref.shape[0]
    _start_row_gather(ys_hbm, p1_ref, y1_ref, sem.at[0], n)
    _start_row_gather(ys_hbm, p2_ref, y2_ref, sem.at[1], n)
    _wait_row_gather(ys_hbm, y1_ref, sem.at[0], n)
    _wait_row_gather(ys_hbm, y2_ref, sem.at[1], n)
    w1 = route_ref[:, 2:3]
    w2 = route_ref[:, 3:4]
    y = jnp.concatenate([w1 * _load_row_tiles(y1_ref, lt, n) + w2 * _load_row_tiles(y2_ref, lt, n)
                         for lt in range(ROW_TILES)], axis=1)
    z = ALPHA * x_ref[...] + gate_ref[0] * y
    o_ref[...] = _layer_norm(z, lng_ref[...], lnb_ref[...])


def _combine_ln(ys, pos1, pos2, route, x2d, vecs, ln_g, ln_b, rows_per_batch):
    r = x2d.shape[0]
    g = GATHER_ROWS
    idx_spec = pl.BlockSpec((None, 1, g), lambda i: (i, 0, 0), memory_space=pltpu.SMEM)
    return pl.pallas_call(
        _combine_ln_kernel,
        grid=(r // g,),
        in_specs=[idx_spec, idx_spec, pl.BlockSpec(memory_space=pl.ANY),
                  pl.BlockSpec((g, LANES), lambda i: (i, 0)),
                  pl.BlockSpec((g, D_MODEL), lambda i: (i, 0)),
                  _vec_spec(5, g, rows_per_batch), _full_spec((1, D_MODEL)), _full_spec((1, D_MODEL))],
        out_specs=pl.BlockSpec((g, D_MODEL), lambda i: (i, 0)),
        out_shape=jax.ShapeDtypeStruct((r, D_MODEL), F32),
        scratch_shapes=[pltpu.VMEM((g * ROW_TILES, LANES), F32), pltpu.VMEM((g * ROW_TILES, LANES), F32),
                        pltpu.SemaphoreType.DMA((2,))],
        compiler_params=_cparams("arbitrary"),
        name="combine_ln",
    )(pos1.reshape(r // g, 1, g), pos2.reshape(r // g, 1, g), ys, route, x2d, vecs,
      ln_g.reshape(1, D_MODEL), ln_b.reshape(1, D_MODEL))


def _routing_plan(route, tm):
    n = route.shape[0]
    e = jnp.concatenate([route[:, 0], route[:, 1]]).astype(jnp.int32)
    tok = jnp.concatenate([jnp.arange(n, dtype=jnp.int32)] * 2)
    onehot = (e[:, None] == jnp.arange(N_EXPERTS, dtype=jnp.int32)[None, :]).astype(jnp.int32)
    csum = jnp.cumsum(onehot, axis=0)
    rank = jnp.sum(csum * onehot, axis=1) - 1
    counts = csum[-1]
    padded = ((counts + tm - 1) // tm) * tm
    ends = jnp.cumsum(padded)
    starts = ends - padded
    pos = jnp.sum(starts[None, :] * onehot, axis=1) + rank
    p = 2 * n + N_EXPERTS * tm
    src = jnp.zeros((p,), jnp.int32).at[pos].set(tok)
    tile_start = jnp.arange(p // tm, dtype=jnp.int32) * tm
    tile_expert = jnp.minimum(jnp.sum((tile_start[:, None] >= ends[None, :]).astype(jnp.int32), axis=1),
                              N_EXPERTS - 1).astype(jnp.int32)
    n_used = (ends[-1] // tm).astype(jnp.int32).reshape(1)
    return src, tile_expert, n_used, pos[:n], pos[n:]


def kernel(x, c, ctx, c_ctx, w_mod, b_mod, ln_g, ln_b, ab_w_in, ab_conv_w, ab_conv_g, ab_conv_b, ab_q_g, ab_k_g,
           ab_w_out, ffn_w_gate, ffn_w_up, ffn_w_down, na_w_qkv, na_rpb, na_w_out, moe_w_router, moe_w_gate,
           moe_w_up, moe_w_down):
    bsz, t, d = x.shape
    n_ctx = ctx.shape[1]
    n = bsz * t
    nc = bsz * n_ctx
    x2 = x.reshape(n, d)
    c2 = ctx.reshape(nc, d)

    cc = jnp.concatenate([c, c_ctx[None, :], jnp.zeros((8 - bsz - 1, d), F32)], axis=0)
    mod = _modulation(cc, w_mod, b_mod)
    vec0 = mod[0].reshape(8 * 6, 1, d)
    vec1 = mod[1].reshape(8 * 6, 1, d)

    w_in = ab_w_in[0].astype(BF16)
    w_out = ab_w_out[0].astype(BF16)
    pa, pq = _inproj0(x2, vec0, w_in, t, 512)
    pac, pqc = _inproj0(c2, vec0, w_in, None, 512)
    a = _conformer_conv(pa, ab_conv_w[0], ab_conv_g[0], ab_conv_b[0], t, 512)
    ac = _conformer_conv(pac, ab_conv_w[0], ab_conv_g[0], ab_conv_b[0], n_ctx, n_ctx)
    cos_t, sin_t = _rope_tables(t)
    qt, k, vt = _prep_qkv(pq, cos_t, sin_t, ab_q_g[0], ab_k_g[0], bsz, t)
    ones = jnp.ones((HEAD_DIM, n_ctx), F32)
    qtc, kc, vtc = _prep_qkv(pqc, ones, jnp.zeros_like(ones), ab_q_g[0], ab_k_g[0], bsz, n_ctx)
    o = _gqa_attention(qt, jnp.concatenate([kc, k], axis=1), jnp.concatenate([vtc, vt], axis=2), t)
    oc = _gqa_attention(qtc, kc, vtc, n_ctx)
    x2 = _outproj_ln([a, o], w_out, x2, vec0, 2, ln_g[0, 0], ln_b[0, 0], t, 512)
    c2 = _outproj_ln([ac, oc], w_out, c2, vec0, 2, ln_g[0, 0], ln_b[0, 0], None, 512)
    wg = ffn_w_gate[0].astype(BF16)
    wu = ffn_w_up[0].astype(BF16)
    wd = ffn_w_down[0].astype(BF16)
    x2 = _ffn_ln(x2, vec0, wg, wu, wd, ln_g[0, 1], ln_b[0, 1], t, 512, 1408)
    c2 = _ffn_ln(c2, vec0, wg, wu, wd, ln_g[0, 1], ln_b[0, 1], None, 512, 1408)

    w_qkv = na_w_qkv[0].astype(BF16)
    qt1, k1, vt1 = _inproj1(x2, vec1, w_qkv, t, 512, True)
    kc1, vct1 = _inproj1(c2, vec1, w_qkv[:, d:], None, 512, False)
    o = _neighbourhood_attention(qt1, k1, vt1, kc1, vct1, _na_bias(na_rpb[0], t // GRID_W, n_ctx), bsz, t)
    x2 = _outproj_ln([o], na_w_out[0].astype(BF16), x2, vec1, 2, ln_g[1, 0], ln_b[1, 0], t, 512)

    h, route = _router(x2, vec1, moe_w_router[0], t, 512)
    src, tile_expert, n_used, pos1, pos2 = _routing_plan(route, MOE_TM)
    ys = _moe_ffn(h, src, tile_expert, n_used, moe_w_gate[0].astype(BF16), moe_w_up[0].astype(BF16),
                  moe_w_down[0].astype(BF16))
    x2 = _combine_ln(ys, pos1, pos2, route, x2, vec1, ln_g[1, 1], ln_b[1, 1], t)
    return x2.reshape(bsz, t, d)
```

```python
import functools

import numpy as np
import jax
import jax.numpy as jnp
from jax import lax
from jax.experimental import pallas as pl
from jax.experimental.pallas import tpu as pltpu

F32 = jnp.float32
BF16 = jnp.bfloat16

D_MODEL = 1024
GRID_W = 64
HEAD_DIM = 64
CONV_CH = 512
CONV_WIDTH = 31
CONV_HALO = 16
Q_COLS = 512
KV_COLS = 128
A_COLS = 2 * CONV_CH
ROPE_THETA = 10000.0
WIN_R = 8
WIN_C = 16
N_EXPERTS = 8
DEPTH = 2
ALPHA = (2 * DEPTH) ** 0.25
LN_EPS = 1e-5
RMS_EPS = 1e-6
ATT_SCALE = HEAD_DIM ** -0.5
LOG2E = 1.4426950408889634
MASK_VALUE = -1e30

LANES = 128
ROW_TILES = D_MODEL // LANES
VMEM_LIMIT = 56 * 1024 * 1024

ATT_TQ = 256
ATT_SK = 256
ATT_CW = 256
VT_ROWS = HEAD_DIM + 16
NA_ROWS = 4
NA_KROWS = 12
NA_PAIRS = 4
MOE_TM = 512
MOE_TF = 1792
GATHER_ROWS = 256


def _cparams(*sem):
    return pltpu.CompilerParams(dimension_semantics=sem, vmem_limit_bytes=VMEM_LIMIT)


def _layer_norm(z, g, b):
    mu = jnp.mean(z, axis=-1, keepdims=True)
    zc = z - mu
    var = jnp.mean(zc * zc, axis=-1, keepdims=True)
    return zc * lax.rsqrt(var + LN_EPS) * g + b


def _vec_spec(k, tm, rows_per_batch):
    if rows_per_batch is None:
        return pl.BlockSpec((1, 1, D_MODEL), lambda i, *_: (4 * 6 + k, 0, 0))
    return pl.BlockSpec((1, 1, D_MODEL), lambda i, *_: ((i * tm // rows_per_batch) * 6 + k, 0, 0))


def _full_spec(shape):
    nd = len(shape)
    return pl.BlockSpec(shape, lambda *_: (0,) * nd)


def _mod_kernel(c_ref, w_ref, b_ref, o_ref):
    c = c_ref[...]
    s = c * jax.nn.sigmoid(c)
    o_ref[...] = jnp.dot(s.astype(BF16), w_ref[...].astype(BF16), preferred_element_type=F32) + b_ref[...]


def _modulation(cc, w_mod, b_mod):
    n = 6 * D_MODEL
    tn = D_MODEL
    return pl.pallas_call(
        _mod_kernel,
        grid=(DEPTH, n // tn),
        in_specs=[pl.BlockSpec((8, D_MODEL), lambda l, j: (0, 0)),
                  pl.BlockSpec((None, D_MODEL, tn), lambda l, j: (l, 0, j)),
                  pl.BlockSpec((None, 1, tn), lambda l, j: (l, 0, j))],
        out_specs=pl.BlockSpec((None, 8, tn), lambda l, j: (l, 0, j)),
        out_shape=jax.ShapeDtypeStruct((DEPTH, 8, n), F32),
        compiler_params=_cparams("parallel", "parallel"),
        name="modulation",
    )(cc, w_mod, b_mod.reshape(DEPTH, 1, n))


def _inproj0_kernel(x_ref, sc_ref, sh_ref, w_ref, oa_ref, oq_ref):
    h = x_ref[...] * (1.0 + sc_ref[0]) + sh_ref[0]
    o = jnp.dot(h.astype(BF16), w_ref[...], preferred_element_type=F32)
    oa_ref[...] = o[:, :A_COLS]
    oq_ref[...] = o[:, A_COLS:]


def _inproj0(x2d, vecs, w, rows_per_batch, tm):
    r = x2d.shape[0]
    nq = w.shape[1] - A_COLS
    return pl.pallas_call(
        _inproj0_kernel,
        grid=(r // tm,),
        in_specs=[pl.BlockSpec((tm, D_MODEL), lambda i: (i, 0)),
                  _vec_spec(1, tm, rows_per_batch), _vec_spec(0, tm, rows_per_batch),
                  _full_spec(w.shape)],
        out_specs=[pl.BlockSpec((tm, A_COLS), lambda i: (i, 0)),
                   pl.BlockSpec((tm, nq), lambda i: (i, 0))],
        out_shape=[jax.ShapeDtypeStruct((r, A_COLS), F32), jax.ShapeDtypeStruct((r, nq), F32)],
        compiler_params=_cparams("parallel"),
        name="inproj0",
    )(x2d, vecs, vecs, w)


def _conv_kernel(pm_ref, pp_ref, pn_ref, w_ref, g_ref, b_ref, o_ref, u_ref, us_ref, cv_ref, *, tt, tx):
    i = pl.program_id(0)

    def sigmoid(v):
        return 0.5 * jnp.tanh(0.5 * v) + 0.5

    def glu(p):
        return p[:, :CONV_CH] * sigmoid(p[:, CONV_CH:])

    first = (i * tt) % tx == 0
    last = ((i + 1) * tt) % tx == 0
    u_ref[0:CONV_HALO, :] = jnp.where(first, 0.0, glu(pp_ref[...]))
    u_ref[CONV_HALO:CONV_HALO + tt, :] = glu(pm_ref[...])
    u_ref[CONV_HALO + tt:2 * CONV_HALO + tt, :] = jnp.where(last, 0.0, glu(pn_ref[...]))
    u_ref[2 * CONV_HALO + tt:, :] = jnp.zeros((8, CONV_CH), F32)
    ch = 32
    nrow = tt + 2 * CONV_HALO
    base = CONV_HALO - CONV_WIDTH // 2

    def shift_body(c, carry):
        r0 = pl.multiple_of(c * ch, ch)
        w = u_ref[pl.ds(r0, ch + 8), :]
        for s in range(1, 8):
            us_ref[s - 1, pl.ds(r0, ch), :] = pltpu.roll(w, ch + 8 - s, axis=0)[0:ch]
        return carry

    lax.fori_loop(0, nrow // ch, shift_body, 0)

    def body(c, carry):
        r0 = pl.multiple_of(c * ch, ch)
        acc = jnp.zeros((ch, CONV_CH), F32)
        for k in range(CONV_WIDTH):
            a, s = divmod(k + base, 8)
            src = u_ref if s == 0 else us_ref.at[s - 1]
            acc = acc + src[pl.ds(r0 + 8 * a, ch), :] * w_ref[pl.ds(k, 1), :]
        cv_ref[pl.ds(r0, ch), :] = acc
        return carry

    lax.fori_loop(0, tt // ch, body, 0)
    y = _layer_norm(cv_ref[...], g_ref[...], b_ref[...])
    o_ref[...] = (y * sigmoid(y)).astype(BF16)


def _conformer_conv(pa, conv_w, conv_g, conv_b, tx, tt):
    r = pa.shape[0]
    hb = tt // CONV_HALO
    nhb = r // CONV_HALO
    return pl.pallas_call(
        functools.partial(_conv_kernel, tt=tt, tx=tx),
        grid=(r // tt,),
        in_specs=[pl.BlockSpec((tt, A_COLS), lambda i: (i, 0)),
                  pl.BlockSpec((CONV_HALO, A_COLS), lambda i: (jnp.maximum(i * hb - 1, 0), 0)),
                  pl.BlockSpec((CONV_HALO, A_COLS), lambda i: (jnp.minimum((i + 1) * hb, nhb - 1), 0)),
                  _full_spec((CONV_WIDTH, CONV_CH)), _full_spec((1, CONV_CH)), _full_spec((1, CONV_CH))],
        out_specs=pl.BlockSpec((tt, CONV_CH), lambda i: (i, 0)),
        out_shape=jax.ShapeDtypeStruct((r, CONV_CH), BF16),
        scratch_shapes=[pltpu.VMEM((tt + 2 * CONV_HALO + 8, CONV_CH), F32),
                        pltpu.VMEM((7, tt + 2 * CONV_HALO, CONV_CH), F32),
                        pltpu.VMEM((tt, CONV_CH), F32)],
        compiler_params=_cparams("parallel"),
        name="conformer_conv",
    )(pa, pa, pa, conv_w, conv_g.reshape(1, CONV_CH), conv_b.reshape(1, CONV_CH))


def _prep_kernel(p_ref, cos_ref, sin_ref, qg_ref, kg_ref, qt_ref, k_ref, vt_ref, *, tq):
    x = p_ref[...]
    cos = cos_ref[...]
    sin = sin_ref[...]

    def norm_rope(xh, g):
        ms = jnp.mean(xh * xh, axis=0, keepdims=True)
        y = xh * lax.rsqrt(ms + RMS_EPS) * g
        sw = jnp.concatenate([y[16:32], y[0:16], y[48:64], y[32:48]], axis=0)
        return y * cos + sw * sin

    zeros = jnp.zeros((HEAD_DIM, tq), BF16)
    for p in range(Q_COLS // LANES):
        xp = x[:, LANES * p:LANES * (p + 1)].T
        for half in range(2):
            h = 2 * p + half
            j, g = h // 4, h % 4
            r = (norm_rope(xp[HEAD_DIM * half:HEAD_DIM * (half + 1)], qg_ref[...]) * (ATT_SCALE * LOG2E)).astype(BF16)
            qt_ref[j, HEAD_DIM * j:HEAD_DIM * (j + 1), g * tq:(g + 1) * tq] = r
            qt_ref[j, HEAD_DIM * (1 - j):HEAD_DIM * (2 - j), g * tq:(g + 1) * tq] = zeros
    xk = x[:, Q_COLS:Q_COLS + KV_COLS].T
    k0 = norm_rope(xk[0:HEAD_DIM], kg_ref[...])
    k1 = norm_rope(xk[HEAD_DIM:2 * HEAD_DIM], kg_ref[...])
    k_ref[...] = jnp.concatenate([k0, k1], axis=0).T.astype(BF16)
    xv = x[:, Q_COLS + KV_COLS:].T.astype(BF16)
    ones = jnp.ones((VT_ROWS - HEAD_DIM, tq), BF16)
    for j in range(2):
        vt_ref[j, 0:HEAD_DIM, :] = xv[HEAD_DIM * j:HEAD_DIM * (j + 1)]
        vt_ref[j, HEAD_DIM:VT_ROWS, :] = ones


def _prep_qkv(pq, cos_t, sin_t, q_g, k_g, bx, tx):
    tq = ATT_TQ
    nq = tx // tq
    return pl.pallas_call(
        functools.partial(_prep_kernel, tq=tq),
        grid=(bx, nq),
        in_specs=[pl.BlockSpec((tq, Q_COLS + 2 * KV_COLS), lambda b, t: (b * nq + t, 0)),
                  pl.BlockSpec((HEAD_DIM, tq), lambda b, t: (0, t)),
                  pl.BlockSpec((HEAD_DIM, tq), lambda b, t: (0, t)),
                  _full_spec((HEAD_DIM, 1)), _full_spec((HEAD_DIM, 1))],
        out_specs=[pl.BlockSpec((None, None, 2, LANES, 4 * tq), lambda b, t: (b, t, 0, 0, 0)),
                   pl.BlockSpec((None, None, tq, LANES), lambda b, t: (b, t, 0, 0)),
                   pl.BlockSpec((None, 2, None, VT_ROWS, tq), lambda b, t: (b, 0, t, 0, 0))],
        out_shape=[jax.ShapeDtypeStruct((bx, nq, 2, LANES, 4 * tq), BF16),
                   jax.ShapeDtypeStruct((bx, nq, tq, LANES), BF16),
                   jax.ShapeDtypeStruct((bx, 2, nq, VT_ROWS, tq), BF16)],
        compiler_params=_cparams("parallel", "parallel"),
        name="prep_qkv",
    )(pq, cos_t, sin_t, q_g.reshape(HEAD_DIM, 1), k_g.reshape(HEAD_DIM, 1))


def _rope_tables(t):
    pos = np.arange(t)
    half = HEAD_DIM // 4
    inv = ROPE_THETA ** (-jnp.arange(half, dtype=F32) * 2.0 / (HEAD_DIM // 2))
    dd = np.arange(HEAD_DIM)
    part_pos = np.where((dd // (HEAD_DIM // 2))[:, None] == 0, (pos // GRID_W)[None, :], (pos % GRID_W)[None, :])
    ang = jnp.asarray(part_pos, F32) * inv[dd % half][:, None]
    sign = jnp.asarray(np.where((dd % (HEAD_DIM // 2)) < half, -1.0, 1.0)[:, None], F32)
    return jnp.cos(ang), jnp.sin(ang) * sign


def _attn_kernel(qt_ref, k_ref, vt_ref, o_ref, acc_ref, *, nk, tq):
    qt = qt_ref[...]
    ncol = 4 * tq
    acc_ref[...] = jnp.zeros_like(acc_ref)

    cw = ATT_CW
    nstrip = ncol // cw

    sk = ATT_SK
    per = tq // sk
    nsub = nk * per

    def scores(i, n):
        c, u = divmod(i, per)
        return jnp.dot(k_ref[c, sk * u:sk * (u + 1), :], qt[:, cw * n:cw * (n + 1)], preferred_element_type=F32)

    s = [scores(0, n) for n in range(nstrip)]
    m_prev = [jnp.full((1, cw), -jnp.inf, F32)] * nstrip
    m = [jnp.max(sn, axis=0, keepdims=True) for sn in s]
    for i in range(nsub):
        c, u = divmod(i, per)
        vt = vt_ref[c, :, sk * u:sk * (u + 1)]
        for n in range(nstrip):
            s_next = scores(i + 1, n) if i + 1 < nsub else None
            alpha = jnp.exp2(m_prev[n] - m[n])
            p = jnp.exp2((s[n] - m[n]).astype(BF16))
            pv = jnp.dot(vt, p, preferred_element_type=F32)
            acc_ref[:, cw * n:cw * (n + 1)] = acc_ref[:, cw * n:cw * (n + 1)] * alpha + pv
            if s_next is not None:
                m_prev[n] = m[n]
                m[n] = jnp.maximum(m[n], jnp.max(s_next, axis=0, keepdims=True))
                s[n] = s_next
    o = acc_ref[0:HEAD_DIM, :] * (1.0 / acc_ref[HEAD_DIM:HEAD_DIM + 1, :])
    for pp in range(2):
        blk = jnp.concatenate([o[:, (2 * pp) * tq:(2 * pp + 1) * tq],
                               o[:, (2 * pp + 1) * tq:(2 * pp + 2) * tq]], axis=0)
        o_ref[:, LANES * pp:LANES * (pp + 1)] = blk.T.astype(BF16)


def _gqa_attention(qt, k, vt, tx):
    bx, nq = qt.shape[0], qt.shape[1]
    nk = k.shape[1]
    tq = ATT_TQ
    return pl.pallas_call(
        functools.partial(_attn_kernel, nk=nk, tq=tq),
        grid=(bx, 2, nq),
        in_specs=[pl.BlockSpec((None, None, None, LANES, 4 * tq), lambda b, j, t: (b, t, j, 0, 0)),
                  pl.BlockSpec((None, nk, tq, LANES), lambda b, j, t: (b, 0, 0, 0)),
                  pl.BlockSpec((None, None, nk, VT_ROWS, tq), lambda b, j, t: (b, j, 0, 0, 0))],
        out_specs=pl.BlockSpec((tq, 2 * LANES), lambda b, j, t: (b * nq + t, j)),
        out_shape=jax.ShapeDtypeStruct((bx * tx, Q_COLS), BF16),
        scratch_shapes=[pltpu.VMEM((VT_ROWS, 4 * tq), F32)],
        compiler_params=_cparams("parallel", "parallel", "parallel"),
        name="gqa_attention",
    )(qt, k, vt)


def _outproj_ln_kernel(*refs, n_lhs):
    lhs = refs[:n_lhs]
    w_ref, x_ref, gate_ref, lng_ref, lnb_ref, o_ref = refs[n_lhs:]
    y = None
    off = 0
    for r in lhs:
        kk = r.shape[1]
        t = jnp.dot(r[...], w_ref[off:off + kk, :], preferred_element_type=F32)
        off += kk
        y = t if y is None else y + t
    z = ALPHA * x_ref[...] + gate_ref[0] * y
    o_ref[...] = _layer_norm(z, lng_ref[...], lnb_ref[...])


def _outproj_ln(lhs, w, x2d, vecs, gate_k, ln_g, ln_b, rows_per_batch, tm):
    r = x2d.shape[0]
    return pl.pallas_call(
        functools.partial(_outproj_ln_kernel, n_lhs=len(lhs)),
        grid=(r // tm,),
        in_specs=[pl.BlockSpec((tm, a.shape[1]), lambda i: (i, 0)) for a in lhs]
        + [_full_spec(w.shape), pl.BlockSpec((tm, D_MODEL), lambda i: (i, 0)),
           _vec_spec(gate_k, tm, rows_per_batch), _full_spec((1, D_MODEL)), _full_spec((1, D_MODEL))],
        out_specs=pl.BlockSpec((tm, D_MODEL), lambda i: (i, 0)),
        out_shape=jax.ShapeDtypeStruct((r, D_MODEL), F32),
        compiler_params=_cparams("parallel"),
        name="outproj_ln",
    )(*lhs, w, x2d, vecs, ln_g.reshape(1, D_MODEL), ln_b.reshape(1, D_MODEL))


def _ffn_kernel(x_ref, sc_ref, sh_ref, gate_ref, wg_ref, wu_ref, wd_ref, lng_ref, lnb_ref, o_ref):
    x = x_ref[...]
    hb = (x * (1.0 + sc_ref[0]) + sh_ref[0]).astype(BF16)
    hg = jnp.dot(hb, wg_ref[...], preferred_element_type=F32)
    hu = jnp.dot(hb, wu_ref[...], preferred_element_type=F32)
    h1 = (hg * jax.nn.sigmoid(hg) * hu).astype(BF16)
    y = jnp.dot(h1, wd_ref[...], preferred_element_type=F32)
    o_ref[...] = _layer_norm(ALPHA * x + gate_ref[0] * y, lng_ref[...], lnb_ref[...])


def _ffn_ln(x2d, vecs, wg, wu, wd, ln_g, ln_b, rows_per_batch, tm):
    r = x2d.shape[0]

    def resident(shape):
        return pl.BlockSpec(shape, lambda i: (0, 0), pipeline_mode=pl.Buffered(1))

    return pl.pallas_call(
        _ffn_kernel,
        grid=(r // tm,),
        in_specs=[pl.BlockSpec((tm, D_MODEL), lambda i: (i, 0)),
                  _vec_spec(4, tm, rows_per_batch), _vec_spec(3, tm, rows_per_batch),
                  _vec_spec(5, tm, rows_per_batch),
                  resident(wg.shape), resident(wu.shape), resident(wd.shape),
                  _full_spec((1, D_MODEL)), _full_spec((1, D_MODEL))],
        out_specs=pl.BlockSpec((tm, D_MODEL), lambda i: (i, 0)),
        out_shape=jax.ShapeDtypeStruct((r, D_MODEL), F32),
        compiler_params=_cparams("parallel"),
        name="ffn_ln",
    )(x2d, vecs, vecs, vecs, wg, wu, wd, ln_g.reshape(1, D_MODEL), ln_b.reshape(1, D_MODEL))


def _inproj1_kernel(x_ref, sc_ref, sh_ref, w_ref, *o_refs, with_q):
    h = x_ref[...] * (1.0 + sc_ref[0]) + sh_ref[0]
    o = jnp.dot(h.astype(BF16), w_ref[...], preferred_element_type=F32)
    if with_q:
        qt_ref, k_ref, vt_ref = o_refs
        qt_ref[...] = (o[:, :D_MODEL] * (ATT_SCALE * LOG2E)).T.astype(BF16)
    else:
        k_ref, vt_ref = o_refs
    nk = o.shape[1] - 2 * D_MODEL
    k_ref[...] = o[:, nk:nk + D_MODEL].astype(BF16)
    vt_ref[...] = o[:, nk + D_MODEL:].T.astype(BF16)


def _inproj1(x2d, vecs, w, rows_per_batch, tm, with_q):
    r = x2d.shape[0]
    nat = pl.BlockSpec((tm, D_MODEL), lambda i: (i, 0))
    tr = pl.BlockSpec((D_MODEL, tm), lambda i: (0, i))
    nat_shape = jax.ShapeDtypeStruct((r, D_MODEL), BF16)
    tr_shape = jax.ShapeDtypeStruct((D_MODEL, r), BF16)
    return pl.pallas_call(
        functools.partial(_inproj1_kernel, with_q=with_q),
        grid=(r // tm,),
        in_specs=[pl.BlockSpec((tm, D_MODEL), lambda i: (i, 0)),
                  _vec_spec(1, tm, rows_per_batch), _vec_spec(0, tm, rows_per_batch),
                  _full_spec(w.shape)],
        out_specs=([tr] if with_q else []) + [nat, tr],
        out_shape=([tr_shape] if with_q else []) + [nat_shape, tr_shape],
        compiler_params=_cparams("parallel"),
        name="inproj1",
    )(x2d, vecs, vecs, w)


def _na_bias(rpb, n_rows, n_ctx):
    nh = rpb.shape[0]
    c = np.arange(GRID_W)
    cs = np.clip(c - WIN_C // 2, 0, GRID_W - WIN_C)
    in_c = (c[None, :] >= cs[:, None]) & (c[None, :] < cs[:, None] + WIN_C)
    dc = c[None, :] - c[:, None] + WIN_C - 1
    pick = ((dc[None] == np.arange(2 * WIN_C - 1)[:, None, None]) & in_c[None]).astype(np.float32)
    cols = jnp.einsum("hrd,dkc->hrkc", rpb, jnp.asarray(pick.transpose(0, 2, 1)), precision=lax.Precision.HIGHEST)
    cols = jnp.where(jnp.asarray(in_c.T)[None, None], cols * LOG2E, MASK_VALUE)
    n_dr = 2 * WIN_R - 1
    cols = jnp.concatenate([cols, jnp.full((nh, 1, GRID_W, GRID_W), MASK_VALUE, F32)], axis=1)
    pick_dr = np.full((3, NA_ROWS, NA_KROWS), n_dr, np.int32)
    for case, r0 in enumerate((0, NA_ROWS, n_rows - NA_ROWS)):
        start = int(np.clip(r0 - WIN_R // 2, 0, n_rows - NA_KROWS))
        for ri in range(NA_ROWS):
            r = r0 + ri
            rs = int(np.clip(r - WIN_R // 2, 0, n_rows - WIN_R))
            for ki in range(NA_KROWS):
                kr = start + ki
                if rs <= kr < rs + WIN_R:
                    pick_dr[case, ri, ki] = kr - r + WIN_R - 1
    npair = nh // 2
    nkeys = NA_KROWS * GRID_W
    out = pl.pallas_call(
        functools.partial(_na_bias_kernel, nkeys=nkeys),
        grid_spec=pltpu.PrefetchScalarGridSpec(
            num_scalar_prefetch=1,
            grid=(3, npair),
            in_specs=[pl.BlockSpec((2, n_dr + 1, GRID_W, GRID_W), lambda case, p, dr: (p, 0, 0, 0))],
            out_specs=pl.BlockSpec((None, None, nkeys + n_ctx, 2 * NA_ROWS * GRID_W),
                                   lambda case, p, dr: (case, p, 0, 0)),
        ),
        out_shape=jax.ShapeDtypeStruct((3, npair, nkeys + n_ctx, 2 * NA_ROWS * GRID_W), F32),
        compiler_params=_cparams("parallel", "parallel"),
        name="na_bias",
    )(jnp.asarray(pick_dr.reshape(-1)), cols)
    return out.reshape(3, npair // NA_PAIRS, NA_PAIRS, nkeys + n_ctx, 2 * NA_ROWS * GRID_W)


def _na_bias_kernel(dr_ref, cols_ref, o_ref, *, nkeys):
    case = pl.program_id(0)
    for ki in range(NA_KROWS):
        pieces = []
        for e in range(2):
            for ri in range(NA_ROWS):
                d = dr_ref[(case * NA_ROWS + ri) * NA_KROWS + ki]
                pieces.append(cols_ref[e, d])
        o_ref[GRID_W * ki:GRID_W * (ki + 1), :] = jnp.concatenate(pieces, axis=1)
    o_ref[nkeys:, :] = jnp.zeros((o_ref.shape[0] - nkeys, o_ref.shape[1]), F32)


def _na_kernel(qt_ref, k0_ref, k1_ref, k2_ref, kc_ref, v0_ref, v1_ref, v2_ref, vc_ref, bias_ref, o_ref):
    qt = qt_ref[...]
    kk = jnp.concatenate([k0_ref[...], k1_ref[...], k2_ref[...], kc_ref[...]], axis=0)
    vt = jnp.concatenate([v0_ref[...], v1_ref[...], v2_ref[...], vc_ref[...]], axis=1)
    nq = qt.shape[1]
    zeros = jnp.zeros((HEAD_DIM, nq), BF16)
    ones = jnp.ones((VT_ROWS - HEAD_DIM, kk.shape[0]), BF16)

    def scores(pp):
        q2 = qt[LANES * pp:LANES * (pp + 1)]
        qcat = jnp.concatenate([jnp.concatenate([q2[0:HEAD_DIM], zeros], axis=0),
                                jnp.concatenate([zeros, q2[HEAD_DIM:]], axis=0)], axis=1)
        return jnp.dot(kk[:, LANES * pp:LANES * (pp + 1)], qcat, preferred_element_type=F32) + bias_ref[pp]

    s_all = [scores(pp) for pp in range(NA_PAIRS)]
    for pp in range(NA_PAIRS):
        s = s_all[pp]
        m = jnp.max(s, axis=0, keepdims=True)
        p = jnp.exp2((s - m).astype(BF16))
        vext = jnp.concatenate([vt[LANES * pp:LANES * (pp + 1)], ones], axis=0)
        pv = jnp.dot(vext, p, preferred_element_type=F32)
        o0 = pv[0:HEAD_DIM, 0:nq] * (1.0 / pv[LANES:LANES + 1, 0:nq])
        o1 = pv[HEAD_DIM:LANES, nq:] * (1.0 / pv[LANES:LANES + 1, nq:])
        o_ref[:, LANES * pp:LANES * (pp + 1)] = jnp.concatenate([o0, o1], axis=0).T.astype(BF16)


def _neighbourhood_attention(qt, k, vt, kc, vct, bias, bsz, t):
    nq = NA_ROWS * GRID_W
    nrb = t // nq
    wl = NA_PAIRS * LANES
    ngrp = D_MODEL // wl
    nctx = kc.shape[0] // bsz
    nwin = NA_KROWS // NA_ROWS

    def first_kblock(b, rb):
        return b * nrb + jnp.clip(rb - 1, 0, nrb - nwin)

    def kspec(d):
        return pl.BlockSpec((nq, wl), lambda hp, b, rb: (first_kblock(b, rb) + d, hp))

    def vspec(d):
        return pl.BlockSpec((wl, nq), lambda hp, b, rb: (hp, first_kblock(b, rb) + d))

    def case(rb):
        return jnp.where(rb == 0, 0, jnp.where(rb == nrb - 1, 2, 1))

    return pl.pallas_call(
        _na_kernel,
        grid=(ngrp, bsz, nrb),
        in_specs=[pl.BlockSpec((wl, nq), lambda hp, b, rb: (hp, b * nrb + rb)),
                  kspec(0), kspec(1), kspec(2), pl.BlockSpec((nctx, wl), lambda hp, b, rb: (b, hp)),
                  vspec(0), vspec(1), vspec(2), pl.BlockSpec((wl, nctx), lambda hp, b, rb: (hp, b)),
                  pl.BlockSpec((None, None, NA_PAIRS, NA_KROWS * GRID_W + nctx, 2 * nq),
                               lambda hp, b, rb: (case(rb), hp, 0, 0, 0))],
        out_specs=pl.BlockSpec((nq, wl), lambda hp, b, rb: (b * nrb + rb, hp)),
        out_shape=jax.ShapeDtypeStruct((bsz * t, D_MODEL), BF16),
        compiler_params=_cparams("parallel", "parallel", "parallel"),
        name="neighbourhood_attention",
    )(qt, k, k, k, kc, vt, vt, vt, vct, bias)


def _store_row_tiles(dst_ref, val):
    rows = val.shape[0]
    for lt in range(ROW_TILES):
        dst_ref[pl.ds(lt, rows, stride=ROW_TILES), :] = val[:, lt * LANES:(lt + 1) * LANES]


def _load_row_tiles(src_ref, lt, rows):
    return src_ref[pl.ds(lt, rows, stride=ROW_TILES), :]


def _router_kernel(x_ref, sc_ref, sh_ref, wr_ref, h_ref, r_ref):
    h = x_ref[...] * (1.0 + sc_ref[0]) + sh_ref[0]
    _store_row_tiles(h_ref, h)
    hi = h.astype(BF16)
    lo = (h - hi.astype(F32)).astype(BF16)
    w = wr_ref[...]
    whi = w.astype(BF16)
    wlo = (w - whi.astype(F32)).astype(BF16)
    lg = (jnp.dot(hi, whi, preferred_element_type=F32)
          + (jnp.dot(hi, wlo, preferred_element_type=F32) + jnp.dot(lo, whi, preferred_element_type=F32)))
    lane = lax.broadcasted_iota(jnp.int32, lg.shape, 1).astype(F32)
    lg = jnp.where(lane < N_EXPERTS, lg, -jnp.inf)
    v1 = jnp.max(lg, axis=1, keepdims=True)
    i1 = jnp.min(jnp.where(lg == v1, lane, float(LANES)), axis=1, keepdims=True)
    lg2 = jnp.where(lane == i1, -jnp.inf, lg)
    v2 = jnp.max(lg2, axis=1, keepdims=True)
    i2 = jnp.min(jnp.where(lg2 == v2, lane, float(LANES)), axis=1, keepdims=True)
    e = jnp.exp(v2 - v1)
    w1 = 1.0 / (1.0 + e)
    w2 = e / (1.0 + e)
    r_ref[...] = jnp.where(lane == 0, i1, jnp.where(lane == 1, i2, jnp.where(lane == 2, w1,
                                                                              jnp.where(lane == 3, w2, 0.0))))


def _router(x2d, vecs, w_router, rows_per_batch, tm):
    r = x2d.shape[0]
    wr = jnp.pad(w_router, ((0, 0), (0, LANES - N_EXPERTS)))
    return pl.pallas_call(
        _router_kernel,
        grid=(r // tm,),
        in_specs=[pl.BlockSpec((tm, D_MODEL), lambda i: (i, 0)),
                  _vec_spec(4, tm, rows_per_batch), _vec_spec(3, tm, rows_per_batch),
                  _full_spec((D_MODEL, LANES))],
        out_specs=[pl.BlockSpec((tm * ROW_TILES, LANES), lambda i: (i, 0)),
                   pl.BlockSpec((tm, LANES), lambda i: (i, 0))],
        out_shape=[jax.ShapeDtypeStruct((r * ROW_TILES, LANES), F32), jax.ShapeDtypeStruct((r, LANES), F32)],
        compiler_params=_cparams("parallel"),
        name="router",
    )(x2d, vecs, vecs, wr)


def _row_copy(src_hbm, row, r, dst_ref, sem):
    def tile(i):
        start = i * ROW_TILES
        return pl.ds(start if isinstance(i, int) else pl.multiple_of(start, ROW_TILES), ROW_TILES)

    return pltpu.make_async_copy(src_hbm.at[tile(row), :], dst_ref.at[tile(r), :], sem)


def _start_row_gather(src_hbm, idx_ref, dst_ref, sem, n, priorities):
    def issue(i, carry):
        for u in range(2):
            r = 2 * i + u
            _row_copy(src_hbm, idx_ref[0, r], r, dst_ref, sem).start(priority=priorities[u])
        return carry

    lax.fori_loop(0, n // 2, issue, 0, unroll=4)


def _wait_row_gather(src_hbm, dst_ref, sem, n):
    def wait(r, carry):
        _row_copy(src_hbm, 0, r, dst_ref, sem).wait()
        return carry

    lax.fori_loop(0, n, wait, 0, unroll=8)


def _moe_ffn_kernel(te_ref, nu_ref, idx0_ref, idxn_ref, h_hbm, wg_ref, wu_ref, wd_ref, o_ref,
                    xbuf_ref, hb_ref, acc_ref, sem):
    t = pl.program_id(0)
    j = pl.program_id(1)
    nj = pl.num_programs(1)
    used = t < nu_ref[0]
    tm = hb_ref.shape[0]
    slot = t % 2

    @pl.when(jnp.logical_and(used, j == 0))
    def _():
        @pl.when(t == 0)
        def _():
            _start_row_gather(h_hbm, idx0_ref, xbuf_ref.at[0], sem.at[0], tm, (0, 0))

        @pl.when(t + 1 < nu_ref[0])
        def _():
            _start_row_gather(h_hbm, idxn_ref, xbuf_ref.at[1 - slot], sem.at[1 - slot], tm, (0, 0))

        _wait_row_gather(h_hbm, xbuf_ref.at[slot], sem.at[slot], tm)

    @pl.when(jnp.logical_and(used, j == 0))
    def _():
        for lt in range(ROW_TILES):
            hb_ref[:, lt * LANES:(lt + 1) * LANES] = _load_row_tiles(xbuf_ref.at[slot], lt, tm).astype(BF16)
        acc_ref[...] = jnp.zeros_like(acc_ref)

    @pl.when(used)
    def _():
        hb = hb_ref[...]
        hg = jnp.dot(hb, wg_ref[...], preferred_element_type=F32)
        hu = jnp.dot(hb, wu_ref[...], preferred_element_type=F32)
        h1 = (hg * jax.nn.sigmoid(hg) * hu).astype(BF16)
        acc_ref[...] += jnp.dot(h1, wd_ref[...], preferred_element_type=F32)

    @pl.when(jnp.logical_and(used, j == nj - 1))
    def _():
        _store_row_tiles(o_ref, acc_ref[...])

    @pl.when(jnp.logical_and(jnp.logical_not(used), j == nj - 1))
    def _():
        o_ref[...] = jnp.zeros_like(o_ref)


def _moe_ffn(h, src, tile_expert, n_used, wg, wu, wd):
    p = src.shape[0]
    tm, tf = MOE_TM, MOE_TF
    nt = p // tm
    f = wg.shape[2]

    def jj(t, j, nu):
        return jnp.where(t < nu[0], j, 0)

    grid_spec = pltpu.PrefetchScalarGridSpec(
        num_scalar_prefetch=2,
        grid=(nt, f // tf),
        in_specs=[pl.BlockSpec((None, 1, tm), lambda t, j, te, nu: (0, 0, 0), memory_space=pltpu.SMEM),
                  pl.BlockSpec((None, 1, tm), lambda t, j, te, nu: (jnp.minimum(t + 1, nt - 1), 0, 0),
                               memory_space=pltpu.SMEM),
                  pl.BlockSpec(memory_space=pl.ANY),
                  pl.BlockSpec((None, D_MODEL, tf), lambda t, j, te, nu: (te[t], 0, jj(t, j, nu))),
                  pl.BlockSpec((None, D_MODEL, tf), lambda t, j, te, nu: (te[t], 0, jj(t, j, nu))),
                  pl.BlockSpec((None, tf, D_MODEL), lambda t, j, te, nu: (te[t], jj(t, j, nu), 0))],
        out_specs=pl.BlockSpec((tm * ROW_TILES, LANES), lambda t, j, te, nu: (t, 0)),
        scratch_shapes=[pltpu.VMEM((2, tm * ROW_TILES, LANES), F32), pltpu.VMEM((tm, D_MODEL), BF16),
                        pltpu.VMEM((tm, D_MODEL), F32), pltpu.SemaphoreType.DMA((2,))],
    )
    idx = src.reshape(nt, 1, tm)
    return pl.pallas_call(
        _moe_ffn_kernel,
        grid_spec=grid_spec,
        out_shape=jax.ShapeDtypeStruct((p * ROW_TILES, LANES), F32),
        compiler_params=_cparams("arbitrary", "arbitrary"),
        name="moe_ffn",
    )(tile_expert, n_used, idx, idx, h, wg, wu, wd)


def _combine_ln_kernel(p1_ref, p2_ref, ys_hbm, route_ref, x_ref, gate_ref, lng_ref, lnb_ref, o_ref,
                       y1_ref, y2_ref, sem):
    n = x_ref.shape[0]
    _start_row_gather(ys_hbm, p1_ref, y1_ref, sem.at[0], n, (0, 1))
    _start_row_gather(ys_hbm, p2_ref, y2_ref, sem.at[1], n, (0, 1))
    _wait_row_gather(ys_hbm, y1_ref, sem.at[0], n)
    _wait_row_gather(ys_hbm, y2_ref, sem.at[1], n)
    w1 = route_ref[:, 2:3]
    w2 = route_ref[:, 3:4]
    y = jnp.concatenate([w1 * _load_row_tiles(y1_ref, lt, n) + w2 * _load_row_tiles(y2_ref, lt, n)
                         for lt in range(ROW_TILES)], axis=1)
    z = ALPHA * x_ref[...] + gate_ref[0] * y
    o_ref[...] = _layer_norm(z, lng_ref[...], lnb_ref[...])


def _combine_ln(ys, pos1, pos2, route, x2d, vecs, ln_g, ln_b, rows_per_batch):
    r = x2d.shape[0]
    g = GATHER_ROWS
    idx_spec = pl.BlockSpec((None, 1, g), lambda i: (i, 0, 0), memory_space=pltpu.SMEM)
    return pl.pallas_call(
        _combine_ln_kernel,
        grid=(r // g,),
        in_specs=[idx_spec, idx_spec, pl.BlockSpec(memory_space=pl.ANY),
                  pl.BlockSpec((g, LANES), lambda i: (i, 0)),
                  pl.BlockSpec((g, D_MODEL), lambda i: (i, 0)),
                  _vec_spec(5, g, rows_per_batch), _full_spec((1, D_MODEL)), _full_spec((1, D_MODEL))],
        out_specs=pl.BlockSpec((g, D_MODEL), lambda i: (i, 0)),
        out_shape=jax.ShapeDtypeStruct((r, D_MODEL), F32),
        scratch_shapes=[pltpu.VMEM((g * ROW_TILES, LANES), F32), pltpu.VMEM((g * ROW_TILES, LANES), F32),
                        pltpu.SemaphoreType.DMA((2,))],
        compiler_params=_cparams("arbitrary"),
        name="combine_ln",
    )(pos1.reshape(r // g, 1, g), pos2.reshape(r // g, 1, g), ys, route, x2d, vecs,
      ln_g.reshape(1, D_MODEL), ln_b.reshape(1, D_MODEL))


def _routing_plan(route, tm):
    n = route.shape[0]
    e = jnp.concatenate([route[:, 0], route[:, 1]]).astype(jnp.int32)
    tok = jnp.concatenate([jnp.arange(n, dtype=jnp.int32)] * 2)
    onehot = (e[:, None] == jnp.arange(N_EXPERTS, dtype=jnp.int32)[None, :]).astype(jnp.int32)
    csum = jnp.cumsum(onehot, axis=0)
    rank = jnp.sum(csum * onehot, axis=1) - 1
    counts = csum[-1]
    padded = ((counts + tm - 1) // tm) * tm
    ends = jnp.cumsum(padded)
    starts = ends - padded
    pos = jnp.sum(starts[None, :] * onehot, axis=1) + rank
    p = 2 * n + N_EXPERTS * tm
    src = jnp.zeros((p,), jnp.int32).at[pos].set(tok)
    tile_start = jnp.arange(p // tm, dtype=jnp.int32) * tm
    tile_expert = jnp.minimum(jnp.sum((tile_start[:, None] >= ends[None, :]).astype(jnp.int32), axis=1),
                              N_EXPERTS - 1).astype(jnp.int32)
    n_used = (ends[-1] // tm).astype(jnp.int32).reshape(1)
    return src, tile_expert, n_used, pos[:n], pos[n:]


def kernel(x, c, ctx, c_ctx, w_mod, b_mod, ln_g, ln_b, ab_w_in, ab_conv_w, ab_conv_g, ab_conv_b, ab_q_g, ab_k_g,
           ab_w_out, ffn_w_gate, ffn_w_up, ffn_w_down, na_w_qkv, na_rpb, na_w_out, moe_w_router, moe_w_gate,
           moe_w_up, moe_w_down):
    bsz, t, d = x.shape
    n_ctx = ctx.shape[1]
    n = bsz * t
    nc = bsz * n_ctx
    x2 = x.reshape(n, d)
    c2 = ctx.reshape(nc, d)

    cc = jnp.concatenate([c, c_ctx[None, :], jnp.zeros((8 - bsz - 1, d), F32)], axis=0)
    mod = _modulation(cc, w_mod, b_mod)
    vec0 = mod[0].reshape(8 * 6, 1, d)
    vec1 = mod[1].reshape(8 * 6, 1, d)

    w_in = ab_w_in[0].astype(BF16)
    w_out = ab_w_out[0].astype(BF16)
    pa, pq = _inproj0(x2, vec0, w_in, t, 512)
    pac, pqc = _inproj0(c2, vec0, w_in, None, 512)
    a = _conformer_conv(pa, ab_conv_w[0], ab_conv_g[0], ab_conv_b[0], t, 512)
    ac = _conformer_conv(pac, ab_conv_w[0], ab_conv_g[0], ab_conv_b[0], n_ctx, n_ctx)
    cos_t, sin_t = _rope_tables(t)
    qt, k, vt = _prep_qkv(pq, cos_t, sin_t, ab_q_g[0], ab_k_g[0], bsz, t)
    ones = jnp.ones((HEAD_DIM, n_ctx), F32)
    qtc, kc, vtc = _prep_qkv(pqc, ones, jnp.zeros_like(ones), ab_q_g[0], ab_k_g[0], bsz, n_ctx)
    o = _gqa_attention(qt, jnp.concatenate([kc, k], axis=1), jnp.concatenate([vtc, vt], axis=2), t)
    oc = _gqa_attention(qtc, kc, vtc, n_ctx)
    x2 = _outproj_ln([a, o], w_out, x2, vec0, 2, ln_g[0, 0], ln_b[0, 0], t, 512)
    c2 = _outproj_ln([ac, oc], w_out, c2, vec0, 2, ln_g[0, 0], ln_b[0, 0], None, 512)
    wg = ffn_w_gate[0].astype(BF16)
    wu = ffn_w_up[0].astype(BF16)
    wd = ffn_w_down[0].astype(BF16)
    x2 = _ffn_ln(x2, vec0, wg, wu, wd, ln_g[0, 1], ln_b[0, 1], t, 512)
    c2 = _ffn_ln(c2, vec0, wg, wu, wd, ln_g[0, 1], ln_b[0, 1], None, 512)

    w_qkv = na_w_qkv[0].astype(BF16)
    qt1, k1, vt1 = _inproj1(x2, vec1, w_qkv, t, 512, True)
    kc1, vct1 = _inproj1(c2, vec1, w_qkv[:, d:], None, 512, False)
    o = _neighbourhood_attention(qt1, k1, vt1, kc1, vct1, _na_bias(na_rpb[0], t // GRID_W, n_ctx), bsz, t)
    x2 = _outproj_ln([o], na_w_out[0].astype(BF16), x2, vec1, 2, ln_g[1, 0], ln_b[1, 0], t, 512)

    h, route = _router(x2, vec1, moe_w_router[0], t, 512)
    src, tile_expert, n_used, pos1, pos2 = _routing_plan(route, MOE_TM)
    ys = _moe_ffn(h, src, tile_expert, n_used, moe_w_gate[0].astype(BF16), moe_w_up[0].astype(BF16),
                  moe_w_down[0].astype(BF16))
    x2 = _combine_ln(ys, pos1, pos2, route, x2, vec1, ln_g[1, 1], ln_b[1, 1], t)
    return x2.reshape(bsz, t, d)
```

```python
import functools

import numpy as np
import jax
import jax.numpy as jnp
from jax import lax
from jax.experimental import pallas as pl
from jax.experimental.pallas import tpu as pltpu

F32 = jnp.float32
BF16 = jnp.bfloat16

D_MODEL = 1024
GRID_W = 64
HEAD_DIM = 64
CONV_CH = 512
CONV_WIDTH = 31
CONV_HALO = 16
Q_COLS = 512
KV_COLS = 128
A_COLS = 2 * CONV_CH
ROPE_THETA = 10000.0
WIN_R = 8
WIN_C = 16
N_EXPERTS = 8
DEPTH = 2
ALPHA = (2 * DEPTH) ** 0.25
LN_EPS = 1e-5
RMS_EPS = 1e-6
ATT_SCALE = HEAD_DIM ** -0.5
LOG2E = 1.4426950408889634
MASK_VALUE = -1e30

LANES = 128
ROW_TILES = D_MODEL // LANES
VMEM_LIMIT = 56 * 1024 * 1024

ATT_TQ = 256
ATT_CW = 256
VT_ROWS = HEAD_DIM + 16
NA_ROWS = 4
NA_KROWS = 12
NA_PAIRS = 4
MOE_TM = 512
MOE_TF = 1792
GATHER_ROWS = 256


def _cparams(*sem):
    return pltpu.CompilerParams(dimension_semantics=sem, vmem_limit_bytes=VMEM_LIMIT)


def _layer_norm(z, g, b):
    mu = jnp.mean(z, axis=-1, keepdims=True)
    zc = z - mu
    var = jnp.mean(zc * zc, axis=-1, keepdims=True)
    return zc * lax.rsqrt(var + LN_EPS) * g + b


def _vec_spec(k, tm, rows_per_batch):
    if rows_per_batch is None:
        return pl.BlockSpec((1, 1, D_MODEL), lambda i, *_: (4 * 6 + k, 0, 0))
    return pl.BlockSpec((1, 1, D_MODEL), lambda i, *_: ((i * tm // rows_per_batch) * 6 + k, 0, 0))


def _full_spec(shape):
    nd = len(shape)
    return pl.BlockSpec(shape, lambda *_: (0,) * nd)


def _mod_kernel(c_ref, w_ref, b_ref, o_ref):
    c = c_ref[...]
    s = c * jax.nn.sigmoid(c)
    o_ref[...] = jnp.dot(s.astype(BF16), w_ref[...].astype(BF16), preferred_element_type=F32) + b_ref[...]


def _modulation(cc, w_mod, b_mod):
    n = 6 * D_MODEL
    tn = D_MODEL
    return pl.pallas_call(
        _mod_kernel,
        grid=(DEPTH, n // tn),
        in_specs=[pl.BlockSpec((8, D_MODEL), lambda l, j: (0, 0)),
                  pl.BlockSpec((None, D_MODEL, tn), lambda l, j: (l, 0, j)),
                  pl.BlockSpec((None, 1, tn), lambda l, j: (l, 0, j))],
        out_specs=pl.BlockSpec((None, 8, tn), lambda l, j: (l, 0, j)),
        out_shape=jax.ShapeDtypeStruct((DEPTH, 8, n), F32),
        compiler_params=_cparams("parallel", "parallel"),
        name="modulation",
    )(cc, w_mod, b_mod.reshape(DEPTH, 1, n))


def _inproj0_kernel(x_ref, sc_ref, sh_ref, w_ref, oa_ref, oq_ref):
    h = x_ref[...] * (1.0 + sc_ref[0]) + sh_ref[0]
    o = jnp.dot(h.astype(BF16), w_ref[...], preferred_element_type=F32)
    oa_ref[...] = o[:, :A_COLS]
    oq_ref[...] = o[:, A_COLS:]


def _inproj0(x2d, vecs, w, rows_per_batch, tm):
    r = x2d.shape[0]
    nq = w.shape[1] - A_COLS
    return pl.pallas_call(
        _inproj0_kernel,
        grid=(r // tm,),
        in_specs=[pl.BlockSpec((tm, D_MODEL), lambda i: (i, 0)),
                  _vec_spec(1, tm, rows_per_batch), _vec_spec(0, tm, rows_per_batch),
                  _full_spec(w.shape)],
        out_specs=[pl.BlockSpec((tm, A_COLS), lambda i: (i, 0)),
                   pl.BlockSpec((tm, nq), lambda i: (i, 0))],
        out_shape=[jax.ShapeDtypeStruct((r, A_COLS), F32), jax.ShapeDtypeStruct((r, nq), F32)],
        compiler_params=_cparams("parallel"),
        name="inproj0",
    )(x2d, vecs, vecs, w)


def _conv_kernel(pm_ref, pp_ref, pn_ref, w_ref, g_ref, b_ref, o_ref, u_ref, us_ref, cv_ref, *, tt, tx):
    i = pl.program_id(0)

    def sigmoid(v):
        return 0.5 * jnp.tanh(0.5 * v) + 0.5

    def glu(p):
        return p[:, :CONV_CH] * sigmoid(p[:, CONV_CH:])

    first = (i * tt) % tx == 0
    last = ((i + 1) * tt) % tx == 0
    u_ref[0:CONV_HALO, :] = jnp.where(first, 0.0, glu(pp_ref[...]))
    u_ref[CONV_HALO:CONV_HALO + tt, :] = glu(pm_ref[...])
    u_ref[CONV_HALO + tt:2 * CONV_HALO + tt, :] = jnp.where(last, 0.0, glu(pn_ref[...]))
    u_ref[2 * CONV_HALO + tt:, :] = jnp.zeros((8, CONV_CH), F32)
    ch = 32
    nrow = tt + 2 * CONV_HALO
    base = CONV_HALO - CONV_WIDTH // 2

    def shift_body(c, carry):
        r0 = pl.multiple_of(c * ch, ch)
        w = u_ref[pl.ds(r0, ch + 8), :]
        for s in range(1, 8):
            us_ref[s - 1, pl.ds(r0, ch), :] = pltpu.roll(w, ch + 8 - s, axis=0)[0:ch]
        return carry

    lax.fori_loop(0, nrow // ch, shift_body, 0)

    def body(c, carry):
        r0 = pl.multiple_of(c * ch, ch)
        acc = jnp.zeros((ch, CONV_CH), F32)
        for k in range(CONV_WIDTH):
            a, s = divmod(k + base, 8)
            src = u_ref if s == 0 else us_ref.at[s - 1]
            acc = acc + src[pl.ds(r0 + 8 * a, ch), :] * w_ref[pl.ds(k, 1), :]
        cv_ref[pl.ds(r0, ch), :] = acc
        return carry

    lax.fori_loop(0, tt // ch, body, 0)
    y = _layer_norm(cv_ref[...], g_ref[...], b_ref[...])
    o_ref[...] = (y * sigmoid(y)).astype(BF16)


def _conformer_conv(pa, conv_w, conv_g, conv_b, tx, tt):
    r = pa.shape[0]
    hb = tt // CONV_HALO
    nhb = r // CONV_HALO
    return pl.pallas_call(
        functools.partial(_conv_kernel, tt=tt, tx=tx),
        grid=(r // tt,),
        in_specs=[pl.BlockSpec((tt, A_COLS), lambda i: (i, 0)),
                  pl.BlockSpec((CONV_HALO, A_COLS), lambda i: (jnp.maximum(i * hb - 1, 0), 0)),
                  pl.BlockSpec((CONV_HALO, A_COLS), lambda i: (jnp.minimum((i + 1) * hb, nhb - 1), 0)),
                  _full_spec((CONV_WIDTH, CONV_CH)), _full_spec((1, CONV_CH)), _full_spec((1, CONV_CH))],
        out_specs=pl.BlockSpec((tt, CONV_CH), lambda i: (i, 0)),
        out_shape=jax.ShapeDtypeStruct((r, CONV_CH), BF16),
        scratch_shapes=[pltpu.VMEM((tt + 2 * CONV_HALO + 8, CONV_CH), F32),
                        pltpu.VMEM((7, tt + 2 * CONV_HALO, CONV_CH), F32),
                        pltpu.VMEM((tt, CONV_CH), F32)],
        compiler_params=_cparams("parallel"),
        name="conformer_conv",
    )(pa, pa, pa, conv_w, conv_g.reshape(1, CONV_CH), conv_b.reshape(1, CONV_CH))


def _prep_kernel(p_ref, cos_ref, sin_ref, qg_ref, kg_ref, qt_ref, k_ref, vt_ref, *, tq):
    x = p_ref[...]
    cos = cos_ref[...]
    sin = sin_ref[...]

    def norm_rope(xh, g):
        ms = jnp.mean(xh * xh, axis=0, keepdims=True)
        y = xh * lax.rsqrt(ms + RMS_EPS) * g
        sw = jnp.concatenate([y[16:32], y[0:16], y[48:64], y[32:48]], axis=0)
        return y * cos + sw * sin

    zeros = jnp.zeros((HEAD_DIM, tq), BF16)
    for p in range(Q_COLS // LANES):
        xp = x[:, LANES * p:LANES * (p + 1)].T
        for half in range(2):
            h = 2 * p + half
            j, g = h // 4, h % 4
            r = (norm_rope(xp[HEAD_DIM * half:HEAD_DIM * (half + 1)], qg_ref[...]) * (ATT_SCALE * LOG2E)).astype(BF16)
            qt_ref[j, HEAD_DIM * j:HEAD_DIM * (j + 1), g * tq:(g + 1) * tq] = r
            qt_ref[j, HEAD_DIM * (1 - j):HEAD_DIM * (2 - j), g * tq:(g + 1) * tq] = zeros
    xk = x[:, Q_COLS:Q_COLS + KV_COLS].T
    k0 = norm_rope(xk[0:HEAD_DIM], kg_ref[...])
    k1 = norm_rope(xk[HEAD_DIM:2 * HEAD_DIM], kg_ref[...])
    k_ref[...] = jnp.concatenate([k0, k1], axis=0).T.astype(BF16)
    xv = x[:, Q_COLS + KV_COLS:].T.astype(BF16)
    ones = jnp.ones((VT_ROWS - HEAD_DIM, tq), BF16)
    for j in range(2):
        vt_ref[j, 0:HEAD_DIM, :] = xv[HEAD_DIM * j:HEAD_DIM * (j + 1)]
        vt_ref[j, HEAD_DIM:VT_ROWS, :] = ones


def _prep_qkv(pq, cos_t, sin_t, q_g, k_g, bx, tx):
    tq = ATT_TQ
    nq = tx // tq
    return pl.pallas_call(
        functools.partial(_prep_kernel, tq=tq),
        grid=(bx, nq),
        in_specs=[pl.BlockSpec((tq, Q_COLS + 2 * KV_COLS), lambda b, t: (b * nq + t, 0)),
                  pl.BlockSpec((HEAD_DIM, tq), lambda b, t: (0, t)),
                  pl.BlockSpec((HEAD_DIM, tq), lambda b, t: (0, t)),
                  _full_spec((HEAD_DIM, 1)), _full_spec((HEAD_DIM, 1))],
        out_specs=[pl.BlockSpec((None, None, 2, LANES, 4 * tq), lambda b, t: (b, t, 0, 0, 0)),
                   pl.BlockSpec((None, None, tq, LANES), lambda b, t: (b, t, 0, 0)),
                   pl.BlockSpec((None, 2, None, VT_ROWS, tq), lambda b, t: (b, 0, t, 0, 0))],
        out_shape=[jax.ShapeDtypeStruct((bx, nq, 2, LANES, 4 * tq), BF16),
                   jax.ShapeDtypeStruct((bx, nq, tq, LANES), BF16),
                   jax.ShapeDtypeStruct((bx, 2, nq, VT_ROWS, tq), BF16)],
        compiler_params=_cparams("parallel", "parallel"),
        name="prep_qkv",
    )(pq, cos_t, sin_t, q_g.reshape(HEAD_DIM, 1), k_g.reshape(HEAD_DIM, 1))


def _rope_tables(t):
    pos = np.arange(t)
    half = HEAD_DIM // 4
    inv = ROPE_THETA ** (-jnp.arange(half, dtype=F32) * 2.0 / (HEAD_DIM // 2))
    dd = np.arange(HEAD_DIM)
    part_pos = np.where((dd // (HEAD_DIM // 2))[:, None] == 0, (pos // GRID_W)[None, :], (pos % GRID_W)[None, :])
    ang = jnp.asarray(part_pos, F32) * inv[dd % half][:, None]
    sign = jnp.asarray(np.where((dd % (HEAD_DIM // 2)) < half, -1.0, 1.0)[:, None], F32)
    return jnp.cos(ang), jnp.sin(ang) * sign


def _attn_kernel(qt_ref, k_ref, vt_ref, o_ref, acc_ref, *, nk, tq):
    ngrp = qt_ref.shape[0]
    acc_ref[...] = jnp.zeros_like(acc_ref)
    cw = ATT_CW
    strips = [(j, n) for j in range(ngrp) for n in range(4 * tq // cw)]

    def scores(c, j, n):
        return jnp.dot(k_ref[c], qt_ref[j, :, cw * n:cw * (n + 1)], preferred_element_type=F32)

    s = [scores(0, j, n) for j, n in strips]
    m_prev = [jnp.full((1, cw), -jnp.inf, F32)] * len(strips)
    m = [jnp.max(sn, axis=0, keepdims=True) for sn in s]
    for c in range(nk):
        for i, (j, n) in enumerate(strips):
            s_next = scores(c + 1, j, n) if c + 1 < nk else None
            alpha = jnp.exp2(m_prev[i] - m[i])
            p = jnp.exp2((s[i] - m[i]).astype(BF16))
            pv = jnp.dot(vt_ref[j, c], p, preferred_element_type=F32)
            acc_ref[j, :, cw * n:cw * (n + 1)] = acc_ref[j, :, cw * n:cw * (n + 1)] * alpha + pv
            if s_next is not None:
                m_prev[i] = m[i]
                m[i] = jnp.maximum(m[i], jnp.max(s_next, axis=0, keepdims=True))
                s[i] = s_next
    for j in range(ngrp):
        o = acc_ref[j, 0:HEAD_DIM, :] * (1.0 / acc_ref[j, HEAD_DIM:HEAD_DIM + 1, :])
        for pp in range(2):
            blk = jnp.concatenate([o[:, (2 * pp) * tq:(2 * pp + 1) * tq],
                                   o[:, (2 * pp + 1) * tq:(2 * pp + 2) * tq]], axis=0)
            lane0 = LANES * (2 * j + pp)
            o_ref[:, lane0:lane0 + LANES] = blk.T.astype(BF16)


def _gqa_attention(qt, k, vt, tx):
    bx, nq = qt.shape[0], qt.shape[1]
    nk = k.shape[1]
    tq = ATT_TQ
    return pl.pallas_call(
        functools.partial(_attn_kernel, nk=nk, tq=tq),
        grid=(bx, nq),
        in_specs=[pl.BlockSpec((None, None, 2, LANES, 4 * tq), lambda b, t: (b, t, 0, 0, 0)),
                  pl.BlockSpec((None, nk, tq, LANES), lambda b, t: (b, 0, 0, 0)),
                  pl.BlockSpec((None, 2, nk, VT_ROWS, tq), lambda b, t: (b, 0, 0, 0, 0))],
        out_specs=pl.BlockSpec((tq, Q_COLS), lambda b, t: (b * nq + t, 0)),
        out_shape=jax.ShapeDtypeStruct((bx * tx, Q_COLS), BF16),
        scratch_shapes=[pltpu.VMEM((2, VT_ROWS, 4 * tq), F32)],
        compiler_params=_cparams("parallel", "parallel"),
        name="gqa_attention",
    )(qt, k, vt)


def _outproj_ln_kernel(*refs, n_lhs):
    lhs = refs[:n_lhs]
    w_ref, x_ref, gate_ref, lng_ref, lnb_ref, o_ref = refs[n_lhs:]
    y = None
    off = 0
    for r in lhs:
        kk = r.shape[1]
        t = jnp.dot(r[...], w_ref[off:off + kk, :], preferred_element_type=F32)
        off += kk
        y = t if y is None else y + t
    z = ALPHA * x_ref[...] + gate_ref[0] * y
    o_ref[...] = _layer_norm(z, lng_ref[...], lnb_ref[...])


def _outproj_ln(lhs, w, x2d, vecs, gate_k, ln_g, ln_b, rows_per_batch, tm):
    r = x2d.shape[0]
    return pl.pallas_call(
        functools.partial(_outproj_ln_kernel, n_lhs=len(lhs)),
        grid=(r // tm,),
        in_specs=[pl.BlockSpec((tm, a.shape[1]), lambda i: (i, 0)) for a in lhs]
        + [_full_spec(w.shape), pl.BlockSpec((tm, D_MODEL), lambda i: (i, 0)),
           _vec_spec(gate_k, tm, rows_per_batch), _full_spec((1, D_MODEL)), _full_spec((1, D_MODEL))],
        out_specs=pl.BlockSpec((tm, D_MODEL), lambda i: (i, 0)),
        out_shape=jax.ShapeDtypeStruct((r, D_MODEL), F32),
        compiler_params=_cparams("parallel"),
        name="outproj_ln",
    )(*lhs, w, x2d, vecs, ln_g.reshape(1, D_MODEL), ln_b.reshape(1, D_MODEL))


def _ffn_kernel(x_ref, sc_ref, sh_ref, gate_ref, wg_ref, wu_ref, wd_ref, lng_ref, lnb_ref, o_ref):
    x = x_ref[...]
    hb = (x * (1.0 + sc_ref[0]) + sh_ref[0]).astype(BF16)
    hg = jnp.dot(hb, wg_ref[...], preferred_element_type=F32)
    hu = jnp.dot(hb, wu_ref[...], preferred_element_type=F32)
    h1 = (hg * jax.nn.sigmoid(hg) * hu).astype(BF16)
    y = jnp.dot(h1, wd_ref[...], preferred_element_type=F32)
    o_ref[...] = _layer_norm(ALPHA * x + gate_ref[0] * y, lng_ref[...], lnb_ref[...])


def _ffn_ln(x2d, vecs, wg, wu, wd, ln_g, ln_b, rows_per_batch, tm):
    r = x2d.shape[0]

    def resident(shape):
        return pl.BlockSpec(shape, lambda i: (0, 0), pipeline_mode=pl.Buffered(1))

    return pl.pallas_call(
        _ffn_kernel,
        grid=(r // tm,),
        in_specs=[pl.BlockSpec((tm, D_MODEL), lambda i: (i, 0)),
                  _vec_spec(4, tm, rows_per_batch), _vec_spec(3, tm, rows_per_batch),
                  _vec_spec(5, tm, rows_per_batch),
                  resident(wg.shape), resident(wu.shape), resident(wd.shape),
                  _full_spec((1, D_MODEL)), _full_spec((1, D_MODEL))],
        out_specs=pl.BlockSpec((tm, D_MODEL), lambda i: (i, 0)),
        out_shape=jax.ShapeDtypeStruct((r, D_MODEL), F32),
        compiler_params=_cparams("parallel"),
        name="ffn_ln",
    )(x2d, vecs, vecs, vecs, wg, wu, wd, ln_g.reshape(1, D_MODEL), ln_b.reshape(1, D_MODEL))


def _inproj1_kernel(x_ref, sc_ref, sh_ref, w_ref, *o_refs, with_q):
    h = x_ref[...] * (1.0 + sc_ref[0]) + sh_ref[0]
    o = jnp.dot(h.astype(BF16), w_ref[...], preferred_element_type=F32)
    if with_q:
        qt_ref, k_ref, vt_ref = o_refs
        qt_ref[...] = (o[:, :D_MODEL] * (ATT_SCALE * LOG2E)).T.astype(BF16)
    else:
        k_ref, vt_ref = o_refs
    nk = o.shape[1] - 2 * D_MODEL
    k_ref[...] = o[:, nk:nk + D_MODEL].astype(BF16)
    vt_ref[...] = o[:, nk + D_MODEL:].T.astype(BF16)


def _inproj1(x2d, vecs, w, rows_per_batch, tm, with_q):
    r = x2d.shape[0]
    nat = pl.BlockSpec((tm, D_MODEL), lambda i: (i, 0))
    tr = pl.BlockSpec((D_MODEL, tm), lambda i: (0, i))
    nat_shape = jax.ShapeDtypeStruct((r, D_MODEL), BF16)
    tr_shape = jax.ShapeDtypeStruct((D_MODEL, r), BF16)
    return pl.pallas_call(
        functools.partial(_inproj1_kernel, with_q=with_q),
        grid=(r // tm,),
        in_specs=[pl.BlockSpec((tm, D_MODEL), lambda i: (i, 0)),
                  _vec_spec(1, tm, rows_per_batch), _vec_spec(0, tm, rows_per_batch),
                  _full_spec(w.shape)],
        out_specs=([tr] if with_q else []) + [nat, tr],
        out_shape=([tr_shape] if with_q else []) + [nat_shape, tr_shape],
        compiler_params=_cparams("parallel"),
        name="inproj1",
    )(x2d, vecs, vecs, w)


def _na_bias(rpb, n_rows, n_ctx):
    nh = rpb.shape[0]
    c = np.arange(GRID_W)
    cs = np.clip(c - WIN_C // 2, 0, GRID_W - WIN_C)
    in_c = (c[None, :] >= cs[:, None]) & (c[None, :] < cs[:, None] + WIN_C)
    dc = c[None, :] - c[:, None] + WIN_C - 1
    pick = ((dc[None] == np.arange(2 * WIN_C - 1)[:, None, None]) & in_c[None]).astype(np.float32)
    cols = jnp.einsum("hrd,dkc->hrkc", rpb, jnp.asarray(pick.transpose(0, 2, 1)), precision=lax.Precision.HIGHEST)
    cols = jnp.where(jnp.asarray(in_c.T)[None, None], cols * LOG2E, MASK_VALUE)
    n_dr = 2 * WIN_R - 1
    cols = jnp.concatenate([cols, jnp.full((nh, 1, GRID_W, GRID_W), MASK_VALUE, F32)], axis=1)
    pick_dr = np.full((3, NA_ROWS, NA_KROWS), n_dr, np.int32)
    for case, r0 in enumerate((0, NA_ROWS, n_rows - NA_ROWS)):
        start = int(np.clip(r0 - WIN_R // 2, 0, n_rows - NA_KROWS))
        for ri in range(NA_ROWS):
            r = r0 + ri
            rs = int(np.clip(r - WIN_R // 2, 0, n_rows - WIN_R))
            for ki in range(NA_KROWS):
                kr = start + ki
                if rs <= kr < rs + WIN_R:
                    pick_dr[case, ri, ki] = kr - r + WIN_R - 1
    npair = nh // 2
    nkeys = NA_KROWS * GRID_W
    out = pl.pallas_call(
        functools.partial(_na_bias_kernel, nkeys=nkeys),
        grid_spec=pltpu.PrefetchScalarGridSpec(
            num_scalar_prefetch=1,
            grid=(3, npair),
            in_specs=[pl.BlockSpec((2, n_dr + 1, GRID_W, GRID_W), lambda case, p, dr: (p, 0, 0, 0))],
            out_specs=pl.BlockSpec((None, None, nkeys + n_ctx, 2 * NA_ROWS * GRID_W),
                                   lambda case, p, dr: (case, p, 0, 0)),
        ),
        out_shape=jax.ShapeDtypeStruct((3, npair, nkeys + n_ctx, 2 * NA_ROWS * GRID_W), F32),
        compiler_params=_cparams("parallel", "parallel"),
        name="na_bias",
    )(jnp.asarray(pick_dr.reshape(-1)), cols)
    return out.reshape(3, npair // NA_PAIRS, NA_PAIRS, nkeys + n_ctx, 2 * NA_ROWS * GRID_W)


def _na_bias_kernel(dr_ref, cols_ref, o_ref, *, nkeys):
    case = pl.program_id(0)
    for ki in range(NA_KROWS):
        pieces = []
        for e in range(2):
            for ri in range(NA_ROWS):
                d = dr_ref[(case * NA_ROWS + ri) * NA_KROWS + ki]
                pieces.append(cols_ref[e, d])
        o_ref[GRID_W * ki:GRID_W * (ki + 1), :] = jnp.concatenate(pieces, axis=1)
    o_ref[nkeys:, :] = jnp.zeros((o_ref.shape[0] - nkeys, o_ref.shape[1]), F32)


def _na_kernel(qt_ref, k0_ref, k1_ref, k2_ref, kc_ref, v0_ref, v1_ref, v2_ref, vc_ref, bias_ref, o_ref):
    qt = qt_ref[...]
    kk = jnp.concatenate([k0_ref[...], k1_ref[...], k2_ref[...], kc_ref[...]], axis=0)
    vt = jnp.concatenate([v0_ref[...], v1_ref[...], v2_ref[...], vc_ref[...]], axis=1)
    nq = qt.shape[1]
    zeros = jnp.zeros((HEAD_DIM, nq), BF16)
    ones = jnp.ones((VT_ROWS - HEAD_DIM, kk.shape[0]), BF16)

    def scores(pp):
        q2 = qt[LANES * pp:LANES * (pp + 1)]
        qcat = jnp.concatenate([jnp.concatenate([q2[0:HEAD_DIM], zeros], axis=0),
                                jnp.concatenate([zeros, q2[HEAD_DIM:]], axis=0)], axis=1)
        return jnp.dot(kk[:, LANES * pp:LANES * (pp + 1)], qcat, preferred_element_type=F32) + bias_ref[pp]

    s_all = [scores(pp) for pp in range(NA_PAIRS)]
    for pp in range(NA_PAIRS):
        s = s_all[pp]
        m = jnp.max(s, axis=0, keepdims=True)
        p = jnp.exp2((s - m).astype(BF16))
        vext = jnp.concatenate([vt[LANES * pp:LANES * (pp + 1)], ones], axis=0)
        pv = jnp.dot(vext, p, preferred_element_type=F32)
        o0 = pv[0:HEAD_DIM, 0:nq] * (1.0 / pv[LANES:LANES + 1, 0:nq])
        o1 = pv[HEAD_DIM:LANES, nq:] * (1.0 / pv[LANES:LANES + 1, nq:])
        o_ref[:, LANES * pp:LANES * (pp + 1)] = jnp.concatenate([o0, o1], axis=0).T.astype(BF16)


def _neighbourhood_attention(qt, k, vt, kc, vct, bias, bsz, t):
    nq = NA_ROWS * GRID_W
    nrb = t // nq
    wl = NA_PAIRS * LANES
    ngrp = D_MODEL // wl
    nctx = kc.shape[0] // bsz
    nwin = NA_KROWS // NA_ROWS

    def first_kblock(b, rb):
        return b * nrb + jnp.clip(rb - 1, 0, nrb - nwin)

    def kspec(d):
        return pl.BlockSpec((nq, wl), lambda hp, b, rb: (first_kblock(b, rb) + d, hp))

    def vspec(d):
        return pl.BlockSpec((wl, nq), lambda hp, b, rb: (hp, first_kblock(b, rb) + d))

    def case(rb):
        return jnp.where(rb == 0, 0, jnp.where(rb == nrb - 1, 2, 1))

    return pl.pallas_call(
        _na_kernel,
        grid=(ngrp, bsz, nrb),
        in_specs=[pl.BlockSpec((wl, nq), lambda hp, b, rb: (hp, b * nrb + rb)),
                  kspec(0), kspec(1), kspec(2), pl.BlockSpec((nctx, wl), lambda hp, b, rb: (b, hp)),
                  vspec(0), vspec(1), vspec(2), pl.BlockSpec((wl, nctx), lambda hp, b, rb: (hp, b)),
                  pl.BlockSpec((None, None, NA_PAIRS, NA_KROWS * GRID_W + nctx, 2 * nq),
                               lambda hp, b, rb: (case(rb), hp, 0, 0, 0))],
        out_specs=pl.BlockSpec((nq, wl), lambda hp, b, rb: (b * nrb + rb, hp)),
        out_shape=jax.ShapeDtypeStruct((bsz * t, D_MODEL), BF16),
        compiler_params=_cparams("parallel", "parallel", "parallel"),
        name="neighbourhood_attention",
    )(qt, k, k, k, kc, vt, vt, vt, vct, bias)


def _store_row_tiles(dst_ref, val):
    rows = val.shape[0]
    for lt in range(ROW_TILES):
        dst_ref[pl.ds(lt, rows, stride=ROW_TILES), :] = val[:, lt * LANES:(lt + 1) * LANES]


def _load_row_tiles(src_ref, lt, rows):
    return src_ref[pl.ds(lt, rows, stride=ROW_TILES), :]


def _router_kernel(x_ref, sc_ref, sh_ref, wr_ref, h_ref, r_ref):
    h = x_ref[...] * (1.0 + sc_ref[0]) + sh_ref[0]
    _store_row_tiles(h_ref, h)
    hi = h.astype(BF16)
    lo = (h - hi.astype(F32)).astype(BF16)
    w = wr_ref[...]
    whi = w.astype(BF16)
    wlo = (w - whi.astype(F32)).astype(BF16)
    lg = (jnp.dot(hi, whi, preferred_element_type=F32)
          + (jnp.dot(hi, wlo, preferred_element_type=F32) + jnp.dot(lo, whi, preferred_element_type=F32)))
    lane = lax.broadcasted_iota(jnp.int32, lg.shape, 1).astype(F32)
    lg = jnp.where(lane < N_EXPERTS, lg, -jnp.inf)
    v1 = jnp.max(lg, axis=1, keepdims=True)
    i1 = jnp.min(jnp.where(lg == v1, lane, float(LANES)), axis=1, keepdims=True)
    lg2 = jnp.where(lane == i1, -jnp.inf, lg)
    v2 = jnp.max(lg2, axis=1, keepdims=True)
    i2 = jnp.min(jnp.where(lg2 == v2, lane, float(LANES)), axis=1, keepdims=True)
    e = jnp.exp(v2 - v1)
    w1 = 1.0 / (1.0 + e)
    w2 = e / (1.0 + e)
    r_ref[...] = jnp.where(lane == 0, i1, jnp.where(lane == 1, i2, jnp.where(lane == 2, w1,
                                                                              jnp.where(lane == 3, w2, 0.0))))


def _router(x2d, vecs, w_router, rows_per_batch, tm):
    r = x2d.shape[0]
    wr = jnp.pad(w_router, ((0, 0), (0, LANES - N_EXPERTS)))
    return pl.pallas_call(
        _router_kernel,
        grid=(r // tm,),
        in_specs=[pl.BlockSpec((tm, D_MODEL), lambda i: (i, 0)),
                  _vec_spec(4, tm, rows_per_batch), _vec_spec(3, tm, rows_per_batch),
                  _full_spec((D_MODEL, LANES))],
        out_specs=[pl.BlockSpec((tm * ROW_TILES, LANES), lambda i: (i, 0)),
                   pl.BlockSpec((tm, LANES), lambda i: (i, 0))],
        out_shape=[jax.ShapeDtypeStruct((r * ROW_TILES, LANES), F32), jax.ShapeDtypeStruct((r, LANES), F32)],
        compiler_params=_cparams("parallel"),
        name="router",
    )(x2d, vecs, vecs, wr)


def _row_copy(src_hbm, row, r, dst_ref, sem):
    def tile(i):
        start = i * ROW_TILES
        return pl.ds(start if isinstance(i, int) else pl.multiple_of(start, ROW_TILES), ROW_TILES)

    return pltpu.make_async_copy(src_hbm.at[tile(row), :], dst_ref.at[tile(r), :], sem)


def _start_row_gather(src_hbm, idx_ref, dst_ref, sem, n, priorities):
    def issue(i, carry):
        for u in range(2):
            r = 2 * i + u
            _row_copy(src_hbm, idx_ref[0, r], r, dst_ref, sem).start(priority=priorities[u])
        return carry

    lax.fori_loop(0, n // 2, issue, 0, unroll=4)


def _wait_row_gather(src_hbm, dst_ref, sem, n):
    def wait(r, carry):
        _row_copy(src_hbm, 0, r, dst_ref, sem).wait()
        return carry

    lax.fori_loop(0, n, wait, 0, unroll=8)


def _moe_ffn_kernel(te_ref, nu_ref, idx0_ref, idxn_ref, h_hbm, wg_ref, wu_ref, wd_ref, o_ref,
                    xbuf_ref, hb_ref, acc_ref, sem):
    t = pl.program_id(0)
    j = pl.program_id(1)
    nj = pl.num_programs(1)
    used = t < nu_ref[0]
    tm = hb_ref.shape[0]
    slot = t % 2

    @pl.when(jnp.logical_and(used, j == 0))
    def _():
        @pl.when(t == 0)
        def _():
            _start_row_gather(h_hbm, idx0_ref, xbuf_ref.at[0], sem.at[0], tm, (0, 0))

        @pl.when(t + 1 < nu_ref[0])
        def _():
            _start_row_gather(h_hbm, idxn_ref, xbuf_ref.at[1 - slot], sem.at[1 - slot], tm, (0, 0))

        _wait_row_gather(h_hbm, xbuf_ref.at[slot], sem.at[slot], tm)

    @pl.when(jnp.logical_and(used, j == 0))
    def _():
        for lt in range(ROW_TILES):
            hb_ref[:, lt * LANES:(lt + 1) * LANES] = _load_row_tiles(xbuf_ref.at[slot], lt, tm).astype(BF16)
        acc_ref[...] = jnp.zeros_like(acc_ref)

    @pl.when(used)
    def _():
        hb = hb_ref[...]
        hg = jnp.dot(hb, wg_ref[...], preferred_element_type=F32)
        hu = jnp.dot(hb, wu_ref[...], preferred_element_type=F32)
        h1 = (hg * jax.nn.sigmoid(hg) * hu).astype(BF16)
        acc_ref[...] += jnp.dot(h1, wd_ref[...], preferred_element_type=F32)

    @pl.when(jnp.logical_and(used, j == nj - 1))
    def _():
        _store_row_tiles(o_ref, acc_ref[...])

    @pl.when(jnp.logical_and(jnp.logical_not(used), j == nj - 1))
    def _():
        o_ref[...] = jnp.zeros_like(o_ref)


def _moe_ffn(h, src, tile_expert, n_used, wg, wu, wd):
    p = src.shape[0]
    tm, tf = MOE_TM, MOE_TF
    nt = p // tm
    f = wg.shape[2]

    def jj(t, j, nu):
        return jnp.where(t < nu[0], j, 0)

    grid_spec = pltpu.PrefetchScalarGridSpec(
        num_scalar_prefetch=2,
        grid=(nt, f // tf),
        in_specs=[pl.BlockSpec((None, 1, tm), lambda t, j, te, nu: (0, 0, 0), memory_space=pltpu.SMEM),
                  pl.BlockSpec((None, 1, tm), lambda t, j, te, nu: (jnp.minimum(t + 1, nt - 1), 0, 0),
                               memory_space=pltpu.SMEM),
                  pl.BlockSpec(memory_space=pl.ANY),
                  pl.BlockSpec((None, D_MODEL, tf), lambda t, j, te, nu: (te[t], 0, jj(t, j, nu))),
                  pl.BlockSpec((None, D_MODEL, tf), lambda t, j, te, nu: (te[t], 0, jj(t, j, nu))),
                  pl.BlockSpec((None, tf, D_MODEL), lambda t, j, te, nu: (te[t], jj(t, j, nu), 0))],
        out_specs=pl.BlockSpec((tm * ROW_TILES, LANES), lambda t, j, te, nu: (t, 0)),
        scratch_shapes=[pltpu.VMEM((2, tm * ROW_TILES, LANES), F32), pltpu.VMEM((tm, D_MODEL), BF16),
                        pltpu.VMEM((tm, D_MODEL), F32), pltpu.SemaphoreType.DMA((2,))],
    )
    idx = src.reshape(nt, 1, tm)
    return pl.pallas_call(
        _moe_ffn_kernel,
        grid_spec=grid_spec,
        out_shape=jax.ShapeDtypeStruct((p * ROW_TILES, LANES), F32),
        compiler_params=_cparams("arbitrary", "arbitrary"),
        name="moe_ffn",
    )(tile_expert, n_used, idx, idx, h, wg, wu, wd)


def _combine_ln_kernel(p1a_ref, p2a_ref, p1b_ref, p2b_ref, ys_hbm, route_ref, x_ref, gate_ref, lng_ref, lnb_ref,
                       o_ref, y_ref, sem):
    i = pl.program_id(0)
    n = x_ref.shape[0]
    slot = i % 2

    def start(p1_ref, p2_ref, s):
        _start_row_gather(ys_hbm, p1_ref, y_ref.at[s, 0], sem.at[s, 0], n, (0, 1))
        _start_row_gather(ys_hbm, p2_ref, y_ref.at[s, 1], sem.at[s, 1], n, (0, 1))

    @pl.when(i == 0)
    def _():
        start(p1a_ref, p2a_ref, 0)

    @pl.when(i + 1 < pl.num_programs(0))
    def _():
        start(p1b_ref, p2b_ref, 1 - slot)

    for e in range(2):
        _wait_row_gather(ys_hbm, y_ref.at[slot, e], sem.at[slot, e], n)
    w1 = route_ref[:, 2:3]
    w2 = route_ref[:, 3:4]
    y = jnp.concatenate([w1 * _load_row_tiles(y_ref.at[slot, 0], lt, n) + w2 * _load_row_tiles(y_ref.at[slot, 1], lt, n)
                         for lt in range(ROW_TILES)], axis=1)
    z = ALPHA * x_ref[...] + gate_ref[0] * y
    o_ref[...] = _layer_norm(z, lng_ref[...], lnb_ref[...])


def _combine_ln(ys, pos1, pos2, route, x2d, vecs, ln_g, ln_b, rows_per_batch):
    r = x2d.shape[0]
    g = GATHER_ROWS
    ns = r // g
    first = pl.BlockSpec((None, 1, g), lambda i: (0, 0, 0), memory_space=pltpu.SMEM)
    ahead = pl.BlockSpec((None, 1, g), lambda i: (jnp.minimum(i + 1, ns - 1), 0, 0), memory_space=pltpu.SMEM)
    p1 = pos1.reshape(ns, 1, g)
    p2 = pos2.reshape(ns, 1, g)
    return pl.pallas_call(
        _combine_ln_kernel,
        grid=(ns,),
        in_specs=[first, first, ahead, ahead, pl.BlockSpec(memory_space=pl.ANY),
                  pl.BlockSpec((g, LANES), lambda i: (i, 0)),
                  pl.BlockSpec((g, D_MODEL), lambda i: (i, 0)),
                  _vec_spec(5, g, rows_per_batch), _full_spec((1, D_MODEL)), _full_spec((1, D_MODEL))],
        out_specs=pl.BlockSpec((g, D_MODEL), lambda i: (i, 0)),
        out_shape=jax.ShapeDtypeStruct((r, D_MODEL), F32),
        scratch_shapes=[pltpu.VMEM((2, 2, g * ROW_TILES, LANES), F32), pltpu.SemaphoreType.DMA((2, 2))],
        compiler_params=_cparams("arbitrary"),
        name="combine_ln",
    )(p1, p2, p1, p2, ys, route, x2d, vecs, ln_g.reshape(1, D_MODEL), ln_b.reshape(1, D_MODEL))


def _routing_plan(route, tm):
    n = route.shape[0]
    e = jnp.concatenate([route[:, 0], route[:, 1]]).astype(jnp.int32)
    tok = jnp.concatenate([jnp.arange(n, dtype=jnp.int32)] * 2)
    onehot = (e[:, None] == jnp.arange(N_EXPERTS, dtype=jnp.int32)[None, :]).astype(jnp.int32)
    csum = jnp.cumsum(onehot, axis=0)
    rank = jnp.sum(csum * onehot, axis=1) - 1
    counts = csum[-1]
    padded = ((counts + tm - 1) // tm) * tm
    ends = jnp.cumsum(padded)
    starts = ends - padded
    pos = jnp.sum(starts[None, :] * onehot, axis=1) + rank
    p = 2 * n + N_EXPERTS * tm
    src = jnp.zeros((p,), jnp.int32).at[pos].set(tok)
    tile_start = jnp.arange(p // tm, dtype=jnp.int32) * tm
    tile_expert = jnp.minimum(jnp.sum((tile_start[:, None] >= ends[None, :]).astype(jnp.int32), axis=1),
                              N_EXPERTS - 1).astype(jnp.int32)
    n_used = (ends[-1] // tm).astype(jnp.int32).reshape(1)
    return src, tile_expert, n_used, pos[:n], pos[n:]


def kernel(x, c, ctx, c_ctx, w_mod, b_mod, ln_g, ln_b, ab_w_in, ab_conv_w, ab_conv_g, ab_conv_b, ab_q_g, ab_k_g,
           ab_w_out, ffn_w_gate, ffn_w_up, ffn_w_down, na_w_qkv, na_rpb, na_w_out, moe_w_router, moe_w_gate,
           moe_w_up, moe_w_down):
    bsz, t, d = x.shape
    n_ctx = ctx.shape[1]
    n = bsz * t
    nc = bsz * n_ctx
    x2 = x.reshape(n, d)
    c2 = ctx.reshape(nc, d)

    cc = jnp.concatenate([c, c_ctx[None, :], jnp.zeros((8 - bsz - 1, d), F32)], axis=0)
    mod = _modulation(cc, w_mod, b_mod)
    vec0 = mod[0].reshape(8 * 6, 1, d)
    vec1 = mod[1].reshape(8 * 6, 1, d)

    w_in = ab_w_in[0].astype(BF16)
    w_out = ab_w_out[0].astype(BF16)
    pa, pq = _inproj0(x2, vec0, w_in, t, 512)
    pac, pqc = _inproj0(c2, vec0, w_in, None, 512)
    a = _conformer_conv(pa, ab_conv_w[0], ab_conv_g[0], ab_conv_b[0], t, 512)
    ac = _conformer_conv(pac, ab_conv_w[0], ab_conv_g[0], ab_conv_b[0], n_ctx, n_ctx)
    cos_t, sin_t = _rope_tables(t)
    qt, k, vt = _prep_qkv(pq, cos_t, sin_t, ab_q_g[0], ab_k_g[0], bsz, t)
    ones = jnp.ones((HEAD_DIM, n_ctx), F32)
    qtc, kc, vtc = _prep_qkv(pqc, ones, jnp.zeros_like(ones), ab_q_g[0], ab_k_g[0], bsz, n_ctx)
    o = _gqa_attention(qt, jnp.concatenate([kc, k], axis=1), jnp.concatenate([vtc, vt], axis=2), t)
    oc = _gqa_attention(qtc, kc, vtc, n_ctx)
    x2 = _outproj_ln([a, o], w_out, x2, vec0, 2, ln_g[0, 0], ln_b[0, 0], t, 512)
    c2 = _outproj_ln([ac, oc], w_out, c2, vec0, 2, ln_g[0, 0], ln_b[0, 0], None, 512)
    wg = ffn_w_gate[0].astype(BF16)
    wu = ffn_w_up[0].astype(BF16)
    wd = ffn_w_down[0].astype(BF16)
    x2 = _ffn_ln(x2, vec0, wg, wu, wd, ln_g[0, 1], ln_b[0, 1], t, 512)
    c2 = _ffn_ln(c2, vec0, wg, wu, wd, ln_g[0, 1], ln_b[0, 1], None, 512)

    w_qkv = na_w_qkv[0].astype(BF16)
    qt1, k1, vt1 = _inproj1(x2, vec1, w_qkv, t, 512, True)
    kc1, vct1 = _inproj1(c2, vec1, w_qkv[:, d:], None, 512, False)
    o = _neighbourhood_attention(qt1, k1, vt1, kc1, vct1, _na_bias(na_rpb[0], t // GRID_W, n_ctx), bsz, t)
    x2 = _outproj_ln([o], na_w_out[0].astype(BF16), x2, vec1, 2, ln_g[1, 0], ln_b[1, 0], t, 512)

    h, route = _router(x2, vec1, moe_w_router[0], t, 512)
    src, tile_expert, n_used, pos1, pos2 = _routing_plan(route, MOE_TM)
    ys = _moe_ffn(h, src, tile_expert, n_used, moe_w_gate[0].astype(BF16), moe_w_up[0].astype(BF16),
                  moe_w_down[0].astype(BF16))
    x2 = _combine_ln(ys, pos1, pos2, route, x2, vec1, ln_g[1, 1], ln_b[1, 1], t)
    return x2.reshape(bsz, t, d)
```

```python
import functools

import numpy as np
import jax
import jax.numpy as jnp
from jax import lax
from jax.experimental import pallas as pl
from jax.experimental.pallas import tpu as pltpu

F32 = jnp.float32
BF16 = jnp.bfloat16

D_MODEL = 1024
GRID_W = 64
HEAD_DIM = 64
CONV_CH = 512
CONV_WIDTH = 31
CONV_HALO = 16
Q_COLS = 512
KV_COLS = 128
A_COLS = 2 * CONV_CH
ROPE_THETA = 10000.0
WIN_R = 8
WIN_C = 16
N_EXPERTS = 8
DEPTH = 2
ALPHA = (2 * DEPTH) ** 0.25
LN_EPS = 1e-5
RMS_EPS = 1e-6
ATT_SCALE = HEAD_DIM ** -0.5
LOG2E = 1.4426950408889634
MASK_VALUE = -1e30

LANES = 128
ROW_TILES = D_MODEL // LANES
VMEM_LIMIT = 56 * 1024 * 1024

ATT_TQ = 256
ATT_CW = 256
VT_ROWS = HEAD_DIM + 16
NA_ROWS = 4
NA_KROWS = 12
NA_PAIRS = 4
MOE_TM = 512
MOE_TF = 1792
GATHER_ROWS = 256


def _cparams(*sem):
    return pltpu.CompilerParams(dimension_semantics=sem, vmem_limit_bytes=VMEM_LIMIT)


def _layer_norm(z, g, b):
    mu = jnp.mean(z, axis=-1, keepdims=True)
    zc = z - mu
    var = jnp.mean(zc * zc, axis=-1, keepdims=True)
    return zc * lax.rsqrt(var + LN_EPS) * g + b


def _vec_spec(k, tm, rows_per_batch):
    if rows_per_batch is None:
        return pl.BlockSpec((1, 1, D_MODEL), lambda i, *_: (4 * 6 + k, 0, 0))
    return pl.BlockSpec((1, 1, D_MODEL), lambda i, *_: ((i * tm // rows_per_batch) * 6 + k, 0, 0))


def _full_spec(shape):
    nd = len(shape)
    return pl.BlockSpec(shape, lambda *_: (0,) * nd)


def _mod_kernel(c_ref, w_ref, b_ref, o_ref):
    c = c_ref[...]
    s = c * jax.nn.sigmoid(c)
    o_ref[...] = jnp.dot(s.astype(BF16), w_ref[...].astype(BF16), preferred_element_type=F32) + b_ref[...]


def _modulation(cc, w_mod, b_mod):
    n = 6 * D_MODEL
    tn = D_MODEL
    return pl.pallas_call(
        _mod_kernel,
        grid=(DEPTH, n // tn),
        in_specs=[pl.BlockSpec((8, D_MODEL), lambda l, j: (0, 0)),
                  pl.BlockSpec((None, D_MODEL, tn), lambda l, j: (l, 0, j)),
                  pl.BlockSpec((None, 1, tn), lambda l, j: (l, 0, j))],
        out_specs=pl.BlockSpec((None, 8, tn), lambda l, j: (l, 0, j)),
        out_shape=jax.ShapeDtypeStruct((DEPTH, 8, n), F32),
        compiler_params=_cparams("parallel", "parallel"),
        name="modulation",
    )(cc, w_mod, b_mod.reshape(DEPTH, 1, n))


def _inproj0_kernel(x_ref, sc_ref, sh_ref, w_ref, oa_ref, oq_ref):
    h = x_ref[...] * (1.0 + sc_ref[0]) + sh_ref[0]
    o = jnp.dot(h.astype(BF16), w_ref[...], preferred_element_type=F32)
    oa_ref[...] = o[:, :A_COLS]
    oq_ref[...] = o[:, A_COLS:]


def _inproj0(x2d, vecs, w, rows_per_batch, tm):
    r = x2d.shape[0]
    nq = w.shape[1] - A_COLS
    return pl.pallas_call(
        _inproj0_kernel,
        grid=(r // tm,),
        in_specs=[pl.BlockSpec((tm, D_MODEL), lambda i: (i, 0)),
                  _vec_spec(1, tm, rows_per_batch), _vec_spec(0, tm, rows_per_batch),
                  _full_spec(w.shape)],
        out_specs=[pl.BlockSpec((tm, A_COLS), lambda i: (i, 0)),
                   pl.BlockSpec((tm, nq), lambda i: (i, 0))],
        out_shape=[jax.ShapeDtypeStruct((r, A_COLS), F32), jax.ShapeDtypeStruct((r, nq), F32)],
        compiler_params=_cparams("parallel"),
        name="inproj0",
    )(x2d, vecs, vecs, w)


def _conv_kernel(pm_ref, pp_ref, pn_ref, w_ref, g_ref, b_ref, o_ref, u_ref, us_ref, cv_ref, *, tt, tx):
    i = pl.program_id(0)

    def sigmoid(v):
        return 0.5 * jnp.tanh(0.5 * v) + 0.5

    def glu(p):
        return p[:, :CONV_CH] * sigmoid(p[:, CONV_CH:])

    first = (i * tt) % tx == 0
    last = ((i + 1) * tt) % tx == 0
    u_ref[0:CONV_HALO, :] = jnp.where(first, 0.0, glu(pp_ref[...]))
    u_ref[CONV_HALO:CONV_HALO + tt, :] = glu(pm_ref[...])
    u_ref[CONV_HALO + tt:2 * CONV_HALO + tt, :] = jnp.where(last, 0.0, glu(pn_ref[...]))
    u_ref[2 * CONV_HALO + tt:, :] = jnp.zeros((8, CONV_CH), F32)
    ch = 32
    nrow = tt + 2 * CONV_HALO
    base = CONV_HALO - CONV_WIDTH // 2

    def shift_body(c, carry):
        r0 = pl.multiple_of(c * ch, ch)
        w = u_ref[pl.ds(r0, ch + 8), :]
        for s in range(1, 8):
            us_ref[s - 1, pl.ds(r0, ch), :] = pltpu.roll(w, ch + 8 - s, axis=0)[0:ch]
        return carry

    lax.fori_loop(0, nrow // ch, shift_body, 0)

    def body(c, carry):
        r0 = pl.multiple_of(c * ch, ch)
        acc = jnp.zeros((ch, CONV_CH), F32)
        for k in range(CONV_WIDTH):
            a, s = divmod(k + base, 8)
            src = u_ref if s == 0 else us_ref.at[s - 1]
            acc = acc + src[pl.ds(r0 + 8 * a, ch), :] * w_ref[pl.ds(k, 1), :]
        cv_ref[pl.ds(r0, ch), :] = acc
        return carry

    lax.fori_loop(0, tt // ch, body, 0)
    y = _layer_norm(cv_ref[...], g_ref[...], b_ref[...])
    o_ref[...] = (y * sigmoid(y)).astype(BF16)


def _conformer_conv(pa, conv_w, conv_g, conv_b, tx, tt):
    r = pa.shape[0]
    hb = tt // CONV_HALO
    nhb = r // CONV_HALO
    return pl.pallas_call(
        functools.partial(_conv_kernel, tt=tt, tx=tx),
        grid=(r // tt,),
        in_specs=[pl.BlockSpec((tt, A_COLS), lambda i: (i, 0)),
                  pl.BlockSpec((CONV_HALO, A_COLS), lambda i: (jnp.maximum(i * hb - 1, 0), 0)),
                  pl.BlockSpec((CONV_HALO, A_COLS), lambda i: (jnp.minimum((i + 1) * hb, nhb - 1), 0)),
                  _full_spec((CONV_WIDTH, CONV_CH)), _full_spec((1, CONV_CH)), _full_spec((1, CONV_CH))],
        out_specs=pl.BlockSpec((tt, CONV_CH), lambda i: (i, 0)),
        out_shape=jax.ShapeDtypeStruct((r, CONV_CH), BF16),
        scratch_shapes=[pltpu.VMEM((tt + 2 * CONV_HALO + 8, CONV_CH), F32),
                        pltpu.VMEM((7, tt + 2 * CONV_HALO, CONV_CH), F32),
                        pltpu.VMEM((tt, CONV_CH), F32)],
        compiler_params=_cparams("parallel"),
        name="conformer_conv",
    )(pa, pa, pa, conv_w, conv_g.reshape(1, CONV_CH), conv_b.reshape(1, CONV_CH))


def _prep_kernel(p_ref, cos_ref, sin_ref, qg_ref, kg_ref, qt_ref, k_ref, vt_ref, *, tq):
    x = p_ref[...]
    cos = cos_ref[...]
    sin = sin_ref[...]

    def norm_rope(xh, g):
        ms = jnp.mean(xh * xh, axis=0, keepdims=True)
        y = xh * lax.rsqrt(ms + RMS_EPS) * g
        sw = jnp.concatenate([y[16:32], y[0:16], y[48:64], y[32:48]], axis=0)
        return y * cos + sw * sin

    zeros = jnp.zeros((HEAD_DIM, tq), BF16)
    for p in range(Q_COLS // LANES):
        xp = x[:, LANES * p:LANES * (p + 1)].T
        for half in range(2):
            h = 2 * p + half
            j, g = h // 4, h % 4
            r = (norm_rope(xp[HEAD_DIM * half:HEAD_DIM * (half + 1)], qg_ref[...]) * (ATT_SCALE * LOG2E)).astype(BF16)
            qt_ref[j, HEAD_DIM * j:HEAD_DIM * (j + 1), g * tq:(g + 1) * tq] = r
            qt_ref[j, HEAD_DIM * (1 - j):HEAD_DIM * (2 - j), g * tq:(g + 1) * tq] = zeros
    xk = x[:, Q_COLS:Q_COLS + KV_COLS].T
    k0 = norm_rope(xk[0:HEAD_DIM], kg_ref[...])
    k1 = norm_rope(xk[HEAD_DIM:2 * HEAD_DIM], kg_ref[...])
    k_ref[...] = jnp.concatenate([k0, k1], axis=0).T.astype(BF16)
    xv = x[:, Q_COLS + KV_COLS:].T.astype(BF16)
    ones = jnp.ones((VT_ROWS - HEAD_DIM, tq), BF16)
    for j in range(2):
        vt_ref[j, 0:HEAD_DIM, :] = xv[HEAD_DIM * j:HEAD_DIM * (j + 1)]
        vt_ref[j, HEAD_DIM:VT_ROWS, :] = ones


def _prep_qkv(pq, cos_t, sin_t, q_g, k_g, bx, tx):
    tq = ATT_TQ
    nq = tx // tq
    return pl.pallas_call(
        functools.partial(_prep_kernel, tq=tq),
        grid=(bx, nq),
        in_specs=[pl.BlockSpec((tq, Q_COLS + 2 * KV_COLS), lambda b, t: (b * nq + t, 0)),
                  pl.BlockSpec((HEAD_DIM, tq), lambda b, t: (0, t)),
                  pl.BlockSpec((HEAD_DIM, tq), lambda b, t: (0, t)),
                  _full_spec((HEAD_DIM, 1)), _full_spec((HEAD_DIM, 1))],
        out_specs=[pl.BlockSpec((None, None, 2, LANES, 4 * tq), lambda b, t: (b, t, 0, 0, 0)),
                   pl.BlockSpec((None, None, tq, LANES), lambda b, t: (b, t, 0, 0)),
                   pl.BlockSpec((None, 2, None, VT_ROWS, tq), lambda b, t: (b, 0, t, 0, 0))],
        out_shape=[jax.ShapeDtypeStruct((bx, nq, 2, LANES, 4 * tq), BF16),
                   jax.ShapeDtypeStruct((bx, nq, tq, LANES), BF16),
                   jax.ShapeDtypeStruct((bx, 2, nq, VT_ROWS, tq), BF16)],
        compiler_params=_cparams("parallel", "parallel"),
        name="prep_qkv",
    )(pq, cos_t, sin_t, q_g.reshape(HEAD_DIM, 1), k_g.reshape(HEAD_DIM, 1))


def _rope_tables(t):
    pos = np.arange(t)
    half = HEAD_DIM // 4
    inv = ROPE_THETA ** (-jnp.arange(half, dtype=F32) * 2.0 / (HEAD_DIM // 2))
    dd = np.arange(HEAD_DIM)
    part_pos = np.where((dd // (HEAD_DIM // 2))[:, None] == 0, (pos // GRID_W)[None, :], (pos % GRID_W)[None, :])
    ang = jnp.asarray(part_pos, F32) * inv[dd % half][:, None]
    sign = jnp.asarray(np.where((dd % (HEAD_DIM // 2)) < half, -1.0, 1.0)[:, None], F32)
    return jnp.cos(ang), jnp.sin(ang) * sign


def _attn_kernel(qt_ref, k_ref, vt_ref, o_ref, acc_ref, *, nk, tq):
    ngrp = qt_ref.shape[0]
    acc_ref[...] = jnp.zeros_like(acc_ref)
    cw = ATT_CW
    strips = [(j, n) for j in range(ngrp) for n in range(4 * tq // cw)]

    def scores(c, j, n):
        return jnp.dot(k_ref[c], qt_ref[j, :, cw * n:cw * (n + 1)], preferred_element_type=F32)

    s = [scores(0, j, n) for j, n in strips]
    m_prev = [jnp.full((1, cw), -jnp.inf, F32)] * len(strips)
    m = [jnp.max(sn, axis=0, keepdims=True) for sn in s]
    for c in range(nk):
        for i, (j, n) in enumerate(strips):
            s_next = scores(c + 1, j, n) if c + 1 < nk else None
            alpha = jnp.exp2(m_prev[i] - m[i])
            p = jnp.exp2((s[i] - m[i]).astype(BF16))
            pv = jnp.dot(vt_ref[j, c], p, preferred_element_type=F32)
            acc_ref[j, :, cw * n:cw * (n + 1)] = acc_ref[j, :, cw * n:cw * (n + 1)] * alpha + pv
            if s_next is not None:
                m_prev[i] = m[i]
                m[i] = jnp.maximum(m[i], jnp.max(s_next, axis=0, keepdims=True))
                s[i] = s_next
    for j in range(ngrp):
        o = acc_ref[j, 0:HEAD_DIM, :] * (1.0 / acc_ref[j, HEAD_DIM:HEAD_DIM + 1, :])
        for pp in range(2):
            blk = jnp.concatenate([o[:, (2 * pp) * tq:(2 * pp + 1) * tq],
                                   o[:, (2 * pp + 1) * tq:(2 * pp + 2) * tq]], axis=0)
            lane0 = LANES * (2 * j + pp)
            o_ref[:, lane0:lane0 + LANES] = blk.T.astype(BF16)


def _gqa_attention(qt, k, vt, tx):
    bx, nq = qt.shape[0], qt.shape[1]
    nk = k.shape[1]
    tq = ATT_TQ
    return pl.pallas_call(
        functools.partial(_attn_kernel, nk=nk, tq=tq),
        grid=(bx, nq),
        in_specs=[pl.BlockSpec((None, None, 2, LANES, 4 * tq), lambda b, t: (b, t, 0, 0, 0)),
                  pl.BlockSpec((None, nk, tq, LANES), lambda b, t: (b, 0, 0, 0)),
                  pl.BlockSpec((None, 2, nk, VT_ROWS, tq), lambda b, t: (b, 0, 0, 0, 0))],
        out_specs=pl.BlockSpec((tq, Q_COLS), lambda b, t: (b * nq + t, 0)),
        out_shape=jax.ShapeDtypeStruct((bx * tx, Q_COLS), BF16),
        scratch_shapes=[pltpu.VMEM((2, VT_ROWS, 4 * tq), F32)],
        compiler_params=_cparams("parallel", "parallel"),
        name="gqa_attention",
    )(qt, k, vt)


def _outproj_ln_kernel(*refs, n_lhs):
    lhs = refs[:n_lhs]
    w_ref, x_ref, gate_ref, lng_ref, lnb_ref, o_ref = refs[n_lhs:]
    y = None
    off = 0
    for r in lhs:
        kk = r.shape[1]
        t = jnp.dot(r[...], w_ref[off:off + kk, :], preferred_element_type=F32)
        off += kk
        y = t if y is None else y + t
    z = ALPHA * x_ref[...] + gate_ref[0] * y
    o_ref[...] = _layer_norm(z, lng_ref[...], lnb_ref[...])


def _outproj_ln(lhs, w, x2d, vecs, gate_k, ln_g, ln_b, rows_per_batch, tm):
    r = x2d.shape[0]
    return pl.pallas_call(
        functools.partial(_outproj_ln_kernel, n_lhs=len(lhs)),
        grid=(r // tm,),
        in_specs=[pl.BlockSpec((tm, a.shape[1]), lambda i: (i, 0)) for a in lhs]
        + [_full_spec(w.shape), pl.BlockSpec((tm, D_MODEL), lambda i: (i, 0)),
           _vec_spec(gate_k, tm, rows_per_batch), _full_spec((1, D_MODEL)), _full_spec((1, D_MODEL))],
        out_specs=pl.BlockSpec((tm, D_MODEL), lambda i: (i, 0)),
        out_shape=jax.ShapeDtypeStruct((r, D_MODEL), F32),
        compiler_params=_cparams("parallel"),
        name="outproj_ln",
    )(*lhs, w, x2d, vecs, ln_g.reshape(1, D_MODEL), ln_b.reshape(1, D_MODEL))


def _ffn_kernel(x_ref, sc_ref, sh_ref, gate_ref, wg_ref, wu_ref, wd_ref, lng_ref, lnb_ref, o_ref):
    x = x_ref[...]
    hb = (x * (1.0 + sc_ref[0]) + sh_ref[0]).astype(BF16)
    hg = jnp.dot(hb, wg_ref[...], preferred_element_type=F32)
    hu = jnp.dot(hb, wu_ref[...], preferred_element_type=F32)
    h1 = (hg * jax.nn.sigmoid(hg) * hu).astype(BF16)
    y = jnp.dot(h1, wd_ref[...], preferred_element_type=F32)
    o_ref[...] = _layer_norm(ALPHA * x + gate_ref[0] * y, lng_ref[...], lnb_ref[...])


def _ffn_ln(x2d, vecs, wg, wu, wd, ln_g, ln_b, rows_per_batch, tm):
    r = x2d.shape[0]

    def resident(shape):
        return pl.BlockSpec(shape, lambda i: (0, 0), pipeline_mode=pl.Buffered(1))

    return pl.pallas_call(
        _ffn_kernel,
        grid=(r // tm,),
        in_specs=[pl.BlockSpec((tm, D_MODEL), lambda i: (i, 0)),
                  _vec_spec(4, tm, rows_per_batch), _vec_spec(3, tm, rows_per_batch),
                  _vec_spec(5, tm, rows_per_batch),
                  resident(wg.shape), resident(wu.shape), resident(wd.shape),
                  _full_spec((1, D_MODEL)), _full_spec((1, D_MODEL))],
        out_specs=pl.BlockSpec((tm, D_MODEL), lambda i: (i, 0)),
        out_shape=jax.ShapeDtypeStruct((r, D_MODEL), F32),
        compiler_params=_cparams("parallel"),
        name="ffn_ln",
    )(x2d, vecs, vecs, vecs, wg, wu, wd, ln_g.reshape(1, D_MODEL), ln_b.reshape(1, D_MODEL))


def _inproj1_kernel(x_ref, sc_ref, sh_ref, w_ref, *o_refs, with_q):
    h = x_ref[...] * (1.0 + sc_ref[0]) + sh_ref[0]
    o = jnp.dot(h.astype(BF16), w_ref[...], preferred_element_type=F32)
    if with_q:
        qt_ref, k_ref, vt_ref = o_refs
        qt_ref[...] = (o[:, :D_MODEL] * (ATT_SCALE * LOG2E)).T.astype(BF16)
    else:
        k_ref, vt_ref = o_refs
    nk = o.shape[1] - 2 * D_MODEL
    k_ref[...] = o[:, nk:nk + D_MODEL].astype(BF16)
    vt_ref[...] = o[:, nk + D_MODEL:].T.astype(BF16)


def _inproj1(x2d, vecs, w, rows_per_batch, tm, with_q):
    r = x2d.shape[0]
    nat = pl.BlockSpec((tm, D_MODEL), lambda i: (i, 0))
    tr = pl.BlockSpec((D_MODEL, tm), lambda i: (0, i))
    nat_shape = jax.ShapeDtypeStruct((r, D_MODEL), BF16)
    tr_shape = jax.ShapeDtypeStruct((D_MODEL, r), BF16)
    return pl.pallas_call(
        functools.partial(_inproj1_kernel, with_q=with_q),
        grid=(r // tm,),
        in_specs=[pl.BlockSpec((tm, D_MODEL), lambda i: (i, 0)),
                  _vec_spec(1, tm, rows_per_batch), _vec_spec(0, tm, rows_per_batch),
                  _full_spec(w.shape)],
        out_specs=([tr] if with_q else []) + [nat, tr],
        out_shape=([tr_shape] if with_q else []) + [nat_shape, tr_shape],
        compiler_params=_cparams("parallel"),
        name="inproj1",
    )(x2d, vecs, vecs, w)


def _na_bias(rpb, n_rows, n_ctx):
    nh = rpb.shape[0]
    c = np.arange(GRID_W)
    cs = np.clip(c - WIN_C // 2, 0, GRID_W - WIN_C)
    in_c = (c[None, :] >= cs[:, None]) & (c[None, :] < cs[:, None] + WIN_C)
    dc = c[None, :] - c[:, None] + WIN_C - 1
    pick = ((dc[None] == np.arange(2 * WIN_C - 1)[:, None, None]) & in_c[None]).astype(np.float32)
    cols = jnp.einsum("hrd,dkc->hrkc", rpb, jnp.asarray(pick.transpose(0, 2, 1)), precision=lax.Precision.HIGHEST)
    cols = jnp.where(jnp.asarray(in_c.T)[None, None], cols * LOG2E, MASK_VALUE)
    n_dr = 2 * WIN_R - 1
    cols = jnp.concatenate([cols, jnp.full((nh, 1, GRID_W, GRID_W), MASK_VALUE, F32)], axis=1)
    pick_dr = np.full((3, NA_ROWS, NA_KROWS), n_dr, np.int32)
    for case, r0 in enumerate((0, NA_ROWS, n_rows - NA_ROWS)):
        start = int(np.clip(r0 - WIN_R // 2, 0, n_rows - NA_KROWS))
        for ri in range(NA_ROWS):
            r = r0 + ri
            rs = int(np.clip(r - WIN_R // 2, 0, n_rows - WIN_R))
            for ki in range(NA_KROWS):
                kr = start + ki
                if rs <= kr < rs + WIN_R:
                    pick_dr[case, ri, ki] = kr - r + WIN_R - 1
    npair = nh // 2
    nkeys = NA_KROWS * GRID_W
    out = pl.pallas_call(
        functools.partial(_na_bias_kernel, nkeys=nkeys),
        grid_spec=pltpu.PrefetchScalarGridSpec(
            num_scalar_prefetch=1,
            grid=(3, npair),
            in_specs=[pl.BlockSpec((2, n_dr + 1, GRID_W, GRID_W), lambda case, p, dr: (p, 0, 0, 0))],
            out_specs=pl.BlockSpec((None, None, nkeys + n_ctx, 2 * NA_ROWS * GRID_W),
                                   lambda case, p, dr: (case, p, 0, 0)),
        ),
        out_shape=jax.ShapeDtypeStruct((3, npair, nkeys + n_ctx, 2 * NA_ROWS * GRID_W), F32),
        compiler_params=_cparams("parallel", "parallel"),
        name="na_bias",
    )(jnp.asarray(pick_dr.reshape(-1)), cols)
    return out.reshape(3, npair // NA_PAIRS, NA_PAIRS, nkeys + n_ctx, 2 * NA_ROWS * GRID_W)


def _na_bias_kernel(dr_ref, cols_ref, o_ref, *, nkeys):
    case = pl.program_id(0)
    for ki in range(NA_KROWS):
        pieces = []
        for e in range(2):
            for ri in range(NA_ROWS):
                d = dr_ref[(case * NA_ROWS + ri) * NA_KROWS + ki]
                pieces.append(cols_ref[e, d])
        o_ref[GRID_W * ki:GRID_W * (ki + 1), :] = jnp.concatenate(pieces, axis=1)
    o_ref[nkeys:, :] = jnp.zeros((o_ref.shape[0] - nkeys, o_ref.shape[1]), F32)


def _na_kernel(qt_ref, k0_ref, k1_ref, k2_ref, kc_ref, v0_ref, v1_ref, v2_ref, vc_ref, bias_ref, o_ref):
    qt = qt_ref[...]
    kk = jnp.concatenate([k0_ref[...], k1_ref[...], k2_ref[...], kc_ref[...]], axis=0)
    vt = jnp.concatenate([v0_ref[...], v1_ref[...], v2_ref[...], vc_ref[...]], axis=1)
    nq = qt.shape[1]
    zeros = jnp.zeros((HEAD_DIM, nq), BF16)
    ones = jnp.ones((VT_ROWS - HEAD_DIM, kk.shape[0]), BF16)

    def scores(pp):
        q2 = qt[LANES * pp:LANES * (pp + 1)]
        qcat = jnp.concatenate([jnp.concatenate([q2[0:HEAD_DIM], zeros], axis=0),
                                jnp.concatenate([zeros, q2[HEAD_DIM:]], axis=0)], axis=1)
        return jnp.dot(kk[:, LANES * pp:LANES * (pp + 1)], qcat, preferred_element_type=F32) + bias_ref[pp]

    s_all = [scores(pp) for pp in range(NA_PAIRS)]
    for pp in range(NA_PAIRS):
        s = s_all[pp]
        m = jnp.max(s, axis=0, keepdims=True)
        p = jnp.exp2((s - m).astype(BF16))
        vext = jnp.concatenate([vt[LANES * pp:LANES * (pp + 1)], ones], axis=0)
        pv = jnp.dot(vext, p, preferred_element_type=F32)
        o0 = pv[0:HEAD_DIM, 0:nq] * (1.0 / pv[LANES:LANES + 1, 0:nq])
        o1 = pv[HEAD_DIM:LANES, nq:] * (1.0 / pv[LANES:LANES + 1, nq:])
        o_ref[:, LANES * pp:LANES * (pp + 1)] = jnp.concatenate([o0, o1], axis=0).T.astype(BF16)


def _neighbourhood_attention(qt, k, vt, kc, vct, bias, bsz, t):
    nq = NA_ROWS * GRID_W
    nrb = t // nq
    wl = NA_PAIRS * LANES
    ngrp = D_MODEL // wl
    nctx = kc.shape[0] // bsz
    nwin = NA_KROWS // NA_ROWS

    def first_kblock(b, rb):
        return b * nrb + jnp.clip(rb - 1, 0, nrb - nwin)

    def kspec(d):
        return pl.BlockSpec((nq, wl), lambda hp, b, rb: (first_kblock(b, rb) + d, hp))

    def vspec(d):
        return pl.BlockSpec((wl, nq), lambda hp, b, rb: (hp, first_kblock(b, rb) + d))

    def case(rb):
        return jnp.where(rb == 0, 0, jnp.where(rb == nrb - 1, 2, 1))

    return pl.pallas_call(
        _na_kernel,
        grid=(ngrp, bsz, nrb),
        in_specs=[pl.BlockSpec((wl, nq), lambda hp, b, rb: (hp, b * nrb + rb)),
                  kspec(0), kspec(1), kspec(2), pl.BlockSpec((nctx, wl), lambda hp, b, rb: (b, hp)),
                  vspec(0), vspec(1), vspec(2), pl.BlockSpec((wl, nctx), lambda hp, b, rb: (hp, b)),
                  pl.BlockSpec((None, None, NA_PAIRS, NA_KROWS * GRID_W + nctx, 2 * nq),
                               lambda hp, b, rb: (case(rb), hp, 0, 0, 0))],
        out_specs=pl.BlockSpec((nq, wl), lambda hp, b, rb: (b * nrb + rb, hp)),
        out_shape=jax.ShapeDtypeStruct((bsz * t, D_MODEL), BF16),
        compiler_params=_cparams("parallel", "parallel", "parallel"),
        name="neighbourhood_attention",
    )(qt, k, k, k, kc, vt, vt, vt, vct, bias)


def _store_row_tiles(dst_ref, val):
    rows = val.shape[0]
    for lt in range(ROW_TILES):
        dst_ref[pl.ds(lt, rows, stride=ROW_TILES), :] = val[:, lt * LANES:(lt + 1) * LANES]


def _load_row_tiles(src_ref, lt, rows):
    return src_ref[pl.ds(lt, rows, stride=ROW_TILES), :]


def _router_kernel(x_ref, sc_ref, sh_ref, wr_ref, h_ref, r_ref):
    h = x_ref[...] * (1.0 + sc_ref[0]) + sh_ref[0]
    _store_row_tiles(h_ref, h)
    hi = h.astype(BF16)
    lo = (h - hi.astype(F32)).astype(BF16)
    w = wr_ref[...]
    whi = w.astype(BF16)
    wlo = (w - whi.astype(F32)).astype(BF16)
    lg = (jnp.dot(hi, whi, preferred_element_type=F32)
          + (jnp.dot(hi, wlo, preferred_element_type=F32) + jnp.dot(lo, whi, preferred_element_type=F32)))
    lane = lax.broadcasted_iota(jnp.int32, lg.shape, 1).astype(F32)
    lg = jnp.where(lane < N_EXPERTS, lg, -jnp.inf)
    v1 = jnp.max(lg, axis=1, keepdims=True)
    i1 = jnp.min(jnp.where(lg == v1, lane, float(LANES)), axis=1, keepdims=True)
    lg2 = jnp.where(lane == i1, -jnp.inf, lg)
    v2 = jnp.max(lg2, axis=1, keepdims=True)
    i2 = jnp.min(jnp.where(lg2 == v2, lane, float(LANES)), axis=1, keepdims=True)
    e = jnp.exp(v2 - v1)
    w1 = 1.0 / (1.0 + e)
    w2 = e / (1.0 + e)
    r_ref[...] = jnp.where(lane == 0, i1, jnp.where(lane == 1, i2, jnp.where(lane == 2, w1,
                                                                              jnp.where(lane == 3, w2, 0.0))))


def _router(x2d, vecs, w_router, rows_per_batch, tm):
    r = x2d.shape[0]
    wr = jnp.pad(w_router, ((0, 0), (0, LANES - N_EXPERTS)))
    return pl.pallas_call(
        _router_kernel,
        grid=(r // tm,),
        in_specs=[pl.BlockSpec((tm, D_MODEL), lambda i: (i, 0)),
                  _vec_spec(4, tm, rows_per_batch), _vec_spec(3, tm, rows_per_batch),
                  _full_spec((D_MODEL, LANES))],
        out_specs=[pl.BlockSpec((tm * ROW_TILES, LANES), lambda i: (i, 0)),
                   pl.BlockSpec((tm, LANES), lambda i: (i, 0))],
        out_shape=[jax.ShapeDtypeStruct((r * ROW_TILES, LANES), F32), jax.ShapeDtypeStruct((r, LANES), F32)],
        compiler_params=_cparams("parallel"),
        name="router",
    )(x2d, vecs, vecs, wr)


def _row_copy(src_hbm, row, r, dst_ref, sem):
    def tile(i):
        start = i * ROW_TILES
        return pl.ds(start if isinstance(i, int) else pl.multiple_of(start, ROW_TILES), ROW_TILES)

    return pltpu.make_async_copy(src_hbm.at[tile(row), :], dst_ref.at[tile(r), :], sem)


def _start_row_gather(src_hbm, idx_ref, dst_ref, sem, n, priorities):
    def issue(i, carry):
        for u in range(2):
            r = 2 * i + u
            _row_copy(src_hbm, idx_ref[0, r], r, dst_ref, sem).start(priority=priorities[u])
        return carry

    lax.fori_loop(0, n // 2, issue, 0, unroll=4)


def _wait_row_gather(src_hbm, dst_ref, sem, n):
    def wait(r, carry):
        _row_copy(src_hbm, 0, r, dst_ref, sem).wait()
        return carry

    lax.fori_loop(0, n, wait, 0, unroll=8)


def _moe_ffn_kernel(te_ref, nu_ref, idx0_ref, idxn_ref, h_hbm, wg_ref, wu_ref, wd_ref, o_ref,
                    xbuf_ref, acc_ref, sem):
    t = pl.program_id(0)
    j = pl.program_id(1)
    nj = pl.num_programs(1)
    used = t < nu_ref[0]
    tm = acc_ref.shape[0]
    slot = t % 2

    @pl.when(jnp.logical_and(used, j == 0))
    def _():
        @pl.when(t == 0)
        def _():
            _start_row_gather(h_hbm, idx0_ref, xbuf_ref.at[0], sem.at[0], tm, (0, 0))

        @pl.when(t + 1 < nu_ref[0])
        def _():
            _start_row_gather(h_hbm, idxn_ref, xbuf_ref.at[1 - slot], sem.at[1 - slot], tm, (0, 0))

        _wait_row_gather(h_hbm, xbuf_ref.at[slot], sem.at[slot], tm)

    @pl.when(jnp.logical_and(t == 0, j == 0))
    def _():
        acc_ref[...] = jnp.zeros_like(acc_ref)

    @pl.when(used)
    def _():
        x = xbuf_ref.at[slot]
        hb = jnp.concatenate([_load_row_tiles(x, lt, tm).astype(BF16) for lt in range(ROW_TILES)], axis=1)
        hg = jnp.dot(hb, wg_ref[...], preferred_element_type=F32)
        hu = jnp.dot(hb, wu_ref[...], preferred_element_type=F32)
        h1 = (hg * jax.nn.sigmoid(hg) * hu).astype(BF16)
        acc = jnp.where(j == 0, 0.0, acc_ref[...]) + jnp.dot(h1, wd_ref[...], preferred_element_type=F32)
        acc_ref[...] = acc
        _store_row_tiles(o_ref, acc)

    @pl.when(jnp.logical_and(jnp.logical_not(used), j == nj - 1))
    def _():
        o_ref[...] = jnp.zeros_like(o_ref)


def _moe_ffn(h, src, tile_expert, n_used, wg, wu, wd):
    p = src.shape[0]
    tm, tf = MOE_TM, MOE_TF
    nt = p // tm
    f = wg.shape[2]

    def jj(t, j, nu):
        return jnp.where(t < nu[0], j, 0)

    grid_spec = pltpu.PrefetchScalarGridSpec(
        num_scalar_prefetch=2,
        grid=(nt, f // tf),
        in_specs=[pl.BlockSpec((None, 1, tm), lambda t, j, te, nu: (0, 0, 0), memory_space=pltpu.SMEM),
                  pl.BlockSpec((None, 1, tm), lambda t, j, te, nu: (jnp.minimum(t + 1, nt - 1), 0, 0),
                               memory_space=pltpu.SMEM),
                  pl.BlockSpec(memory_space=pl.ANY),
                  pl.BlockSpec((None, D_MODEL, tf), lambda t, j, te, nu: (te[t], 0, jj(t, j, nu))),
                  pl.BlockSpec((None, D_MODEL, tf), lambda t, j, te, nu: (te[t], 0, jj(t, j, nu))),
                  pl.BlockSpec((None, tf, D_MODEL), lambda t, j, te, nu: (te[t], jj(t, j, nu), 0))],
        out_specs=pl.BlockSpec((tm * ROW_TILES, LANES), lambda t, j, te, nu: (t, 0)),
        scratch_shapes=[pltpu.VMEM((2, tm * ROW_TILES, LANES), F32), pltpu.VMEM((tm, D_MODEL), F32),
                        pltpu.SemaphoreType.DMA((2,))],
    )
    idx = src.reshape(nt, 1, tm)
    return pl.pallas_call(
        _moe_ffn_kernel,
        grid_spec=grid_spec,
        out_shape=jax.ShapeDtypeStruct((p * ROW_TILES, LANES), F32),
        compiler_params=_cparams("arbitrary", "arbitrary"),
        name="moe_ffn",
    )(tile_expert, n_used, idx, idx, h, wg, wu, wd)


def _combine_ln_kernel(p1a_ref, p2a_ref, p1b_ref, p2b_ref, ys_hbm, route_ref, x_ref, gate_ref, lng_ref, lnb_ref,
                       o_ref, y_ref, sem):
    i = pl.program_id(0)
    n = x_ref.shape[0]
    slot = i % 2

    def start(p1_ref, p2_ref, s):
        _start_row_gather(ys_hbm, p1_ref, y_ref.at[s, 0], sem.at[s, 0], n, (0, 1))
        _start_row_gather(ys_hbm, p2_ref, y_ref.at[s, 1], sem.at[s, 1], n, (0, 1))

    @pl.when(i == 0)
    def _():
        start(p1a_ref, p2a_ref, 0)

    @pl.when(i + 1 < pl.num_programs(0))
    def _():
        start(p1b_ref, p2b_ref, 1 - slot)

    for e in range(2):
        _wait_row_gather(ys_hbm, y_ref.at[slot, e], sem.at[slot, e], n)
    w1 = route_ref[:, 2:3]
    w2 = route_ref[:, 3:4]
    y = jnp.concatenate([w1 * _load_row_tiles(y_ref.at[slot, 0], lt, n) + w2 * _load_row_tiles(y_ref.at[slot, 1], lt, n)
                         for lt in range(ROW_TILES)], axis=1)
    z = ALPHA * x_ref[...] + gate_ref[0] * y
    o_ref[...] = _layer_norm(z, lng_ref[...], lnb_ref[...])


def _combine_ln(ys, pos1, pos2, route, x2d, vecs, ln_g, ln_b, rows_per_batch):
    r = x2d.shape[0]
    g = GATHER_ROWS
    ns = r // g
    first = pl.BlockSpec((None, 1, g), lambda i: (0, 0, 0), memory_space=pltpu.SMEM)
    ahead = pl.BlockSpec((None, 1, g), lambda i: (jnp.minimum(i + 1, ns - 1), 0, 0), memory_space=pltpu.SMEM)
    p1 = pos1.reshape(ns, 1, g)
    p2 = pos2.reshape(ns, 1, g)
    return pl.pallas_call(
        _combine_ln_kernel,
        grid=(ns,),
        in_specs=[first, first, ahead, ahead, pl.BlockSpec(memory_space=pl.ANY),
                  pl.BlockSpec((g, LANES), lambda i: (i, 0)),
                  pl.BlockSpec((g, D_MODEL), lambda i: (i, 0)),
                  _vec_spec(5, g, rows_per_batch), _full_spec((1, D_MODEL)), _full_spec((1, D_MODEL))],
        out_specs=pl.BlockSpec((g, D_MODEL), lambda i: (i, 0)),
        out_shape=jax.ShapeDtypeStruct((r, D_MODEL), F32),
        scratch_shapes=[pltpu.VMEM((2, 2, g * ROW_TILES, LANES), F32), pltpu.SemaphoreType.DMA((2, 2))],
        compiler_params=_cparams("arbitrary"),
        name="combine_ln",
    )(p1, p2, p1, p2, ys, route, x2d, vecs, ln_g.reshape(1, D_MODEL), ln_b.reshape(1, D_MODEL))


def _routing_plan(route, tm):
    n = route.shape[0]
    e = jnp.concatenate([route[:, 0], route[:, 1]]).astype(jnp.int32)
    tok = jnp.concatenate([jnp.arange(n, dtype=jnp.int32)] * 2)
    onehot = (e[:, None] == jnp.arange(N_EXPERTS, dtype=jnp.int32)[None, :]).astype(jnp.int32)
    csum = jnp.cumsum(onehot, axis=0)
    rank = jnp.sum(csum * onehot, axis=1) - 1
    counts = csum[-1]
    padded = ((counts + tm - 1) // tm) * tm
    ends = jnp.cumsum(padded)
    starts = ends - padded
    pos = jnp.sum(starts[None, :] * onehot, axis=1) + rank
    p = 2 * n + N_EXPERTS * tm
    src = jnp.zeros((p,), jnp.int32).at[pos].set(tok)
    tile_start = jnp.arange(p // tm, dtype=jnp.int32) * tm
    tile_expert = jnp.minimum(jnp.sum((tile_start[:, None] >= ends[None, :]).astype(jnp.int32), axis=1),
                              N_EXPERTS - 1).astype(jnp.int32)
    n_used = (ends[-1] // tm).astype(jnp.int32).reshape(1)
    return src, tile_expert, n_used, pos[:n], pos[n:]


def kernel(x, c, ctx, c_ctx, w_mod, b_mod, ln_g, ln_b, ab_w_in, ab_conv_w, ab_conv_g, ab_conv_b, ab_q_g, ab_k_g,
           ab_w_out, ffn_w_gate, ffn_w_up, ffn_w_down, na_w_qkv, na_rpb, na_w_out, moe_w_router, moe_w_gate,
           moe_w_up, moe_w_down):
    bsz, t, d = x.shape
    n_ctx = ctx.shape[1]
    n = bsz * t
    nc = bsz * n_ctx
    x2 = x.reshape(n, d)
    c2 = ctx.reshape(nc, d)

    cc = jnp.concatenate([c, c_ctx[None, :], jnp.zeros((8 - bsz - 1, d), F32)], axis=0)
    mod = _modulation(cc, w_mod, b_mod)
    vec0 = mod[0].reshape(8 * 6, 1, d)
    vec1 = mod[1].reshape(8 * 6, 1, d)

    w_in = ab_w_in[0].astype(BF16)
    w_out = ab_w_out[0].astype(BF16)
    pa, pq = _inproj0(x2, vec0, w_in, t, 512)
    pac, pqc = _inproj0(c2, vec0, w_in, None, 512)
    a = _conformer_conv(pa, ab_conv_w[0], ab_conv_g[0], ab_conv_b[0], t, 512)
    ac = _conformer_conv(pac, ab_conv_w[0], ab_conv_g[0], ab_conv_b[0], n_ctx, n_ctx)
    cos_t, sin_t = _rope_tables(t)
    qt, k, vt = _prep_qkv(pq, cos_t, sin_t, ab_q_g[0], ab_k_g[0], bsz, t)
    ones = jnp.ones((HEAD_DIM, n_ctx), F32)
    qtc, kc, vtc = _prep_qkv(pqc, ones, jnp.zeros_like(ones), ab_q_g[0], ab_k_g[0], bsz, n_ctx)
    o = _gqa_attention(qt, jnp.concatenate([kc, k], axis=1), jnp.concatenate([vtc, vt], axis=2), t)
    oc = _gqa_attention(qtc, kc, vtc, n_ctx)
    x2 = _outproj_ln([a, o], w_out, x2, vec0, 2, ln_g[0, 0], ln_b[0, 0], t, 512)
    c2 = _outproj_ln([ac, oc], w_out, c2, vec0, 2, ln_g[0, 0], ln_b[0, 0], None, 512)
    wg = ffn_w_gate[0].astype(BF16)
    wu = ffn_w_up[0].astype(BF16)
    wd = ffn_w_down[0].astype(BF16)
    x2 = _ffn_ln(x2, vec0, wg, wu, wd, ln_g[0, 1], ln_b[0, 1], t, 512)
    c2 = _ffn_ln(c2, vec0, wg, wu, wd, ln_g[0, 1], ln_b[0, 1], None, 512)

    w_qkv = na_w_qkv[0].astype(BF16)
    qt1, k1, vt1 = _inproj1(x2, vec1, w_qkv, t, 512, True)
    kc1, vct1 = _inproj1(c2, vec1, w_qkv[:, d:], None, 512, False)
    o = _neighbourhood_attention(qt1, k1, vt1, kc1, vct1, _na_bias(na_rpb[0], t // GRID_W, n_ctx), bsz, t)
    x2 = _outproj_ln([o], na_w_out[0].astype(BF16), x2, vec1, 2, ln_g[1, 0], ln_b[1, 0], t, 512)

    h, route = _router(x2, vec1, moe_w_router[0], t, 512)
    src, tile_expert, n_used, pos1, pos2 = _routing_plan(route, MOE_TM)
    ys = _moe_ffn(h, src, tile_expert, n_used, moe_w_gate[0].astype(BF16), moe_w_up[0].astype(BF16),
                  moe_w_down[0].astype(BF16))
    x2 = _combine_ln(ys, pos1, pos2, route, x2, vec1, ln_g[1, 1], ln_b[1, 1], t)
    return x2.reshape(bsz, t, d)
```

```python
import functools

import numpy as np
import jax
import jax.numpy as jnp
from jax import lax
from jax.experimental import pallas as pl
from jax.experimental.pallas import tpu as pltpu

F32 = jnp.float32
BF16 = jnp.bfloat16

D_MODEL = 1024
GRID_W = 64
HEAD_DIM = 64
CONV_CH = 512
CONV_WIDTH = 31
CONV_HALO = 16
Q_COLS = 512
KV_COLS = 128
A_COLS = 2 * CONV_CH
ROPE_THETA = 10000.0
WIN_R = 8
WIN_C = 16
N_EXPERTS = 8
DEPTH = 2
ALPHA = (2 * DEPTH) ** 0.25
LN_EPS = 1e-5
RMS_EPS = 1e-6
ATT_SCALE = HEAD_DIM ** -0.5
LOG2E = 1.4426950408889634
MASK_VALUE = -1e30

LANES = 128
ROW_TILES = D_MODEL // LANES
VMEM_LIMIT = 56 * 1024 * 1024

ATT_TQ = 256
ATT_CW = 256
VT_ROWS = HEAD_DIM + 16
NA_ROWS = 4
NA_KROWS = 12
NA_PAIRS = 4
MOE_TM = 512
MOE_TF = 1792
GATHER_ROWS = 256


def _cparams(*sem):
    return pltpu.CompilerParams(dimension_semantics=sem, vmem_limit_bytes=VMEM_LIMIT)


def _layer_norm(z, g, b):
    mu = jnp.mean(z, axis=-1, keepdims=True)
    zc = z - mu
    var = jnp.mean(zc * zc, axis=-1, keepdims=True)
    return zc * lax.rsqrt(var + LN_EPS) * g + b


def _vec_spec(k, tm, rows_per_batch):
    if rows_per_batch is None:
        return pl.BlockSpec((1, 1, D_MODEL), lambda i, *_: (4 * 6 + k, 0, 0))
    return pl.BlockSpec((1, 1, D_MODEL), lambda i, *_: ((i * tm // rows_per_batch) * 6 + k, 0, 0))


def _full_spec(shape):
    nd = len(shape)
    return pl.BlockSpec(shape, lambda *_: (0,) * nd)


def _mod_kernel(c_ref, w_ref, b_ref, o_ref):
    c = c_ref[...]
    s = c * jax.nn.sigmoid(c)
    o_ref[...] = jnp.dot(s.astype(BF16), w_ref[...].astype(BF16), preferred_element_type=F32) + b_ref[...]


def _modulation(cc, w_mod, b_mod):
    n = 6 * D_MODEL
    tn = D_MODEL
    return pl.pallas_call(
        _mod_kernel,
        grid=(DEPTH, n // tn),
        in_specs=[pl.BlockSpec((8, D_MODEL), lambda l, j: (0, 0)),
                  pl.BlockSpec((None, D_MODEL, tn), lambda l, j: (l, 0, j)),
                  pl.BlockSpec((None, 1, tn), lambda l, j: (l, 0, j))],
        out_specs=pl.BlockSpec((None, 8, tn), lambda l, j: (l, 0, j)),
        out_shape=jax.ShapeDtypeStruct((DEPTH, 8, n), F32),
        compiler_params=_cparams("parallel", "parallel"),
        name="modulation",
    )(cc, w_mod, b_mod.reshape(DEPTH, 1, n))


def _inproj0_kernel(x_ref, sc_ref, sh_ref, w_ref, oa_ref, oq_ref):
    h = x_ref[...] * (1.0 + sc_ref[0]) + sh_ref[0]
    o = jnp.dot(h.astype(BF16), w_ref[...], preferred_element_type=F32)
    oa_ref[...] = o[:, :A_COLS]
    oq_ref[...] = o[:, A_COLS:]


def _inproj0(x2d, vecs, w, rows_per_batch, tm):
    r = x2d.shape[0]
    nq = w.shape[1] - A_COLS
    return pl.pallas_call(
        _inproj0_kernel,
        grid=(r // tm,),
        in_specs=[pl.BlockSpec((tm, D_MODEL), lambda i: (i, 0)),
                  _vec_spec(1, tm, rows_per_batch), _vec_spec(0, tm, rows_per_batch),
                  _full_spec(w.shape)],
        out_specs=[pl.BlockSpec((tm, A_COLS), lambda i: (i, 0)),
                   pl.BlockSpec((tm, nq), lambda i: (i, 0))],
        out_shape=[jax.ShapeDtypeStruct((r, A_COLS), F32), jax.ShapeDtypeStruct((r, nq), F32)],
        compiler_params=_cparams("parallel"),
        name="inproj0",
    )(x2d, vecs, vecs, w)


def _conv_kernel(pm_ref, pp_ref, pn_ref, w_ref, g_ref, b_ref, o_ref, u_ref, us_ref, cv_ref, *, tt, tx):
    i = pl.program_id(0)

    def sigmoid(v):
        return 0.5 * jnp.tanh(0.5 * v) + 0.5

    def glu(p):
        return p[:, :CONV_CH] * sigmoid(p[:, CONV_CH:])

    first = (i * tt) % tx == 0
    last = ((i + 1) * tt) % tx == 0
    u_ref[0:CONV_HALO, :] = jnp.where(first, 0.0, glu(pp_ref[...]))
    u_ref[CONV_HALO:CONV_HALO + tt, :] = glu(pm_ref[...])
    u_ref[CONV_HALO + tt:2 * CONV_HALO + tt, :] = jnp.where(last, 0.0, glu(pn_ref[...]))
    u_ref[2 * CONV_HALO + tt:, :] = jnp.zeros((8, CONV_CH), F32)
    ch = 32
    nrow = tt + 2 * CONV_HALO
    base = CONV_HALO - CONV_WIDTH // 2

    def shift_body(c, carry):
        r0 = pl.multiple_of(c * ch, ch)
        w = u_ref[pl.ds(r0, ch + 8), :]
        for s in range(1, 8):
            us_ref[s - 1, pl.ds(r0, ch), :] = pltpu.roll(w, ch + 8 - s, axis=0)[0:ch]
        return carry

    lax.fori_loop(0, nrow // ch, shift_body, 0)

    def body(c, carry):
        r0 = pl.multiple_of(c * ch, ch)
        acc = jnp.zeros((ch, CONV_CH), F32)
        for k in range(CONV_WIDTH):
            a, s = divmod(k + base, 8)
            src = u_ref if s == 0 else us_ref.at[s - 1]
            acc = acc + src[pl.ds(r0 + 8 * a, ch), :] * w_ref[pl.ds(k, 1), :]
        cv_ref[pl.ds(r0, ch), :] = acc
        return carry

    lax.fori_loop(0, tt // ch, body, 0)
    y = _layer_norm(cv_ref[...], g_ref[...], b_ref[...])
    o_ref[...] = (y * sigmoid(y)).astype(BF16)


def _conformer_conv(pa, conv_w, conv_g, conv_b, tx, tt):
    r = pa.shape[0]
    hb = tt // CONV_HALO
    nhb = r // CONV_HALO
    return pl.pallas_call(
        functools.partial(_conv_kernel, tt=tt, tx=tx),
        grid=(r // tt,),
        in_specs=[pl.BlockSpec((tt, A_COLS), lambda i: (i, 0)),
                  pl.BlockSpec((CONV_HALO, A_COLS), lambda i: (jnp.maximum(i * hb - 1, 0), 0)),
                  pl.BlockSpec((CONV_HALO, A_COLS), lambda i: (jnp.minimum((i + 1) * hb, nhb - 1), 0)),
                  _full_spec((CONV_WIDTH, CONV_CH)), _full_spec((1, CONV_CH)), _full_spec((1, CONV_CH))],
        out_specs=pl.BlockSpec((tt, CONV_CH), lambda i: (i, 0)),
        out_shape=jax.ShapeDtypeStruct((r, CONV_CH), BF16),
        scratch_shapes=[pltpu.VMEM((tt + 2 * CONV_HALO + 8, CONV_CH), F32),
                        pltpu.VMEM((7, tt + 2 * CONV_HALO, CONV_CH), F32),
                        pltpu.VMEM((tt, CONV_CH), F32)],
        compiler_params=_cparams("parallel"),
        name="conformer_conv",
    )(pa, pa, pa, conv_w, conv_g.reshape(1, CONV_CH), conv_b.reshape(1, CONV_CH))


def _prep_kernel(p_ref, cos_ref, sin_ref, qg_ref, kg_ref, qt_ref, k_ref, vt_ref, *, tq):
    x = p_ref[...]
    cos = cos_ref[...]
    sin = sin_ref[...]

    def norm_rope(xh, g):
        ms = jnp.mean(xh * xh, axis=0, keepdims=True)
        y = xh * lax.rsqrt(ms + RMS_EPS) * g
        sw = jnp.concatenate([y[16:32], y[0:16], y[48:64], y[32:48]], axis=0)
        return y * cos + sw * sin

    zeros = jnp.zeros((HEAD_DIM, tq), BF16)
    for p in range(Q_COLS // LANES):
        xp = x[:, LANES * p:LANES * (p + 1)].T
        for half in range(2):
            h = 2 * p + half
            j, g = h // 4, h % 4
            r = (norm_rope(xp[HEAD_DIM * half:HEAD_DIM * (half + 1)], qg_ref[...]) * (ATT_SCALE * LOG2E)).astype(BF16)
            qt_ref[j, HEAD_DIM * j:HEAD_DIM * (j + 1), g * tq:(g + 1) * tq] = r
            qt_ref[j, HEAD_DIM * (1 - j):HEAD_DIM * (2 - j), g * tq:(g + 1) * tq] = zeros
    xk = x[:, Q_COLS:Q_COLS + KV_COLS].T
    k0 = norm_rope(xk[0:HEAD_DIM], kg_ref[...])
    k1 = norm_rope(xk[HEAD_DIM:2 * HEAD_DIM], kg_ref[...])
    k_ref[...] = jnp.concatenate([k0, k1], axis=0).T.astype(BF16)
    xv = x[:, Q_COLS + KV_COLS:].T.astype(BF16)
    ones = jnp.ones((VT_ROWS - HEAD_DIM, tq), BF16)
    for j in range(2):
        vt_ref[j, 0:HEAD_DIM, :] = xv[HEAD_DIM * j:HEAD_DIM * (j + 1)]
        vt_ref[j, HEAD_DIM:VT_ROWS, :] = ones


def _prep_qkv(pq, cos_t, sin_t, q_g, k_g, bx, tx):
    tq = ATT_TQ
    nq = tx // tq
    return pl.pallas_call(
        functools.partial(_prep_kernel, tq=tq),
        grid=(bx, nq),
        in_specs=[pl.BlockSpec((tq, Q_COLS + 2 * KV_COLS), lambda b, t: (b * nq + t, 0)),
                  pl.BlockSpec((HEAD_DIM, tq), lambda b, t: (0, t)),
                  pl.BlockSpec((HEAD_DIM, tq), lambda b, t: (0, t)),
                  _full_spec((HEAD_DIM, 1)), _full_spec((HEAD_DIM, 1))],
        out_specs=[pl.BlockSpec((None, None, 2, LANES, 4 * tq), lambda b, t: (b, t, 0, 0, 0)),
                   pl.BlockSpec((None, None, tq, LANES), lambda b, t: (b, t, 0, 0)),
                   pl.BlockSpec((None, 2, None, VT_ROWS, tq), lambda b, t: (b, 0, t, 0, 0))],
        out_shape=[jax.ShapeDtypeStruct((bx, nq, 2, LANES, 4 * tq), BF16),
                   jax.ShapeDtypeStruct((bx, nq, tq, LANES), BF16),
                   jax.ShapeDtypeStruct((bx, 2, nq, VT_ROWS, tq), BF16)],
        compiler_params=_cparams("parallel", "parallel"),
        name="prep_qkv",
    )(pq, cos_t, sin_t, q_g.reshape(HEAD_DIM, 1), k_g.reshape(HEAD_DIM, 1))


def _rope_tables(t):
    pos = np.arange(t)
    half = HEAD_DIM // 4
    inv = ROPE_THETA ** (-jnp.arange(half, dtype=F32) * 2.0 / (HEAD_DIM // 2))
    dd = np.arange(HEAD_DIM)
    part_pos = np.where((dd // (HEAD_DIM // 2))[:, None] == 0, (pos // GRID_W)[None, :], (pos % GRID_W)[None, :])
    ang = jnp.asarray(part_pos, F32) * inv[dd % half][:, None]
    sign = jnp.asarray(np.where((dd % (HEAD_DIM // 2)) < half, -1.0, 1.0)[:, None], F32)
    return jnp.cos(ang), jnp.sin(ang) * sign


def _attn_kernel(qt_ref, k_ref, vt_ref, o_ref, acc_ref, *, nk, tq):
    ngrp = qt_ref.shape[0]
    acc_ref[...] = jnp.zeros_like(acc_ref)
    cw = ATT_CW
    strips = [(j, n) for j in range(ngrp) for n in range(4 * tq // cw)]

    def scores(c, j, n):
        return jnp.dot(k_ref[c], qt_ref[j, :, cw * n:cw * (n + 1)], preferred_element_type=F32)

    s = [scores(0, j, n) for j, n in strips]
    m_prev = [jnp.full((1, cw), -jnp.inf, F32)] * len(strips)
    m = [jnp.max(sn, axis=0, keepdims=True) for sn in s]
    for c in range(nk):
        for i, (j, n) in enumerate(strips):
            s_next = scores(c + 1, j, n) if c + 1 < nk else None
            alpha = jnp.exp2(m_prev[i] - m[i])
            p = jnp.exp2((s[i] - m[i]).astype(BF16))
            pv = jnp.dot(vt_ref[j, c], p, preferred_element_type=F32)
            acc_ref[j, :, cw * n:cw * (n + 1)] = acc_ref[j, :, cw * n:cw * (n + 1)] * alpha + pv
            if s_next is not None:
                m_prev[i] = m[i]
                m[i] = jnp.maximum(m[i], jnp.max(s_next, axis=0, keepdims=True))
                s[i] = s_next
    for j in range(ngrp):
        o = acc_ref[j, 0:HEAD_DIM, :] * (1.0 / acc_ref[j, HEAD_DIM:HEAD_DIM + 1, :])
        for pp in range(2):
            blk = jnp.concatenate([o[:, (2 * pp) * tq:(2 * pp + 1) * tq],
                                   o[:, (2 * pp + 1) * tq:(2 * pp + 2) * tq]], axis=0)
            lane0 = LANES * (2 * j + pp)
            o_ref[:, lane0:lane0 + LANES] = blk.T.astype(BF16)


def _gqa_attention(qt, k, vt, tx):
    bx, nq = qt.shape[0], qt.shape[1]
    nk = k.shape[1]
    tq = ATT_TQ
    return pl.pallas_call(
        functools.partial(_attn_kernel, nk=nk, tq=tq),
        grid=(bx, nq),
        in_specs=[pl.BlockSpec((None, None, 2, LANES, 4 * tq), lambda b, t: (b, t, 0, 0, 0)),
                  pl.BlockSpec((None, nk, tq, LANES), lambda b, t: (b, 0, 0, 0)),
                  pl.BlockSpec((None, 2, nk, VT_ROWS, tq), lambda b, t: (b, 0, 0, 0, 0))],
        out_specs=pl.BlockSpec((tq, Q_COLS), lambda b, t: (b * nq + t, 0)),
        out_shape=jax.ShapeDtypeStruct((bx * tx, Q_COLS), BF16),
        scratch_shapes=[pltpu.VMEM((2, VT_ROWS, 4 * tq), F32)],
        compiler_params=_cparams("parallel", "parallel"),
        name="gqa_attention",
    )(qt, k, vt)


def _outproj_ln_kernel(*refs, n_lhs):
    lhs = refs[:n_lhs]
    w_ref, x_ref, gate_ref, lng_ref, lnb_ref, o_ref = refs[n_lhs:]
    y = None
    off = 0
    for r in lhs:
        kk = r.shape[1]
        t = jnp.dot(r[...], w_ref[off:off + kk, :], preferred_element_type=F32)
        off += kk
        y = t if y is None else y + t
    z = ALPHA * x_ref[...] + gate_ref[0] * y
    o_ref[...] = _layer_norm(z, lng_ref[...], lnb_ref[...])


def _outproj_ln(lhs, w, x2d, vecs, gate_k, ln_g, ln_b, rows_per_batch, tm):
    r = x2d.shape[0]
    return pl.pallas_call(
        functools.partial(_outproj_ln_kernel, n_lhs=len(lhs)),
        grid=(r // tm,),
        in_specs=[pl.BlockSpec((tm, a.shape[1]), lambda i: (i, 0)) for a in lhs]
        + [_full_spec(w.shape), pl.BlockSpec((tm, D_MODEL), lambda i: (i, 0)),
           _vec_spec(gate_k, tm, rows_per_batch), _full_spec((1, D_MODEL)), _full_spec((1, D_MODEL))],
        out_specs=pl.BlockSpec((tm, D_MODEL), lambda i: (i, 0)),
        out_shape=jax.ShapeDtypeStruct((r, D_MODEL), F32),
        compiler_params=_cparams("parallel"),
        name="outproj_ln",
    )(*lhs, w, x2d, vecs, ln_g.reshape(1, D_MODEL), ln_b.reshape(1, D_MODEL))


def _ffn_kernel(x_ref, sc_ref, sh_ref, gate_ref, wg_ref, wu_ref, wd_ref, lng_ref, lnb_ref, o_ref):
    x = x_ref[...]
    hb = (x * (1.0 + sc_ref[0]) + sh_ref[0]).astype(BF16)
    hg = jnp.dot(hb, wg_ref[...], preferred_element_type=F32)
    hu = jnp.dot(hb, wu_ref[...], preferred_element_type=F32)
    h1 = (hg * jax.nn.sigmoid(hg) * hu).astype(BF16)
    y = jnp.dot(h1, wd_ref[...], preferred_element_type=F32)
    o_ref[...] = _layer_norm(ALPHA * x + gate_ref[0] * y, lng_ref[...], lnb_ref[...])


def _ffn_ln(x2d, vecs, wg, wu, wd, ln_g, ln_b, rows_per_batch, tm):
    r = x2d.shape[0]

    def resident(shape):
        return pl.BlockSpec(shape, lambda i: (0, 0), pipeline_mode=pl.Buffered(1))

    return pl.pallas_call(
        _ffn_kernel,
        grid=(r // tm,),
        in_specs=[pl.BlockSpec((tm, D_MODEL), lambda i: (i, 0)),
                  _vec_spec(4, tm, rows_per_batch), _vec_spec(3, tm, rows_per_batch),
                  _vec_spec(5, tm, rows_per_batch),
                  resident(wg.shape), resident(wu.shape), resident(wd.shape),
                  _full_spec((1, D_MODEL)), _full_spec((1, D_MODEL))],
        out_specs=pl.BlockSpec((tm, D_MODEL), lambda i: (i, 0)),
        out_shape=jax.ShapeDtypeStruct((r, D_MODEL), F32),
        compiler_params=_cparams("parallel"),
        name="ffn_ln",
    )(x2d, vecs, vecs, vecs, wg, wu, wd, ln_g.reshape(1, D_MODEL), ln_b.reshape(1, D_MODEL))


def _inproj1_kernel(x_ref, sc_ref, sh_ref, w_ref, *o_refs, with_q):
    h = x_ref[...] * (1.0 + sc_ref[0]) + sh_ref[0]
    o = jnp.dot(h.astype(BF16), w_ref[...], preferred_element_type=F32)
    if with_q:
        qt_ref, k_ref, vt_ref = o_refs
        qt_ref[...] = (o[:, :D_MODEL] * (ATT_SCALE * LOG2E)).T.astype(BF16)
    else:
        k_ref, vt_ref = o_refs
    nk = o.shape[1] - 2 * D_MODEL
    k_ref[...] = o[:, nk:nk + D_MODEL].astype(BF16)
    vt_ref[...] = o[:, nk + D_MODEL:].T.astype(BF16)


def _inproj1(x2d, vecs, w, rows_per_batch, tm, with_q):
    r = x2d.shape[0]
    nat = pl.BlockSpec((tm, D_MODEL), lambda i: (i, 0))
    tr = pl.BlockSpec((D_MODEL, tm), lambda i: (0, i))
    nat_shape = jax.ShapeDtypeStruct((r, D_MODEL), BF16)
    tr_shape = jax.ShapeDtypeStruct((D_MODEL, r), BF16)
    return pl.pallas_call(
        functools.partial(_inproj1_kernel, with_q=with_q),
        grid=(r // tm,),
        in_specs=[pl.BlockSpec((tm, D_MODEL), lambda i: (i, 0)),
                  _vec_spec(1, tm, rows_per_batch), _vec_spec(0, tm, rows_per_batch),
                  _full_spec(w.shape)],
        out_specs=([tr] if with_q else []) + [nat, tr],
        out_shape=([tr_shape] if with_q else []) + [nat_shape, tr_shape],
        compiler_params=_cparams("parallel"),
        name="inproj1",
    )(x2d, vecs, vecs, w)


def _na_bias(rpb, n_rows, n_ctx):
    nh = rpb.shape[0]
    c = np.arange(GRID_W)
    cs = np.clip(c - WIN_C // 2, 0, GRID_W - WIN_C)
    in_c = (c[None, :] >= cs[:, None]) & (c[None, :] < cs[:, None] + WIN_C)
    dc = c[None, :] - c[:, None] + WIN_C - 1
    pick = ((dc[None] == np.arange(2 * WIN_C - 1)[:, None, None]) & in_c[None]).astype(np.float32)
    cols = jnp.einsum("hrd,dkc->hrkc", rpb, jnp.asarray(pick.transpose(0, 2, 1)), precision=lax.Precision.HIGHEST)
    cols = jnp.where(jnp.asarray(in_c.T)[None, None], cols * LOG2E, MASK_VALUE)
    n_dr = 2 * WIN_R - 1
    cols = jnp.concatenate([cols, jnp.full((nh, 1, GRID_W, GRID_W), MASK_VALUE, F32)], axis=1)
    pick_dr = np.full((3, NA_ROWS, NA_KROWS), n_dr, np.int32)
    for case, r0 in enumerate((0, NA_ROWS, n_rows - NA_ROWS)):
        start = int(np.clip(r0 - WIN_R // 2, 0, n_rows - NA_KROWS))
        for ri in range(NA_ROWS):
            r = r0 + ri
            rs = int(np.clip(r - WIN_R // 2, 0, n_rows - WIN_R))
            for ki in range(NA_KROWS):
                kr = start + ki
                if rs <= kr < rs + WIN_R:
                    pick_dr[case, ri, ki] = kr - r + WIN_R - 1
    npair = nh // 2
    nkeys = NA_KROWS * GRID_W
    out = pl.pallas_call(
        functools.partial(_na_bias_kernel, nkeys=nkeys),
        grid_spec=pltpu.PrefetchScalarGridSpec(
            num_scalar_prefetch=1,
            grid=(3, npair),
            in_specs=[pl.BlockSpec((2, n_dr + 1, GRID_W, GRID_W), lambda case, p, dr: (p, 0, 0, 0))],
            out_specs=pl.BlockSpec((None, None, nkeys + n_ctx, 2 * NA_ROWS * GRID_W),
                                   lambda case, p, dr: (case, p, 0, 0)),
        ),
        out_shape=jax.ShapeDtypeStruct((3, npair, nkeys + n_ctx, 2 * NA_ROWS * GRID_W), F32),
        compiler_params=_cparams("parallel", "parallel"),
        name="na_bias",
    )(jnp.asarray(pick_dr.reshape(-1)), cols)
    return out.reshape(3, npair // NA_PAIRS, NA_PAIRS, nkeys + n_ctx, 2 * NA_ROWS * GRID_W)


def _na_bias_kernel(dr_ref, cols_ref, o_ref, *, nkeys):
    case = pl.program_id(0)
    for ki in range(NA_KROWS):
        pieces = []
        for e in range(2):
            for ri in range(NA_ROWS):
                d = dr_ref[(case * NA_ROWS + ri) * NA_KROWS + ki]
                pieces.append(cols_ref[e, d])
        o_ref[GRID_W * ki:GRID_W * (ki + 1), :] = jnp.concatenate(pieces, axis=1)
    o_ref[nkeys:, :] = jnp.zeros((o_ref.shape[0] - nkeys, o_ref.shape[1]), F32)


def _na_kernel(qt_ref, k0_ref, k1_ref, k2_ref, kc_ref, v0_ref, v1_ref, v2_ref, vc_ref, bias_ref, o_ref):
    qt = qt_ref[...]
    kk = jnp.concatenate([k0_ref[...], k1_ref[...], k2_ref[...], kc_ref[...]], axis=0)
    vt = jnp.concatenate([v0_ref[...], v1_ref[...], v2_ref[...], vc_ref[...]], axis=1)
    nq = qt.shape[1]
    zeros = jnp.zeros((HEAD_DIM, nq), BF16)
    ones = jnp.ones((VT_ROWS - HEAD_DIM, kk.shape[0]), BF16)

    def scores(pp):
        q2 = qt[LANES * pp:LANES * (pp + 1)]
        qcat = jnp.concatenate([jnp.concatenate([q2[0:HEAD_DIM], zeros], axis=0),
                                jnp.concatenate([zeros, q2[HEAD_DIM:]], axis=0)], axis=1)
        return jnp.dot(kk[:, LANES * pp:LANES * (pp + 1)], qcat, preferred_element_type=F32) + bias_ref[pp]

    s_all = [scores(pp) for pp in range(NA_PAIRS)]
    for pp in range(NA_PAIRS):
        s = s_all[pp]
        m = jnp.max(s, axis=0, keepdims=True)
        p = jnp.exp2((s - m).astype(BF16))
        vext = jnp.concatenate([vt[LANES * pp:LANES * (pp + 1)], ones], axis=0)
        pv = jnp.dot(vext, p, preferred_element_type=F32)
        o0 = pv[0:HEAD_DIM, 0:nq] * (1.0 / pv[LANES:LANES + 1, 0:nq])
        o1 = pv[HEAD_DIM:LANES, nq:] * (1.0 / pv[LANES:LANES + 1, nq:])
        o_ref[:, LANES * pp:LANES * (pp + 1)] = jnp.concatenate([o0, o1], axis=0).T.astype(BF16)


def _neighbourhood_attention(qt, k, vt, kc, vct, bias, bsz, t):
    nq = NA_ROWS * GRID_W
    nrb = t // nq
    wl = NA_PAIRS * LANES
    ngrp = D_MODEL // wl
    nctx = kc.shape[0] // bsz
    nwin = NA_KROWS // NA_ROWS

    def first_kblock(b, rb):
        return b * nrb + jnp.clip(rb - 1, 0, nrb - nwin)

    def kspec(d):
        return pl.BlockSpec((nq, wl), lambda hp, b, rb: (first_kblock(b, rb) + d, hp))

    def vspec(d):
        return pl.BlockSpec((wl, nq), lambda hp, b, rb: (hp, first_kblock(b, rb) + d))

    def case(rb):
        return jnp.where(rb == 0, 0, jnp.where(rb == nrb - 1, 2, 1))

    return pl.pallas_call(
        _na_kernel,
        grid=(ngrp, bsz, nrb),
        in_specs=[pl.BlockSpec((wl, nq), lambda hp, b, rb: (hp, b * nrb + rb)),
                  kspec(0), kspec(1), kspec(2), pl.BlockSpec((nctx, wl), lambda hp, b, rb: (b, hp)),
                  vspec(0), vspec(1), vspec(2), pl.BlockSpec((wl, nctx), lambda hp, b, rb: (hp, b)),
                  pl.BlockSpec((None, None, NA_PAIRS, NA_KROWS * GRID_W + nctx, 2 * nq),
                               lambda hp, b, rb: (case(rb), hp, 0, 0, 0))],
        out_specs=pl.BlockSpec((nq, wl), lambda hp, b, rb: (b * nrb + rb, hp)),
        out_shape=jax.ShapeDtypeStruct((bsz * t, D_MODEL), BF16),
        compiler_params=_cparams("parallel", "parallel", "parallel"),
        name="neighbourhood_attention",
    )(qt, k, k, k, kc, vt, vt, vt, vct, bias)


def _store_row_tiles(dst_ref, val):
    rows = val.shape[0]
    for lt in range(ROW_TILES):
        dst_ref[pl.ds(lt, rows, stride=ROW_TILES), :] = val[:, lt * LANES:(lt + 1) * LANES]


def _load_row_tiles(src_ref, lt, rows):
    return src_ref[pl.ds(lt, rows, stride=ROW_TILES), :]


def _router_kernel(x_ref, sc_ref, sh_ref, wr_ref, h_ref, r_ref):
    h = x_ref[...] * (1.0 + sc_ref[0]) + sh_ref[0]
    _store_row_tiles(h_ref, h)
    hi = h.astype(BF16)
    lo = (h - hi.astype(F32)).astype(BF16)
    w = wr_ref[...]
    whi = w.astype(BF16)
    wlo = (w - whi.astype(F32)).astype(BF16)
    lg = (jnp.dot(hi, whi, preferred_element_type=F32)
          + (jnp.dot(hi, wlo, preferred_element_type=F32) + jnp.dot(lo, whi, preferred_element_type=F32)))
    lane = lax.broadcasted_iota(jnp.int32, lg.shape, 1).astype(F32)
    lg = jnp.where(lane < N_EXPERTS, lg, -jnp.inf)
    v1 = jnp.max(lg, axis=1, keepdims=True)
    i1 = jnp.min(jnp.where(lg == v1, lane, float(LANES)), axis=1, keepdims=True)
    lg2 = jnp.where(lane == i1, -jnp.inf, lg)
    v2 = jnp.max(lg2, axis=1, keepdims=True)
    i2 = jnp.min(jnp.where(lg2 == v2, lane, float(LANES)), axis=1, keepdims=True)
    e = jnp.exp(v2 - v1)
    w1 = 1.0 / (1.0 + e)
    w2 = e / (1.0 + e)
    r_ref[...] = jnp.where(lane == 0, i1, jnp.where(lane == 1, i2, jnp.where(lane == 2, w1,
                                                                              jnp.where(lane == 3, w2, 0.0))))


def _router(x2d, vecs, w_router, rows_per_batch, tm):
    r = x2d.shape[0]
    wr = jnp.pad(w_router, ((0, 0), (0, LANES - N_EXPERTS)))
    return pl.pallas_call(
        _router_kernel,
        grid=(r // tm,),
        in_specs=[pl.BlockSpec((tm, D_MODEL), lambda i: (i, 0)),
                  _vec_spec(4, tm, rows_per_batch), _vec_spec(3, tm, rows_per_batch),
                  _full_spec((D_MODEL, LANES))],
        out_specs=[pl.BlockSpec((tm * ROW_TILES, LANES), lambda i: (i, 0)),
                   pl.BlockSpec((tm, LANES), lambda i: (i, 0))],
        out_shape=[jax.ShapeDtypeStruct((r * ROW_TILES, LANES), F32), jax.ShapeDtypeStruct((r, LANES), F32)],
        compiler_params=_cparams("parallel"),
        name="router",
    )(x2d, vecs, vecs, wr)


def _row_copy(src_hbm, row, r, dst_ref, sem):
    def tile(i):
        start = i * ROW_TILES
        return pl.ds(start if isinstance(i, int) else pl.multiple_of(start, ROW_TILES), ROW_TILES)

    return pltpu.make_async_copy(src_hbm.at[tile(row), :], dst_ref.at[tile(r), :], sem)


def _start_row_gather(src_hbm, idx_ref, dst_ref, sem, n, priorities):
    def issue(i, carry):
        for u in range(2):
            r = 2 * i + u
            _row_copy(src_hbm, idx_ref[0, r], r, dst_ref, sem).start(priority=priorities[u])
        return carry

    lax.fori_loop(0, n // 2, issue, 0, unroll=4)


def _wait_row_gather(src_hbm, dst_ref, sem, n):
    def wait(r, carry):
        _row_copy(src_hbm, 0, r, dst_ref, sem).wait()
        return carry

    lax.fori_loop(0, n, wait, 0, unroll=8)


def _moe_ffn_kernel(te_ref, nu_ref, idx0_ref, idxn_ref, h_hbm, wg_ref, wu_ref, wd_ref, o_ref,
                    xbuf_ref, acc_ref, sem):
    t = pl.program_id(0)
    j = pl.program_id(1)
    nj = pl.num_programs(1)
    used = t < nu_ref[0]
    tm = acc_ref.shape[0]
    slot = t % 2

    @pl.when(jnp.logical_and(used, j == 0))
    def _():
        @pl.when(t == 0)
        def _():
            _start_row_gather(h_hbm, idx0_ref, xbuf_ref.at[0], sem.at[0], tm, (0, 0))

        @pl.when(t + 1 < nu_ref[0])
        def _():
            _start_row_gather(h_hbm, idxn_ref, xbuf_ref.at[1 - slot], sem.at[1 - slot], tm, (0, 0))

        _wait_row_gather(h_hbm, xbuf_ref.at[slot], sem.at[slot], tm)

    @pl.when(jnp.logical_and(t == 0, j == 0))
    def _():
        acc_ref[...] = jnp.zeros_like(acc_ref)

    @pl.when(used)
    def _():
        x = xbuf_ref.at[slot]
        hb = jnp.concatenate([_load_row_tiles(x, lt, tm).astype(BF16) for lt in range(ROW_TILES)], axis=1)
        hg = jnp.dot(hb, wg_ref[...], preferred_element_type=F32)
        hu = jnp.dot(hb, wu_ref[...], preferred_element_type=F32)
        h1 = (hg * jax.nn.sigmoid(hg) * hu).astype(BF16)
        acc = jnp.where(j == 0, 0.0, acc_ref[...]) + jnp.dot(h1, wd_ref[...], preferred_element_type=F32)
        acc_ref[...] = acc
        _store_row_tiles(o_ref, acc)

    @pl.when(jnp.logical_and(jnp.logical_not(used), j == nj - 1))
    def _():
        o_ref[...] = jnp.zeros_like(o_ref)


def _moe_ffn(h, src, tile_expert, n_used, wg, wu, wd):
    p = src.shape[0]
    tm, tf = MOE_TM, MOE_TF
    nt = p // tm
    f = wg.shape[2]

    def jj(t, j, nu):
        return jnp.where(t < nu[0], j, 0)

    grid_spec = pltpu.PrefetchScalarGridSpec(
        num_scalar_prefetch=2,
        grid=(nt, f // tf),
        in_specs=[pl.BlockSpec((None, 1, tm), lambda t, j, te, nu: (0, 0, 0), memory_space=pltpu.SMEM),
                  pl.BlockSpec((None, 1, tm), lambda t, j, te, nu: (jnp.minimum(t + 1, nt - 1), 0, 0),
                               memory_space=pltpu.SMEM),
                  pl.BlockSpec(memory_space=pl.ANY),
                  pl.BlockSpec((None, D_MODEL, tf), lambda t, j, te, nu: (te[t], 0, jj(t, j, nu))),
                  pl.BlockSpec((None, D_MODEL, tf), lambda t, j, te, nu: (te[t], 0, jj(t, j, nu))),
                  pl.BlockSpec((None, tf, D_MODEL), lambda t, j, te, nu: (te[t], jj(t, j, nu), 0))],
        out_specs=pl.BlockSpec((tm * ROW_TILES, LANES), lambda t, j, te, nu: (t, 0)),
        scratch_shapes=[pltpu.VMEM((2, tm * ROW_TILES, LANES), F32), pltpu.VMEM((tm, D_MODEL), F32),
                        pltpu.SemaphoreType.DMA((2,))],
    )
    idx = src.reshape(nt, 1, tm)
    return pl.pallas_call(
        _moe_ffn_kernel,
        grid_spec=grid_spec,
        out_shape=jax.ShapeDtypeStruct((p * ROW_TILES, LANES), F32),
        compiler_params=_cparams("arbitrary", "arbitrary"),
        name="moe_ffn",
    )(tile_expert, n_used, idx, idx, h, wg, wu, wd)


def _combine_ln_kernel(p1a_ref, p2a_ref, p1b_ref, p2b_ref, ys_hbm, route_ref, x_ref, gate_ref, lng_ref, lnb_ref,
                       o_ref, y_ref, sem):
    i = pl.program_id(0)
    n = x_ref.shape[0]
    slot = i % 2

    def start(p1_ref, p2_ref, s):
        _start_row_gather(ys_hbm, p1_ref, y_ref.at[s, 0], sem.at[s, 0], n, (0, 1))
        _start_row_gather(ys_hbm, p2_ref, y_ref.at[s, 1], sem.at[s, 1], n, (0, 1))

    @pl.when(i == 0)
    def _():
        start(p1a_ref, p2a_ref, 0)

    @pl.when(i + 1 < pl.num_programs(0))
    def _():
        start(p1b_ref, p2b_ref, 1 - slot)

    for e in range(2):
        _wait_row_gather(ys_hbm, y_ref.at[slot, e], sem.at[slot, e], n)
    w1 = route_ref[:, 2:3]
    w2 = route_ref[:, 3:4]
    y = jnp.concatenate([w1 * _load_row_tiles(y_ref.at[slot, 0], lt, n) + w2 * _load_row_tiles(y_ref.at[slot, 1], lt, n)
                         for lt in range(ROW_TILES)], axis=1)
    z = ALPHA * x_ref[...] + gate_ref[0] * y
    o_ref[...] = _layer_norm(z, lng_ref[...], lnb_ref[...])


def _combine_ln(ys, pos1, pos2, route, x2d, vecs, ln_g, ln_b, rows_per_batch):
    r = x2d.shape[0]
    g = GATHER_ROWS
    ns = r // g
    first = pl.BlockSpec((None, 1, g), lambda i: (0, 0, 0), memory_space=pltpu.SMEM)
    ahead = pl.BlockSpec((None, 1, g), lambda i: (jnp.minimum(i + 1, ns - 1), 0, 0), memory_space=pltpu.SMEM)
    p1 = pos1.reshape(ns, 1, g)
    p2 = pos2.reshape(ns, 1, g)
    return pl.pallas_call(
        _combine_ln_kernel,
        grid=(ns,),
        in_specs=[first, first, ahead, ahead, pl.BlockSpec(memory_space=pl.ANY),
                  pl.BlockSpec((g, LANES), lambda i: (i, 0)),
                  pl.BlockSpec((g, D_MODEL), lambda i: (i, 0)),
                  _vec_spec(5, g, rows_per_batch), _full_spec((1, D_MODEL)), _full_spec((1, D_MODEL))],
        out_specs=pl.BlockSpec((g, D_MODEL), lambda i: (i, 0)),
        out_shape=jax.ShapeDtypeStruct((r, D_MODEL), F32),
        scratch_shapes=[pltpu.VMEM((2, 2, g * ROW_TILES, LANES), F32), pltpu.SemaphoreType.DMA((2, 2))],
        compiler_params=_cparams("arbitrary"),
        name="combine_ln",
    )(p1, p2, p1, p2, ys, route, x2d, vecs, ln_g.reshape(1, D_MODEL), ln_b.reshape(1, D_MODEL))


def _routing_plan(route, tm):
    n = route.shape[0]
    e = jnp.concatenate([route[:, 0], route[:, 1]]).astype(jnp.int32)
    onehot = (e[:, None] == jnp.arange(N_EXPERTS, dtype=jnp.int32)[None, :]).astype(jnp.int32)
    csum = jnp.cumsum(onehot, axis=0)
    rank = jnp.sum(csum * onehot, axis=1) - 1
    counts = csum[-1]
    padded = ((counts + tm - 1) // tm) * tm
    ends = jnp.cumsum(padded)
    starts = ends - padded
    pos = jnp.sum(starts[None, :] * onehot, axis=1) + rank
    p = 2 * n + N_EXPERTS * tm
    by_expert = jnp.argsort(e, stable=True).astype(jnp.int32)
    tok_sorted = jnp.pad(jnp.where(by_expert >= n, by_expert - n, by_expert), (N_EXPERTS * tm, p - 2 * n))
    shift = starts - (jnp.cumsum(counts) - counts)
    slot = jnp.arange(p, dtype=jnp.int32)
    src = jnp.zeros((p,), jnp.int32)
    for g in range(N_EXPERTS):
        cand = lax.dynamic_slice(tok_sorted, (N_EXPERTS * tm - shift[g],), (p,))
        src = jnp.where((slot >= starts[g]) & (slot < ends[g]), cand, src)
    tile_start = jnp.arange(p // tm, dtype=jnp.int32) * tm
    tile_expert = jnp.minimum(jnp.sum((tile_start[:, None] >= ends[None, :]).astype(jnp.int32), axis=1),
                              N_EXPERTS - 1).astype(jnp.int32)
    n_used = (ends[-1] // tm).astype(jnp.int32).reshape(1)
    return src, tile_expert, n_used, pos[:n], pos[n:]


def kernel(x, c, ctx, c_ctx, w_mod, b_mod, ln_g, ln_b, ab_w_in, ab_conv_w, ab_conv_g, ab_conv_b, ab_q_g, ab_k_g,
           ab_w_out, ffn_w_gate, ffn_w_up, ffn_w_down, na_w_qkv, na_rpb, na_w_out, moe_w_router, moe_w_gate,
           moe_w_up, moe_w_down):
    bsz, t, d = x.shape
    n_ctx = ctx.shape[1]
    n = bsz * t
    nc = bsz * n_ctx
    x2 = x.reshape(n, d)
    c2 = ctx.reshape(nc, d)

    cc = jnp.concatenate([c, c_ctx[None, :], jnp.zeros((8 - bsz - 1, d), F32)], axis=0)
    mod = _modulation(cc, w_mod, b_mod)
    vec0 = mod[0].reshape(8 * 6, 1, d)
    vec1 = mod[1].reshape(8 * 6, 1, d)

    w_in = ab_w_in[0].astype(BF16)
    w_out = ab_w_out[0].astype(BF16)
    pa, pq = _inproj0(x2, vec0, w_in, t, 1024)
    pac, pqc = _inproj0(c2, vec0, w_in, None, 512)
    a = _conformer_conv(pa, ab_conv_w[0], ab_conv_g[0], ab_conv_b[0], t, 512)
    ac = _conformer_conv(pac, ab_conv_w[0], ab_conv_g[0], ab_conv_b[0], n_ctx, n_ctx)
    cos_t, sin_t = _rope_tables(t)
    qt, k, vt = _prep_qkv(pq, cos_t, sin_t, ab_q_g[0], ab_k_g[0], bsz, t)
    ones = jnp.ones((HEAD_DIM, n_ctx), F32)
    qtc, kc, vtc = _prep_qkv(pqc, ones, jnp.zeros_like(ones), ab_q_g[0], ab_k_g[0], bsz, n_ctx)
    o = _gqa_attention(qt, jnp.concatenate([kc, k], axis=1), jnp.concatenate([vtc, vt], axis=2), t)
    oc = _gqa_attention(qtc, kc, vtc, n_ctx)
    x2 = _outproj_ln([a, o], w_out, x2, vec0, 2, ln_g[0, 0], ln_b[0, 0], t, 1024)
    c2 = _outproj_ln([ac, oc], w_out, c2, vec0, 2, ln_g[0, 0], ln_b[0, 0], None, 512)
    wg = ffn_w_gate[0].astype(BF16)
    wu = ffn_w_up[0].astype(BF16)
    wd = ffn_w_down[0].astype(BF16)
    x2 = _ffn_ln(x2, vec0, wg, wu, wd, ln_g[0, 1], ln_b[0, 1], t, 512)
    c2 = _ffn_ln(c2, vec0, wg, wu, wd, ln_g[0, 1], ln_b[0, 1], None, 512)

    w_qkv = na_w_qkv[0].astype(BF16)
    qt1, k1, vt1 = _inproj1(x2, vec1, w_qkv, t, 512, True)
    kc1, vct1 = _inproj1(c2, vec1, w_qkv[:, d:], None, 512, False)
    o = _neighbourhood_attention(qt1, k1, vt1, kc1, vct1, _na_bias(na_rpb[0], t // GRID_W, n_ctx), bsz, t)
    x2 = _outproj_ln([o], na_w_out[0].astype(BF16), x2, vec1, 2, ln_g[1, 0], ln_b[1, 0], t, 1024)

    h, route = _router(x2, vec1, moe_w_router[0], t, 1024)
    src, tile_expert, n_used, pos1, pos2 = _routing_plan(route, MOE_TM)
    ys = _moe_ffn(h, src, tile_expert, n_used, moe_w_gate[0].astype(BF16), moe_w_up[0].astype(BF16),
                  moe_w_down[0].astype(BF16))
    x2 = _combine_ln(ys, pos1, pos2, route, x2, vec1, ln_g[1, 1], ln_b[1, 1], t)
    return x2.reshape(bsz, t, d)
```

```python
import functools

import numpy as np
import jax
import jax.numpy as jnp
from jax import lax
from jax.experimental import pallas as pl
from jax.experimental.pallas import tpu as pltpu

F32 = jnp.float32
BF16 = jnp.bfloat16

D_MODEL = 1024
GRID_W = 64
HEAD_DIM = 64
CONV_CH = 512
CONV_WIDTH = 31
CONV_HALO = 16
Q_COLS = 512
KV_COLS = 128
A_COLS = 2 * CONV_CH
ROPE_THETA = 10000.0
WIN_R = 8
WIN_C = 16
N_EXPERTS = 8
DEPTH = 2
ALPHA = (2 * DEPTH) ** 0.25
LN_EPS = 1e-5
RMS_EPS = 1e-6
ATT_SCALE = HEAD_DIM ** -0.5
LOG2E = 1.4426950408889634
MASK_VALUE = -1e30

LANES = 128
ROW_TILES = D_MODEL // LANES
VMEM_LIMIT = 56 * 1024 * 1024

ATT_TQ = 256
ATT_CW = 256
VT_ROWS = HEAD_DIM + 16
NA_ROWS = 4
NA_KROWS = 12
NA_PAIRS = 4
MOE_TM = 512
MOE_TF = 1792
GATHER_ROWS = 512


def _cparams(*sem):
    return pltpu.CompilerParams(dimension_semantics=sem, vmem_limit_bytes=VMEM_LIMIT)


def _layer_norm(z, g, b):
    mu = jnp.mean(z, axis=-1, keepdims=True)
    zc = z - mu
    var = jnp.mean(zc * zc, axis=-1, keepdims=True)
    return zc * lax.rsqrt(var + LN_EPS) * g + b


def _vec_spec(k, tm, rows_per_batch):
    if rows_per_batch is None:
        return pl.BlockSpec((1, 1, D_MODEL), lambda i, *_: (4 * 6 + k, 0, 0))
    return pl.BlockSpec((1, 1, D_MODEL), lambda i, *_: ((i * tm // rows_per_batch) * 6 + k, 0, 0))


def _full_spec(shape):
    nd = len(shape)
    return pl.BlockSpec(shape, lambda *_: (0,) * nd)


def _mod_kernel(c_ref, w_ref, b_ref, o_ref):
    c = c_ref[...]
    s = c * jax.nn.sigmoid(c)
    o_ref[...] = jnp.dot(s.astype(BF16), w_ref[...].astype(BF16), preferred_element_type=F32) + b_ref[...]


def _modulation(cc, w_mod, b_mod):
    n = 6 * D_MODEL
    tn = D_MODEL
    return pl.pallas_call(
        _mod_kernel,
        grid=(DEPTH, n // tn),
        in_specs=[pl.BlockSpec((8, D_MODEL), lambda l, j: (0, 0)),
                  pl.BlockSpec((None, D_MODEL, tn), lambda l, j: (l, 0, j)),
                  pl.BlockSpec((None, 1, tn), lambda l, j: (l, 0, j))],
        out_specs=pl.BlockSpec((None, 8, tn), lambda l, j: (l, 0, j)),
        out_shape=jax.ShapeDtypeStruct((DEPTH, 8, n), F32),
        compiler_params=_cparams("parallel", "parallel"),
        name="modulation",
    )(cc, w_mod, b_mod.reshape(DEPTH, 1, n))


def _inproj0_kernel(x_ref, sc_ref, sh_ref, w_ref, oa_ref, oq_ref):
    h = x_ref[...] * (1.0 + sc_ref[0]) + sh_ref[0]
    o = jnp.dot(h.astype(BF16), w_ref[...], preferred_element_type=F32)
    oa_ref[...] = o[:, :A_COLS]
    oq_ref[...] = o[:, A_COLS:]


def _inproj0(x2d, vecs, w, rows_per_batch, tm):
    r = x2d.shape[0]
    nq = w.shape[1] - A_COLS
    return pl.pallas_call(
        _inproj0_kernel,
        grid=(r // tm,),
        in_specs=[pl.BlockSpec((tm, D_MODEL), lambda i: (i, 0)),
                  _vec_spec(1, tm, rows_per_batch), _vec_spec(0, tm, rows_per_batch),
                  _full_spec(w.shape)],
        out_specs=[pl.BlockSpec((tm, A_COLS), lambda i: (i, 0)),
                   pl.BlockSpec((tm, nq), lambda i: (i, 0))],
        out_shape=[jax.ShapeDtypeStruct((r, A_COLS), F32), jax.ShapeDtypeStruct((r, nq), F32)],
        compiler_params=_cparams("parallel"),
        name="inproj0",
    )(x2d, vecs, vecs, w)


def _conv_kernel(pm_ref, pp_ref, pn_ref, w_ref, g_ref, b_ref, o_ref, u_ref, us_ref, cv_ref, *, tt, tx):
    i = pl.program_id(0)

    def sigmoid(v):
        return 0.5 * jnp.tanh(0.5 * v) + 0.5

    def glu(p):
        return p[:, :CONV_CH] * sigmoid(p[:, CONV_CH:])

    first = (i * tt) % tx == 0
    last = ((i + 1) * tt) % tx == 0
    u_ref[0:CONV_HALO, :] = jnp.where(first, 0.0, glu(pp_ref[...]))
    u_ref[CONV_HALO:CONV_HALO + tt, :] = glu(pm_ref[...])
    u_ref[CONV_HALO + tt:2 * CONV_HALO + tt, :] = jnp.where(last, 0.0, glu(pn_ref[...]))
    u_ref[2 * CONV_HALO + tt:, :] = jnp.zeros((8, CONV_CH), F32)
    ch = 32
    nrow = tt + 2 * CONV_HALO
    base = CONV_HALO - CONV_WIDTH // 2

    def shift_body(c, carry):
        r0 = pl.multiple_of(c * ch, ch)
        w = u_ref[pl.ds(r0, ch + 8), :]
        for s in range(1, 8):
            us_ref[s - 1, pl.ds(r0, ch), :] = pltpu.roll(w, ch + 8 - s, axis=0)[0:ch]
        return carry

    lax.fori_loop(0, nrow // ch, shift_body, 0)

    def body(c, carry):
        r0 = pl.multiple_of(c * ch, ch)
        acc = jnp.zeros((ch, CONV_CH), F32)
        for k in range(CONV_WIDTH):
            a, s = divmod(k + base, 8)
            src = u_ref if s == 0 else us_ref.at[s - 1]
            acc = acc + src[pl.ds(r0 + 8 * a, ch), :] * w_ref[pl.ds(k, 1), :]
        cv_ref[pl.ds(r0, ch), :] = acc
        return carry

    lax.fori_loop(0, tt // ch, body, 0)
    y = _layer_norm(cv_ref[...], g_ref[...], b_ref[...])
    o_ref[...] = (y * sigmoid(y)).astype(BF16)


def _conformer_conv(pa, conv_w, conv_g, conv_b, tx, tt):
    r = pa.shape[0]
    hb = tt // CONV_HALO
    nhb = r // CONV_HALO
    return pl.pallas_call(
        functools.partial(_conv_kernel, tt=tt, tx=tx),
        grid=(r // tt,),
        in_specs=[pl.BlockSpec((tt, A_COLS), lambda i: (i, 0)),
                  pl.BlockSpec((CONV_HALO, A_COLS), lambda i: (jnp.maximum(i * hb - 1, 0), 0)),
                  pl.BlockSpec((CONV_HALO, A_COLS), lambda i: (jnp.minimum((i + 1) * hb, nhb - 1), 0)),
                  _full_spec((CONV_WIDTH, CONV_CH)), _full_spec((1, CONV_CH)), _full_spec((1, CONV_CH))],
        out_specs=pl.BlockSpec((tt, CONV_CH), lambda i: (i, 0)),
        out_shape=jax.ShapeDtypeStruct((r, CONV_CH), BF16),
        scratch_shapes=[pltpu.VMEM((tt + 2 * CONV_HALO + 8, CONV_CH), F32),
                        pltpu.VMEM((7, tt + 2 * CONV_HALO, CONV_CH), F32),
                        pltpu.VMEM((tt, CONV_CH), F32)],
        compiler_params=_cparams("parallel"),
        name="conformer_conv",
    )(pa, pa, pa, conv_w, conv_g.reshape(1, CONV_CH), conv_b.reshape(1, CONV_CH))


def _prep_kernel(p_ref, cos_ref, sin_ref, qg_ref, kg_ref, qt_ref, k_ref, vt_ref, *, tq):
    x = p_ref[...]
    cos = cos_ref[...]
    sin = sin_ref[...]

    def norm_rope(xh, g):
        ms = jnp.mean(xh * xh, axis=0, keepdims=True)
        y = xh * lax.rsqrt(ms + RMS_EPS) * g
        sw = jnp.concatenate([y[16:32], y[0:16], y[48:64], y[32:48]], axis=0)
        return y * cos + sw * sin

    zeros = jnp.zeros((HEAD_DIM, tq), BF16)
    for p in range(Q_COLS // LANES):
        xp = x[:, LANES * p:LANES * (p + 1)].T
        for half in range(2):
            h = 2 * p + half
            j, g = h // 4, h % 4
            r = (norm_rope(xp[HEAD_DIM * half:HEAD_DIM * (half + 1)], qg_ref[...]) * (ATT_SCALE * LOG2E)).astype(BF16)
            qt_ref[j, HEAD_DIM * j:HEAD_DIM * (j + 1), g * tq:(g + 1) * tq] = r
            qt_ref[j, HEAD_DIM * (1 - j):HEAD_DIM * (2 - j), g * tq:(g + 1) * tq] = zeros
    xk = x[:, Q_COLS:Q_COLS + KV_COLS].T
    k0 = norm_rope(xk[0:HEAD_DIM], kg_ref[...])
    k1 = norm_rope(xk[HEAD_DIM:2 * HEAD_DIM], kg_ref[...])
    k_ref[...] = jnp.concatenate([k0, k1], axis=0).T.astype(BF16)
    xv = x[:, Q_COLS + KV_COLS:].T.astype(BF16)
    ones = jnp.ones((VT_ROWS - HEAD_DIM, tq), BF16)
    for j in range(2):
        vt_ref[j, 0:HEAD_DIM, :] = xv[HEAD_DIM * j:HEAD_DIM * (j + 1)]
        vt_ref[j, HEAD_DIM:VT_ROWS, :] = ones


def _prep_qkv(pq, cos_t, sin_t, q_g, k_g, bx, tx):
    tq = ATT_TQ
    nq = tx // tq
    return pl.pallas_call(
        functools.partial(_prep_kernel, tq=tq),
        grid=(bx, nq),
        in_specs=[pl.BlockSpec((tq, Q_COLS + 2 * KV_COLS), lambda b, t: (b * nq + t, 0)),
                  pl.BlockSpec((HEAD_DIM, tq), lambda b, t: (0, t)),
                  pl.BlockSpec((HEAD_DIM, tq), lambda b, t: (0, t)),
                  _full_spec((HEAD_DIM, 1)), _full_spec((HEAD_DIM, 1))],
        out_specs=[pl.BlockSpec((None, None, 2, LANES, 4 * tq), lambda b, t: (b, t, 0, 0, 0)),
                   pl.BlockSpec((None, None, tq, LANES), lambda b, t: (b, t, 0, 0)),
                   pl.BlockSpec((None, 2, None, VT_ROWS, tq), lambda b, t: (b, 0, t, 0, 0))],
        out_shape=[jax.ShapeDtypeStruct((bx, nq, 2, LANES, 4 * tq), BF16),
                   jax.ShapeDtypeStruct((bx, nq, tq, LANES), BF16),
                   jax.ShapeDtypeStruct((bx, 2, nq, VT_ROWS, tq), BF16)],
        compiler_params=_cparams("parallel", "parallel"),
        name="prep_qkv",
    )(pq, cos_t, sin_t, q_g.reshape(HEAD_DIM, 1), k_g.reshape(HEAD_DIM, 1))


def _rope_tables(t):
    pos = np.arange(t)
    half = HEAD_DIM // 4
    inv = ROPE_THETA ** (-jnp.arange(half, dtype=F32) * 2.0 / (HEAD_DIM // 2))
    dd = np.arange(HEAD_DIM)
    part_pos = np.where((dd // (HEAD_DIM // 2))[:, None] == 0, (pos // GRID_W)[None, :], (pos % GRID_W)[None, :])
    ang = jnp.asarray(part_pos, F32) * inv[dd % half][:, None]
    sign = jnp.asarray(np.where((dd % (HEAD_DIM // 2)) < half, -1.0, 1.0)[:, None], F32)
    return jnp.cos(ang), jnp.sin(ang) * sign


def _attn_kernel(qt_ref, k_ref, vt_ref, o_ref, acc_ref, *, nk, tq):
    ngrp = qt_ref.shape[0]
    acc_ref[...] = jnp.zeros_like(acc_ref)
    cw = ATT_CW
    strips = [(j, n) for j in range(ngrp) for n in range(4 * tq // cw)]

    def scores(c, j, n):
        return jnp.dot(k_ref[c], qt_ref[j, :, cw * n:cw * (n + 1)], preferred_element_type=F32)

    s = [scores(0, j, n) for j, n in strips]
    m_prev = [jnp.full((1, cw), -jnp.inf, F32)] * len(strips)
    m = [jnp.max(sn, axis=0, keepdims=True) for sn in s]
    for c in range(nk):
        for i, (j, n) in enumerate(strips):
            s_next = scores(c + 1, j, n) if c + 1 < nk else None
            alpha = jnp.exp2(m_prev[i] - m[i])
            p = jnp.exp2((s[i] - m[i]).astype(BF16))
            pv = jnp.dot(vt_ref[j, c], p, preferred_element_type=F32)
            acc_ref[j, :, cw * n:cw * (n + 1)] = acc_ref[j, :, cw * n:cw * (n + 1)] * alpha + pv
            if s_next is not None:
                m_prev[i] = m[i]
                m[i] = jnp.maximum(m[i], jnp.max(s_next, axis=0, keepdims=True))
                s[i] = s_next
    for j in range(ngrp):
        o = acc_ref[j, 0:HEAD_DIM, :] * (1.0 / acc_ref[j, HEAD_DIM:HEAD_DIM + 1, :])
        for pp in range(2):
            blk = jnp.concatenate([o[:, (2 * pp) * tq:(2 * pp + 1) * tq],
                                   o[:, (2 * pp + 1) * tq:(2 * pp + 2) * tq]], axis=0)
            lane0 = LANES * (2 * j + pp)
            o_ref[:, lane0:lane0 + LANES] = blk.T.astype(BF16)


def _gqa_attention(qt, k, vt, tx):
    bx, nq = qt.shape[0], qt.shape[1]
    nk = k.shape[1]
    tq = ATT_TQ
    return pl.pallas_call(
        functools.partial(_attn_kernel, nk=nk, tq=tq),
        grid=(bx, nq),
        in_specs=[pl.BlockSpec((None, None, 2, LANES, 4 * tq), lambda b, t: (b, t, 0, 0, 0)),
                  pl.BlockSpec((None, nk, tq, LANES), lambda b, t: (b, 0, 0, 0)),
                  pl.BlockSpec((None, 2, nk, VT_ROWS, tq), lambda b, t: (b, 0, 0, 0, 0))],
        out_specs=pl.BlockSpec((tq, Q_COLS), lambda b, t: (b * nq + t, 0)),
        out_shape=jax.ShapeDtypeStruct((bx * tx, Q_COLS), BF16),
        scratch_shapes=[pltpu.VMEM((2, VT_ROWS, 4 * tq), F32)],
        compiler_params=_cparams("parallel", "parallel"),
        name="gqa_attention",
    )(qt, k, vt)


def _outproj_ln_kernel(*refs, n_lhs):
    lhs = refs[:n_lhs]
    w_ref, x_ref, gate_ref, lng_ref, lnb_ref, o_ref = refs[n_lhs:]
    y = None
    off = 0
    for r in lhs:
        kk = r.shape[1]
        t = jnp.dot(r[...], w_ref[off:off + kk, :], preferred_element_type=F32)
        off += kk
        y = t if y is None else y + t
    z = ALPHA * x_ref[...] + gate_ref[0] * y
    o_ref[...] = _layer_norm(z, lng_ref[...], lnb_ref[...])


def _outproj_ln(lhs, w, x2d, vecs, gate_k, ln_g, ln_b, rows_per_batch, tm):
    r = x2d.shape[0]
    return pl.pallas_call(
        functools.partial(_outproj_ln_kernel, n_lhs=len(lhs)),
        grid=(r // tm,),
        in_specs=[pl.BlockSpec((tm, a.shape[1]), lambda i: (i, 0)) for a in lhs]
        + [_full_spec(w.shape), pl.BlockSpec((tm, D_MODEL), lambda i: (i, 0)),
           _vec_spec(gate_k, tm, rows_per_batch), _full_spec((1, D_MODEL)), _full_spec((1, D_MODEL))],
        out_specs=pl.BlockSpec((tm, D_MODEL), lambda i: (i, 0)),
        out_shape=jax.ShapeDtypeStruct((r, D_MODEL), F32),
        compiler_params=_cparams("parallel"),
        name="outproj_ln",
    )(*lhs, w, x2d, vecs, ln_g.reshape(1, D_MODEL), ln_b.reshape(1, D_MODEL))


def _ffn_kernel(x_ref, sc_ref, sh_ref, gate_ref, wg_ref, wu_ref, wd_ref, lng_ref, lnb_ref, o_ref):
    x = x_ref[...]
    hb = (x * (1.0 + sc_ref[0]) + sh_ref[0]).astype(BF16)
    hg = jnp.dot(hb, wg_ref[...], preferred_element_type=F32)
    hu = jnp.dot(hb, wu_ref[...], preferred_element_type=F32)
    h1 = (hg * jax.nn.sigmoid(hg) * hu).astype(BF16)
    y = jnp.dot(h1, wd_ref[...], preferred_element_type=F32)
    o_ref[...] = _layer_norm(ALPHA * x + gate_ref[0] * y, lng_ref[...], lnb_ref[...])


def _ffn_ln(x2d, vecs, wg, wu, wd, ln_g, ln_b, rows_per_batch, tm):
    r = x2d.shape[0]

    def resident(shape):
        return pl.BlockSpec(shape, lambda i: (0, 0), pipeline_mode=pl.Buffered(1))

    return pl.pallas_call(
        _ffn_kernel,
        grid=(r // tm,),
        in_specs=[pl.BlockSpec((tm, D_MODEL), lambda i: (i, 0)),
                  _vec_spec(4, tm, rows_per_batch), _vec_spec(3, tm, rows_per_batch),
                  _vec_spec(5, tm, rows_per_batch),
                  resident(wg.shape), resident(wu.shape), resident(wd.shape),
                  _full_spec((1, D_MODEL)), _full_spec((1, D_MODEL))],
        out_specs=pl.BlockSpec((tm, D_MODEL), lambda i: (i, 0)),
        out_shape=jax.ShapeDtypeStruct((r, D_MODEL), F32),
        compiler_params=_cparams("parallel"),
        name="ffn_ln",
    )(x2d, vecs, vecs, vecs, wg, wu, wd, ln_g.reshape(1, D_MODEL), ln_b.reshape(1, D_MODEL))


def _inproj1_kernel(x_ref, sc_ref, sh_ref, w_ref, *o_refs, with_q):
    h = x_ref[...] * (1.0 + sc_ref[0]) + sh_ref[0]
    o = jnp.dot(h.astype(BF16), w_ref[...], preferred_element_type=F32)
    if with_q:
        qt_ref, k_ref, vt_ref = o_refs
        qt_ref[...] = (o[:, :D_MODEL] * (ATT_SCALE * LOG2E)).T.astype(BF16)
    else:
        k_ref, vt_ref = o_refs
    nk = o.shape[1] - 2 * D_MODEL
    k_ref[...] = o[:, nk:nk + D_MODEL].astype(BF16)
    vt_ref[...] = o[:, nk + D_MODEL:].T.astype(BF16)


def _inproj1(x2d, vecs, w, rows_per_batch, tm, with_q):
    r = x2d.shape[0]
    nat = pl.BlockSpec((tm, D_MODEL), lambda i: (i, 0))
    tr = pl.BlockSpec((D_MODEL, tm), lambda i: (0, i))
    nat_shape = jax.ShapeDtypeStruct((r, D_MODEL), BF16)
    tr_shape = jax.ShapeDtypeStruct((D_MODEL, r), BF16)
    return pl.pallas_call(
        functools.partial(_inproj1_kernel, with_q=with_q),
        grid=(r // tm,),
        in_specs=[pl.BlockSpec((tm, D_MODEL), lambda i: (i, 0)),
                  _vec_spec(1, tm, rows_per_batch), _vec_spec(0, tm, rows_per_batch),
                  _full_spec(w.shape)],
        out_specs=([tr] if with_q else []) + [nat, tr],
        out_shape=([tr_shape] if with_q else []) + [nat_shape, tr_shape],
        compiler_params=_cparams("parallel"),
        name="inproj1",
    )(x2d, vecs, vecs, w)


def _na_bias(rpb, n_rows, n_ctx):
    nh = rpb.shape[0]
    c = np.arange(GRID_W)
    cs = np.clip(c - WIN_C // 2, 0, GRID_W - WIN_C)
    in_c = (c[None, :] >= cs[:, None]) & (c[None, :] < cs[:, None] + WIN_C)
    dc = c[None, :] - c[:, None] + WIN_C - 1
    pick = ((dc[None] == np.arange(2 * WIN_C - 1)[:, None, None]) & in_c[None]).astype(np.float32)
    cols = jnp.einsum("hrd,dkc->hrkc", rpb, jnp.asarray(pick.transpose(0, 2, 1)), precision=lax.Precision.HIGHEST)
    cols = jnp.where(jnp.asarray(in_c.T)[None, None], cols * LOG2E, MASK_VALUE)
    n_dr = 2 * WIN_R - 1
    cols = jnp.concatenate([cols, jnp.full((nh, 1, GRID_W, GRID_W), MASK_VALUE, F32)], axis=1)
    pick_dr = np.full((3, NA_ROWS, NA_KROWS), n_dr, np.int32)
    for case, r0 in enumerate((0, NA_ROWS, n_rows - NA_ROWS)):
        start = int(np.clip(r0 - WIN_R // 2, 0, n_rows - NA_KROWS))
        for ri in range(NA_ROWS):
            r = r0 + ri
            rs = int(np.clip(r - WIN_R // 2, 0, n_rows - WIN_R))
            for ki in range(NA_KROWS):
                kr = start + ki
                if rs <= kr < rs + WIN_R:
                    pick_dr[case, ri, ki] = kr - r + WIN_R - 1
    npair = nh // 2
    nkeys = NA_KROWS * GRID_W
    out = pl.pallas_call(
        functools.partial(_na_bias_kernel, nkeys=nkeys),
        grid_spec=pltpu.PrefetchScalarGridSpec(
            num_scalar_prefetch=1,
            grid=(3, npair),
            in_specs=[pl.BlockSpec((2, n_dr + 1, GRID_W, GRID_W), lambda case, p, dr: (p, 0, 0, 0))],
            out_specs=pl.BlockSpec((None, None, nkeys + n_ctx, 2 * NA_ROWS * GRID_W),
                                   lambda case, p, dr: (case, p, 0, 0)),
        ),
        out_shape=jax.ShapeDtypeStruct((3, npair, nkeys + n_ctx, 2 * NA_ROWS * GRID_W), F32),
        compiler_params=_cparams("parallel", "parallel"),
        name="na_bias",
    )(jnp.asarray(pick_dr.reshape(-1)), cols)
    return out.reshape(3, npair // NA_PAIRS, NA_PAIRS, nkeys + n_ctx, 2 * NA_ROWS * GRID_W)


def _na_bias_kernel(dr_ref, cols_ref, o_ref, *, nkeys):
    case = pl.program_id(0)
    for ki in range(NA_KROWS):
        pieces = []
        for e in range(2):
            for ri in range(NA_ROWS):
                d = dr_ref[(case * NA_ROWS + ri) * NA_KROWS + ki]
                pieces.append(cols_ref[e, d])
        o_ref[GRID_W * ki:GRID_W * (ki + 1), :] = jnp.concatenate(pieces, axis=1)
    o_ref[nkeys:, :] = jnp.zeros((o_ref.shape[0] - nkeys, o_ref.shape[1]), F32)


def _na_kernel(qt_ref, k0_ref, k1_ref, k2_ref, kc_ref, v0_ref, v1_ref, v2_ref, vc_ref, bias_ref, o_ref):
    qt = qt_ref[...]
    kk = jnp.concatenate([k0_ref[...], k1_ref[...], k2_ref[...], kc_ref[...]], axis=0)
    vt = jnp.concatenate([v0_ref[...], v1_ref[...], v2_ref[...], vc_ref[...]], axis=1)
    nq = qt.shape[1]
    zeros = jnp.zeros((HEAD_DIM, nq), BF16)
    ones = jnp.ones((VT_ROWS - HEAD_DIM, kk.shape[0]), BF16)

    def scores(pp):
        q2 = qt[LANES * pp:LANES * (pp + 1)]
        qcat = jnp.concatenate([jnp.concatenate([q2[0:HEAD_DIM], zeros], axis=0),
                                jnp.concatenate([zeros, q2[HEAD_DIM:]], axis=0)], axis=1)
        return jnp.dot(kk[:, LANES * pp:LANES * (pp + 1)], qcat, preferred_element_type=F32) + bias_ref[pp]

    s_all = [scores(pp) for pp in range(NA_PAIRS)]
    for pp in range(NA_PAIRS):
        s = s_all[pp]
        m = jnp.max(s, axis=0, keepdims=True)
        p = jnp.exp2((s - m).astype(BF16))
        vext = jnp.concatenate([vt[LANES * pp:LANES * (pp + 1)], ones], axis=0)
        pv = jnp.dot(vext, p, preferred_element_type=F32)
        o0 = pv[0:HEAD_DIM, 0:nq] * (1.0 / pv[LANES:LANES + 1, 0:nq])
        o1 = pv[HEAD_DIM:LANES, nq:] * (1.0 / pv[LANES:LANES + 1, nq:])
        o_ref[:, LANES * pp:LANES * (pp + 1)] = jnp.concatenate([o0, o1], axis=0).T.astype(BF16)


def _neighbourhood_attention(qt, k, vt, kc, vct, bias, bsz, t):
    nq = NA_ROWS * GRID_W
    nrb = t // nq
    wl = NA_PAIRS * LANES
    ngrp = D_MODEL // wl
    nctx = kc.shape[0] // bsz
    nwin = NA_KROWS // NA_ROWS

    def first_kblock(b, rb):
        return b * nrb + jnp.clip(rb - 1, 0, nrb - nwin)

    def kspec(d):
        return pl.BlockSpec((nq, wl), lambda hp, b, rb: (first_kblock(b, rb) + d, hp))

    def vspec(d):
        return pl.BlockSpec((wl, nq), lambda hp, b, rb: (hp, first_kblock(b, rb) + d))

    def case(rb):
        return jnp.where(rb == 0, 0, jnp.where(rb == nrb - 1, 2, 1))

    return pl.pallas_call(
        _na_kernel,
        grid=(ngrp, bsz, nrb),
        in_specs=[pl.BlockSpec((wl, nq), lambda hp, b, rb: (hp, b * nrb + rb)),
                  kspec(0), kspec(1), kspec(2), pl.BlockSpec((nctx, wl), lambda hp, b, rb: (b, hp)),
                  vspec(0), vspec(1), vspec(2), pl.BlockSpec((wl, nctx), lambda hp, b, rb: (hp, b)),
                  pl.BlockSpec((None, None, NA_PAIRS, NA_KROWS * GRID_W + nctx, 2 * nq),
                               lambda hp, b, rb: (case(rb), hp, 0, 0, 0))],
        out_specs=pl.BlockSpec((nq, wl), lambda hp, b, rb: (b * nrb + rb, hp)),
        out_shape=jax.ShapeDtypeStruct((bsz * t, D_MODEL), BF16),
        compiler_params=_cparams("parallel", "parallel", "parallel"),
        name="neighbourhood_attention",
    )(qt, k, k, k, kc, vt, vt, vt, vct, bias)


def _store_row_tiles(dst_ref, val):
    rows = val.shape[0]
    for lt in range(ROW_TILES):
        dst_ref[pl.ds(lt, rows, stride=ROW_TILES), :] = val[:, lt * LANES:(lt + 1) * LANES]


def _load_row_tiles(src_ref, lt, rows):
    return src_ref[pl.ds(lt, rows, stride=ROW_TILES), :]


def _router_kernel(x_ref, sc_ref, sh_ref, wr_ref, h_ref, r_ref):
    h = x_ref[...] * (1.0 + sc_ref[0]) + sh_ref[0]
    _store_row_tiles(h_ref, h)
    hi = h.astype(BF16)
    lo = (h - hi.astype(F32)).astype(BF16)
    w = wr_ref[...]
    whi = w.astype(BF16)
    wlo = (w - whi.astype(F32)).astype(BF16)
    lg = (jnp.dot(hi, whi, preferred_element_type=F32)
          + (jnp.dot(hi, wlo, preferred_element_type=F32) + jnp.dot(lo, whi, preferred_element_type=F32)))
    lane = lax.broadcasted_iota(jnp.int32, lg.shape, 1).astype(F32)
    lg = jnp.where(lane < N_EXPERTS, lg, -jnp.inf)
    v1 = jnp.max(lg, axis=1, keepdims=True)
    i1 = jnp.min(jnp.where(lg == v1, lane, float(LANES)), axis=1, keepdims=True)
    lg2 = jnp.where(lane == i1, -jnp.inf, lg)
    v2 = jnp.max(lg2, axis=1, keepdims=True)
    i2 = jnp.min(jnp.where(lg2 == v2, lane, float(LANES)), axis=1, keepdims=True)
    e = jnp.exp(v2 - v1)
    w1 = 1.0 / (1.0 + e)
    w2 = e / (1.0 + e)
    r_ref[...] = jnp.where(lane == 0, i1, jnp.where(lane == 1, i2, jnp.where(lane == 2, w1,
                                                                              jnp.where(lane == 3, w2, 0.0))))


def _router(x2d, vecs, w_router, rows_per_batch, tm):
    r = x2d.shape[0]
    wr = jnp.pad(w_router, ((0, 0), (0, LANES - N_EXPERTS)))
    return pl.pallas_call(
        _router_kernel,
        grid=(r // tm,),
        in_specs=[pl.BlockSpec((tm, D_MODEL), lambda i: (i, 0)),
                  _vec_spec(4, tm, rows_per_batch), _vec_spec(3, tm, rows_per_batch),
                  _full_spec((D_MODEL, LANES))],
        out_specs=[pl.BlockSpec((tm * ROW_TILES, LANES), lambda i: (i, 0)),
                   pl.BlockSpec((tm, LANES), lambda i: (i, 0))],
        out_shape=[jax.ShapeDtypeStruct((r * ROW_TILES, LANES), F32), jax.ShapeDtypeStruct((r, LANES), F32)],
        compiler_params=_cparams("parallel"),
        name="router",
    )(x2d, vecs, vecs, wr)


def _row_copy(src_hbm, row, r, dst_ref, sem):
    def tile(i):
        start = i * ROW_TILES
        return pl.ds(start if isinstance(i, int) else pl.multiple_of(start, ROW_TILES), ROW_TILES)

    return pltpu.make_async_copy(src_hbm.at[tile(row), :], dst_ref.at[tile(r), :], sem)


def _start_row_gather(src_hbm, idx_ref, dst_ref, sem, n, priorities):
    def issue(i, carry):
        for u in range(2):
            r = 2 * i + u
            _row_copy(src_hbm, idx_ref[0, r], r, dst_ref, sem).start(priority=priorities[u])
        return carry

    lax.fori_loop(0, n // 2, issue, 0, unroll=4)


def _wait_row_gather(src_hbm, dst_ref, sem, n):
    pltpu.make_async_copy(src_hbm.at[pl.ds(0, n * ROW_TILES), :], dst_ref, sem).wait()


def _moe_ffn_kernel(te_ref, nu_ref, idx0_ref, idxn_ref, h_hbm, wg_ref, wu_ref, wd_ref, o_ref,
                    xbuf_ref, acc_ref, sem):
    t = pl.program_id(0)
    j = pl.program_id(1)
    nj = pl.num_programs(1)
    used = t < nu_ref[0]
    tm = acc_ref.shape[0]
    slot = t % 2

    @pl.when(jnp.logical_and(used, j == 0))
    def _():
        @pl.when(t == 0)
        def _():
            _start_row_gather(h_hbm, idx0_ref, xbuf_ref.at[0], sem.at[0], tm, (0, 0))

        @pl.when(t + 1 < nu_ref[0])
        def _():
            _start_row_gather(h_hbm, idxn_ref, xbuf_ref.at[1 - slot], sem.at[1 - slot], tm, (0, 0))

        _wait_row_gather(h_hbm, xbuf_ref.at[slot], sem.at[slot], tm)

    @pl.when(jnp.logical_and(t == 0, j == 0))
    def _():
        acc_ref[...] = jnp.zeros_like(acc_ref)

    @pl.when(used)
    def _():
        x = xbuf_ref.at[slot]
        hb = jnp.concatenate([_load_row_tiles(x, lt, tm).astype(BF16) for lt in range(ROW_TILES)], axis=1)
        hg = jnp.dot(hb, wg_ref[...], preferred_element_type=F32)
        hu = jnp.dot(hb, wu_ref[...], preferred_element_type=F32)
        h1 = (hg * jax.nn.sigmoid(hg) * hu).astype(BF16)
        acc = jnp.where(j == 0, 0.0, acc_ref[...]) + jnp.dot(h1, wd_ref[...], preferred_element_type=F32)
        acc_ref[...] = acc
        _store_row_tiles(o_ref, acc)

    @pl.when(jnp.logical_and(jnp.logical_not(used), j == nj - 1))
    def _():
        o_ref[...] = jnp.zeros_like(o_ref)


def _moe_ffn(h, src, tile_expert, n_used, wg, wu, wd):
    p = src.shape[0]
    tm, tf = MOE_TM, MOE_TF
    nt = p // tm
    f = wg.shape[2]

    def jj(t, j, nu):
        return jnp.where(t < nu[0], j, 0)

    grid_spec = pltpu.PrefetchScalarGridSpec(
        num_scalar_prefetch=2,
        grid=(nt, f // tf),
        in_specs=[pl.BlockSpec((None, 1, tm), lambda t, j, te, nu: (0, 0, 0), memory_space=pltpu.SMEM),
                  pl.BlockSpec((None, 1, tm), lambda t, j, te, nu: (jnp.minimum(t + 1, nt - 1), 0, 0),
                               memory_space=pltpu.SMEM),
                  pl.BlockSpec(memory_space=pl.ANY),
                  pl.BlockSpec((None, D_MODEL, tf), lambda t, j, te, nu: (te[t], 0, jj(t, j, nu))),
                  pl.BlockSpec((None, D_MODEL, tf), lambda t, j, te, nu: (te[t], 0, jj(t, j, nu))),
                  pl.BlockSpec((None, tf, D_MODEL), lambda t, j, te, nu: (te[t], jj(t, j, nu), 0))],
        out_specs=pl.BlockSpec((tm * ROW_TILES, LANES), lambda t, j, te, nu: (t, 0)),
        scratch_shapes=[pltpu.VMEM((2, tm * ROW_TILES, LANES), F32), pltpu.VMEM((tm, D_MODEL), F32),
                        pltpu.SemaphoreType.DMA((2,))],
    )
    idx = src.reshape(nt, 1, tm)
    return pl.pallas_call(
        _moe_ffn_kernel,
        grid_spec=grid_spec,
        out_shape=jax.ShapeDtypeStruct((p * ROW_TILES, LANES), F32),
        compiler_params=_cparams("arbitrary", "arbitrary"),
        name="moe_ffn",
    )(tile_expert, n_used, idx, idx, h, wg, wu, wd)


def _combine_ln_kernel(p1a_ref, p2a_ref, p1b_ref, p2b_ref, ys_hbm, route_ref, x_ref, gate_ref, lng_ref, lnb_ref,
                       o_ref, y_ref, sem):
    i = pl.program_id(0)
    n = x_ref.shape[0]
    slot = i % 2

    def start(p1_ref, p2_ref, s):
        _start_row_gather(ys_hbm, p1_ref, y_ref.at[s, 0], sem.at[s, 0], n, (0, 1))
        _start_row_gather(ys_hbm, p2_ref, y_ref.at[s, 1], sem.at[s, 1], n, (0, 1))

    @pl.when(i == 0)
    def _():
        start(p1a_ref, p2a_ref, 0)

    @pl.when(i + 1 < pl.num_programs(0))
    def _():
        start(p1b_ref, p2b_ref, 1 - slot)

    for e in range(2):
        _wait_row_gather(ys_hbm, y_ref.at[slot, e], sem.at[slot, e], n)
    w1 = route_ref[:, 2:3]
    w2 = route_ref[:, 3:4]
    y = jnp.concatenate([w1 * _load_row_tiles(y_ref.at[slot, 0], lt, n) + w2 * _load_row_tiles(y_ref.at[slot, 1], lt, n)
                         for lt in range(ROW_TILES)], axis=1)
    z = ALPHA * x_ref[...] + gate_ref[0] * y
    o_ref[...] = _layer_norm(z, lng_ref[...], lnb_ref[...])


def _combine_ln(ys, pos1, pos2, route, x2d, vecs, ln_g, ln_b, rows_per_batch):
    r = x2d.shape[0]
    g = GATHER_ROWS
    ns = r // g
    first = pl.BlockSpec((None, 1, g), lambda i: (0, 0, 0), memory_space=pltpu.SMEM)
    ahead = pl.BlockSpec((None, 1, g), lambda i: (jnp.minimum(i + 1, ns - 1), 0, 0), memory_space=pltpu.SMEM)
    p1 = pos1.reshape(ns, 1, g)
    p2 = pos2.reshape(ns, 1, g)
    return pl.pallas_call(
        _combine_ln_kernel,
        grid=(ns,),
        in_specs=[first, first, ahead, ahead, pl.BlockSpec(memory_space=pl.ANY),
                  pl.BlockSpec((g, LANES), lambda i: (i, 0)),
                  pl.BlockSpec((g, D_MODEL), lambda i: (i, 0)),
                  _vec_spec(5, g, rows_per_batch), _full_spec((1, D_MODEL)), _full_spec((1, D_MODEL))],
        out_specs=pl.BlockSpec((g, D_MODEL), lambda i: (i, 0)),
        out_shape=jax.ShapeDtypeStruct((r, D_MODEL), F32),
        scratch_shapes=[pltpu.VMEM((2, 2, g * ROW_TILES, LANES), F32), pltpu.SemaphoreType.DMA((2, 2))],
        compiler_params=_cparams("arbitrary"),
        name="combine_ln",
    )(p1, p2, p1, p2, ys, route, x2d, vecs, ln_g.reshape(1, D_MODEL), ln_b.reshape(1, D_MODEL))


def _routing_plan(route, tm):
    n = route.shape[0]
    e = jnp.concatenate([route[:, 0], route[:, 1]]).astype(jnp.int32)
    onehot = (e[:, None] == jnp.arange(N_EXPERTS, dtype=jnp.int32)[None, :]).astype(jnp.int32)
    csum = jnp.cumsum(onehot, axis=0)
    rank = jnp.sum(csum * onehot, axis=1) - 1
    counts = csum[-1]
    padded = ((counts + tm - 1) // tm) * tm
    ends = jnp.cumsum(padded)
    starts = ends - padded
    pos = jnp.sum(starts[None, :] * onehot, axis=1) + rank
    p = 2 * n + N_EXPERTS * tm
    by_expert = jnp.argsort(e, stable=True).astype(jnp.int32)
    tok_sorted = jnp.pad(jnp.where(by_expert >= n, by_expert - n, by_expert), (N_EXPERTS * tm, p - 2 * n))
    shift = starts - (jnp.cumsum(counts) - counts)
    slot = jnp.arange(p, dtype=jnp.int32)
    src = jnp.zeros((p,), jnp.int32)
    for g in range(N_EXPERTS):
        cand = lax.dynamic_slice(tok_sorted, (N_EXPERTS * tm - shift[g],), (p,))
        src = jnp.where((slot >= starts[g]) & (slot < ends[g]), cand, src)
    tile_start = jnp.arange(p // tm, dtype=jnp.int32) * tm
    tile_expert = jnp.minimum(jnp.sum((tile_start[:, None] >= ends[None, :]).astype(jnp.int32), axis=1),
                              N_EXPERTS - 1).astype(jnp.int32)
    n_used = (ends[-1] // tm).astype(jnp.int32).reshape(1)
    return src, tile_expert, n_used, pos[:n], pos[n:]


def kernel(x, c, ctx, c_ctx, w_mod, b_mod, ln_g, ln_b, ab_w_in, ab_conv_w, ab_conv_g, ab_conv_b, ab_q_g, ab_k_g,
           ab_w_out, ffn_w_gate, ffn_w_up, ffn_w_down, na_w_qkv, na_rpb, na_w_out, moe_w_router, moe_w_gate,
           moe_w_up, moe_w_down):
    bsz, t, d = x.shape
    n_ctx = ctx.shape[1]
    n = bsz * t
    nc = bsz * n_ctx
    x2 = x.reshape(n, d)
    c2 = ctx.reshape(nc, d)

    cc = jnp.concatenate([c, c_ctx[None, :], jnp.zeros((8 - bsz - 1, d), F32)], axis=0)
    mod = _modulation(cc, w_mod, b_mod)
    vec0 = mod[0].reshape(8 * 6, 1, d)
    vec1 = mod[1].reshape(8 * 6, 1, d)

    w_in = ab_w_in[0].astype(BF16)
    w_out = ab_w_out[0].astype(BF16)
    pa, pq = _inproj0(x2, vec0, w_in, t, 1024)
    pac, pqc = _inproj0(c2, vec0, w_in, None, 512)
    a = _conformer_conv(pa, ab_conv_w[0], ab_conv_g[0], ab_conv_b[0], t, 512)
    ac = _conformer_conv(pac, ab_conv_w[0], ab_conv_g[0], ab_conv_b[0], n_ctx, n_ctx)
    cos_t, sin_t = _rope_tables(t)
    qt, k, vt = _prep_qkv(pq, cos_t, sin_t, ab_q_g[0], ab_k_g[0], bsz, t)
    ones = jnp.ones((HEAD_DIM, n_ctx), F32)
    qtc, kc, vtc = _prep_qkv(pqc, ones, jnp.zeros_like(ones), ab_q_g[0], ab_k_g[0], bsz, n_ctx)
    o = _gqa_attention(qt, jnp.concatenate([kc, k], axis=1), jnp.concatenate([vtc, vt], axis=2), t)
    oc = _gqa_attention(qtc, kc, vtc, n_ctx)
    x2 = _outproj_ln([a, o], w_out, x2, vec0, 2, ln_g[0, 0], ln_b[0, 0], t, 1024)
    c2 = _outproj_ln([ac, oc], w_out, c2, vec0, 2, ln_g[0, 0], ln_b[0, 0], None, 512)
    wg = ffn_w_gate[0].astype(BF16)
    wu = ffn_w_up[0].astype(BF16)
    wd = ffn_w_down[0].astype(BF16)
    x2 = _ffn_ln(x2, vec0, wg, wu, wd, ln_g[0, 1], ln_b[0, 1], t, 512)
    c2 = _ffn_ln(c2, vec0, wg, wu, wd, ln_g[0, 1], ln_b[0, 1], None, 512)

    w_qkv = na_w_qkv[0].astype(BF16)
    qt1, k1, vt1 = _inproj1(x2, vec1, w_qkv, t, 1024, True)
    kc1, vct1 = _inproj1(c2, vec1, w_qkv[:, d:], None, 512, False)
    o = _neighbourhood_attention(qt1, k1, vt1, kc1, vct1, _na_bias(na_rpb[0], t // GRID_W, n_ctx), bsz, t)
    x2 = _outproj_ln([o], na_w_out[0].astype(BF16), x2, vec1, 2, ln_g[1, 0], ln_b[1, 0], t, 1024)

    h, route = _router(x2, vec1, moe_w_router[0], t, 1024)
    src, tile_expert, n_used, pos1, pos2 = _routing_plan(route, MOE_TM)
    ys = _moe_ffn(h, src, tile_expert, n_used, moe_w_gate[0].astype(BF16), moe_w_up[0].astype(BF16),
                  moe_w_down[0].astype(BF16))
    x2 = _combine_ln(ys, pos1, pos2, route, x2, vec1, ln_g[1, 1], ln_b[1, 1], t)
    return x2.reshape(bsz, t, d)
```

```python
import functools

import numpy as np
import jax
import jax.numpy as jnp
from jax import lax
from jax.experimental import pallas as pl
from jax.experimental.pallas import tpu as pltpu

F32 = jnp.float32
BF16 = jnp.bfloat16

D_MODEL = 1024
GRID_W = 64
HEAD_DIM = 64
CONV_CH = 512
CONV_WIDTH = 31
CONV_HALO = 16
Q_COLS = 512
KV_COLS = 128
A_COLS = 2 * CONV_CH
ROPE_THETA = 10000.0
WIN_R = 8
WIN_C = 16
N_EXPERTS = 8
DEPTH = 2
ALPHA = (2 * DEPTH) ** 0.25
LN_EPS = 1e-5
RMS_EPS = 1e-6
ATT_SCALE = HEAD_DIM ** -0.5
LOG2E = 1.4426950408889634
MASK_VALUE = -1e30

LANES = 128
ROW_TILES = D_MODEL // LANES
VMEM_LIMIT = 56 * 1024 * 1024

ATT_TQ = 256
ATT_CW = 256
VT_ROWS = HEAD_DIM + 16
NA_ROWS = 4
NA_KROWS = 12
NA_PAIRS = 4
MOE_TM = 512
MOE_TF = 1792
GATHER_ROWS = 1024


def _cparams(*sem):
    return pltpu.CompilerParams(dimension_semantics=sem, vmem_limit_bytes=VMEM_LIMIT)


def _layer_norm(z, g, b):
    mu = jnp.mean(z, axis=-1, keepdims=True)
    zc = z - mu
    var = jnp.mean(zc * zc, axis=-1, keepdims=True)
    return zc * lax.rsqrt(var + LN_EPS) * g + b


def _vec_spec(k, tm, rows_per_batch):
    if rows_per_batch is None:
        return pl.BlockSpec((1, 1, D_MODEL), lambda i, *_: (4 * 6 + k, 0, 0))
    return pl.BlockSpec((1, 1, D_MODEL), lambda i, *_: ((i * tm // rows_per_batch) * 6 + k, 0, 0))


def _full_spec(shape):
    nd = len(shape)
    return pl.BlockSpec(shape, lambda *_: (0,) * nd)


def _mod_kernel(c_ref, w_ref, b_ref, o_ref):
    c = c_ref[...]
    s = c * jax.nn.sigmoid(c)
    o_ref[...] = jnp.dot(s.astype(BF16), w_ref[...].astype(BF16), preferred_element_type=F32) + b_ref[...]


def _modulation(cc, w_mod, b_mod):
    n = 6 * D_MODEL
    tn = D_MODEL
    return pl.pallas_call(
        _mod_kernel,
        grid=(DEPTH, n // tn),
        in_specs=[pl.BlockSpec((8, D_MODEL), lambda l, j: (0, 0)),
                  pl.BlockSpec((None, D_MODEL, tn), lambda l, j: (l, 0, j)),
                  pl.BlockSpec((None, 1, tn), lambda l, j: (l, 0, j))],
        out_specs=pl.BlockSpec((None, 8, tn), lambda l, j: (l, 0, j)),
        out_shape=jax.ShapeDtypeStruct((DEPTH, 8, n), F32),
        compiler_params=_cparams("parallel", "parallel"),
        name="modulation",
    )(cc, w_mod, b_mod.reshape(DEPTH, 1, n))


def _inproj0_kernel(x_ref, sc_ref, sh_ref, w_ref, oa_ref, oq_ref):
    h = x_ref[...] * (1.0 + sc_ref[0]) + sh_ref[0]
    o = jnp.dot(h.astype(BF16), w_ref[...], preferred_element_type=F32)
    oa_ref[...] = o[:, :A_COLS]
    oq_ref[...] = o[:, A_COLS:]


def _inproj0(x2d, vecs, w, rows_per_batch, tm):
    r = x2d.shape[0]
    nq = w.shape[1] - A_COLS
    return pl.pallas_call(
        _inproj0_kernel,
        grid=(r // tm,),
        in_specs=[pl.BlockSpec((tm, D_MODEL), lambda i: (i, 0)),
                  _vec_spec(1, tm, rows_per_batch), _vec_spec(0, tm, rows_per_batch),
                  _full_spec(w.shape)],
        out_specs=[pl.BlockSpec((tm, A_COLS), lambda i: (i, 0)),
                   pl.BlockSpec((tm, nq), lambda i: (i, 0))],
        out_shape=[jax.ShapeDtypeStruct((r, A_COLS), F32), jax.ShapeDtypeStruct((r, nq), F32)],
        compiler_params=_cparams("parallel"),
        name="inproj0",
    )(x2d, vecs, vecs, w)


def _conv_kernel(pm_ref, pp_ref, pn_ref, w_ref, g_ref, b_ref, o_ref, u_ref, us_ref, cv_ref, *, tt, tx):
    i = pl.program_id(0)

    def sigmoid(v):
        return 0.5 * jnp.tanh(0.5 * v) + 0.5

    def glu(p):
        return p[:, :CONV_CH] * sigmoid(p[:, CONV_CH:])

    first = (i * tt) % tx == 0
    last = ((i + 1) * tt) % tx == 0
    u_ref[0:CONV_HALO, :] = jnp.where(first, 0.0, glu(pp_ref[...]))
    u_ref[CONV_HALO:CONV_HALO + tt, :] = glu(pm_ref[...])
    u_ref[CONV_HALO + tt:2 * CONV_HALO + tt, :] = jnp.where(last, 0.0, glu(pn_ref[...]))
    u_ref[2 * CONV_HALO + tt:, :] = jnp.zeros((8, CONV_CH), F32)
    ch = 32
    nrow = tt + 2 * CONV_HALO
    base = CONV_HALO - CONV_WIDTH // 2

    def shift_body(c, carry):
        r0 = pl.multiple_of(c * ch, ch)
        w = u_ref[pl.ds(r0, ch + 8), :]
        for s in range(1, 8):
            us_ref[s - 1, pl.ds(r0, ch), :] = pltpu.roll(w, ch + 8 - s, axis=0)[0:ch]
        return carry

    lax.fori_loop(0, nrow // ch, shift_body, 0)

    def body(c, carry):
        r0 = pl.multiple_of(c * ch, ch)
        acc = jnp.zeros((ch, CONV_CH), F32)
        for k in range(CONV_WIDTH):
            a, s = divmod(k + base, 8)
            src = u_ref if s == 0 else us_ref.at[s - 1]
            acc = acc + src[pl.ds(r0 + 8 * a, ch), :] * w_ref[pl.ds(k, 1), :]
        cv_ref[pl.ds(r0, ch), :] = acc
        return carry

    lax.fori_loop(0, tt // ch, body, 0)
    y = _layer_norm(cv_ref[...], g_ref[...], b_ref[...])
    o_ref[...] = (y * sigmoid(y)).astype(BF16)


def _conformer_conv(pa, conv_w, conv_g, conv_b, tx, tt):
    r = pa.shape[0]
    hb = tt // CONV_HALO
    nhb = r // CONV_HALO
    return pl.pallas_call(
        functools.partial(_conv_kernel, tt=tt, tx=tx),
        grid=(r // tt,),
        in_specs=[pl.BlockSpec((tt, A_COLS), lambda i: (i, 0)),
                  pl.BlockSpec((CONV_HALO, A_COLS), lambda i: (jnp.maximum(i * hb - 1, 0), 0)),
                  pl.BlockSpec((CONV_HALO, A_COLS), lambda i: (jnp.minimum((i + 1) * hb, nhb - 1), 0)),
                  _full_spec((CONV_WIDTH, CONV_CH)), _full_spec((1, CONV_CH)), _full_spec((1, CONV_CH))],
        out_specs=pl.BlockSpec((tt, CONV_CH), lambda i: (i, 0)),
        out_shape=jax.ShapeDtypeStruct((r, CONV_CH), BF16),
        scratch_shapes=[pltpu.VMEM((tt + 2 * CONV_HALO + 8, CONV_CH), F32),
                        pltpu.VMEM((7, tt + 2 * CONV_HALO, CONV_CH), F32),
                        pltpu.VMEM((tt, CONV_CH), F32)],
        compiler_params=_cparams("parallel"),
        name="conformer_conv",
    )(pa, pa, pa, conv_w, conv_g.reshape(1, CONV_CH), conv_b.reshape(1, CONV_CH))


def _prep_kernel(p_ref, cos_ref, sin_ref, qg_ref, kg_ref, qt_ref, k_ref, vt_ref, *, tq):
    x = p_ref[...]
    cos = cos_ref[...]
    sin = sin_ref[...]

    def norm_rope(xh, g):
        ms = jnp.mean(xh * xh, axis=0, keepdims=True)
        y = xh * lax.rsqrt(ms + RMS_EPS) * g
        sw = jnp.concatenate([y[16:32], y[0:16], y[48:64], y[32:48]], axis=0)
        return y * cos + sw * sin

    zeros = jnp.zeros((HEAD_DIM, tq), BF16)
    for p in range(Q_COLS // LANES):
        xp = x[:, LANES * p:LANES * (p + 1)].T
        for half in range(2):
            h = 2 * p + half
            j, g = h // 4, h % 4
            r = (norm_rope(xp[HEAD_DIM * half:HEAD_DIM * (half + 1)], qg_ref[...]) * (ATT_SCALE * LOG2E)).astype(BF16)
            qt_ref[j, HEAD_DIM * j:HEAD_DIM * (j + 1), g * tq:(g + 1) * tq] = r
            qt_ref[j, HEAD_DIM * (1 - j):HEAD_DIM * (2 - j), g * tq:(g + 1) * tq] = zeros
    xk = x[:, Q_COLS:Q_COLS + KV_COLS].T
    k0 = norm_rope(xk[0:HEAD_DIM], kg_ref[...])
    k1 = norm_rope(xk[HEAD_DIM:2 * HEAD_DIM], kg_ref[...])
    k_ref[...] = jnp.concatenate([k0, k1], axis=0).T.astype(BF16)
    xv = x[:, Q_COLS + KV_COLS:].T.astype(BF16)
    ones = jnp.ones((VT_ROWS - HEAD_DIM, tq), BF16)
    for j in range(2):
        vt_ref[j, 0:HEAD_DIM, :] = xv[HEAD_DIM * j:HEAD_DIM * (j + 1)]
        vt_ref[j, HEAD_DIM:VT_ROWS, :] = ones


def _prep_qkv(pq, cos_t, sin_t, q_g, k_g, bx, tx):
    tq = ATT_TQ
    nq = tx // tq
    return pl.pallas_call(
        functools.partial(_prep_kernel, tq=tq),
        grid=(bx, nq),
        in_specs=[pl.BlockSpec((tq, Q_COLS + 2 * KV_COLS), lambda b, t: (b * nq + t, 0)),
                  pl.BlockSpec((HEAD_DIM, tq), lambda b, t: (0, t)),
                  pl.BlockSpec((HEAD_DIM, tq), lambda b, t: (0, t)),
                  _full_spec((HEAD_DIM, 1)), _full_spec((HEAD_DIM, 1))],
        out_specs=[pl.BlockSpec((None, None, 2, LANES, 4 * tq), lambda b, t: (b, t, 0, 0, 0)),
                   pl.BlockSpec((None, None, tq, LANES), lambda b, t: (b, t, 0, 0)),
                   pl.BlockSpec((None, 2, None, VT_ROWS, tq), lambda b, t: (b, 0, t, 0, 0))],
        out_shape=[jax.ShapeDtypeStruct((bx, nq, 2, LANES, 4 * tq), BF16),
                   jax.ShapeDtypeStruct((bx, nq, tq, LANES), BF16),
                   jax.ShapeDtypeStruct((bx, 2, nq, VT_ROWS, tq), BF16)],
        compiler_params=_cparams("parallel", "parallel"),
        name="prep_qkv",
    )(pq, cos_t, sin_t, q_g.reshape(HEAD_DIM, 1), k_g.reshape(HEAD_DIM, 1))


def _rope_tables(t):
    pos = np.arange(t)
    half = HEAD_DIM // 4
    inv = ROPE_THETA ** (-jnp.arange(half, dtype=F32) * 2.0 / (HEAD_DIM // 2))
    dd = np.arange(HEAD_DIM)
    part_pos = np.where((dd // (HEAD_DIM // 2))[:, None] == 0, (pos // GRID_W)[None, :], (pos % GRID_W)[None, :])
    ang = jnp.asarray(part_pos, F32) * inv[dd % half][:, None]
    sign = jnp.asarray(np.where((dd % (HEAD_DIM // 2)) < half, -1.0, 1.0)[:, None], F32)
    return jnp.cos(ang), jnp.sin(ang) * sign


def _attn_kernel(qt_ref, k_ref, vt_ref, o_ref, acc_ref, *, nk, tq):
    ngrp = qt_ref.shape[0]
    acc_ref[...] = jnp.zeros_like(acc_ref)
    cw = ATT_CW
    strips = [(j, n) for j in range(ngrp) for n in range(4 * tq // cw)]

    def scores(c, j, n):
        return jnp.dot(k_ref[c], qt_ref[j, :, cw * n:cw * (n + 1)], preferred_element_type=F32)

    s = [scores(0, j, n) for j, n in strips]
    m_prev = [jnp.full((1, cw), -jnp.inf, F32)] * len(strips)
    m = [jnp.max(sn, axis=0, keepdims=True) for sn in s]
    for c in range(nk):
        for i, (j, n) in enumerate(strips):
            s_next = scores(c + 1, j, n) if c + 1 < nk else None
            alpha = jnp.exp2(m_prev[i] - m[i])
            p = jnp.exp2((s[i] - m[i]).astype(BF16))
            pv = jnp.dot(vt_ref[j, c], p, preferred_element_type=F32)
            acc_ref[j, :, cw * n:cw * (n + 1)] = acc_ref[j, :, cw * n:cw * (n + 1)] * alpha + pv
            if s_next is not None:
                m_prev[i] = m[i]
                m[i] = jnp.maximum(m[i], jnp.max(s_next, axis=0, keepdims=True))
                s[i] = s_next
    for j in range(ngrp):
        o = acc_ref[j, 0:HEAD_DIM, :] * (1.0 / acc_ref[j, HEAD_DIM:HEAD_DIM + 1, :])
        for pp in range(2):
            blk = jnp.concatenate([o[:, (2 * pp) * tq:(2 * pp + 1) * tq],
                                   o[:, (2 * pp + 1) * tq:(2 * pp + 2) * tq]], axis=0)
            lane0 = LANES * (2 * j + pp)
            o_ref[:, lane0:lane0 + LANES] = blk.T.astype(BF16)


def _gqa_attention(qt, k, vt, tx):
    bx, nq = qt.shape[0], qt.shape[1]
    nk = k.shape[1]
    tq = ATT_TQ
    return pl.pallas_call(
        functools.partial(_attn_kernel, nk=nk, tq=tq),
        grid=(bx, nq),
        in_specs=[pl.BlockSpec((None, None, 2, LANES, 4 * tq), lambda b, t: (b, t, 0, 0, 0)),
                  pl.BlockSpec((None, nk, tq, LANES), lambda b, t: (b, 0, 0, 0)),
                  pl.BlockSpec((None, 2, nk, VT_ROWS, tq), lambda b, t: (b, 0, 0, 0, 0))],
        out_specs=pl.BlockSpec((tq, Q_COLS), lambda b, t: (b * nq + t, 0)),
        out_shape=jax.ShapeDtypeStruct((bx * tx, Q_COLS), BF16),
        scratch_shapes=[pltpu.VMEM((2, VT_ROWS, 4 * tq), F32)],
        compiler_params=_cparams("parallel", "parallel"),
        name="gqa_attention",
    )(qt, k, vt)


def _outproj_ln_kernel(*refs, n_lhs):
    lhs = refs[:n_lhs]
    w_ref, x_ref, gate_ref, lng_ref, lnb_ref, o_ref = refs[n_lhs:]
    y = None
    off = 0
    for r in lhs:
        kk = r.shape[1]
        t = jnp.dot(r[...], w_ref[off:off + kk, :], preferred_element_type=F32)
        off += kk
        y = t if y is None else y + t
    z = ALPHA * x_ref[...] + gate_ref[0] * y
    o_ref[...] = _layer_norm(z, lng_ref[...], lnb_ref[...])


def _outproj_ln(lhs, w, x2d, vecs, gate_k, ln_g, ln_b, rows_per_batch, tm):
    r = x2d.shape[0]
    return pl.pallas_call(
        functools.partial(_outproj_ln_kernel, n_lhs=len(lhs)),
        grid=(r // tm,),
        in_specs=[pl.BlockSpec((tm, a.shape[1]), lambda i: (i, 0)) for a in lhs]
        + [_full_spec(w.shape), pl.BlockSpec((tm, D_MODEL), lambda i: (i, 0)),
           _vec_spec(gate_k, tm, rows_per_batch), _full_spec((1, D_MODEL)), _full_spec((1, D_MODEL))],
        out_specs=pl.BlockSpec((tm, D_MODEL), lambda i: (i, 0)),
        out_shape=jax.ShapeDtypeStruct((r, D_MODEL), F32),
        compiler_params=_cparams("parallel"),
        name="outproj_ln",
    )(*lhs, w, x2d, vecs, ln_g.reshape(1, D_MODEL), ln_b.reshape(1, D_MODEL))


def _ffn_kernel(x_ref, sc_ref, sh_ref, gate_ref, wg_ref, wu_ref, wd_ref, lng_ref, lnb_ref, o_ref):
    x = x_ref[...]
    hb = (x * (1.0 + sc_ref[0]) + sh_ref[0]).astype(BF16)
    hg = jnp.dot(hb, wg_ref[...], preferred_element_type=F32)
    hu = jnp.dot(hb, wu_ref[...], preferred_element_type=F32)
    h1 = (hg * jax.nn.sigmoid(hg) * hu).astype(BF16)
    y = jnp.dot(h1, wd_ref[...], preferred_element_type=F32)
    o_ref[...] = _layer_norm(ALPHA * x + gate_ref[0] * y, lng_ref[...], lnb_ref[...])


def _ffn_ln(x2d, vecs, wg, wu, wd, ln_g, ln_b, rows_per_batch, tm):
    r = x2d.shape[0]

    def resident(shape):
        return pl.BlockSpec(shape, lambda i: (0, 0), pipeline_mode=pl.Buffered(1))

    return pl.pallas_call(
        _ffn_kernel,
        grid=(r // tm,),
        in_specs=[pl.BlockSpec((tm, D_MODEL), lambda i: (i, 0)),
                  _vec_spec(4, tm, rows_per_batch), _vec_spec(3, tm, rows_per_batch),
                  _vec_spec(5, tm, rows_per_batch),
                  resident(wg.shape), resident(wu.shape), resident(wd.shape),
                  _full_spec((1, D_MODEL)), _full_spec((1, D_MODEL))],
        out_specs=pl.BlockSpec((tm, D_MODEL), lambda i: (i, 0)),
        out_shape=jax.ShapeDtypeStruct((r, D_MODEL), F32),
        compiler_params=_cparams("parallel"),
        name="ffn_ln",
    )(x2d, vecs, vecs, vecs, wg, wu, wd, ln_g.reshape(1, D_MODEL), ln_b.reshape(1, D_MODEL))


def _inproj1_kernel(x_ref, sc_ref, sh_ref, w_ref, *o_refs, with_q):
    h = x_ref[...] * (1.0 + sc_ref[0]) + sh_ref[0]
    o = jnp.dot(h.astype(BF16), w_ref[...], preferred_element_type=F32)
    if with_q:
        qt_ref, k_ref, vt_ref = o_refs
        qt_ref[...] = (o[:, :D_MODEL] * (ATT_SCALE * LOG2E)).T.astype(BF16)
    else:
        k_ref, vt_ref = o_refs
    nk = o.shape[1] - 2 * D_MODEL
    k_ref[...] = o[:, nk:nk + D_MODEL].astype(BF16)
    vt_ref[...] = o[:, nk + D_MODEL:].T.astype(BF16)


def _inproj1(x2d, vecs, w, rows_per_batch, tm, with_q):
    r = x2d.shape[0]
    nat = pl.BlockSpec((tm, D_MODEL), lambda i: (i, 0))
    tr = pl.BlockSpec((D_MODEL, tm), lambda i: (0, i))
    nat_shape = jax.ShapeDtypeStruct((r, D_MODEL), BF16)
    tr_shape = jax.ShapeDtypeStruct((D_MODEL, r), BF16)
    return pl.pallas_call(
        functools.partial(_inproj1_kernel, with_q=with_q),
        grid=(r // tm,),
        in_specs=[pl.BlockSpec((tm, D_MODEL), lambda i: (i, 0)),
                  _vec_spec(1, tm, rows_per_batch), _vec_spec(0, tm, rows_per_batch),
                  _full_spec(w.shape)],
        out_specs=([tr] if with_q else []) + [nat, tr],
        out_shape=([tr_shape] if with_q else []) + [nat_shape, tr_shape],
        compiler_params=_cparams("parallel"),
        name="inproj1",
    )(x2d, vecs, vecs, w)


def _na_bias(rpb, n_rows, n_ctx):
    nh = rpb.shape[0]
    c = np.arange(GRID_W)
    cs = np.clip(c - WIN_C // 2, 0, GRID_W - WIN_C)
    in_c = (c[None, :] >= cs[:, None]) & (c[None, :] < cs[:, None] + WIN_C)
    dc = c[None, :] - c[:, None] + WIN_C - 1
    pick = ((dc[None] == np.arange(2 * WIN_C - 1)[:, None, None]) & in_c[None]).astype(np.float32)
    cols = jnp.einsum("hrd,dkc->hrkc", rpb, jnp.asarray(pick.transpose(0, 2, 1)), precision=lax.Precision.HIGHEST)
    cols = jnp.where(jnp.asarray(in_c.T)[None, None], cols * LOG2E, MASK_VALUE)
    n_dr = 2 * WIN_R - 1
    cols = jnp.concatenate([cols, jnp.full((nh, 1, GRID_W, GRID_W), MASK_VALUE, F32)], axis=1)
    pick_dr = np.full((3, NA_ROWS, NA_KROWS), n_dr, np.int32)
    for case, r0 in enumerate((0, NA_ROWS, n_rows - NA_ROWS)):
        start = int(np.clip(r0 - WIN_R // 2, 0, n_rows - NA_KROWS))
        for ri in range(NA_ROWS):
            r = r0 + ri
            rs = int(np.clip(r - WIN_R // 2, 0, n_rows - WIN_R))
            for ki in range(NA_KROWS):
                kr = start + ki
                if rs <= kr < rs + WIN_R:
                    pick_dr[case, ri, ki] = kr - r + WIN_R - 1
    npair = nh // 2
    nkeys = NA_KROWS * GRID_W
    out = pl.pallas_call(
        functools.partial(_na_bias_kernel, nkeys=nkeys),
        grid_spec=pltpu.PrefetchScalarGridSpec(
            num_scalar_prefetch=1,
            grid=(3, npair),
            in_specs=[pl.BlockSpec((2, n_dr + 1, GRID_W, GRID_W), lambda case, p, dr: (p, 0, 0, 0))],
            out_specs=pl.BlockSpec((None, None, nkeys + n_ctx, 2 * NA_ROWS * GRID_W),
                                   lambda case, p, dr: (case, p, 0, 0)),
        ),
        out_shape=jax.ShapeDtypeStruct((3, npair, nkeys + n_ctx, 2 * NA_ROWS * GRID_W), F32),
        compiler_params=_cparams("parallel", "parallel"),
        name="na_bias",
    )(jnp.asarray(pick_dr.reshape(-1)), cols)
    return out.reshape(3, npair // NA_PAIRS, NA_PAIRS, nkeys + n_ctx, 2 * NA_ROWS * GRID_W)


def _na_bias_kernel(dr_ref, cols_ref, o_ref, *, nkeys):
    case = pl.program_id(0)
    for ki in range(NA_KROWS):
        pieces = []
        for e in range(2):
            for ri in range(NA_ROWS):
                d = dr_ref[(case * NA_ROWS + ri) * NA_KROWS + ki]
                pieces.append(cols_ref[e, d])
        o_ref[GRID_W * ki:GRID_W * (ki + 1), :] = jnp.concatenate(pieces, axis=1)
    o_ref[nkeys:, :] = jnp.zeros((o_ref.shape[0] - nkeys, o_ref.shape[1]), F32)


def _na_kernel(qt_ref, k0_ref, k1_ref, k2_ref, kc_ref, v0_ref, v1_ref, v2_ref, vc_ref, bias_ref, o_ref):
    qt = qt_ref[...]
    kk = jnp.concatenate([k0_ref[...], k1_ref[...], k2_ref[...], kc_ref[...]], axis=0)
    vt = jnp.concatenate([v0_ref[...], v1_ref[...], v2_ref[...], vc_ref[...]], axis=1)
    nq = qt.shape[1]
    zeros = jnp.zeros((HEAD_DIM, nq), BF16)
    ones = jnp.ones((VT_ROWS - HEAD_DIM, kk.shape[0]), BF16)

    def scores(pp):
        q2 = qt[LANES * pp:LANES * (pp + 1)]
        qcat = jnp.concatenate([jnp.concatenate([q2[0:HEAD_DIM], zeros], axis=0),
                                jnp.concatenate([zeros, q2[HEAD_DIM:]], axis=0)], axis=1)
        return jnp.dot(kk[:, LANES * pp:LANES * (pp + 1)], qcat, preferred_element_type=F32) + bias_ref[pp]

    s_all = [scores(pp) for pp in range(NA_PAIRS)]
    for pp in range(NA_PAIRS):
        s = s_all[pp]
        m = jnp.max(s, axis=0, keepdims=True)
        p = jnp.exp2((s - m).astype(BF16))
        vext = jnp.concatenate([vt[LANES * pp:LANES * (pp + 1)], ones], axis=0)
        pv = jnp.dot(vext, p, preferred_element_type=F32)
        o0 = pv[0:HEAD_DIM, 0:nq] * (1.0 / pv[LANES:LANES + 1, 0:nq])
        o1 = pv[HEAD_DIM:LANES, nq:] * (1.0 / pv[LANES:LANES + 1, nq:])
        o_ref[:, LANES * pp:LANES * (pp + 1)] = jnp.concatenate([o0, o1], axis=0).T.astype(BF16)


def _neighbourhood_attention(qt, k, vt, kc, vct, bias, bsz, t):
    nq = NA_ROWS * GRID_W
    nrb = t // nq
    wl = NA_PAIRS * LANES
    ngrp = D_MODEL // wl
    nctx = kc.shape[0] // bsz
    nwin = NA_KROWS // NA_ROWS

    def first_kblock(b, rb):
        return b * nrb + jnp.clip(rb - 1, 0, nrb - nwin)

    def kspec(d):
        return pl.BlockSpec((nq, wl), lambda hp, b, rb: (first_kblock(b, rb) + d, hp))

    def vspec(d):
        return pl.BlockSpec((wl, nq), lambda hp, b, rb: (hp, first_kblock(b, rb) + d))

    def case(rb):
        return jnp.where(rb == 0, 0, jnp.where(rb == nrb - 1, 2, 1))

    return pl.pallas_call(
        _na_kernel,
        grid=(ngrp, bsz, nrb),
        in_specs=[pl.BlockSpec((wl, nq), lambda hp, b, rb: (hp, b * nrb + rb)),
                  kspec(0), kspec(1), kspec(2), pl.BlockSpec((nctx, wl), lambda hp, b, rb: (b, hp)),
                  vspec(0), vspec(1), vspec(2), pl.BlockSpec((wl, nctx), lambda hp, b, rb: (hp, b)),
                  pl.BlockSpec((None, None, NA_PAIRS, NA_KROWS * GRID_W + nctx, 2 * nq),
                               lambda hp, b, rb: (case(rb), hp, 0, 0, 0))],
        out_specs=pl.BlockSpec((nq, wl), lambda hp, b, rb: (b * nrb + rb, hp)),
        out_shape=jax.ShapeDtypeStruct((bsz * t, D_MODEL), BF16),
        compiler_params=_cparams("parallel", "parallel", "parallel"),
        name="neighbourhood_attention",
    )(qt, k, k, k, kc, vt, vt, vt, vct, bias)


def _store_row_tiles(dst_ref, val):
    rows = val.shape[0]
    for lt in range(ROW_TILES):
        dst_ref[pl.ds(lt, rows, stride=ROW_TILES), :] = val[:, lt * LANES:(lt + 1) * LANES]


def _load_row_tiles(src_ref, lt, rows):
    return src_ref[pl.ds(lt, rows, stride=ROW_TILES), :]


def _router_kernel(x_ref, sc_ref, sh_ref, wr_ref, h_ref, r_ref):
    h = x_ref[...] * (1.0 + sc_ref[0]) + sh_ref[0]
    _store_row_tiles(h_ref, h)
    hi = h.astype(BF16)
    lo = (h - hi.astype(F32)).astype(BF16)
    w = wr_ref[...]
    whi = w.astype(BF16)
    wlo = (w - whi.astype(F32)).astype(BF16)
    lg = (jnp.dot(hi, whi, preferred_element_type=F32)
          + (jnp.dot(hi, wlo, preferred_element_type=F32) + jnp.dot(lo, whi, preferred_element_type=F32)))
    lane = lax.broadcasted_iota(jnp.int32, lg.shape, 1).astype(F32)
    lg = jnp.where(lane < N_EXPERTS, lg, -jnp.inf)
    v1 = jnp.max(lg, axis=1, keepdims=True)
    i1 = jnp.min(jnp.where(lg == v1, lane, float(LANES)), axis=1, keepdims=True)
    lg2 = jnp.where(lane == i1, -jnp.inf, lg)
    v2 = jnp.max(lg2, axis=1, keepdims=True)
    i2 = jnp.min(jnp.where(lg2 == v2, lane, float(LANES)), axis=1, keepdims=True)
    e = jnp.exp(v2 - v1)
    w1 = 1.0 / (1.0 + e)
    w2 = e / (1.0 + e)
    r_ref[...] = jnp.where(lane == 0, i1, jnp.where(lane == 1, i2, jnp.where(lane == 2, w1,
                                                                              jnp.where(lane == 3, w2, 0.0))))


def _router(x2d, vecs, w_router, rows_per_batch, tm):
    r = x2d.shape[0]
    wr = jnp.pad(w_router, ((0, 0), (0, LANES - N_EXPERTS)))
    return pl.pallas_call(
        _router_kernel,
        grid=(r // tm,),
        in_specs=[pl.BlockSpec((tm, D_MODEL), lambda i: (i, 0)),
                  _vec_spec(4, tm, rows_per_batch), _vec_spec(3, tm, rows_per_batch),
                  _full_spec((D_MODEL, LANES))],
        out_specs=[pl.BlockSpec((tm * ROW_TILES, LANES), lambda i: (i, 0)),
                   pl.BlockSpec((tm, LANES), lambda i: (i, 0))],
        out_shape=[jax.ShapeDtypeStruct((r * ROW_TILES, LANES), F32), jax.ShapeDtypeStruct((r, LANES), F32)],
        compiler_params=_cparams("parallel"),
        name="router",
    )(x2d, vecs, vecs, wr)


def _row_copy(src_hbm, first_sublane, r, dst_ref, sem):
    return pltpu.make_async_copy(src_hbm.at[pl.ds(pl.multiple_of(first_sublane, ROW_TILES), ROW_TILES), :],
                                 dst_ref.at[pl.ds(pl.multiple_of(r * ROW_TILES, ROW_TILES), ROW_TILES), :], sem)


def _start_row_gather(src_hbm, idx_ref, dst_ref, sem, n, priorities):
    def issue(i, carry):
        for u in range(2):
            r = 2 * i + u
            _row_copy(src_hbm, idx_ref[0, r], r, dst_ref, sem).start(priority=priorities[u])
        return carry

    lax.fori_loop(0, n // 2, issue, 0, unroll=4)


def _wait_row_gather(src_hbm, dst_ref, sem, n):
    pltpu.make_async_copy(src_hbm.at[pl.ds(0, n * ROW_TILES), :], dst_ref, sem).wait()


def _moe_ffn_kernel(te_ref, nu_ref, nv_ref, idx0_ref, idxn_ref, h_hbm, wg_ref, wu_ref, wd_ref, o_ref,
                    xbuf_ref, acc_ref, sem):
    t = pl.program_id(0)
    j = pl.program_id(1)
    nj = pl.num_programs(1)
    used = t < nu_ref[0]
    tm = acc_ref.shape[0]
    half = tm // 2
    slot = t % 2

    @pl.when(jnp.logical_and(used, j == 0))
    def _():
        @pl.when(t == 0)
        def _():
            _start_row_gather(h_hbm, idx0_ref, xbuf_ref.at[0], sem.at[0], tm, (0, 0))

        @pl.when(t + 1 < nu_ref[0])
        def _():
            _start_row_gather(h_hbm, idxn_ref, xbuf_ref.at[1 - slot], sem.at[1 - slot], tm, (0, 0))

        _wait_row_gather(h_hbm, xbuf_ref.at[slot], sem.at[slot], tm)

    @pl.when(jnp.logical_and(t == 0, j == 0))
    def _():
        acc_ref[...] = jnp.zeros_like(acc_ref)

    def compute(rows):
        x = xbuf_ref.at[slot]
        hb = jnp.concatenate([_load_row_tiles(x, lt, rows).astype(BF16) for lt in range(ROW_TILES)], axis=1)
        hg = jnp.dot(hb, wg_ref[...], preferred_element_type=F32)
        hu = jnp.dot(hb, wu_ref[...], preferred_element_type=F32)
        h1 = (hg * jax.nn.sigmoid(hg) * hu).astype(BF16)
        acc = jnp.where(j == 0, 0.0, acc_ref[0:rows, :]) + jnp.dot(h1, wd_ref[...], preferred_element_type=F32)
        acc_ref[0:rows, :] = acc
        _store_row_tiles(o_ref, acc)
        if rows < tm:
            o_ref[rows * ROW_TILES:, :] = jnp.zeros(((tm - rows) * ROW_TILES, LANES), F32)

    @pl.when(nv_ref[t] > half)
    def _():
        compute(tm)

    @pl.when(jnp.logical_and(used, nv_ref[t] <= half))
    def _():
        compute(half)

    @pl.when(jnp.logical_and(jnp.logical_not(used), j == nj - 1))
    def _():
        o_ref[...] = jnp.zeros_like(o_ref)


def _moe_ffn(h, src, tile_expert, n_used, tile_rows, wg, wu, wd):
    p = src.shape[0]
    tm, tf = MOE_TM, MOE_TF
    nt = p // tm
    f = wg.shape[2]

    def jj(t, j, nu):
        return jnp.where(t < nu[0], j, 0)

    grid_spec = pltpu.PrefetchScalarGridSpec(
        num_scalar_prefetch=3,
        grid=(nt, f // tf),
        in_specs=[pl.BlockSpec((None, 1, tm), lambda t, j, te, nu, nv: (0, 0, 0), memory_space=pltpu.SMEM),
                  pl.BlockSpec((None, 1, tm), lambda t, j, te, nu, nv: (jnp.minimum(t + 1, nt - 1), 0, 0),
                               memory_space=pltpu.SMEM),
                  pl.BlockSpec(memory_space=pl.ANY),
                  pl.BlockSpec((None, D_MODEL, tf), lambda t, j, te, nu, nv: (te[t], 0, jj(t, j, nu))),
                  pl.BlockSpec((None, D_MODEL, tf), lambda t, j, te, nu, nv: (te[t], 0, jj(t, j, nu))),
                  pl.BlockSpec((None, tf, D_MODEL), lambda t, j, te, nu, nv: (te[t], jj(t, j, nu), 0))],
        out_specs=pl.BlockSpec((tm * ROW_TILES, LANES), lambda t, j, te, nu, nv: (t, 0)),
        scratch_shapes=[pltpu.VMEM((2, tm * ROW_TILES, LANES), F32), pltpu.VMEM((tm, D_MODEL), F32),
                        pltpu.SemaphoreType.DMA((2,))],
    )
    idx = src.reshape(nt, 1, tm)
    return pl.pallas_call(
        _moe_ffn_kernel,
        grid_spec=grid_spec,
        out_shape=jax.ShapeDtypeStruct((p * ROW_TILES, LANES), F32),
        compiler_params=_cparams("arbitrary", "arbitrary"),
        name="moe_ffn",
    )(tile_expert, n_used, tile_rows, idx, idx, h, wg, wu, wd)


def _combine_ln_kernel(p1a_ref, p2a_ref, p1b_ref, p2b_ref, ys_hbm, route_ref, x_ref, gate_ref, lng_ref, lnb_ref,
                       o_ref, y_ref, sem):
    i = pl.program_id(0)
    n = x_ref.shape[0]
    slot = i % 2

    def start(p1_ref, p2_ref, s):
        _start_row_gather(ys_hbm, p1_ref, y_ref.at[s, 0], sem.at[s, 0], n, (0, 1))
        _start_row_gather(ys_hbm, p2_ref, y_ref.at[s, 1], sem.at[s, 1], n, (0, 1))

    @pl.when(i == 0)
    def _():
        start(p1a_ref, p2a_ref, 0)

    @pl.when(i + 1 < pl.num_programs(0))
    def _():
        start(p1b_ref, p2b_ref, 1 - slot)

    for e in range(2):
        _wait_row_gather(ys_hbm, y_ref.at[slot, e], sem.at[slot, e], n)
    w1 = route_ref[:, 2:3]
    w2 = route_ref[:, 3:4]
    y = jnp.concatenate([w1 * _load_row_tiles(y_ref.at[slot, 0], lt, n) + w2 * _load_row_tiles(y_ref.at[slot, 1], lt, n)
                         for lt in range(ROW_TILES)], axis=1)
    z = ALPHA * x_ref[...] + gate_ref[0] * y
    o_ref[...] = _layer_norm(z, lng_ref[...], lnb_ref[...])


def _combine_ln(ys, pos1, pos2, route, x2d, vecs, ln_g, ln_b, rows_per_batch):
    r = x2d.shape[0]
    g = GATHER_ROWS
    ns = r // g
    first = pl.BlockSpec((None, 1, g), lambda i: (0, 0, 0), memory_space=pltpu.SMEM)
    ahead = pl.BlockSpec((None, 1, g), lambda i: (jnp.minimum(i + 1, ns - 1), 0, 0), memory_space=pltpu.SMEM)
    p1 = pos1.reshape(ns, 1, g)
    p2 = pos2.reshape(ns, 1, g)
    return pl.pallas_call(
        _combine_ln_kernel,
        grid=(ns,),
        in_specs=[first, first, ahead, ahead, pl.BlockSpec(memory_space=pl.ANY),
                  pl.BlockSpec((g, LANES), lambda i: (i, 0)),
                  pl.BlockSpec((g, D_MODEL), lambda i: (i, 0)),
                  _vec_spec(5, g, rows_per_batch), _full_spec((1, D_MODEL)), _full_spec((1, D_MODEL))],
        out_specs=pl.BlockSpec((g, D_MODEL), lambda i: (i, 0)),
        out_shape=jax.ShapeDtypeStruct((r, D_MODEL), F32),
        scratch_shapes=[pltpu.VMEM((2, 2, g * ROW_TILES, LANES), F32), pltpu.SemaphoreType.DMA((2, 2))],
        compiler_params=_cparams("arbitrary"),
        name="combine_ln",
    )(p1, p2, p1, p2, ys, route, x2d, vecs, ln_g.reshape(1, D_MODEL), ln_b.reshape(1, D_MODEL))


def _routing_plan(route, tm):
    n = route.shape[0]
    e = jnp.concatenate([route[:, 0], route[:, 1]]).astype(jnp.int32)
    onehot = (e[:, None] == jnp.arange(N_EXPERTS, dtype=jnp.int32)[None, :]).astype(jnp.int32)
    csum = jnp.cumsum(onehot, axis=0)
    rank = jnp.sum(csum * onehot, axis=1) - 1
    counts = csum[-1]
    padded = ((counts + tm - 1) // tm) * tm
    ends = jnp.cumsum(padded)
    starts = ends - padded
    pos = jnp.sum(starts[None, :] * onehot, axis=1) + rank
    p = 2 * n + N_EXPERTS * tm
    by_expert = jnp.argsort(e, stable=True).astype(jnp.int32)
    tok_sorted = jnp.pad(jnp.where(by_expert >= n, by_expert - n, by_expert), (N_EXPERTS * tm, p - 2 * n))
    shift = starts - (jnp.cumsum(counts) - counts)
    slot = jnp.arange(p, dtype=jnp.int32)
    src = jnp.zeros((p,), jnp.int32)
    for g in range(N_EXPERTS):
        cand = lax.dynamic_slice(tok_sorted, (N_EXPERTS * tm - shift[g],), (p,))
        src = jnp.where((slot >= starts[g]) & (slot < ends[g]), cand, src)
    tile_start = jnp.arange(p // tm, dtype=jnp.int32) * tm
    tile_expert = jnp.minimum(jnp.sum((tile_start[:, None] >= ends[None, :]).astype(jnp.int32), axis=1),
                              N_EXPERTS - 1).astype(jnp.int32)
    n_used = (ends[-1] // tm).astype(jnp.int32).reshape(1)
    group_end = jnp.sum(jnp.where(tile_expert[:, None] == jnp.arange(N_EXPERTS)[None, :], (starts + counts)[None, :], 0),
                        axis=1)
    tile_rows = jnp.clip(group_end - tile_start, 0, tm).astype(jnp.int32)
    return src * ROW_TILES, tile_expert, n_used, tile_rows, pos[:n] * ROW_TILES, pos[n:] * ROW_TILES


def kernel(x, c, ctx, c_ctx, w_mod, b_mod, ln_g, ln_b, ab_w_in, ab_conv_w, ab_conv_g, ab_conv_b, ab_q_g, ab_k_g,
           ab_w_out, ffn_w_gate, ffn_w_up, ffn_w_down, na_w_qkv, na_rpb, na_w_out, moe_w_router, moe_w_gate,
           moe_w_up, moe_w_down):
    bsz, t, d = x.shape
    n_ctx = ctx.shape[1]
    n = bsz * t
    nc = bsz * n_ctx
    x2 = x.reshape(n, d)
    c2 = ctx.reshape(nc, d)

    cc = jnp.concatenate([c, c_ctx[None, :], jnp.zeros((8 - bsz - 1, d), F32)], axis=0)
    mod = _modulation(cc, w_mod, b_mod)
    vec0 = mod[0].reshape(8 * 6, 1, d)
    vec1 = mod[1].reshape(8 * 6, 1, d)

    w_in = ab_w_in[0].astype(BF16)
    w_out = ab_w_out[0].astype(BF16)
    pa, pq = _inproj0(x2, vec0, w_in, t, 1024)
    pac, pqc = _inproj0(c2, vec0, w_in, None, 512)
    a = _conformer_conv(pa, ab_conv_w[0], ab_conv_g[0], ab_conv_b[0], t, 512)
    ac = _conformer_conv(pac, ab_conv_w[0], ab_conv_g[0], ab_conv_b[0], n_ctx, n_ctx)
    cos_t, sin_t = _rope_tables(t)
    qt, k, vt = _prep_qkv(pq, cos_t, sin_t, ab_q_g[0], ab_k_g[0], bsz, t)
    ones = jnp.ones((HEAD_DIM, n_ctx), F32)
    qtc, kc, vtc = _prep_qkv(pqc, ones, jnp.zeros_like(ones), ab_q_g[0], ab_k_g[0], bsz, n_ctx)
    o = _gqa_attention(qt, jnp.concatenate([kc, k], axis=1), jnp.concatenate([vtc, vt], axis=2), t)
    oc = _gqa_attention(qtc, kc, vtc, n_ctx)
    x2 = _outproj_ln([a, o], w_out, x2, vec0, 2, ln_g[0, 0], ln_b[0, 0], t, 1024)
    c2 = _outproj_ln([ac, oc], w_out, c2, vec0, 2, ln_g[0, 0], ln_b[0, 0], None, 512)
    wg = ffn_w_gate[0].astype(BF16)
    wu = ffn_w_up[0].astype(BF16)
    wd = ffn_w_down[0].astype(BF16)
    x2 = _ffn_ln(x2, vec0, wg, wu, wd, ln_g[0, 1], ln_b[0, 1], t, 512)
    c2 = _ffn_ln(c2, vec0, wg, wu, wd, ln_g[0, 1], ln_b[0, 1], None, 512)

    w_qkv = na_w_qkv[0].astype(BF16)
    qt1, k1, vt1 = _inproj1(x2, vec1, w_qkv, t, 1024, True)
    kc1, vct1 = _inproj1(c2, vec1, w_qkv[:, d:], None, 512, False)
    o = _neighbourhood_attention(qt1, k1, vt1, kc1, vct1, _na_bias(na_rpb[0], t // GRID_W, n_ctx), bsz, t)
    x2 = _outproj_ln([o], na_w_out[0].astype(BF16), x2, vec1, 2, ln_g[1, 0], ln_b[1, 0], t, 1024)

    h, route = _router(x2, vec1, moe_w_router[0], t, 1024)
    src, tile_expert, n_used, tile_rows, pos1, pos2 = _routing_plan(route, MOE_TM)
    ys = _moe_ffn(h, src, tile_expert, n_used, tile_rows, moe_w_gate[0].astype(BF16), moe_w_up[0].astype(BF16),
                  moe_w_down[0].astype(BF16))
    x2 = _combine_ln(ys, pos1, pos2, route, x2, vec1, ln_g[1, 1], ln_b[1, 1], t)
    return x2.reshape(bsz, t, d)
```

```python
import functools

import numpy as np
import jax
import jax.numpy as jnp
from jax import lax
from jax.experimental import pallas as pl
from jax.experimental.pallas import tpu as pltpu

F32 = jnp.float32
BF16 = jnp.bfloat16

D_MODEL = 1024
GRID_W = 64
HEAD_DIM = 64
CONV_CH = 512
CONV_WIDTH = 31
CONV_HALO = 16
Q_COLS = 512
KV_COLS = 128
A_COLS = 2 * CONV_CH
ROPE_THETA = 10000.0
WIN_R = 8
WIN_C = 16
N_EXPERTS = 8
DEPTH = 2
ALPHA = (2 * DEPTH) ** 0.25
LN_EPS = 1e-5
RMS_EPS = 1e-6
ATT_SCALE = HEAD_DIM ** -0.5
LOG2E = 1.4426950408889634
MASK_VALUE = -1e30

LANES = 128
ROW_TILES = D_MODEL // LANES
VMEM_LIMIT = 56 * 1024 * 1024

ATT_TQ = 256
ATT_CW = 256
VT_ROWS = HEAD_DIM + 16
NA_ROWS = 4
NA_KROWS = 12
NA_PAIRS = 4
MOE_TM = 512
MOE_TF = 1792
GATHER_ROWS = 512


def _cparams(*sem):
    return pltpu.CompilerParams(dimension_semantics=sem, vmem_limit_bytes=VMEM_LIMIT)


def _layer_norm(z, g, b):
    mu = jnp.mean(z, axis=-1, keepdims=True)
    zc = z - mu
    var = jnp.mean(zc * zc, axis=-1, keepdims=True)
    return zc * lax.rsqrt(var + LN_EPS) * g + b


def _vec_spec(k, tm, rows_per_batch):
    if rows_per_batch is None:
        return pl.BlockSpec((1, 1, D_MODEL), lambda i, *_: (4 * 6 + k, 0, 0))
    return pl.BlockSpec((1, 1, D_MODEL), lambda i, *_: ((i * tm // rows_per_batch) * 6 + k, 0, 0))


def _full_spec(shape):
    nd = len(shape)
    return pl.BlockSpec(shape, lambda *_: (0,) * nd)


def _mod_kernel(c_ref, w_ref, b_ref, o_ref):
    c = c_ref[...]
    s = c * jax.nn.sigmoid(c)
    o_ref[...] = jnp.dot(s.astype(BF16), w_ref[...].astype(BF16), preferred_element_type=F32) + b_ref[...]


def _modulation(cc, w_mod, b_mod):
    n = 6 * D_MODEL
    tn = D_MODEL
    return pl.pallas_call(
        _mod_kernel,
        grid=(DEPTH, n // tn),
        in_specs=[pl.BlockSpec((8, D_MODEL), lambda l, j: (0, 0)),
                  pl.BlockSpec((None, D_MODEL, tn), lambda l, j: (l, 0, j)),
                  pl.BlockSpec((None, 1, tn), lambda l, j: (l, 0, j))],
        out_specs=pl.BlockSpec((None, 8, tn), lambda l, j: (l, 0, j)),
        out_shape=jax.ShapeDtypeStruct((DEPTH, 8, n), F32),
        compiler_params=_cparams("parallel", "parallel"),
        name="modulation",
    )(cc, w_mod, b_mod.reshape(DEPTH, 1, n))


def _inproj0_kernel(x_ref, sc_ref, sh_ref, w_ref, oa_ref, oq_ref):
    h = x_ref[...] * (1.0 + sc_ref[0]) + sh_ref[0]
    o = jnp.dot(h.astype(BF16), w_ref[...], preferred_element_type=F32)
    oa_ref[...] = o[:, :A_COLS]
    oq_ref[...] = o[:, A_COLS:]


def _inproj0(x2d, vecs, w, rows_per_batch, tm):
    r = x2d.shape[0]
    nq = w.shape[1] - A_COLS
    return pl.pallas_call(
        _inproj0_kernel,
        grid=(r // tm,),
        in_specs=[pl.BlockSpec((tm, D_MODEL), lambda i: (i, 0)),
                  _vec_spec(1, tm, rows_per_batch), _vec_spec(0, tm, rows_per_batch),
                  _full_spec(w.shape)],
        out_specs=[pl.BlockSpec((tm, A_COLS), lambda i: (i, 0)),
                   pl.BlockSpec((tm, nq), lambda i: (i, 0))],
        out_shape=[jax.ShapeDtypeStruct((r, A_COLS), F32), jax.ShapeDtypeStruct((r, nq), F32)],
        compiler_params=_cparams("parallel"),
        name="inproj0",
    )(x2d, vecs, vecs, w)


def _conv_kernel(pm_ref, pp_ref, pn_ref, w_ref, g_ref, b_ref, o_ref, u_ref, us_ref, cv_ref, *, tt, tx):
    i = pl.program_id(0)

    def sigmoid(v):
        return 0.5 * jnp.tanh(0.5 * v) + 0.5

    def glu(p):
        return p[:, :CONV_CH] * sigmoid(p[:, CONV_CH:])

    first = (i * tt) % tx == 0
    last = ((i + 1) * tt) % tx == 0
    u_ref[0:CONV_HALO, :] = jnp.where(first, 0.0, glu(pp_ref[...]))
    u_ref[CONV_HALO:CONV_HALO + tt, :] = glu(pm_ref[...])
    u_ref[CONV_HALO + tt:2 * CONV_HALO + tt, :] = jnp.where(last, 0.0, glu(pn_ref[...]))
    u_ref[2 * CONV_HALO + tt:, :] = jnp.zeros((8, CONV_CH), F32)
    ch = 32
    nrow = tt + 2 * CONV_HALO
    base = CONV_HALO - CONV_WIDTH // 2

    def shift_body(c, carry):
        r0 = pl.multiple_of(c * ch, ch)
        w = u_ref[pl.ds(r0, ch + 8), :]
        for s in range(1, 8):
            us_ref[s - 1, pl.ds(r0, ch), :] = pltpu.roll(w, ch + 8 - s, axis=0)[0:ch]
        return carry

    lax.fori_loop(0, nrow // ch, shift_body, 0)

    def body(c, carry):
        r0 = pl.multiple_of(c * ch, ch)
        acc = jnp.zeros((ch, CONV_CH), F32)
        for k in range(CONV_WIDTH):
            a, s = divmod(k + base, 8)
            src = u_ref if s == 0 else us_ref.at[s - 1]
            acc = acc + src[pl.ds(r0 + 8 * a, ch), :] * w_ref[pl.ds(k, 1), :]
        cv_ref[pl.ds(r0, ch), :] = acc
        return carry

    lax.fori_loop(0, tt // ch, body, 0)
    y = _layer_norm(cv_ref[...], g_ref[...], b_ref[...])
    o_ref[...] = (y * sigmoid(y)).astype(BF16)


def _conformer_conv(pa, conv_w, conv_g, conv_b, tx, tt):
    r = pa.shape[0]
    hb = tt // CONV_HALO
    nhb = r // CONV_HALO
    return pl.pallas_call(
        functools.partial(_conv_kernel, tt=tt, tx=tx),
        grid=(r // tt,),
        in_specs=[pl.BlockSpec((tt, A_COLS), lambda i: (i, 0)),
                  pl.BlockSpec((CONV_HALO, A_COLS), lambda i: (jnp.maximum(i * hb - 1, 0), 0)),
                  pl.BlockSpec((CONV_HALO, A_COLS), lambda i: (jnp.minimum((i + 1) * hb, nhb - 1), 0)),
                  _full_spec((CONV_WIDTH, CONV_CH)), _full_spec((1, CONV_CH)), _full_spec((1, CONV_CH))],
        out_specs=pl.BlockSpec((tt, CONV_CH), lambda i: (i, 0)),
        out_shape=jax.ShapeDtypeStruct((r, CONV_CH), BF16),
        scratch_shapes=[pltpu.VMEM((tt + 2 * CONV_HALO + 8, CONV_CH), F32),
                        pltpu.VMEM((7, tt + 2 * CONV_HALO, CONV_CH), F32),
                        pltpu.VMEM((tt, CONV_CH), F32)],
        compiler_params=_cparams("parallel"),
        name="conformer_conv",
    )(pa, pa, pa, conv_w, conv_g.reshape(1, CONV_CH), conv_b.reshape(1, CONV_CH))


def _prep_kernel(p_ref, cos_ref, sin_ref, qg_ref, kg_ref, qt_ref, k_ref, vt_ref, *, tq):
    x = p_ref[...]
    cos = cos_ref[...]
    sin = sin_ref[...]

    def norm_rope(xh, g):
        ms = jnp.mean(xh * xh, axis=0, keepdims=True)
        y = xh * lax.rsqrt(ms + RMS_EPS) * g
        sw = jnp.concatenate([y[16:32], y[0:16], y[48:64], y[32:48]], axis=0)
        return y * cos + sw * sin

    zeros = jnp.zeros((HEAD_DIM, tq), BF16)
    for p in range(Q_COLS // LANES):
        xp = x[:, LANES * p:LANES * (p + 1)].T
        for half in range(2):
            h = 2 * p + half
            j, g = h // 4, h % 4
            r = (norm_rope(xp[HEAD_DIM * half:HEAD_DIM * (half + 1)], qg_ref[...]) * (ATT_SCALE * LOG2E)).astype(BF16)
            qt_ref[j, HEAD_DIM * j:HEAD_DIM * (j + 1), g * tq:(g + 1) * tq] = r
            qt_ref[j, HEAD_DIM * (1 - j):HEAD_DIM * (2 - j), g * tq:(g + 1) * tq] = zeros
    xk = x[:, Q_COLS:Q_COLS + KV_COLS].T
    k0 = norm_rope(xk[0:HEAD_DIM], kg_ref[...])
    k1 = norm_rope(xk[HEAD_DIM:2 * HEAD_DIM], kg_ref[...])
    k_ref[...] = jnp.concatenate([k0, k1], axis=0).T.astype(BF16)
    xv = x[:, Q_COLS + KV_COLS:].T.astype(BF16)
    ones = jnp.ones((VT_ROWS - HEAD_DIM, tq), BF16)
    for j in range(2):
        vt_ref[j, 0:HEAD_DIM, :] = xv[HEAD_DIM * j:HEAD_DIM * (j + 1)]
        vt_ref[j, HEAD_DIM:VT_ROWS, :] = ones


def _prep_qkv(pq, cos_t, sin_t, q_g, k_g, bx, tx):
    tq = ATT_TQ
    nq = tx // tq
    return pl.pallas_call(
        functools.partial(_prep_kernel, tq=tq),
        grid=(bx, nq),
        in_specs=[pl.BlockSpec((tq, Q_COLS + 2 * KV_COLS), lambda b, t: (b * nq + t, 0)),
                  pl.BlockSpec((HEAD_DIM, tq), lambda b, t: (0, t)),
                  pl.BlockSpec((HEAD_DIM, tq), lambda b, t: (0, t)),
                  _full_spec((HEAD_DIM, 1)), _full_spec((HEAD_DIM, 1))],
        out_specs=[pl.BlockSpec((None, None, 2, LANES, 4 * tq), lambda b, t: (b, t, 0, 0, 0)),
                   pl.BlockSpec((None, None, tq, LANES), lambda b, t: (b, t, 0, 0)),
                   pl.BlockSpec((None, 2, None, VT_ROWS, tq), lambda b, t: (b, 0, t, 0, 0))],
        out_shape=[jax.ShapeDtypeStruct((bx, nq, 2, LANES, 4 * tq), BF16),
                   jax.ShapeDtypeStruct((bx, nq, tq, LANES), BF16),
                   jax.ShapeDtypeStruct((bx, 2, nq, VT_ROWS, tq), BF16)],
        compiler_params=_cparams("parallel", "parallel"),
        name="prep_qkv",
    )(pq, cos_t, sin_t, q_g.reshape(HEAD_DIM, 1), k_g.reshape(HEAD_DIM, 1))


def _rope_tables(t):
    pos = np.arange(t)
    half = HEAD_DIM // 4
    inv = ROPE_THETA ** (-jnp.arange(half, dtype=F32) * 2.0 / (HEAD_DIM // 2))
    dd = np.arange(HEAD_DIM)
    part_pos = np.where((dd // (HEAD_DIM // 2))[:, None] == 0, (pos // GRID_W)[None, :], (pos % GRID_W)[None, :])
    ang = jnp.asarray(part_pos, F32) * inv[dd % half][:, None]
    sign = jnp.asarray(np.where((dd % (HEAD_DIM // 2)) < half, -1.0, 1.0)[:, None], F32)
    return jnp.cos(ang), jnp.sin(ang) * sign


def _attn_kernel(qt_ref, k_ref, vt_ref, o_ref, acc_ref, *, nk, tq):
    ngrp = qt_ref.shape[0]
    acc_ref[...] = jnp.zeros_like(acc_ref)
    cw = ATT_CW
    strips = [(j, n) for j in range(ngrp) for n in range(4 * tq // cw)]

    def scores(c, j, n):
        return jnp.dot(k_ref[c], qt_ref[j, :, cw * n:cw * (n + 1)], preferred_element_type=F32)

    s = [scores(0, j, n) for j, n in strips]
    m_prev = [jnp.full((1, cw), -jnp.inf, F32)] * len(strips)
    m = [jnp.max(sn, axis=0, keepdims=True) for sn in s]
    for c in range(nk):
        for i, (j, n) in enumerate(strips):
            s_next = scores(c + 1, j, n) if c + 1 < nk else None
            alpha = jnp.exp2(m_prev[i] - m[i])
            p = jnp.exp2((s[i] - m[i]).astype(BF16))
            pv = jnp.dot(vt_ref[j, c], p, preferred_element_type=F32)
            acc_ref[j, :, cw * n:cw * (n + 1)] = acc_ref[j, :, cw * n:cw * (n + 1)] * alpha + pv
            if s_next is not None:
                m_prev[i] = m[i]
                m[i] = jnp.maximum(m[i], jnp.max(s_next, axis=0, keepdims=True))
                s[i] = s_next
    for j in range(ngrp):
        o = acc_ref[j, 0:HEAD_DIM, :] * (1.0 / acc_ref[j, HEAD_DIM:HEAD_DIM + 1, :])
        for pp in range(2):
            blk = jnp.concatenate([o[:, (2 * pp) * tq:(2 * pp + 1) * tq],
                                   o[:, (2 * pp + 1) * tq:(2 * pp + 2) * tq]], axis=0)
            lane0 = LANES * (2 * j + pp)
            o_ref[:, lane0:lane0 + LANES] = blk.T.astype(BF16)


def _gqa_attention(qt, k, vt, tx):
    bx, nq = qt.shape[0], qt.shape[1]
    nk = k.shape[1]
    tq = ATT_TQ
    return pl.pallas_call(
        functools.partial(_attn_kernel, nk=nk, tq=tq),
        grid=(bx, nq),
        in_specs=[pl.BlockSpec((None, None, 2, LANES, 4 * tq), lambda b, t: (b, t, 0, 0, 0)),
                  pl.BlockSpec((None, nk, tq, LANES), lambda b, t: (b, 0, 0, 0)),
                  pl.BlockSpec((None, 2, nk, VT_ROWS, tq), lambda b, t: (b, 0, 0, 0, 0))],
        out_specs=pl.BlockSpec((tq, Q_COLS), lambda b, t: (b * nq + t, 0)),
        out_shape=jax.ShapeDtypeStruct((bx * tx, Q_COLS), BF16),
        scratch_shapes=[pltpu.VMEM((2, VT_ROWS, 4 * tq), F32)],
        compiler_params=_cparams("parallel", "parallel"),
        name="gqa_attention",
    )(qt, k, vt)


def _outproj_ln_kernel(*refs, n_lhs):
    lhs = refs[:n_lhs]
    w_ref, x_ref, gate_ref, lng_ref, lnb_ref, o_ref = refs[n_lhs:]
    y = None
    off = 0
    for r in lhs:
        kk = r.shape[1]
        t = jnp.dot(r[...], w_ref[off:off + kk, :], preferred_element_type=F32)
        off += kk
        y = t if y is None else y + t
    z = ALPHA * x_ref[...] + gate_ref[0] * y
    o_ref[...] = _layer_norm(z, lng_ref[...], lnb_ref[...])


def _outproj_ln(lhs, w, x2d, vecs, gate_k, ln_g, ln_b, rows_per_batch, tm):
    r = x2d.shape[0]
    return pl.pallas_call(
        functools.partial(_outproj_ln_kernel, n_lhs=len(lhs)),
        grid=(r // tm,),
        in_specs=[pl.BlockSpec((tm, a.shape[1]), lambda i: (i, 0)) for a in lhs]
        + [_full_spec(w.shape), pl.BlockSpec((tm, D_MODEL), lambda i: (i, 0)),
           _vec_spec(gate_k, tm, rows_per_batch), _full_spec((1, D_MODEL)), _full_spec((1, D_MODEL))],
        out_specs=pl.BlockSpec((tm, D_MODEL), lambda i: (i, 0)),
        out_shape=jax.ShapeDtypeStruct((r, D_MODEL), F32),
        compiler_params=_cparams("parallel"),
        name="outproj_ln",
    )(*lhs, w, x2d, vecs, ln_g.reshape(1, D_MODEL), ln_b.reshape(1, D_MODEL))


def _ffn_kernel(x_ref, sc_ref, sh_ref, gate_ref, wg_ref, wu_ref, wd_ref, lng_ref, lnb_ref, o_ref):
    x = x_ref[...]
    hb = (x * (1.0 + sc_ref[0]) + sh_ref[0]).astype(BF16)
    hg = jnp.dot(hb, wg_ref[...], preferred_element_type=F32)
    hu = jnp.dot(hb, wu_ref[...], preferred_element_type=F32)
    h1 = (hg * jax.nn.sigmoid(hg) * hu).astype(BF16)
    y = jnp.dot(h1, wd_ref[...], preferred_element_type=F32)
    o_ref[...] = _layer_norm(ALPHA * x + gate_ref[0] * y, lng_ref[...], lnb_ref[...])


def _ffn_ln(x2d, vecs, wg, wu, wd, ln_g, ln_b, rows_per_batch, tm):
    r = x2d.shape[0]

    def resident(shape):
        return pl.BlockSpec(shape, lambda i: (0, 0), pipeline_mode=pl.Buffered(1))

    return pl.pallas_call(
        _ffn_kernel,
        grid=(r // tm,),
        in_specs=[pl.BlockSpec((tm, D_MODEL), lambda i: (i, 0)),
                  _vec_spec(4, tm, rows_per_batch), _vec_spec(3, tm, rows_per_batch),
                  _vec_spec(5, tm, rows_per_batch),
                  resident(wg.shape), resident(wu.shape), resident(wd.shape),
                  _full_spec((1, D_MODEL)), _full_spec((1, D_MODEL))],
        out_specs=pl.BlockSpec((tm, D_MODEL), lambda i: (i, 0)),
        out_shape=jax.ShapeDtypeStruct((r, D_MODEL), F32),
        compiler_params=_cparams("parallel"),
        name="ffn_ln",
    )(x2d, vecs, vecs, vecs, wg, wu, wd, ln_g.reshape(1, D_MODEL), ln_b.reshape(1, D_MODEL))


def _inproj1_kernel(x_ref, sc_ref, sh_ref, w_ref, *o_refs, with_q):
    h = x_ref[...] * (1.0 + sc_ref[0]) + sh_ref[0]
    o = jnp.dot(h.astype(BF16), w_ref[...], preferred_element_type=F32)
    if with_q:
        qt_ref, k_ref, vt_ref = o_refs
        qt_ref[...] = (o[:, :D_MODEL] * (ATT_SCALE * LOG2E)).T.astype(BF16)
    else:
        k_ref, vt_ref = o_refs
    nk = o.shape[1] - 2 * D_MODEL
    k_ref[...] = o[:, nk:nk + D_MODEL].astype(BF16)
    vt_ref[...] = o[:, nk + D_MODEL:].T.astype(BF16)


def _inproj1(x2d, vecs, w, rows_per_batch, tm, with_q):
    r = x2d.shape[0]
    nat = pl.BlockSpec((tm, D_MODEL), lambda i: (i, 0))
    tr = pl.BlockSpec((D_MODEL, tm), lambda i: (0, i))
    nat_shape = jax.ShapeDtypeStruct((r, D_MODEL), BF16)
    tr_shape = jax.ShapeDtypeStruct((D_MODEL, r), BF16)
    return pl.pallas_call(
        functools.partial(_inproj1_kernel, with_q=with_q),
        grid=(r // tm,),
        in_specs=[pl.BlockSpec((tm, D_MODEL), lambda i: (i, 0)),
                  _vec_spec(1, tm, rows_per_batch), _vec_spec(0, tm, rows_per_batch),
                  _full_spec(w.shape)],
        out_specs=([tr] if with_q else []) + [nat, tr],
        out_shape=([tr_shape] if with_q else []) + [nat_shape, tr_shape],
        compiler_params=_cparams("parallel"),
        name="inproj1",
    )(x2d, vecs, vecs, w)


def _na_bias(rpb, n_rows, n_ctx):
    nh = rpb.shape[0]
    c = np.arange(GRID_W)
    cs = np.clip(c - WIN_C // 2, 0, GRID_W - WIN_C)
    in_c = (c[None, :] >= cs[:, None]) & (c[None, :] < cs[:, None] + WIN_C)
    dc = c[None, :] - c[:, None] + WIN_C - 1
    pick = ((dc[None] == np.arange(2 * WIN_C - 1)[:, None, None]) & in_c[None]).astype(np.float32)
    cols = jnp.einsum("hrd,dkc->hrkc", rpb, jnp.asarray(pick.transpose(0, 2, 1)), precision=lax.Precision.HIGHEST)
    cols = jnp.where(jnp.asarray(in_c.T)[None, None], cols * LOG2E, MASK_VALUE)
    n_dr = 2 * WIN_R - 1
    cols = jnp.concatenate([cols, jnp.full((nh, 1, GRID_W, GRID_W), MASK_VALUE, F32)], axis=1)
    pick_dr = np.full((3, NA_ROWS, NA_KROWS), n_dr, np.int32)
    for case, r0 in enumerate((0, NA_ROWS, n_rows - NA_ROWS)):
        start = int(np.clip(r0 - WIN_R // 2, 0, n_rows - NA_KROWS))
        for ri in range(NA_ROWS):
            r = r0 + ri
            rs = int(np.clip(r - WIN_R // 2, 0, n_rows - WIN_R))
            for ki in range(NA_KROWS):
                kr = start + ki
                if rs <= kr < rs + WIN_R:
                    pick_dr[case, ri, ki] = kr - r + WIN_R - 1
    npair = nh // 2
    nkeys = NA_KROWS * GRID_W
    out = pl.pallas_call(
        functools.partial(_na_bias_kernel, nkeys=nkeys),
        grid_spec=pltpu.PrefetchScalarGridSpec(
            num_scalar_prefetch=1,
            grid=(3, npair),
            in_specs=[pl.BlockSpec((2, n_dr + 1, GRID_W, GRID_W), lambda case, p, dr: (p, 0, 0, 0))],
            out_specs=pl.BlockSpec((None, None, nkeys + n_ctx, 2 * NA_ROWS * GRID_W),
                                   lambda case, p, dr: (case, p, 0, 0)),
        ),
        out_shape=jax.ShapeDtypeStruct((3, npair, nkeys + n_ctx, 2 * NA_ROWS * GRID_W), F32),
        compiler_params=_cparams("parallel", "parallel"),
        name="na_bias",
    )(jnp.asarray(pick_dr.reshape(-1)), cols)
    return out.reshape(3, npair // NA_PAIRS, NA_PAIRS, nkeys + n_ctx, 2 * NA_ROWS * GRID_W)


def _na_bias_kernel(dr_ref, cols_ref, o_ref, *, nkeys):
    case = pl.program_id(0)
    for ki in range(NA_KROWS):
        pieces = []
        for e in range(2):
            for ri in range(NA_ROWS):
                d = dr_ref[(case * NA_ROWS + ri) * NA_KROWS + ki]
                pieces.append(cols_ref[e, d])
        o_ref[GRID_W * ki:GRID_W * (ki + 1), :] = jnp.concatenate(pieces, axis=1)
    o_ref[nkeys:, :] = jnp.zeros((o_ref.shape[0] - nkeys, o_ref.shape[1]), F32)


def _na_kernel(qt_ref, k0_ref, k1_ref, k2_ref, kc_ref, v0_ref, v1_ref, v2_ref, vc_ref, bias_ref, o_ref):
    qt = qt_ref[...]
    kk = jnp.concatenate([k0_ref[...], k1_ref[...], k2_ref[...], kc_ref[...]], axis=0)
    vt = jnp.concatenate([v0_ref[...], v1_ref[...], v2_ref[...], vc_ref[...]], axis=1)
    nq = qt.shape[1]
    zeros = jnp.zeros((HEAD_DIM, nq), BF16)
    ones = jnp.ones((VT_ROWS - HEAD_DIM, kk.shape[0]), BF16)

    def scores(pp):
        q2 = qt[LANES * pp:LANES * (pp + 1)]
        qcat = jnp.concatenate([jnp.concatenate([q2[0:HEAD_DIM], zeros], axis=0),
                                jnp.concatenate([zeros, q2[HEAD_DIM:]], axis=0)], axis=1)
        return jnp.dot(kk[:, LANES * pp:LANES * (pp + 1)], qcat, preferred_element_type=F32) + bias_ref[pp]

    s_all = [scores(pp) for pp in range(NA_PAIRS)]
    for pp in range(NA_PAIRS):
        s = s_all[pp]
        m = jnp.max(s, axis=0, keepdims=True)
        p = jnp.exp2((s - m).astype(BF16))
        vext = jnp.concatenate([vt[LANES * pp:LANES * (pp + 1)], ones], axis=0)
        pv = jnp.dot(vext, p, preferred_element_type=F32)
        o0 = pv[0:HEAD_DIM, 0:nq] * (1.0 / pv[LANES:LANES + 1, 0:nq])
        o1 = pv[HEAD_DIM:LANES, nq:] * (1.0 / pv[LANES:LANES + 1, nq:])
        o_ref[:, LANES * pp:LANES * (pp + 1)] = jnp.concatenate([o0, o1], axis=0).T.astype(BF16)


def _neighbourhood_attention(qt, k, vt, kc, vct, bias, bsz, t):
    nq = NA_ROWS * GRID_W
    nrb = t // nq
    wl = NA_PAIRS * LANES
    ngrp = D_MODEL // wl
    nctx = kc.shape[0] // bsz
    nwin = NA_KROWS // NA_ROWS

    def first_kblock(b, rb):
        return b * nrb + jnp.clip(rb - 1, 0, nrb - nwin)

    def kspec(d):
        return pl.BlockSpec((nq, wl), lambda hp, b, rb: (first_kblock(b, rb) + d, hp))

    def vspec(d):
        return pl.BlockSpec((wl, nq), lambda hp, b, rb: (hp, first_kblock(b, rb) + d))

    def case(rb):
        return jnp.where(rb == 0, 0, jnp.where(rb == nrb - 1, 2, 1))

    return pl.pallas_call(
        _na_kernel,
        grid=(ngrp, bsz, nrb),
        in_specs=[pl.BlockSpec((wl, nq), lambda hp, b, rb: (hp, b * nrb + rb)),
                  kspec(0), kspec(1), kspec(2), pl.BlockSpec((nctx, wl), lambda hp, b, rb: (b, hp)),
                  vspec(0), vspec(1), vspec(2), pl.BlockSpec((wl, nctx), lambda hp, b, rb: (hp, b)),
                  pl.BlockSpec((None, None, NA_PAIRS, NA_KROWS * GRID_W + nctx, 2 * nq),
                               lambda hp, b, rb: (case(rb), hp, 0, 0, 0))],
        out_specs=pl.BlockSpec((nq, wl), lambda hp, b, rb: (b * nrb + rb, hp)),
        out_shape=jax.ShapeDtypeStruct((bsz * t, D_MODEL), BF16),
        compiler_params=_cparams("parallel", "parallel", "parallel"),
        name="neighbourhood_attention",
    )(qt, k, k, k, kc, vt, vt, vt, vct, bias)


def _store_row_tiles(dst_ref, val):
    rows = val.shape[0]
    for lt in range(ROW_TILES):
        dst_ref[pl.ds(lt, rows, stride=ROW_TILES), :] = val[:, lt * LANES:(lt + 1) * LANES]


def _load_row_tiles(src_ref, lt, rows):
    return src_ref[pl.ds(lt, rows, stride=ROW_TILES), :]


def _router_kernel(x_ref, sc_ref, sh_ref, wr_ref, h_ref, r_ref):
    h = x_ref[...] * (1.0 + sc_ref[0]) + sh_ref[0]
    _store_row_tiles(h_ref, h)
    hi = h.astype(BF16)
    lo = (h - hi.astype(F32)).astype(BF16)
    w = wr_ref[...]
    whi = w.astype(BF16)
    wlo = (w - whi.astype(F32)).astype(BF16)
    lg = (jnp.dot(hi, whi, preferred_element_type=F32)
          + (jnp.dot(hi, wlo, preferred_element_type=F32) + jnp.dot(lo, whi, preferred_element_type=F32)))
    lane = lax.broadcasted_iota(jnp.int32, lg.shape, 1).astype(F32)
    lg = jnp.where(lane < N_EXPERTS, lg, -jnp.inf)
    v1 = jnp.max(lg, axis=1, keepdims=True)
    i1 = jnp.min(jnp.where(lg == v1, lane, float(LANES)), axis=1, keepdims=True)
    lg2 = jnp.where(lane == i1, -jnp.inf, lg)
    v2 = jnp.max(lg2, axis=1, keepdims=True)
    i2 = jnp.min(jnp.where(lg2 == v2, lane, float(LANES)), axis=1, keepdims=True)
    e = jnp.exp(v2 - v1)
    w1 = 1.0 / (1.0 + e)
    w2 = e / (1.0 + e)
    r_ref[...] = jnp.where(lane == 0, i1, jnp.where(lane == 1, i2, jnp.where(lane == 2, w1,
                                                                              jnp.where(lane == 3, w2, 0.0))))


def _router(x2d, vecs, w_router, rows_per_batch, tm):
    r = x2d.shape[0]
    wr = jnp.pad(w_router, ((0, 0), (0, LANES - N_EXPERTS)))
    return pl.pallas_call(
        _router_kernel,
        grid=(r // tm,),
        in_specs=[pl.BlockSpec((tm, D_MODEL), lambda i: (i, 0)),
                  _vec_spec(4, tm, rows_per_batch), _vec_spec(3, tm, rows_per_batch),
                  _full_spec((D_MODEL, LANES))],
        out_specs=[pl.BlockSpec((tm * ROW_TILES, LANES), lambda i: (i, 0)),
                   pl.BlockSpec((tm, LANES), lambda i: (i, 0))],
        out_shape=[jax.ShapeDtypeStruct((r * ROW_TILES, LANES), F32), jax.ShapeDtypeStruct((r, LANES), F32)],
        compiler_params=_cparams("parallel"),
        name="router",
    )(x2d, vecs, vecs, wr)


def _row_copy(src_hbm, first_sublane, r, dst_ref, sem):
    return pltpu.make_async_copy(src_hbm.at[pl.ds(pl.multiple_of(first_sublane, ROW_TILES), ROW_TILES), :],
                                 dst_ref.at[pl.ds(pl.multiple_of(r * ROW_TILES, ROW_TILES), ROW_TILES), :], sem)


def _start_row_gather(src_hbm, idx_ref, dst_ref, sem, n, priorities):
    def issue(i, carry):
        for u in range(2):
            r = 2 * i + u
            _row_copy(src_hbm, idx_ref[0, r], r, dst_ref, sem).start(priority=priorities[u])
        return carry

    lax.fori_loop(0, n // 2, issue, 0, unroll=4)


def _wait_row_gather(src_hbm, dst_ref, sem, n):
    pltpu.make_async_copy(src_hbm.at[pl.ds(0, n * ROW_TILES), :], dst_ref, sem).wait()


def _moe_ffn_kernel(te_ref, nu_ref, nv_ref, idx0_ref, idxn_ref, h_hbm, wg_ref, wu_ref, wd_ref, o_ref,
                    xbuf_ref, acc_ref, sem):
    t = pl.program_id(0)
    j = pl.program_id(1)
    nj = pl.num_programs(1)
    used = t < nu_ref[0]
    tm = acc_ref.shape[0]
    half = tm // 2
    slot = t % 2

    @pl.when(jnp.logical_and(used, j == 0))
    def _():
        @pl.when(t == 0)
        def _():
            _start_row_gather(h_hbm, idx0_ref, xbuf_ref.at[0], sem.at[0], tm, (0, 0))

        @pl.when(t + 1 < nu_ref[0])
        def _():
            _start_row_gather(h_hbm, idxn_ref, xbuf_ref.at[1 - slot], sem.at[1 - slot], tm, (0, 0))

        _wait_row_gather(h_hbm, xbuf_ref.at[slot], sem.at[slot], tm)

    @pl.when(jnp.logical_and(t == 0, j == 0))
    def _():
        acc_ref[...] = jnp.zeros_like(acc_ref)

    def compute(rows):
        x = xbuf_ref.at[slot]
        hb = jnp.concatenate([_load_row_tiles(x, lt, rows).astype(BF16) for lt in range(ROW_TILES)], axis=1)
        hg = jnp.dot(hb, wg_ref[...], preferred_element_type=F32)
        hu = jnp.dot(hb, wu_ref[...], preferred_element_type=F32)
        h1 = (hg * jax.nn.sigmoid(hg) * hu).astype(BF16)
        acc = jnp.where(j == 0, 0.0, acc_ref[0:rows, :]) + jnp.dot(h1, wd_ref[...], preferred_element_type=F32)
        acc_ref[0:rows, :] = acc
        _store_row_tiles(o_ref, acc)
        if rows < tm:
            o_ref[rows * ROW_TILES:, :] = jnp.zeros(((tm - rows) * ROW_TILES, LANES), F32)

    @pl.when(nv_ref[t] > half)
    def _():
        compute(tm)

    @pl.when(jnp.logical_and(used, nv_ref[t] <= half))
    def _():
        compute(half)

    @pl.when(jnp.logical_and(jnp.logical_not(used), j == nj - 1))
    def _():
        o_ref[...] = jnp.zeros_like(o_ref)


def _moe_ffn(h, src, tile_expert, n_used, tile_rows, wg, wu, wd):
    p = src.shape[0]
    tm, tf = MOE_TM, MOE_TF
    nt = p // tm
    f = wg.shape[2]

    def jj(t, j, nu):
        return jnp.where(t < nu[0], j, 0)

    grid_spec = pltpu.PrefetchScalarGridSpec(
        num_scalar_prefetch=3,
        grid=(nt, f // tf),
        in_specs=[pl.BlockSpec((None, 1, tm), lambda t, j, te, nu, nv: (0, 0, 0), memory_space=pltpu.SMEM),
                  pl.BlockSpec((None, 1, tm), lambda t, j, te, nu, nv: (jnp.minimum(t + 1, nt - 1), 0, 0),
                               memory_space=pltpu.SMEM),
                  pl.BlockSpec(memory_space=pl.ANY),
                  pl.BlockSpec((None, D_MODEL, tf), lambda t, j, te, nu, nv: (te[t], 0, jj(t, j, nu))),
                  pl.BlockSpec((None, D_MODEL, tf), lambda t, j, te, nu, nv: (te[t], 0, jj(t, j, nu))),
                  pl.BlockSpec((None, tf, D_MODEL), lambda t, j, te, nu, nv: (te[t], jj(t, j, nu), 0))],
        out_specs=pl.BlockSpec((tm * ROW_TILES, LANES), lambda t, j, te, nu, nv: (t, 0)),
        scratch_shapes=[pltpu.VMEM((2, tm * ROW_TILES, LANES), F32), pltpu.VMEM((tm, D_MODEL), F32),
                        pltpu.SemaphoreType.DMA((2,))],
    )
    idx = src.reshape(nt, 1, tm)
    return pl.pallas_call(
        _moe_ffn_kernel,
        grid_spec=grid_spec,
        out_shape=jax.ShapeDtypeStruct((p * ROW_TILES, LANES), F32),
        compiler_params=_cparams("arbitrary", "arbitrary"),
        name="moe_ffn",
    )(tile_expert, n_used, tile_rows, idx, idx, h, wg, wu, wd)


def _combine_ln_kernel(p1a_ref, p2a_ref, p1b_ref, p2b_ref, ys_hbm, route_ref, x_ref, gate_ref, lng_ref, lnb_ref,
                       o_ref, y_ref, sem):
    i = pl.program_id(0)
    n = x_ref.shape[0]
    slot = i % 2

    def start(p1_ref, p2_ref, s):
        _start_row_gather(ys_hbm, p1_ref, y_ref.at[s, 0], sem.at[s, 0], n, (0, 1))
        _start_row_gather(ys_hbm, p2_ref, y_ref.at[s, 1], sem.at[s, 1], n, (0, 1))

    @pl.when(i == 0)
    def _():
        start(p1a_ref, p2a_ref, 0)

    @pl.when(i + 1 < pl.num_programs(0))
    def _():
        start(p1b_ref, p2b_ref, 1 - slot)

    for e in range(2):
        _wait_row_gather(ys_hbm, y_ref.at[slot, e], sem.at[slot, e], n)
    w1 = route_ref[:, 2:3]
    w2 = route_ref[:, 3:4]
    y = jnp.concatenate([w1 * _load_row_tiles(y_ref.at[slot, 0], lt, n) + w2 * _load_row_tiles(y_ref.at[slot, 1], lt, n)
                         for lt in range(ROW_TILES)], axis=1)
    z = ALPHA * x_ref[...] + gate_ref[0] * y
    o_ref[...] = _layer_norm(z, lng_ref[...], lnb_ref[...])


def _combine_ln(ys, pos1, pos2, route, x2d, vecs, ln_g, ln_b, rows_per_batch):
    r = x2d.shape[0]
    g = GATHER_ROWS
    ns = r // g
    first = pl.BlockSpec((None, 1, g), lambda i: (0, 0, 0), memory_space=pltpu.SMEM)
    ahead = pl.BlockSpec((None, 1, g), lambda i: (jnp.minimum(i + 1, ns - 1), 0, 0), memory_space=pltpu.SMEM)
    p1 = pos1.reshape(ns, 1, g)
    p2 = pos2.reshape(ns, 1, g)
    return pl.pallas_call(
        _combine_ln_kernel,
        grid=(ns,),
        in_specs=[first, first, ahead, ahead, pl.BlockSpec(memory_space=pl.ANY),
                  pl.BlockSpec((g, LANES), lambda i: (i, 0)),
                  pl.BlockSpec((g, D_MODEL), lambda i: (i, 0)),
                  _vec_spec(5, g, rows_per_batch), _full_spec((1, D_MODEL)), _full_spec((1, D_MODEL))],
        out_specs=pl.BlockSpec((g, D_MODEL), lambda i: (i, 0)),
        out_shape=jax.ShapeDtypeStruct((r, D_MODEL), F32),
        scratch_shapes=[pltpu.VMEM((2, 2, g * ROW_TILES, LANES), F32), pltpu.SemaphoreType.DMA((2, 2))],
        compiler_params=_cparams("arbitrary"),
        name="combine_ln",
    )(p1, p2, p1, p2, ys, route, x2d, vecs, ln_g.reshape(1, D_MODEL), ln_b.reshape(1, D_MODEL))


def _routing_plan(route, tm):
    n = route.shape[0]
    e = jnp.concatenate([route[:, 0], route[:, 1]]).astype(jnp.int32)
    onehot = (e[:, None] == jnp.arange(N_EXPERTS, dtype=jnp.int32)[None, :]).astype(jnp.int32)
    csum = jnp.cumsum(onehot, axis=0)
    rank = jnp.sum(csum * onehot, axis=1) - 1
    counts = csum[-1]
    padded = ((counts + tm - 1) // tm) * tm
    ends = jnp.cumsum(padded)
    starts = ends - padded
    pos = jnp.sum(starts[None, :] * onehot, axis=1) + rank
    p = 2 * n + N_EXPERTS * tm
    by_expert = jnp.argsort(e, stable=True).astype(jnp.int32)
    tok_sorted = jnp.pad(jnp.where(by_expert >= n, by_expert - n, by_expert), (N_EXPERTS * tm, p - 2 * n))
    shift = starts - (jnp.cumsum(counts) - counts)
    slot = jnp.arange(p, dtype=jnp.int32)
    src = jnp.zeros((p,), jnp.int32)
    for g in range(N_EXPERTS):
        cand = lax.dynamic_slice(tok_sorted, (N_EXPERTS * tm - shift[g],), (p,))
        src = jnp.where((slot >= starts[g]) & (slot < ends[g]), cand, src)
    tile_start = jnp.arange(p // tm, dtype=jnp.int32) * tm
    tile_expert = jnp.minimum(jnp.sum((tile_start[:, None] >= ends[None, :]).astype(jnp.int32), axis=1),
                              N_EXPERTS - 1).astype(jnp.int32)
    n_used = (ends[-1] // tm).astype(jnp.int32).reshape(1)
    group_end = jnp.sum(jnp.where(tile_expert[:, None] == jnp.arange(N_EXPERTS)[None, :], (starts + counts)[None, :], 0),
                        axis=1)
    tile_rows = jnp.clip(group_end - tile_start, 0, tm).astype(jnp.int32)
    return src * ROW_TILES, tile_expert, n_used, tile_rows, pos[:n] * ROW_TILES, pos[n:] * ROW_TILES


def kernel(x, c, ctx, c_ctx, w_mod, b_mod, ln_g, ln_b, ab_w_in, ab_conv_w, ab_conv_g, ab_conv_b, ab_q_g, ab_k_g,
           ab_w_out, ffn_w_gate, ffn_w_up, ffn_w_down, na_w_qkv, na_rpb, na_w_out, moe_w_router, moe_w_gate,
           moe_w_up, moe_w_down):
    bsz, t, d = x.shape
    n_ctx = ctx.shape[1]
    n = bsz * t
    nc = bsz * n_ctx
    x2 = x.reshape(n, d)
    c2 = ctx.reshape(nc, d)

    cc = jnp.concatenate([c, c_ctx[None, :], jnp.zeros((8 - bsz - 1, d), F32)], axis=0)
    mod = _modulation(cc, w_mod, b_mod)
    vec0 = mod[0].reshape(8 * 6, 1, d)
    vec1 = mod[1].reshape(8 * 6, 1, d)

    w_in = ab_w_in[0].astype(BF16)
    w_out = ab_w_out[0].astype(BF16)
    pa, pq = _inproj0(x2, vec0, w_in, t, 1024)
    pac, pqc = _inproj0(c2, vec0, w_in, None, 512)
    a = _conformer_conv(pa, ab_conv_w[0], ab_conv_g[0], ab_conv_b[0], t, 512)
    ac = _conformer_conv(pac, ab_conv_w[0], ab_conv_g[0], ab_conv_b[0], n_ctx, n_ctx)
    cos_t, sin_t = _rope_tables(t)
    qt, k, vt = _prep_qkv(pq, cos_t, sin_t, ab_q_g[0], ab_k_g[0], bsz, t)
    ones = jnp.ones((HEAD_DIM, n_ctx), F32)
    qtc, kc, vtc = _prep_qkv(pqc, ones, jnp.zeros_like(ones), ab_q_g[0], ab_k_g[0], bsz, n_ctx)
    o = _gqa_attention(qt, jnp.concatenate([kc, k], axis=1), jnp.concatenate([vtc, vt], axis=2), t)
    oc = _gqa_attention(qtc, kc, vtc, n_ctx)
    x2 = _outproj_ln([a, o], w_out, x2, vec0, 2, ln_g[0, 0], ln_b[0, 0], t, 1024)
    c2 = _outproj_ln([ac, oc], w_out, c2, vec0, 2, ln_g[0, 0], ln_b[0, 0], None, 512)
    wg = ffn_w_gate[0].astype(BF16)
    wu = ffn_w_up[0].astype(BF16)
    wd = ffn_w_down[0].astype(BF16)
    x2 = _ffn_ln(x2, vec0, wg, wu, wd, ln_g[0, 1], ln_b[0, 1], t, 512)
    c2 = _ffn_ln(c2, vec0, wg, wu, wd, ln_g[0, 1], ln_b[0, 1], None, 512)

    w_qkv = na_w_qkv[0].astype(BF16)
    qt1, k1, vt1 = _inproj1(x2, vec1, w_qkv, t, 1024, True)
    kc1, vct1 = _inproj1(c2, vec1, w_qkv[:, d:], None, 512, False)
    o = _neighbourhood_attention(qt1, k1, vt1, kc1, vct1, _na_bias(na_rpb[0], t // GRID_W, n_ctx), bsz, t)
    x2 = _outproj_ln([o], na_w_out[0].astype(BF16), x2, vec1, 2, ln_g[1, 0], ln_b[1, 0], t, 1024)

    h, route = _router(x2, vec1, moe_w_router[0], t, 1024)
    src, tile_expert, n_used, tile_rows, pos1, pos2 = _routing_plan(route, MOE_TM)
    ys = _moe_ffn(h, src, tile_expert, n_used, tile_rows, moe_w_gate[0].astype(BF16), moe_w_up[0].astype(BF16),
                  moe_w_down[0].astype(BF16))
    x2 = _combine_ln(ys, pos1, pos2, route, x2, vec1, ln_g[1, 1], ln_b[1, 1], t)
    return x2.reshape(bsz, t, d)
```

```python
import functools

import numpy as np
import jax
import jax.numpy as jnp
from jax import lax
from jax.experimental import pallas as pl
from jax.experimental.pallas import tpu as pltpu

F32 = jnp.float32
BF16 = jnp.bfloat16

D_MODEL = 1024
GRID_W = 64
HEAD_DIM = 64
CONV_CH = 512
CONV_WIDTH = 31
CONV_HALO = 16
Q_COLS = 512
KV_COLS = 128
A_COLS = 2 * CONV_CH
ROPE_THETA = 10000.0
WIN_R = 8
WIN_C = 16
N_EXPERTS = 8
DEPTH = 2
ALPHA = (2 * DEPTH) ** 0.25
LN_EPS = 1e-5
RMS_EPS = 1e-6
ATT_SCALE = HEAD_DIM ** -0.5
LOG2E = 1.4426950408889634
MASK_VALUE = -1e30

LANES = 128
ROW_TILES = D_MODEL // LANES
VMEM_LIMIT = 56 * 1024 * 1024

ATT_TQ = 256
ATT_CW = 256
VT_ROWS = HEAD_DIM + 16
NA_ROWS = 4
NA_KROWS = 12
NA_PAIRS = 4
MOE_TM = 512
MOE_TF = 1792
GATHER_ROWS = 512


def _cparams(*sem):
    return pltpu.CompilerParams(dimension_semantics=sem, vmem_limit_bytes=VMEM_LIMIT)


def _layer_norm(z, g, b):
    mu = jnp.mean(z, axis=-1, keepdims=True)
    zc = z - mu
    var = jnp.mean(zc * zc, axis=-1, keepdims=True)
    return zc * lax.rsqrt(var + LN_EPS) * g + b


def _vec_spec(k, tm, rows_per_batch):
    if rows_per_batch is None:
        return pl.BlockSpec((1, 1, D_MODEL), lambda i, *_: (4 * 6 + k, 0, 0))
    return pl.BlockSpec((1, 1, D_MODEL), lambda i, *_: ((i * tm // rows_per_batch) * 6 + k, 0, 0))


def _full_spec(shape):
    nd = len(shape)
    return pl.BlockSpec(shape, lambda *_: (0,) * nd)


def _mod_kernel(c_ref, w_ref, b_ref, o_ref):
    c = c_ref[...]
    s = c * jax.nn.sigmoid(c)
    o_ref[...] = jnp.dot(s.astype(BF16), w_ref[...].astype(BF16), preferred_element_type=F32) + b_ref[...]


def _modulation(cc, w_mod, b_mod):
    n = 6 * D_MODEL
    tn = D_MODEL
    return pl.pallas_call(
        _mod_kernel,
        grid=(DEPTH, n // tn),
        in_specs=[pl.BlockSpec((8, D_MODEL), lambda l, j: (0, 0)),
                  pl.BlockSpec((None, D_MODEL, tn), lambda l, j: (l, 0, j)),
                  pl.BlockSpec((None, 1, tn), lambda l, j: (l, 0, j))],
        out_specs=pl.BlockSpec((None, 8, tn), lambda l, j: (l, 0, j)),
        out_shape=jax.ShapeDtypeStruct((DEPTH, 8, n), F32),
        compiler_params=_cparams("parallel", "parallel"),
        name="modulation",
    )(cc, w_mod, b_mod.reshape(DEPTH, 1, n))


def _inproj0_kernel(x_ref, sc_ref, sh_ref, w_ref, oa_ref, oq_ref):
    h = x_ref[...] * (1.0 + sc_ref[0]) + sh_ref[0]
    o = jnp.dot(h.astype(BF16), w_ref[...], preferred_element_type=F32)
    oa_ref[...] = o[:, :A_COLS]
    oq_ref[...] = o[:, A_COLS:]


def _inproj0(x2d, vecs, w, rows_per_batch, tm):
    r = x2d.shape[0]
    nq = w.shape[1] - A_COLS
    return pl.pallas_call(
        _inproj0_kernel,
        grid=(r // tm,),
        in_specs=[pl.BlockSpec((tm, D_MODEL), lambda i: (i, 0)),
                  _vec_spec(1, tm, rows_per_batch), _vec_spec(0, tm, rows_per_batch),
                  _full_spec(w.shape)],
        out_specs=[pl.BlockSpec((tm, A_COLS), lambda i: (i, 0)),
                   pl.BlockSpec((tm, nq), lambda i: (i, 0))],
        out_shape=[jax.ShapeDtypeStruct((r, A_COLS), F32), jax.ShapeDtypeStruct((r, nq), F32)],
        compiler_params=_cparams("parallel"),
        name="inproj0",
    )(x2d, vecs, vecs, w)


def _conv_kernel(pm_ref, pp_ref, pn_ref, w_ref, g_ref, b_ref, o_ref, u_ref, us_ref, cv_ref, *, tt, tx):
    i = pl.program_id(0)

    def sigmoid(v):
        return 0.5 * jnp.tanh(0.5 * v) + 0.5

    def glu(p):
        return p[:, :CONV_CH] * sigmoid(p[:, CONV_CH:])

    first = (i * tt) % tx == 0
    last = ((i + 1) * tt) % tx == 0
    u_ref[0:CONV_HALO, :] = jnp.where(first, 0.0, glu(pp_ref[...]))
    u_ref[CONV_HALO:CONV_HALO + tt, :] = glu(pm_ref[...])
    u_ref[CONV_HALO + tt:2 * CONV_HALO + tt, :] = jnp.where(last, 0.0, glu(pn_ref[...]))
    u_ref[2 * CONV_HALO + tt:, :] = jnp.zeros((8, CONV_CH), F32)
    ch = 32
    nrow = tt + 2 * CONV_HALO
    base = CONV_HALO - CONV_WIDTH // 2

    def shift_body(c, carry):
        r0 = pl.multiple_of(c * ch, ch)
        w = u_ref[pl.ds(r0, ch + 8), :]
        for s in range(1, 8):
            us_ref[s - 1, pl.ds(r0, ch), :] = pltpu.roll(w, ch + 8 - s, axis=0)[0:ch]
        return carry

    lax.fori_loop(0, nrow // ch, shift_body, 0)

    def body(c, carry):
        r0 = pl.multiple_of(c * ch, ch)
        acc = jnp.zeros((ch, CONV_CH), F32)
        for k in range(CONV_WIDTH):
            a, s = divmod(k + base, 8)
            src = u_ref if s == 0 else us_ref.at[s - 1]
            acc = acc + src[pl.ds(r0 + 8 * a, ch), :] * w_ref[pl.ds(k, 1), :]
        cv_ref[pl.ds(r0, ch), :] = acc
        return carry

    lax.fori_loop(0, tt // ch, body, 0)
    y = _layer_norm(cv_ref[...], g_ref[...], b_ref[...])
    o_ref[...] = (y * sigmoid(y)).astype(BF16)


def _conformer_conv(pa, conv_w, conv_g, conv_b, tx, tt):
    r = pa.shape[0]
    hb = tt // CONV_HALO
    nhb = r // CONV_HALO
    return pl.pallas_call(
        functools.partial(_conv_kernel, tt=tt, tx=tx),
        grid=(r // tt,),
        in_specs=[pl.BlockSpec((tt, A_COLS), lambda i: (i, 0)),
                  pl.BlockSpec((CONV_HALO, A_COLS), lambda i: (jnp.maximum(i * hb - 1, 0), 0)),
                  pl.BlockSpec((CONV_HALO, A_COLS), lambda i: (jnp.minimum((i + 1) * hb, nhb - 1), 0)),
                  _full_spec((CONV_WIDTH, CONV_CH)), _full_spec((1, CONV_CH)), _full_spec((1, CONV_CH))],
        out_specs=pl.BlockSpec((tt, CONV_CH), lambda i: (i, 0)),
        out_shape=jax.ShapeDtypeStruct((r, CONV_CH), BF16),
        scratch_shapes=[pltpu.VMEM((tt + 2 * CONV_HALO + 8, CONV_CH), F32),
                        pltpu.VMEM((7, tt + 2 * CONV_HALO, CONV_CH), F32),
                        pltpu.VMEM((tt, CONV_CH), F32)],
        compiler_params=_cparams("parallel"),
        name="conformer_conv",
    )(pa, pa, pa, conv_w, conv_g.reshape(1, CONV_CH), conv_b.reshape(1, CONV_CH))


def _prep_kernel(p_ref, cos_ref, sin_ref, qg_ref, kg_ref, qt_ref, k_ref, vt_ref, *, tq):
    x = p_ref[...]
    cos = cos_ref[...]
    sin = sin_ref[...]

    def norm_rope(xh, g):
        ms = jnp.mean(xh * xh, axis=0, keepdims=True)
        y = xh * lax.rsqrt(ms + RMS_EPS) * g
        sw = jnp.concatenate([y[16:32], y[0:16], y[48:64], y[32:48]], axis=0)
        return y * cos + sw * sin

    zeros = jnp.zeros((HEAD_DIM, tq), BF16)
    for p in range(Q_COLS // LANES):
        xp = x[:, LANES * p:LANES * (p + 1)].T
        for half in range(2):
            h = 2 * p + half
            j, g = h // 4, h % 4
            r = (norm_rope(xp[HEAD_DIM * half:HEAD_DIM * (half + 1)], qg_ref[...]) * (ATT_SCALE * LOG2E)).astype(BF16)
            qt_ref[j, HEAD_DIM * j:HEAD_DIM * (j + 1), g * tq:(g + 1) * tq] = r
            qt_ref[j, HEAD_DIM * (1 - j):HEAD_DIM * (2 - j), g * tq:(g + 1) * tq] = zeros
    xk = x[:, Q_COLS:Q_COLS + KV_COLS].T
    k0 = norm_rope(xk[0:HEAD_DIM], kg_ref[...])
    k1 = norm_rope(xk[HEAD_DIM:2 * HEAD_DIM], kg_ref[...])
    k_ref[...] = jnp.concatenate([k0, k1], axis=0).T.astype(BF16)
    xv = x[:, Q_COLS + KV_COLS:].T.astype(BF16)
    ones = jnp.ones((VT_ROWS - HEAD_DIM, tq), BF16)
    for j in range(2):
        vt_ref[j, 0:HEAD_DIM, :] = xv[HEAD_DIM * j:HEAD_DIM * (j + 1)]
        vt_ref[j, HEAD_DIM:VT_ROWS, :] = ones


def _prep_qkv(pq, cos_t, sin_t, q_g, k_g, bx, tx):
    tq = ATT_TQ
    nq = tx // tq
    return pl.pallas_call(
        functools.partial(_prep_kernel, tq=tq),
        grid=(bx, nq),
        in_specs=[pl.BlockSpec((tq, Q_COLS + 2 * KV_COLS), lambda b, t: (b * nq + t, 0)),
                  pl.BlockSpec((HEAD_DIM, tq), lambda b, t: (0, t)),
                  pl.BlockSpec((HEAD_DIM, tq), lambda b, t: (0, t)),
                  _full_spec((HEAD_DIM, 1)), _full_spec((HEAD_DIM, 1))],
        out_specs=[pl.BlockSpec((None, None, 2, LANES, 4 * tq), lambda b, t: (b, t, 0, 0, 0)),
                   pl.BlockSpec((None, None, tq, LANES), lambda b, t: (b, t, 0, 0)),
                   pl.BlockSpec((None, 2, None, VT_ROWS, tq), lambda b, t: (b, 0, t, 0, 0))],
        out_shape=[jax.ShapeDtypeStruct((bx, nq, 2, LANES, 4 * tq), BF16),
                   jax.ShapeDtypeStruct((bx, nq, tq, LANES), BF16),
                   jax.ShapeDtypeStruct((bx, 2, nq, VT_ROWS, tq), BF16)],
        compiler_params=_cparams("parallel", "parallel"),
        name="prep_qkv",
    )(pq, cos_t, sin_t, q_g.reshape(HEAD_DIM, 1), k_g.reshape(HEAD_DIM, 1))


def _rope_tables(t):
    pos = np.arange(t)
    half = HEAD_DIM // 4
    inv = ROPE_THETA ** (-jnp.arange(half, dtype=F32) * 2.0 / (HEAD_DIM // 2))
    dd = np.arange(HEAD_DIM)
    part_pos = np.where((dd // (HEAD_DIM // 2))[:, None] == 0, (pos // GRID_W)[None, :], (pos % GRID_W)[None, :])
    ang = jnp.asarray(part_pos, F32) * inv[dd % half][:, None]
    sign = jnp.asarray(np.where((dd % (HEAD_DIM // 2)) < half, -1.0, 1.0)[:, None], F32)
    return jnp.cos(ang), jnp.sin(ang) * sign


def _attn_kernel(qt_ref, k_ref, vt_ref, o_ref, acc_ref, *, nk, tq):
    ngrp = qt_ref.shape[0]
    acc_ref[...] = jnp.zeros_like(acc_ref)
    cw = ATT_CW
    strips = [(j, n) for j in range(ngrp) for n in range(4 * tq // cw)]

    def scores(c, j, n):
        return jnp.dot(k_ref[c], qt_ref[j, :, cw * n:cw * (n + 1)], preferred_element_type=F32)

    s = [scores(0, j, n) for j, n in strips]
    m_prev = [jnp.full((1, cw), -jnp.inf, F32)] * len(strips)
    m = [jnp.max(sn, axis=0, keepdims=True) for sn in s]
    for c in range(nk):
        for i, (j, n) in enumerate(strips):
            s_next = scores(c + 1, j, n) if c + 1 < nk else None
            alpha = jnp.exp2(m_prev[i] - m[i])
            p = jnp.exp2((s[i] - m[i]).astype(BF16))
            pv = jnp.dot(vt_ref[j, c], p, preferred_element_type=F32)
            acc_ref[j, :, cw * n:cw * (n + 1)] = acc_ref[j, :, cw * n:cw * (n + 1)] * alpha + pv
            if s_next is not None:
                m_prev[i] = m[i]
                m[i] = jnp.maximum(m[i], jnp.max(s_next, axis=0, keepdims=True))
                s[i] = s_next
    for j in range(ngrp):
        o = acc_ref[j, 0:HEAD_DIM, :] * (1.0 / acc_ref[j, HEAD_DIM:HEAD_DIM + 1, :])
        for pp in range(2):
            blk = jnp.concatenate([o[:, (2 * pp) * tq:(2 * pp + 1) * tq],
                                   o[:, (2 * pp + 1) * tq:(2 * pp + 2) * tq]], axis=0)
            lane0 = LANES * (2 * j + pp)
            o_ref[:, lane0:lane0 + LANES] = blk.T.astype(BF16)


def _gqa_attention(qt, k, vt, tx):
    bx, nq = qt.shape[0], qt.shape[1]
    nk = k.shape[1]
    tq = ATT_TQ
    return pl.pallas_call(
        functools.partial(_attn_kernel, nk=nk, tq=tq),
        grid=(bx, nq),
        in_specs=[pl.BlockSpec((None, None, 2, LANES, 4 * tq), lambda b, t: (b, t, 0, 0, 0)),
                  pl.BlockSpec((None, nk, tq, LANES), lambda b, t: (b, 0, 0, 0)),
                  pl.BlockSpec((None, 2, nk, VT_ROWS, tq), lambda b, t: (b, 0, 0, 0, 0))],
        out_specs=pl.BlockSpec((tq, Q_COLS), lambda b, t: (b * nq + t, 0)),
        out_shape=jax.ShapeDtypeStruct((bx * tx, Q_COLS), BF16),
        scratch_shapes=[pltpu.VMEM((2, VT_ROWS, 4 * tq), F32)],
        compiler_params=_cparams("parallel", "parallel"),
        name="gqa_attention",
    )(qt, k, vt)


def _outproj_ln_kernel(*refs, n_lhs):
    lhs = refs[:n_lhs]
    w_ref, x_ref, gate_ref, lng_ref, lnb_ref, o_ref = refs[n_lhs:]
    y = None
    off = 0
    for r in lhs:
        kk = r.shape[1]
        t = jnp.dot(r[...], w_ref[off:off + kk, :], preferred_element_type=F32)
        off += kk
        y = t if y is None else y + t
    z = ALPHA * x_ref[...] + gate_ref[0] * y
    o_ref[...] = _layer_norm(z, lng_ref[...], lnb_ref[...])


def _outproj_ln(lhs, w, x2d, vecs, gate_k, ln_g, ln_b, rows_per_batch, tm):
    r = x2d.shape[0]
    return pl.pallas_call(
        functools.partial(_outproj_ln_kernel, n_lhs=len(lhs)),
        grid=(r // tm,),
        in_specs=[pl.BlockSpec((tm, a.shape[1]), lambda i: (i, 0)) for a in lhs]
        + [_full_spec(w.shape), pl.BlockSpec((tm, D_MODEL), lambda i: (i, 0)),
           _vec_spec(gate_k, tm, rows_per_batch), _full_spec((1, D_MODEL)), _full_spec((1, D_MODEL))],
        out_specs=pl.BlockSpec((tm, D_MODEL), lambda i: (i, 0)),
        out_shape=jax.ShapeDtypeStruct((r, D_MODEL), F32),
        compiler_params=_cparams("parallel"),
        name="outproj_ln",
    )(*lhs, w, x2d, vecs, ln_g.reshape(1, D_MODEL), ln_b.reshape(1, D_MODEL))


def _ffn_kernel(x_ref, sc_ref, sh_ref, gate_ref, wg_ref, wu_ref, wd_ref, lng_ref, lnb_ref, o_ref):
    x = x_ref[...]
    hb = (x * (1.0 + sc_ref[0]) + sh_ref[0]).astype(BF16)
    hg = jnp.dot(hb, wg_ref[...], preferred_element_type=F32)
    hu = jnp.dot(hb, wu_ref[...], preferred_element_type=F32)
    h1 = (hg * jax.nn.sigmoid(hg) * hu).astype(BF16)
    y = jnp.dot(h1, wd_ref[...], preferred_element_type=F32)
    o_ref[...] = _layer_norm(ALPHA * x + gate_ref[0] * y, lng_ref[...], lnb_ref[...])


def _ffn_ln(x2d, vecs, wg, wu, wd, ln_g, ln_b, rows_per_batch, tm):
    r = x2d.shape[0]

    def resident(shape):
        return pl.BlockSpec(shape, lambda i: (0, 0), pipeline_mode=pl.Buffered(1))

    return pl.pallas_call(
        _ffn_kernel,
        grid=(r // tm,),
        in_specs=[pl.BlockSpec((tm, D_MODEL), lambda i: (i, 0)),
                  _vec_spec(4, tm, rows_per_batch), _vec_spec(3, tm, rows_per_batch),
                  _vec_spec(5, tm, rows_per_batch),
                  resident(wg.shape), resident(wu.shape), resident(wd.shape),
                  _full_spec((1, D_MODEL)), _full_spec((1, D_MODEL))],
        out_specs=pl.BlockSpec((tm, D_MODEL), lambda i: (i, 0)),
        out_shape=jax.ShapeDtypeStruct((r, D_MODEL), F32),
        compiler_params=_cparams("parallel"),
        name="ffn_ln",
    )(x2d, vecs, vecs, vecs, wg, wu, wd, ln_g.reshape(1, D_MODEL), ln_b.reshape(1, D_MODEL))


def _inproj1_kernel(x_ref, sc_ref, sh_ref, w_ref, *o_refs, with_q):
    h = x_ref[...] * (1.0 + sc_ref[0]) + sh_ref[0]
    o = jnp.dot(h.astype(BF16), w_ref[...], preferred_element_type=F32)
    if with_q:
        qt_ref, k_ref, vt_ref = o_refs
        qt_ref[...] = (o[:, :D_MODEL] * (ATT_SCALE * LOG2E)).T.astype(BF16)
    else:
        k_ref, vt_ref = o_refs
    nk = o.shape[1] - 2 * D_MODEL
    k_ref[...] = o[:, nk:nk + D_MODEL].astype(BF16)
    vt_ref[...] = o[:, nk + D_MODEL:].T.astype(BF16)


def _inproj1(x2d, vecs, w, rows_per_batch, tm, with_q):
    r = x2d.shape[0]
    nat = pl.BlockSpec((tm, D_MODEL), lambda i: (i, 0))
    tr = pl.BlockSpec((D_MODEL, tm), lambda i: (0, i))
    nat_shape = jax.ShapeDtypeStruct((r, D_MODEL), BF16)
    tr_shape = jax.ShapeDtypeStruct((D_MODEL, r), BF16)
    return pl.pallas_call(
        functools.partial(_inproj1_kernel, with_q=with_q),
        grid=(r // tm,),
        in_specs=[pl.BlockSpec((tm, D_MODEL), lambda i: (i, 0)),
                  _vec_spec(1, tm, rows_per_batch), _vec_spec(0, tm, rows_per_batch),
                  _full_spec(w.shape)],
        out_specs=([tr] if with_q else []) + [nat, tr],
        out_shape=([tr_shape] if with_q else []) + [nat_shape, tr_shape],
        compiler_params=_cparams("parallel"),
        name="inproj1",
    )(x2d, vecs, vecs, w)


def _na_bias(rpb, n_rows, n_ctx):
    nh = rpb.shape[0]
    c = np.arange(GRID_W)
    cs = np.clip(c - WIN_C // 2, 0, GRID_W - WIN_C)
    in_c = (c[None, :] >= cs[:, None]) & (c[None, :] < cs[:, None] + WIN_C)
    dc = c[None, :] - c[:, None] + WIN_C - 1
    pick = ((dc[None] == np.arange(2 * WIN_C - 1)[:, None, None]) & in_c[None]).astype(np.float32)
    cols = jnp.einsum("hrd,dkc->hrkc", rpb, jnp.asarray(pick.transpose(0, 2, 1)), precision=lax.Precision.HIGHEST)
    cols = jnp.where(jnp.asarray(in_c.T)[None, None], cols * LOG2E, MASK_VALUE)
    n_dr = 2 * WIN_R - 1
    cols = jnp.concatenate([cols, jnp.full((nh, 1, GRID_W, GRID_W), MASK_VALUE, F32)], axis=1)
    pick_dr = np.full((3, NA_ROWS, NA_KROWS), n_dr, np.int32)
    for case, r0 in enumerate((0, NA_ROWS, n_rows - NA_ROWS)):
        start = int(np.clip(r0 - WIN_R // 2, 0, n_rows - NA_KROWS))
        for ri in range(NA_ROWS):
            r = r0 + ri
            rs = int(np.clip(r - WIN_R // 2, 0, n_rows - WIN_R))
            for ki in range(NA_KROWS):
                kr = start + ki
                if rs <= kr < rs + WIN_R:
                    pick_dr[case, ri, ki] = kr - r + WIN_R - 1
    npair = nh // 2
    nkeys = NA_KROWS * GRID_W
    out = pl.pallas_call(
        functools.partial(_na_bias_kernel, nkeys=nkeys),
        grid_spec=pltpu.PrefetchScalarGridSpec(
            num_scalar_prefetch=1,
            grid=(3, npair),
            in_specs=[pl.BlockSpec((2, n_dr + 1, GRID_W, GRID_W), lambda case, p, dr: (p, 0, 0, 0))],
            out_specs=pl.BlockSpec((None, None, nkeys + n_ctx, 2 * NA_ROWS * GRID_W),
                                   lambda case, p, dr: (case, p, 0, 0)),
        ),
        out_shape=jax.ShapeDtypeStruct((3, npair, nkeys + n_ctx, 2 * NA_ROWS * GRID_W), F32),
        compiler_params=_cparams("parallel", "parallel"),
        name="na_bias",
    )(jnp.asarray(pick_dr.reshape(-1)), cols)
    return out.reshape(3, npair // NA_PAIRS, NA_PAIRS, nkeys + n_ctx, 2 * NA_ROWS * GRID_W)


def _na_bias_kernel(dr_ref, cols_ref, o_ref, *, nkeys):
    case = pl.program_id(0)
    for ki in range(NA_KROWS):
        pieces = []
        for e in range(2):
            for ri in range(NA_ROWS):
                d = dr_ref[(case * NA_ROWS + ri) * NA_KROWS + ki]
                pieces.append(cols_ref[e, d])
        o_ref[GRID_W * ki:GRID_W * (ki + 1), :] = jnp.concatenate(pieces, axis=1)
    o_ref[nkeys:, :] = jnp.zeros((o_ref.shape[0] - nkeys, o_ref.shape[1]), F32)


def _na_kernel(qt_ref, k0_ref, k1_ref, k2_ref, kc_ref, v0_ref, v1_ref, v2_ref, vc_ref, bias_ref, o_ref):
    qt = qt_ref[...]
    kk = jnp.concatenate([k0_ref[...], k1_ref[...], k2_ref[...], kc_ref[...]], axis=0)
    vt = jnp.concatenate([v0_ref[...], v1_ref[...], v2_ref[...], vc_ref[...]], axis=1)
    nq = qt.shape[1]
    zeros = jnp.zeros((HEAD_DIM, nq), BF16)
    ones = jnp.ones((VT_ROWS - HEAD_DIM, kk.shape[0]), BF16)

    def scores(pp):
        q2 = qt[LANES * pp:LANES * (pp + 1)]
        qcat = jnp.concatenate([jnp.concatenate([q2[0:HEAD_DIM], zeros], axis=0),
                                jnp.concatenate([zeros, q2[HEAD_DIM:]], axis=0)], axis=1)
        return jnp.dot(kk[:, LANES * pp:LANES * (pp + 1)], qcat, preferred_element_type=F32) + bias_ref[pp]

    s_all = [scores(pp) for pp in range(NA_PAIRS)]
    for pp in range(NA_PAIRS):
        s = s_all[pp]
        m = jnp.max(s, axis=0, keepdims=True)
        p = jnp.exp2((s - m).astype(BF16))
        vext = jnp.concatenate([vt[LANES * pp:LANES * (pp + 1)], ones], axis=0)
        pv = jnp.dot(vext, p, preferred_element_type=F32)
        o0 = pv[0:HEAD_DIM, 0:nq] * (1.0 / pv[LANES:LANES + 1, 0:nq])
        o1 = pv[HEAD_DIM:LANES, nq:] * (1.0 / pv[LANES:LANES + 1, nq:])
        o_ref[:, LANES * pp:LANES * (pp + 1)] = jnp.concatenate([o0, o1], axis=0).T.astype(BF16)


def _neighbourhood_attention(qt, k, vt, kc, vct, bias, bsz, t):
    nq = NA_ROWS * GRID_W
    nrb = t // nq
    wl = NA_PAIRS * LANES
    ngrp = D_MODEL // wl
    nctx = kc.shape[0] // bsz
    nwin = NA_KROWS // NA_ROWS

    def first_kblock(b, rb):
        return b * nrb + jnp.clip(rb - 1, 0, nrb - nwin)

    def kspec(d):
        return pl.BlockSpec((nq, wl), lambda hp, b, rb: (first_kblock(b, rb) + d, hp))

    def vspec(d):
        return pl.BlockSpec((wl, nq), lambda hp, b, rb: (hp, first_kblock(b, rb) + d))

    def case(rb):
        return jnp.where(rb == 0, 0, jnp.where(rb == nrb - 1, 2, 1))

    return pl.pallas_call(
        _na_kernel,
        grid=(ngrp, bsz, nrb),
        in_specs=[pl.BlockSpec((wl, nq), lambda hp, b, rb: (hp, b * nrb + rb)),
                  kspec(0), kspec(1), kspec(2), pl.BlockSpec((nctx, wl), lambda hp, b, rb: (b, hp)),
                  vspec(0), vspec(1), vspec(2), pl.BlockSpec((wl, nctx), lambda hp, b, rb: (hp, b)),
                  pl.BlockSpec((None, None, NA_PAIRS, NA_KROWS * GRID_W + nctx, 2 * nq),
                               lambda hp, b, rb: (case(rb), hp, 0, 0, 0))],
        out_specs=pl.BlockSpec((nq, wl), lambda hp, b, rb: (b * nrb + rb, hp)),
        out_shape=jax.ShapeDtypeStruct((bsz * t, D_MODEL), BF16),
        compiler_params=_cparams("parallel", "parallel", "parallel"),
        name="neighbourhood_attention",
    )(qt, k, k, k, kc, vt, vt, vt, vct, bias)


def _store_row_tiles(dst_ref, val):
    rows = val.shape[0]
    for lt in range(ROW_TILES):
        dst_ref[pl.ds(lt, rows, stride=ROW_TILES), :] = val[:, lt * LANES:(lt + 1) * LANES]


def _load_row_tiles(src_ref, lt, rows):
    return src_ref[pl.ds(lt, rows, stride=ROW_TILES), :]


def _router_kernel(x_ref, sc_ref, sh_ref, wr_ref, h_ref, r_ref):
    h = x_ref[...] * (1.0 + sc_ref[0]) + sh_ref[0]
    _store_row_tiles(h_ref, h)
    hi = h.astype(BF16)
    lo = (h - hi.astype(F32)).astype(BF16)
    w = wr_ref[...]
    whi = w.astype(BF16)
    wlo = (w - whi.astype(F32)).astype(BF16)
    lg = (jnp.dot(hi, whi, preferred_element_type=F32)
          + (jnp.dot(hi, wlo, preferred_element_type=F32) + jnp.dot(lo, whi, preferred_element_type=F32)))
    lane = lax.broadcasted_iota(jnp.int32, lg.shape, 1).astype(F32)
    lg = jnp.where(lane < N_EXPERTS, lg, -jnp.inf)
    v1 = jnp.max(lg, axis=1, keepdims=True)
    i1 = jnp.min(jnp.where(lg == v1, lane, float(LANES)), axis=1, keepdims=True)
    lg2 = jnp.where(lane == i1, -jnp.inf, lg)
    v2 = jnp.max(lg2, axis=1, keepdims=True)
    i2 = jnp.min(jnp.where(lg2 == v2, lane, float(LANES)), axis=1, keepdims=True)
    e = jnp.exp(v2 - v1)
    w1 = 1.0 / (1.0 + e)
    w2 = e / (1.0 + e)
    r_ref[...] = jnp.where(lane == 0, i1, jnp.where(lane == 1, i2, jnp.where(lane == 2, w1,
                                                                              jnp.where(lane == 3, w2, 0.0))))


def _router(x2d, vecs, w_router, rows_per_batch, tm):
    r = x2d.shape[0]
    wr = jnp.pad(w_router, ((0, 0), (0, LANES - N_EXPERTS)))
    return pl.pallas_call(
        _router_kernel,
        grid=(r // tm,),
        in_specs=[pl.BlockSpec((tm, D_MODEL), lambda i: (i, 0)),
                  _vec_spec(4, tm, rows_per_batch), _vec_spec(3, tm, rows_per_batch),
                  _full_spec((D_MODEL, LANES))],
        out_specs=[pl.BlockSpec((tm * ROW_TILES, LANES), lambda i: (i, 0)),
                   pl.BlockSpec((tm, LANES), lambda i: (i, 0))],
        out_shape=[jax.ShapeDtypeStruct((r * ROW_TILES, LANES), F32), jax.ShapeDtypeStruct((r, LANES), F32)],
        compiler_params=_cparams("parallel"),
        name="router",
    )(x2d, vecs, vecs, wr)


def _row_copy(src_hbm, first_sublane, r, dst_ref, sem):
    return pltpu.make_async_copy(src_hbm.at[pl.ds(pl.multiple_of(first_sublane, ROW_TILES), ROW_TILES), :],
                                 dst_ref.at[pl.ds(pl.multiple_of(r * ROW_TILES, ROW_TILES), ROW_TILES), :], sem)


def _start_row_gather(src_hbm, idx_ref, dst_ref, sem, n, priorities):
    def issue(i, carry):
        for u in range(2):
            r = 2 * i + u
            _row_copy(src_hbm, idx_ref[0, r], r, dst_ref, sem).start(priority=priorities[u])
        return carry

    lax.fori_loop(0, n // 2, issue, 0, unroll=4)


def _wait_row_gather(src_hbm, dst_ref, sem, n):
    pltpu.make_async_copy(src_hbm.at[pl.ds(0, n * ROW_TILES), :], dst_ref, sem).wait()


def _moe_ffn_kernel(te_ref, nu_ref, nv_ref, idx0_ref, idxn_ref, h_hbm, wg_ref, wu_ref, wd_ref, o_ref,
                    xbuf_ref, acc_ref, sem):
    t = pl.program_id(0)
    j = pl.program_id(1)
    nj = pl.num_programs(1)
    used = t < nu_ref[0]
    tm = acc_ref.shape[0]
    slot = t % 2

    @pl.when(jnp.logical_and(used, j == 0))
    def _():
        @pl.when(t == 0)
        def _():
            _start_row_gather(h_hbm, idx0_ref, xbuf_ref.at[0], sem.at[0], tm, (0, 0))

        @pl.when(t + 1 < nu_ref[0])
        def _():
            _start_row_gather(h_hbm, idxn_ref, xbuf_ref.at[1 - slot], sem.at[1 - slot], tm, (0, 0))

        _wait_row_gather(h_hbm, xbuf_ref.at[slot], sem.at[slot], tm)

    @pl.when(jnp.logical_and(t == 0, j == 0))
    def _():
        acc_ref[...] = jnp.zeros_like(acc_ref)

    def compute(rows):
        x = xbuf_ref.at[slot]
        hb = jnp.concatenate([_load_row_tiles(x, lt, rows).astype(BF16) for lt in range(ROW_TILES)], axis=1)
        hg = jnp.dot(hb, wg_ref[...], preferred_element_type=F32)
        hu = jnp.dot(hb, wu_ref[...], preferred_element_type=F32)
        h1 = (hg * jax.nn.sigmoid(hg) * hu).astype(BF16)
        acc = jnp.where(j == 0, 0.0, acc_ref[0:rows, :]) + jnp.dot(h1, wd_ref[...], preferred_element_type=F32)
        acc_ref[0:rows, :] = acc
        _store_row_tiles(o_ref, acc)
        if rows < tm:
            o_ref[rows * ROW_TILES:, :] = jnp.zeros(((tm - rows) * ROW_TILES, LANES), F32)

    quarter = tm // 4
    for q in range(1, 5):
        @pl.when(jnp.logical_and(nv_ref[t] > (q - 1) * quarter, nv_ref[t] <= q * quarter))
        def _(rows=q * quarter):
            compute(rows)

    @pl.when(jnp.logical_and(jnp.logical_not(used), j == nj - 1))
    def _():
        o_ref[...] = jnp.zeros_like(o_ref)


def _moe_ffn(h, src, tile_expert, n_used, tile_rows, wg, wu, wd):
    p = src.shape[0]
    tm, tf = MOE_TM, MOE_TF
    nt = p // tm
    f = wg.shape[2]

    def jj(t, j, nu):
        return jnp.where(t < nu[0], j, 0)

    grid_spec = pltpu.PrefetchScalarGridSpec(
        num_scalar_prefetch=3,
        grid=(nt, f // tf),
        in_specs=[pl.BlockSpec((None, 1, tm), lambda t, j, te, nu, nv: (0, 0, 0), memory_space=pltpu.SMEM),
                  pl.BlockSpec((None, 1, tm), lambda t, j, te, nu, nv: (jnp.minimum(t + 1, nt - 1), 0, 0),
                               memory_space=pltpu.SMEM),
                  pl.BlockSpec(memory_space=pl.ANY),
                  pl.BlockSpec((None, D_MODEL, tf), lambda t, j, te, nu, nv: (te[t], 0, jj(t, j, nu))),
                  pl.BlockSpec((None, D_MODEL, tf), lambda t, j, te, nu, nv: (te[t], 0, jj(t, j, nu))),
                  pl.BlockSpec((None, tf, D_MODEL), lambda t, j, te, nu, nv: (te[t], jj(t, j, nu), 0))],
        out_specs=pl.BlockSpec((tm * ROW_TILES, LANES), lambda t, j, te, nu, nv: (t, 0)),
        scratch_shapes=[pltpu.VMEM((2, tm * ROW_TILES, LANES), F32), pltpu.VMEM((tm, D_MODEL), F32),
                        pltpu.SemaphoreType.DMA((2,))],
    )
    idx = src.reshape(nt, 1, tm)
    return pl.pallas_call(
        _moe_ffn_kernel,
        grid_spec=grid_spec,
        out_shape=jax.ShapeDtypeStruct((p * ROW_TILES, LANES), F32),
        compiler_params=_cparams("arbitrary", "arbitrary"),
        name="moe_ffn",
    )(tile_expert, n_used, tile_rows, idx, idx, h, wg, wu, wd)


def _combine_ln_kernel(p1a_ref, p2a_ref, p1b_ref, p2b_ref, ys_hbm, route_ref, x_ref, gate_ref, lng_ref, lnb_ref,
                       o_ref, y_ref, sem):
    i = pl.program_id(0)
    n = x_ref.shape[0]
    slot = i % 2

    def start(p1_ref, p2_ref, s):
        _start_row_gather(ys_hbm, p1_ref, y_ref.at[s, 0], sem.at[s, 0], n, (0, 1))
        _start_row_gather(ys_hbm, p2_ref, y_ref.at[s, 1], sem.at[s, 1], n, (0, 1))

    @pl.when(i == 0)
    def _():
        start(p1a_ref, p2a_ref, 0)

    @pl.when(i + 1 < pl.num_programs(0))
    def _():
        start(p1b_ref, p2b_ref, 1 - slot)

    for e in range(2):
        _wait_row_gather(ys_hbm, y_ref.at[slot, e], sem.at[slot, e], n)
    w1 = route_ref[:, 2:3]
    w2 = route_ref[:, 3:4]
    y = jnp.concatenate([w1 * _load_row_tiles(y_ref.at[slot, 0], lt, n) + w2 * _load_row_tiles(y_ref.at[slot, 1], lt, n)
                         for lt in range(ROW_TILES)], axis=1)
    z = ALPHA * x_ref[...] + gate_ref[0] * y
    o_ref[...] = _layer_norm(z, lng_ref[...], lnb_ref[...])


def _combine_ln(ys, pos1, pos2, route, x2d, vecs, ln_g, ln_b, rows_per_batch):
    r = x2d.shape[0]
    g = GATHER_ROWS
    ns = r // g
    first = pl.BlockSpec((None, 1, g), lambda i: (0, 0, 0), memory_space=pltpu.SMEM)
    ahead = pl.BlockSpec((None, 1, g), lambda i: (jnp.minimum(i + 1, ns - 1), 0, 0), memory_space=pltpu.SMEM)
    p1 = pos1.reshape(ns, 1, g)
    p2 = pos2.reshape(ns, 1, g)
    return pl.pallas_call(
        _combine_ln_kernel,
        grid=(ns,),
        in_specs=[first, first, ahead, ahead, pl.BlockSpec(memory_space=pl.ANY),
                  pl.BlockSpec((g, LANES), lambda i: (i, 0)),
                  pl.BlockSpec((g, D_MODEL), lambda i: (i, 0)),
                  _vec_spec(5, g, rows_per_batch), _full_spec((1, D_MODEL)), _full_spec((1, D_MODEL))],
        out_specs=pl.BlockSpec((g, D_MODEL), lambda i: (i, 0)),
        out_shape=jax.ShapeDtypeStruct((r, D_MODEL), F32),
        scratch_shapes=[pltpu.VMEM((2, 2, g * ROW_TILES, LANES), F32), pltpu.SemaphoreType.DMA((2, 2))],
        compiler_params=_cparams("arbitrary"),
        name="combine_ln",
    )(p1, p2, p1, p2, ys, route, x2d, vecs, ln_g.reshape(1, D_MODEL), ln_b.reshape(1, D_MODEL))


def _routing_plan(route, tm):
    n = route.shape[0]
    e = jnp.concatenate([route[:, 0], route[:, 1]]).astype(jnp.int32)
    onehot = (e[:, None] == jnp.arange(N_EXPERTS, dtype=jnp.int32)[None, :]).astype(jnp.int32)
    csum = jnp.cumsum(onehot, axis=0)
    rank = jnp.sum(csum * onehot, axis=1) - 1
    counts = csum[-1]
    padded = ((counts + tm - 1) // tm) * tm
    ends = jnp.cumsum(padded)
    starts = ends - padded
    pos = jnp.sum(starts[None, :] * onehot, axis=1) + rank
    p = 2 * n + N_EXPERTS * tm
    by_expert = jnp.argsort(e, stable=True).astype(jnp.int32)
    tok_sorted = jnp.pad(jnp.where(by_expert >= n, by_expert - n, by_expert), (N_EXPERTS * tm, p - 2 * n))
    shift = starts - (jnp.cumsum(counts) - counts)
    slot = jnp.arange(p, dtype=jnp.int32)
    src = jnp.zeros((p,), jnp.int32)
    for g in range(N_EXPERTS):
        cand = lax.dynamic_slice(tok_sorted, (N_EXPERTS * tm - shift[g],), (p,))
        src = jnp.where((slot >= starts[g]) & (slot < ends[g]), cand, src)
    tile_start = jnp.arange(p // tm, dtype=jnp.int32) * tm
    tile_expert = jnp.minimum(jnp.sum((tile_start[:, None] >= ends[None, :]).astype(jnp.int32), axis=1),
                              N_EXPERTS - 1).astype(jnp.int32)
    n_used = (ends[-1] // tm).astype(jnp.int32).reshape(1)
    group_end = jnp.sum(jnp.where(tile_expert[:, None] == jnp.arange(N_EXPERTS)[None, :], (starts + counts)[None, :], 0),
                        axis=1)
    tile_rows = jnp.clip(group_end - tile_start, 0, tm).astype(jnp.int32)
    return src * ROW_TILES, tile_expert, n_used, tile_rows, pos[:n] * ROW_TILES, pos[n:] * ROW_TILES


def kernel(x, c, ctx, c_ctx, w_mod, b_mod, ln_g, ln_b, ab_w_in, ab_conv_w, ab_conv_g, ab_conv_b, ab_q_g, ab_k_g,
           ab_w_out, ffn_w_gate, ffn_w_up, ffn_w_down, na_w_qkv, na_rpb, na_w_out, moe_w_router, moe_w_gate,
           moe_w_up, moe_w_down):
    bsz, t, d = x.shape
    n_ctx = ctx.shape[1]
    n = bsz * t
    nc = bsz * n_ctx
    x2 = x.reshape(n, d)
    c2 = ctx.reshape(nc, d)

    cc = jnp.concatenate([c, c_ctx[None, :], jnp.zeros((8 - bsz - 1, d), F32)], axis=0)
    mod = _modulation(cc, w_mod, b_mod)
    vec0 = mod[0].reshape(8 * 6, 1, d)
    vec1 = mod[1].reshape(8 * 6, 1, d)

    w_in = ab_w_in[0].astype(BF16)
    w_out = ab_w_out[0].astype(BF16)
    pa, pq = _inproj0(x2, vec0, w_in, t, 1024)
    pac, pqc = _inproj0(c2, vec0, w_in, None, 512)
    a = _conformer_conv(pa, ab_conv_w[0], ab_conv_g[0], ab_conv_b[0], t, 512)
    ac = _conformer_conv(pac, ab_conv_w[0], ab_conv_g[0], ab_conv_b[0], n_ctx, n_ctx)
    cos_t, sin_t = _rope_tables(t)
    qt, k, vt = _prep_qkv(pq, cos_t, sin_t, ab_q_g[0], ab_k_g[0], bsz, t)
    ones = jnp.ones((HEAD_DIM, n_ctx), F32)
    qtc, kc, vtc = _prep_qkv(pqc, ones, jnp.zeros_like(ones), ab_q_g[0], ab_k_g[0], bsz, n_ctx)
    o = _gqa_attention(qt, jnp.concatenate([kc, k], axis=1), jnp.concatenate([vtc, vt], axis=2), t)
    oc = _gqa_attention(qtc, kc, vtc, n_ctx)
    x2 = _outproj_ln([a, o], w_out, x2, vec0, 2, ln_g[0, 0], ln_b[0, 0], t, 1024)
    c2 = _outproj_ln([ac, oc], w_out, c2, vec0, 2, ln_g[0, 0], ln_b[0, 0], None, 512)
    wg = ffn_w_gate[0].astype(BF16)
    wu = ffn_w_up[0].astype(BF16)
    wd = ffn_w_down[0].astype(BF16)
    x2 = _ffn_ln(x2, vec0, wg, wu, wd, ln_g[0, 1], ln_b[0, 1], t, 512)
    c2 = _ffn_ln(c2, vec0, wg, wu, wd, ln_g[0, 1], ln_b[0, 1], None, 512)

    w_qkv = na_w_qkv[0].astype(BF16)
    qt1, k1, vt1 = _inproj1(x2, vec1, w_qkv, t, 1024, True)
    kc1, vct1 = _inproj1(c2, vec1, w_qkv[:, d:], None, 512, False)
    o = _neighbourhood_attention(qt1, k1, vt1, kc1, vct1, _na_bias(na_rpb[0], t // GRID_W, n_ctx), bsz, t)
    x2 = _outproj_ln([o], na_w_out[0].astype(BF16), x2, vec1, 2, ln_g[1, 0], ln_b[1, 0], t, 1024)

    h, route = _router(x2, vec1, moe_w_router[0], t, 1024)
    src, tile_expert, n_used, tile_rows, pos1, pos2 = _routing_plan(route, MOE_TM)
    ys = _moe_ffn(h, src, tile_expert, n_used, tile_rows, moe_w_gate[0].astype(BF16), moe_w_up[0].astype(BF16),
                  moe_w_down[0].astype(BF16))
    x2 = _combine_ln(ys, pos1, pos2, route, x2, vec1, ln_g[1, 1], ln_b[1, 1], t)
    return x2.reshape(bsz, t, d)
```

```python
import functools

import numpy as np
import jax
import jax.numpy as jnp
from jax import lax
from jax.experimental import pallas as pl
from jax.experimental.pallas import tpu as pltpu

F32 = jnp.float32
BF16 = jnp.bfloat16

D_MODEL = 1024
GRID_W = 64
HEAD_DIM = 64
CONV_CH = 512
CONV_WIDTH = 31
CONV_HALO = 16
Q_COLS = 512
KV_COLS = 128
A_COLS = 2 * CONV_CH
ROPE_THETA = 10000.0
WIN_R = 8
WIN_C = 16
N_EXPERTS = 8
DEPTH = 2
ALPHA = (2 * DEPTH) ** 0.25
LN_EPS = 1e-5
RMS_EPS = 1e-6
ATT_SCALE = HEAD_DIM ** -0.5
LOG2E = 1.4426950408889634
MASK_VALUE = -1e30

LANES = 128
ROW_TILES = D_MODEL // LANES
VMEM_LIMIT = 56 * 1024 * 1024

ATT_TQ = 256
ATT_CW = 256
VT_ROWS = HEAD_DIM + 16
NA_ROWS = 4
NA_KROWS = 12
NA_PAIRS = 4
MOE_TM = 512
MOE_TF = 1792
GATHER_ROWS = 512


def _cparams(*sem):
    return pltpu.CompilerParams(dimension_semantics=sem, vmem_limit_bytes=VMEM_LIMIT)


def _layer_norm(z, g, b):
    mu = jnp.mean(z, axis=-1, keepdims=True)
    zc = z - mu
    var = jnp.mean(zc * zc, axis=-1, keepdims=True)
    return zc * lax.rsqrt(var + LN_EPS) * g + b


def _vec_spec(k, tm, rows_per_batch):
    if rows_per_batch is None:
        return pl.BlockSpec((1, 1, D_MODEL), lambda i, *_: (4 * 6 + k, 0, 0))
    return pl.BlockSpec((1, 1, D_MODEL), lambda i, *_: ((i * tm // rows_per_batch) * 6 + k, 0, 0))


def _full_spec(shape):
    nd = len(shape)
    return pl.BlockSpec(shape, lambda *_: (0,) * nd, pipeline_mode=pl.Buffered(1))


def _mod_kernel(c_ref, w_ref, b_ref, o_ref):
    c = c_ref[...]
    s = c * jax.nn.sigmoid(c)
    o_ref[...] = jnp.dot(s.astype(BF16), w_ref[...].astype(BF16), preferred_element_type=F32) + b_ref[...]


def _modulation(cc, w_mod, b_mod):
    n = 6 * D_MODEL
    tn = D_MODEL
    return pl.pallas_call(
        _mod_kernel,
        grid=(DEPTH, n // tn),
        in_specs=[pl.BlockSpec((8, D_MODEL), lambda l, j: (0, 0)),
                  pl.BlockSpec((None, D_MODEL, tn), lambda l, j: (l, 0, j)),
                  pl.BlockSpec((None, 1, tn), lambda l, j: (l, 0, j))],
        out_specs=pl.BlockSpec((None, 8, tn), lambda l, j: (l, 0, j)),
        out_shape=jax.ShapeDtypeStruct((DEPTH, 8, n), F32),
        compiler_params=_cparams("parallel", "parallel"),
        name="modulation",
    )(cc, w_mod, b_mod.reshape(DEPTH, 1, n))


def _inproj0_kernel(x_ref, sc_ref, sh_ref, w_ref, oa_ref, oq_ref):
    h = x_ref[...] * (1.0 + sc_ref[0]) + sh_ref[0]
    o = jnp.dot(h.astype(BF16), w_ref[...], preferred_element_type=F32)
    oa_ref[...] = o[:, :A_COLS]
    oq_ref[...] = o[:, A_COLS:]


def _inproj0(x2d, vecs, w, rows_per_batch, tm):
    r = x2d.shape[0]
    nq = w.shape[1] - A_COLS
    return pl.pallas_call(
        _inproj0_kernel,
        grid=(r // tm,),
        in_specs=[pl.BlockSpec((tm, D_MODEL), lambda i: (i, 0)),
                  _vec_spec(1, tm, rows_per_batch), _vec_spec(0, tm, rows_per_batch),
                  _full_spec(w.shape)],
        out_specs=[pl.BlockSpec((tm, A_COLS), lambda i: (i, 0)),
                   pl.BlockSpec((tm, nq), lambda i: (i, 0))],
        out_shape=[jax.ShapeDtypeStruct((r, A_COLS), F32), jax.ShapeDtypeStruct((r, nq), F32)],
        compiler_params=_cparams("parallel"),
        name="inproj0",
    )(x2d, vecs, vecs, w)


def _conv_kernel(pm_ref, pp_ref, pn_ref, w_ref, g_ref, b_ref, o_ref, u_ref, us_ref, cv_ref, *, tt, tx):
    i = pl.program_id(0)

    def sigmoid(v):
        return 0.5 * jnp.tanh(0.5 * v) + 0.5

    def glu(p):
        return p[:, :CONV_CH] * sigmoid(p[:, CONV_CH:])

    first = (i * tt) % tx == 0
    last = ((i + 1) * tt) % tx == 0
    u_ref[0:CONV_HALO, :] = jnp.where(first, 0.0, glu(pp_ref[...]))
    u_ref[CONV_HALO:CONV_HALO + tt, :] = glu(pm_ref[...])
    u_ref[CONV_HALO + tt:2 * CONV_HALO + tt, :] = jnp.where(last, 0.0, glu(pn_ref[...]))
    u_ref[2 * CONV_HALO + tt:, :] = jnp.zeros((8, CONV_CH), F32)
    ch = 32
    nrow = tt + 2 * CONV_HALO
    base = CONV_HALO - CONV_WIDTH // 2

    def shift_body(c, carry):
        r0 = pl.multiple_of(c * ch, ch)
        w = u_ref[pl.ds(r0, ch + 8), :]
        for s in range(1, 8):
            us_ref[s - 1, pl.ds(r0, ch), :] = pltpu.roll(w, ch + 8 - s, axis=0)[0:ch]
        return carry

    lax.fori_loop(0, nrow // ch, shift_body, 0)

    def body(c, carry):
        r0 = pl.multiple_of(c * ch, ch)
        acc = jnp.zeros((ch, CONV_CH), F32)
        for k in range(CONV_WIDTH):
            a, s = divmod(k + base, 8)
            src = u_ref if s == 0 else us_ref.at[s - 1]
            acc = acc + src[pl.ds(r0 + 8 * a, ch), :] * w_ref[pl.ds(k, 1), :]
        cv_ref[pl.ds(r0, ch), :] = acc
        return carry

    lax.fori_loop(0, tt // ch, body, 0)
    y = _layer_norm(cv_ref[...], g_ref[...], b_ref[...])
    o_ref[...] = (y * sigmoid(y)).astype(BF16)


def _conformer_conv(pa, conv_w, conv_g, conv_b, tx, tt):
    r = pa.shape[0]
    hb = tt // CONV_HALO
    nhb = r // CONV_HALO
    return pl.pallas_call(
        functools.partial(_conv_kernel, tt=tt, tx=tx),
        grid=(r // tt,),
        in_specs=[pl.BlockSpec((tt, A_COLS), lambda i: (i, 0)),
                  pl.BlockSpec((CONV_HALO, A_COLS), lambda i: (jnp.maximum(i * hb - 1, 0), 0)),
                  pl.BlockSpec((CONV_HALO, A_COLS), lambda i: (jnp.minimum((i + 1) * hb, nhb - 1), 0)),
                  _full_spec((CONV_WIDTH, CONV_CH)), _full_spec((1, CONV_CH)), _full_spec((1, CONV_CH))],
        out_specs=pl.BlockSpec((tt, CONV_CH), lambda i: (i, 0)),
        out_shape=jax.ShapeDtypeStruct((r, CONV_CH), BF16),
        scratch_shapes=[pltpu.VMEM((tt + 2 * CONV_HALO + 8, CONV_CH), F32),
                        pltpu.VMEM((7, tt + 2 * CONV_HALO, CONV_CH), F32),
                        pltpu.VMEM((tt, CONV_CH), F32)],
        compiler_params=_cparams("parallel"),
        name="conformer_conv",
    )(pa, pa, pa, conv_w, conv_g.reshape(1, CONV_CH), conv_b.reshape(1, CONV_CH))


def _prep_kernel(p_ref, cos_ref, sin_ref, qg_ref, kg_ref, qt_ref, k_ref, vt_ref, *, tq):
    x = p_ref[...]
    cos = cos_ref[...]
    sin = sin_ref[...]

    def norm_rope(xh, g):
        ms = jnp.mean(xh * xh, axis=0, keepdims=True)
        y = xh * lax.rsqrt(ms + RMS_EPS) * g
        sw = jnp.concatenate([y[16:32], y[0:16], y[48:64], y[32:48]], axis=0)
        return y * cos + sw * sin

    zeros = jnp.zeros((HEAD_DIM, tq), BF16)
    for p in range(Q_COLS // LANES):
        xp = x[:, LANES * p:LANES * (p + 1)].T
        for half in range(2):
            h = 2 * p + half
            j, g = h // 4, h % 4
            r = (norm_rope(xp[HEAD_DIM * half:HEAD_DIM * (half + 1)], qg_ref[...]) * (ATT_SCALE * LOG2E)).astype(BF16)
            qt_ref[j, HEAD_DIM * j:HEAD_DIM * (j + 1), g * tq:(g + 1) * tq] = r
            qt_ref[j, HEAD_DIM * (1 - j):HEAD_DIM * (2 - j), g * tq:(g + 1) * tq] = zeros
    xk = x[:, Q_COLS:Q_COLS + KV_COLS].T
    k0 = norm_rope(xk[0:HEAD_DIM], kg_ref[...])
    k1 = norm_rope(xk[HEAD_DIM:2 * HEAD_DIM], kg_ref[...])
    k_ref[...] = jnp.concatenate([k0, k1], axis=0).T.astype(BF16)
    xv = x[:, Q_COLS + KV_COLS:].T.astype(BF16)
    ones = jnp.ones((VT_ROWS - HEAD_DIM, tq), BF16)
    for j in range(2):
        vt_ref[j, 0:HEAD_DIM, :] = xv[HEAD_DIM * j:HEAD_DIM * (j + 1)]
        vt_ref[j, HEAD_DIM:VT_ROWS, :] = ones


def _prep_qkv(pq, cos_t, sin_t, q_g, k_g, bx, tx):
    tq = ATT_TQ
    nq = tx // tq
    return pl.pallas_call(
        functools.partial(_prep_kernel, tq=tq),
        grid=(bx, nq),
        in_specs=[pl.BlockSpec((tq, Q_COLS + 2 * KV_COLS), lambda b, t: (b * nq + t, 0)),
                  pl.BlockSpec((HEAD_DIM, tq), lambda b, t: (0, t)),
                  pl.BlockSpec((HEAD_DIM, tq), lambda b, t: (0, t)),
                  _full_spec((HEAD_DIM, 1)), _full_spec((HEAD_DIM, 1))],
        out_specs=[pl.BlockSpec((None, None, 2, LANES, 4 * tq), lambda b, t: (b, t, 0, 0, 0)),
                   pl.BlockSpec((None, None, tq, LANES), lambda b, t: (b, t, 0, 0)),
                   pl.BlockSpec((None, 2, None, VT_ROWS, tq), lambda b, t: (b, 0, t, 0, 0))],
        out_shape=[jax.ShapeDtypeStruct((bx, nq, 2, LANES, 4 * tq), BF16),
                   jax.ShapeDtypeStruct((bx, nq, tq, LANES), BF16),
                   jax.ShapeDtypeStruct((bx, 2, nq, VT_ROWS, tq), BF16)],
        compiler_params=_cparams("parallel", "parallel"),
        name="prep_qkv",
    )(pq, cos_t, sin_t, q_g.reshape(HEAD_DIM, 1), k_g.reshape(HEAD_DIM, 1))


def _rope_tables(t):
    pos = np.arange(t)
    half = HEAD_DIM // 4
    inv = ROPE_THETA ** (-jnp.arange(half, dtype=F32) * 2.0 / (HEAD_DIM // 2))
    dd = np.arange(HEAD_DIM)
    part_pos = np.where((dd // (HEAD_DIM // 2))[:, None] == 0, (pos // GRID_W)[None, :], (pos % GRID_W)[None, :])
    ang = jnp.asarray(part_pos, F32) * inv[dd % half][:, None]
    sign = jnp.asarray(np.where((dd % (HEAD_DIM // 2)) < half, -1.0, 1.0)[:, None], F32)
    return jnp.cos(ang), jnp.sin(ang) * sign


def _attn_kernel(qt_ref, k_ref, vt_ref, o_ref, acc_ref, *, nk, tq):
    ngrp = qt_ref.shape[0]
    acc_ref[...] = jnp.zeros_like(acc_ref)
    cw = ATT_CW
    strips = [(j, n) for j in range(ngrp) for n in range(4 * tq // cw)]

    def scores(c, j, n):
        return jnp.dot(k_ref[c], qt_ref[j, :, cw * n:cw * (n + 1)], preferred_element_type=F32)

    s = [scores(0, j, n) for j, n in strips]
    m_prev = [jnp.full((1, cw), -jnp.inf, F32)] * len(strips)
    m = [jnp.max(sn, axis=0, keepdims=True) for sn in s]
    for c in range(nk):
        for i, (j, n) in enumerate(strips):
            s_next = scores(c + 1, j, n) if c + 1 < nk else None
            alpha = jnp.exp2(m_prev[i] - m[i])
            p = jnp.exp2((s[i] - m[i]).astype(BF16))
            pv = jnp.dot(vt_ref[j, c], p, preferred_element_type=F32)
            acc_ref[j, :, cw * n:cw * (n + 1)] = acc_ref[j, :, cw * n:cw * (n + 1)] * alpha + pv
            if s_next is not None:
                m_prev[i] = m[i]
                m[i] = jnp.maximum(m[i], jnp.max(s_next, axis=0, keepdims=True))
                s[i] = s_next
    for j in range(ngrp):
        o = acc_ref[j, 0:HEAD_DIM, :] * (1.0 / acc_ref[j, HEAD_DIM:HEAD_DIM + 1, :])
        for pp in range(2):
            blk = jnp.concatenate([o[:, (2 * pp) * tq:(2 * pp + 1) * tq],
                                   o[:, (2 * pp + 1) * tq:(2 * pp + 2) * tq]], axis=0)
            lane0 = LANES * (2 * j + pp)
            o_ref[:, lane0:lane0 + LANES] = blk.T.astype(BF16)


def _gqa_attention(qt, k, vt, tx):
    bx, nq = qt.shape[0], qt.shape[1]
    nk = k.shape[1]
    tq = ATT_TQ
    return pl.pallas_call(
        functools.partial(_attn_kernel, nk=nk, tq=tq),
        grid=(bx, nq),
        in_specs=[pl.BlockSpec((None, None, 2, LANES, 4 * tq), lambda b, t: (b, t, 0, 0, 0)),
                  pl.BlockSpec((None, nk, tq, LANES), lambda b, t: (b, 0, 0, 0)),
                  pl.BlockSpec((None, 2, nk, VT_ROWS, tq), lambda b, t: (b, 0, 0, 0, 0))],
        out_specs=pl.BlockSpec((tq, Q_COLS), lambda b, t: (b * nq + t, 0)),
        out_shape=jax.ShapeDtypeStruct((bx * tx, Q_COLS), BF16),
        scratch_shapes=[pltpu.VMEM((2, VT_ROWS, 4 * tq), F32)],
        compiler_params=_cparams("parallel", "parallel"),
        name="gqa_attention",
    )(qt, k, vt)


def _outproj_ln_kernel(*refs, n_lhs):
    lhs = refs[:n_lhs]
    w_ref, x_ref, gate_ref, lng_ref, lnb_ref, o_ref = refs[n_lhs:]
    y = None
    off = 0
    for r in lhs:
        kk = r.shape[1]
        t = jnp.dot(r[...], w_ref[off:off + kk, :], preferred_element_type=F32)
        off += kk
        y = t if y is None else y + t
    z = ALPHA * x_ref[...] + gate_ref[0] * y
    o_ref[...] = _layer_norm(z, lng_ref[...], lnb_ref[...])


def _outproj_ln(lhs, w, x2d, vecs, gate_k, ln_g, ln_b, rows_per_batch, tm):
    r = x2d.shape[0]
    return pl.pallas_call(
        functools.partial(_outproj_ln_kernel, n_lhs=len(lhs)),
        grid=(r // tm,),
        in_specs=[pl.BlockSpec((tm, a.shape[1]), lambda i: (i, 0)) for a in lhs]
        + [_full_spec(w.shape), pl.BlockSpec((tm, D_MODEL), lambda i: (i, 0)),
           _vec_spec(gate_k, tm, rows_per_batch), _full_spec((1, D_MODEL)), _full_spec((1, D_MODEL))],
        out_specs=pl.BlockSpec((tm, D_MODEL), lambda i: (i, 0)),
        out_shape=jax.ShapeDtypeStruct((r, D_MODEL), F32),
        compiler_params=_cparams("parallel"),
        name="outproj_ln",
    )(*lhs, w, x2d, vecs, ln_g.reshape(1, D_MODEL), ln_b.reshape(1, D_MODEL))


def _ffn_kernel(x_ref, sc_ref, sh_ref, gate_ref, wg_ref, wu_ref, wd_ref, lng_ref, lnb_ref, o_ref):
    x = x_ref[...]
    hb = (x * (1.0 + sc_ref[0]) + sh_ref[0]).astype(BF16)
    hg = jnp.dot(hb, wg_ref[...], preferred_element_type=F32)
    hu = jnp.dot(hb, wu_ref[...], preferred_element_type=F32)
    h1 = (hg * jax.nn.sigmoid(hg) * hu).astype(BF16)
    y = jnp.dot(h1, wd_ref[...], preferred_element_type=F32)
    o_ref[...] = _layer_norm(ALPHA * x + gate_ref[0] * y, lng_ref[...], lnb_ref[...])


def _ffn_ln(x2d, vecs, wg, wu, wd, ln_g, ln_b, rows_per_batch, tm):
    r = x2d.shape[0]

    def resident(shape):
        return pl.BlockSpec(shape, lambda i: (0, 0), pipeline_mode=pl.Buffered(1))

    return pl.pallas_call(
        _ffn_kernel,
        grid=(r // tm,),
        in_specs=[pl.BlockSpec((tm, D_MODEL), lambda i: (i, 0)),
                  _vec_spec(4, tm, rows_per_batch), _vec_spec(3, tm, rows_per_batch),
                  _vec_spec(5, tm, rows_per_batch),
                  resident(wg.shape), resident(wu.shape), resident(wd.shape),
                  _full_spec((1, D_MODEL)), _full_spec((1, D_MODEL))],
        out_specs=pl.BlockSpec((tm, D_MODEL), lambda i: (i, 0)),
        out_shape=jax.ShapeDtypeStruct((r, D_MODEL), F32),
        compiler_params=_cparams("parallel"),
        name="ffn_ln",
    )(x2d, vecs, vecs, vecs, wg, wu, wd, ln_g.reshape(1, D_MODEL), ln_b.reshape(1, D_MODEL))


def _inproj1_kernel(x_ref, sc_ref, sh_ref, w_ref, *o_refs, with_q):
    h = x_ref[...] * (1.0 + sc_ref[0]) + sh_ref[0]
    o = jnp.dot(h.astype(BF16), w_ref[...], preferred_element_type=F32)
    if with_q:
        qt_ref, k_ref, vt_ref = o_refs
        qt_ref[...] = (o[:, :D_MODEL] * (ATT_SCALE * LOG2E)).T.astype(BF16)
    else:
        k_ref, vt_ref = o_refs
    nk = o.shape[1] - 2 * D_MODEL
    k_ref[...] = o[:, nk:nk + D_MODEL].astype(BF16)
    vt_ref[...] = o[:, nk + D_MODEL:].T.astype(BF16)


def _inproj1(x2d, vecs, w, rows_per_batch, tm, with_q):
    r = x2d.shape[0]
    nat = pl.BlockSpec((tm, D_MODEL), lambda i: (i, 0))
    tr = pl.BlockSpec((D_MODEL, tm), lambda i: (0, i))
    nat_shape = jax.ShapeDtypeStruct((r, D_MODEL), BF16)
    tr_shape = jax.ShapeDtypeStruct((D_MODEL, r), BF16)
    return pl.pallas_call(
        functools.partial(_inproj1_kernel, with_q=with_q),
        grid=(r // tm,),
        in_specs=[pl.BlockSpec((tm, D_MODEL), lambda i: (i, 0)),
                  _vec_spec(1, tm, rows_per_batch), _vec_spec(0, tm, rows_per_batch),
                  _full_spec(w.shape)],
        out_specs=([tr] if with_q else []) + [nat, tr],
        out_shape=([tr_shape] if with_q else []) + [nat_shape, tr_shape],
        compiler_params=_cparams("parallel"),
        name="inproj1",
    )(x2d, vecs, vecs, w)


def _na_bias(rpb, n_rows, n_ctx):
    nh = rpb.shape[0]
    c = np.arange(GRID_W)
    cs = np.clip(c - WIN_C // 2, 0, GRID_W - WIN_C)
    in_c = (c[None, :] >= cs[:, None]) & (c[None, :] < cs[:, None] + WIN_C)
    dc = c[None, :] - c[:, None] + WIN_C - 1
    pick = ((dc[None] == np.arange(2 * WIN_C - 1)[:, None, None]) & in_c[None]).astype(np.float32)
    cols = jnp.einsum("hrd,dkc->hrkc", rpb, jnp.asarray(pick.transpose(0, 2, 1)), precision=lax.Precision.HIGHEST)
    cols = jnp.where(jnp.asarray(in_c.T)[None, None], cols * LOG2E, MASK_VALUE)
    n_dr = 2 * WIN_R - 1
    cols = jnp.concatenate([cols, jnp.full((nh, 1, GRID_W, GRID_W), MASK_VALUE, F32)], axis=1)
    pick_dr = np.full((3, NA_ROWS, NA_KROWS), n_dr, np.int32)
    for case, r0 in enumerate((0, NA_ROWS, n_rows - NA_ROWS)):
        start = int(np.clip(r0 - WIN_R // 2, 0, n_rows - NA_KROWS))
        for ri in range(NA_ROWS):
            r = r0 + ri
            rs = int(np.clip(r - WIN_R // 2, 0, n_rows - WIN_R))
            for ki in range(NA_KROWS):
                kr = start + ki
                if rs <= kr < rs + WIN_R:
                    pick_dr[case, ri, ki] = kr - r + WIN_R - 1
    npair = nh // 2
    nkeys = NA_KROWS * GRID_W
    out = pl.pallas_call(
        functools.partial(_na_bias_kernel, nkeys=nkeys),
        grid_spec=pltpu.PrefetchScalarGridSpec(
            num_scalar_prefetch=1,
            grid=(3, npair),
            in_specs=[pl.BlockSpec((2, n_dr + 1, GRID_W, GRID_W), lambda case, p, dr: (p, 0, 0, 0))],
            out_specs=pl.BlockSpec((None, None, nkeys + n_ctx, 2 * NA_ROWS * GRID_W),
                                   lambda case, p, dr: (case, p, 0, 0)),
        ),
        out_shape=jax.ShapeDtypeStruct((3, npair, nkeys + n_ctx, 2 * NA_ROWS * GRID_W), F32),
        compiler_params=_cparams("parallel", "parallel"),
        name="na_bias",
    )(jnp.asarray(pick_dr.reshape(-1)), cols)
    return out.reshape(3, npair // NA_PAIRS, NA_PAIRS, nkeys + n_ctx, 2 * NA_ROWS * GRID_W)


def _na_bias_kernel(dr_ref, cols_ref, o_ref, *, nkeys):
    case = pl.program_id(0)
    for ki in range(NA_KROWS):
        pieces = []
        for e in range(2):
            for ri in range(NA_ROWS):
                d = dr_ref[(case * NA_ROWS + ri) * NA_KROWS + ki]
                pieces.append(cols_ref[e, d])
        o_ref[GRID_W * ki:GRID_W * (ki + 1), :] = jnp.concatenate(pieces, axis=1)
    o_ref[nkeys:, :] = jnp.zeros((o_ref.shape[0] - nkeys, o_ref.shape[1]), F32)


def _na_kernel(qt_ref, k0_ref, k1_ref, k2_ref, kc_ref, v0_ref, v1_ref, v2_ref, vc_ref, bias_ref, o_ref):
    qt = qt_ref[...]
    kk = jnp.concatenate([k0_ref[...], k1_ref[...], k2_ref[...], kc_ref[...]], axis=0)
    vt = jnp.concatenate([v0_ref[...], v1_ref[...], v2_ref[...], vc_ref[...]], axis=1)
    nq = qt.shape[1]
    zeros = jnp.zeros((HEAD_DIM, nq), BF16)
    ones = jnp.ones((VT_ROWS - HEAD_DIM, kk.shape[0]), BF16)

    def scores(pp):
        q2 = qt[LANES * pp:LANES * (pp + 1)]
        qcat = jnp.concatenate([jnp.concatenate([q2[0:HEAD_DIM], zeros], axis=0),
                                jnp.concatenate([zeros, q2[HEAD_DIM:]], axis=0)], axis=1)
        return jnp.dot(kk[:, LANES * pp:LANES * (pp + 1)], qcat, preferred_element_type=F32) + bias_ref[pp]

    s_all = [scores(pp) for pp in range(NA_PAIRS)]
    for pp in range(NA_PAIRS):
        s = s_all[pp]
        m = jnp.max(s, axis=0, keepdims=True)
        p = jnp.exp2((s - m).astype(BF16))
        vext = jnp.concatenate([vt[LANES * pp:LANES * (pp + 1)], ones], axis=0)
        pv = jnp.dot(vext, p, preferred_element_type=F32)
        o0 = pv[0:HEAD_DIM, 0:nq] * (1.0 / pv[LANES:LANES + 1, 0:nq])
        o1 = pv[HEAD_DIM:LANES, nq:] * (1.0 / pv[LANES:LANES + 1, nq:])
        o_ref[:, LANES * pp:LANES * (pp + 1)] = jnp.concatenate([o0, o1], axis=0).T.astype(BF16)


def _neighbourhood_attention(qt, k, vt, kc, vct, bias, bsz, t):
    nq = NA_ROWS * GRID_W
    nrb = t // nq
    wl = NA_PAIRS * LANES
    ngrp = D_MODEL // wl
    nctx = kc.shape[0] // bsz
    nwin = NA_KROWS // NA_ROWS

    def first_kblock(b, rb):
        return b * nrb + jnp.clip(rb - 1, 0, nrb - nwin)

    def kspec(d):
        return pl.BlockSpec((nq, wl), lambda hp, b, rb: (first_kblock(b, rb) + d, hp))

    def vspec(d):
        return pl.BlockSpec((wl, nq), lambda hp, b, rb: (hp, first_kblock(b, rb) + d))

    def case(rb):
        return jnp.where(rb == 0, 0, jnp.where(rb == nrb - 1, 2, 1))

    return pl.pallas_call(
        _na_kernel,
        grid=(ngrp, bsz, nrb),
        in_specs=[pl.BlockSpec((wl, nq), lambda hp, b, rb: (hp, b * nrb + rb)),
                  kspec(0), kspec(1), kspec(2), pl.BlockSpec((nctx, wl), lambda hp, b, rb: (b, hp)),
                  vspec(0), vspec(1), vspec(2), pl.BlockSpec((wl, nctx), lambda hp, b, rb: (hp, b)),
                  pl.BlockSpec((None, None, NA_PAIRS, NA_KROWS * GRID_W + nctx, 2 * nq),
                               lambda hp, b, rb: (case(rb), hp, 0, 0, 0))],
        out_specs=pl.BlockSpec((nq, wl), lambda hp, b, rb: (b * nrb + rb, hp)),
        out_shape=jax.ShapeDtypeStruct((bsz * t, D_MODEL), BF16),
        compiler_params=_cparams("parallel", "parallel", "parallel"),
        name="neighbourhood_attention",
    )(qt, k, k, k, kc, vt, vt, vt, vct, bias)


def _store_row_tiles(dst_ref, val):
    rows = val.shape[0]
    for lt in range(ROW_TILES):
        dst_ref[pl.ds(lt, rows, stride=ROW_TILES), :] = val[:, lt * LANES:(lt + 1) * LANES]


def _load_row_tiles(src_ref, lt, rows):
    return src_ref[pl.ds(lt, rows, stride=ROW_TILES), :]


def _router_kernel(x_ref, sc_ref, sh_ref, wr_ref, h_ref, r_ref):
    h = x_ref[...] * (1.0 + sc_ref[0]) + sh_ref[0]
    _store_row_tiles(h_ref, h)
    hi = h.astype(BF16)
    lo = (h - hi.astype(F32)).astype(BF16)
    w = wr_ref[...]
    whi = w.astype(BF16)
    wlo = (w - whi.astype(F32)).astype(BF16)
    lg = (jnp.dot(hi, whi, preferred_element_type=F32)
          + (jnp.dot(hi, wlo, preferred_element_type=F32) + jnp.dot(lo, whi, preferred_element_type=F32)))
    lane = lax.broadcasted_iota(jnp.int32, lg.shape, 1).astype(F32)
    lg = jnp.where(lane < N_EXPERTS, lg, -jnp.inf)
    v1 = jnp.max(lg, axis=1, keepdims=True)
    i1 = jnp.min(jnp.where(lg == v1, lane, float(LANES)), axis=1, keepdims=True)
    lg2 = jnp.where(lane == i1, -jnp.inf, lg)
    v2 = jnp.max(lg2, axis=1, keepdims=True)
    i2 = jnp.min(jnp.where(lg2 == v2, lane, float(LANES)), axis=1, keepdims=True)
    e = jnp.exp(v2 - v1)
    w1 = 1.0 / (1.0 + e)
    w2 = e / (1.0 + e)
    r_ref[...] = jnp.where(lane == 0, i1, jnp.where(lane == 1, i2, jnp.where(lane == 2, w1,
                                                                              jnp.where(lane == 3, w2, 0.0))))


def _router(x2d, vecs, w_router, rows_per_batch, tm):
    r = x2d.shape[0]
    wr = jnp.pad(w_router, ((0, 0), (0, LANES - N_EXPERTS)))
    return pl.pallas_call(
        _router_kernel,
        grid=(r // tm,),
        in_specs=[pl.BlockSpec((tm, D_MODEL), lambda i: (i, 0)),
                  _vec_spec(4, tm, rows_per_batch), _vec_spec(3, tm, rows_per_batch),
                  _full_spec((D_MODEL, LANES))],
        out_specs=[pl.BlockSpec((tm * ROW_TILES, LANES), lambda i: (i, 0)),
                   pl.BlockSpec((tm, LANES), lambda i: (i, 0))],
        out_shape=[jax.ShapeDtypeStruct((r * ROW_TILES, LANES), F32), jax.ShapeDtypeStruct((r, LANES), F32)],
        compiler_params=_cparams("parallel"),
        name="router",
    )(x2d, vecs, vecs, wr)


def _row_copy(src_hbm, first_sublane, r, dst_ref, sem):
    return pltpu.make_async_copy(src_hbm.at[pl.ds(pl.multiple_of(first_sublane, ROW_TILES), ROW_TILES), :],
                                 dst_ref.at[pl.ds(pl.multiple_of(r * ROW_TILES, ROW_TILES), ROW_TILES), :], sem)


def _start_row_gather(src_hbm, idx_ref, dst_ref, sem, n, priorities):
    def issue(i, carry):
        for u in range(2):
            r = 2 * i + u
            _row_copy(src_hbm, idx_ref[0, r], r, dst_ref, sem).start(priority=priorities[u])
        return carry

    lax.fori_loop(0, n // 2, issue, 0, unroll=4)


def _wait_row_gather(src_hbm, dst_ref, sem, n):
    pltpu.make_async_copy(src_hbm.at[pl.ds(0, n * ROW_TILES), :], dst_ref, sem).wait()


def _moe_ffn_kernel(te_ref, nu_ref, nv_ref, idx0_ref, idxn_ref, h_hbm, wg_ref, wu_ref, wd_ref, o_ref,
                    xbuf_ref, acc_ref, sem):
    t = pl.program_id(0)
    j = pl.program_id(1)
    nj = pl.num_programs(1)
    used = t < nu_ref[0]
    tm = acc_ref.shape[0]
    slot = t % 2

    @pl.when(jnp.logical_and(used, j == 0))
    def _():
        @pl.when(t == 0)
        def _():
            _start_row_gather(h_hbm, idx0_ref, xbuf_ref.at[0], sem.at[0], tm, (0, 0))

        @pl.when(t + 1 < nu_ref[0])
        def _():
            _start_row_gather(h_hbm, idxn_ref, xbuf_ref.at[1 - slot], sem.at[1 - slot], tm, (0, 0))

        _wait_row_gather(h_hbm, xbuf_ref.at[slot], sem.at[slot], tm)

    @pl.when(jnp.logical_and(t == 0, j == 0))
    def _():
        acc_ref[...] = jnp.zeros_like(acc_ref)

    def compute(rows):
        x = xbuf_ref.at[slot]
        hb = jnp.concatenate([_load_row_tiles(x, lt, rows).astype(BF16) for lt in range(ROW_TILES)], axis=1)
        hg = jnp.dot(hb, wg_ref[...], preferred_element_type=F32)
        hu = jnp.dot(hb, wu_ref[...], preferred_element_type=F32)
        h1 = (hg * jax.nn.sigmoid(hg) * hu).astype(BF16)
        acc = jnp.where(j == 0, 0.0, acc_ref[0:rows, :]) + jnp.dot(h1, wd_ref[...], preferred_element_type=F32)
        acc_ref[0:rows, :] = acc
        _store_row_tiles(o_ref, acc)
        if rows < tm:
            o_ref[rows * ROW_TILES:, :] = jnp.zeros(((tm - rows) * ROW_TILES, LANES), F32)

    quarter = tm // 4
    for q in range(1, 5):
        @pl.when(jnp.logical_and(nv_ref[t] > (q - 1) * quarter, nv_ref[t] <= q * quarter))
        def _(rows=q * quarter):
            compute(rows)

    @pl.when(jnp.logical_and(jnp.logical_not(used), j == nj - 1))
    def _():
        o_ref[...] = jnp.zeros_like(o_ref)


def _moe_ffn(h, src, tile_expert, n_used, tile_rows, wg, wu, wd):
    p = src.shape[0]
    tm, tf = MOE_TM, MOE_TF
    nt = p // tm
    f = wg.shape[2]

    def jj(t, j, nu):
        return jnp.where(t < nu[0], j, 0)

    grid_spec = pltpu.PrefetchScalarGridSpec(
        num_scalar_prefetch=3,
        grid=(nt, f // tf),
        in_specs=[pl.BlockSpec((None, 1, tm), lambda t, j, te, nu, nv: (0, 0, 0), memory_space=pltpu.SMEM),
                  pl.BlockSpec((None, 1, tm), lambda t, j, te, nu, nv: (jnp.minimum(t + 1, nt - 1), 0, 0),
                               memory_space=pltpu.SMEM),
                  pl.BlockSpec(memory_space=pl.ANY),
                  pl.BlockSpec((None, D_MODEL, tf), lambda t, j, te, nu, nv: (te[t], 0, jj(t, j, nu))),
                  pl.BlockSpec((None, D_MODEL, tf), lambda t, j, te, nu, nv: (te[t], 0, jj(t, j, nu))),
                  pl.BlockSpec((None, tf, D_MODEL), lambda t, j, te, nu, nv: (te[t], jj(t, j, nu), 0))],
        out_specs=pl.BlockSpec((tm * ROW_TILES, LANES), lambda t, j, te, nu, nv: (t, 0)),
        scratch_shapes=[pltpu.VMEM((2, tm * ROW_TILES, LANES), F32), pltpu.VMEM((tm, D_MODEL), F32),
                        pltpu.SemaphoreType.DMA((2,))],
    )
    idx = src.reshape(nt, 1, tm)
    return pl.pallas_call(
        _moe_ffn_kernel,
        grid_spec=grid_spec,
        out_shape=jax.ShapeDtypeStruct((p * ROW_TILES, LANES), F32),
        compiler_params=_cparams("arbitrary", "arbitrary"),
        name="moe_ffn",
    )(tile_expert, n_used, tile_rows, idx, idx, h, wg, wu, wd)


def _combine_ln_kernel(p1a_ref, p2a_ref, p1b_ref, p2b_ref, ys_hbm, route_ref, x_ref, gate_ref, lng_ref, lnb_ref,
                       o_ref, y_ref, sem):
    i = pl.program_id(0)
    n = x_ref.shape[0]
    slot = i % 2

    def start(p1_ref, p2_ref, s):
        _start_row_gather(ys_hbm, p1_ref, y_ref.at[s, 0], sem.at[s, 0], n, (0, 1))
        _start_row_gather(ys_hbm, p2_ref, y_ref.at[s, 1], sem.at[s, 1], n, (0, 1))

    @pl.when(i == 0)
    def _():
        start(p1a_ref, p2a_ref, 0)

    @pl.when(i + 1 < pl.num_programs(0))
    def _():
        start(p1b_ref, p2b_ref, 1 - slot)

    for e in range(2):
        _wait_row_gather(ys_hbm, y_ref.at[slot, e], sem.at[slot, e], n)
    w1 = route_ref[:, 2:3]
    w2 = route_ref[:, 3:4]
    y = jnp.concatenate([w1 * _load_row_tiles(y_ref.at[slot, 0], lt, n) + w2 * _load_row_tiles(y_ref.at[slot, 1], lt, n)
                         for lt in range(ROW_TILES)], axis=1)
    z = ALPHA * x_ref[...] + gate_ref[0] * y
    o_ref[...] = _layer_norm(z, lng_ref[...], lnb_ref[...])


def _combine_ln(ys, pos1, pos2, route, x2d, vecs, ln_g, ln_b, rows_per_batch):
    r = x2d.shape[0]
    g = GATHER_ROWS
    ns = r // g
    first = pl.BlockSpec((None, 1, g), lambda i: (0, 0, 0), memory_space=pltpu.SMEM)
    ahead = pl.BlockSpec((None, 1, g), lambda i: (jnp.minimum(i + 1, ns - 1), 0, 0), memory_space=pltpu.SMEM)
    p1 = pos1.reshape(ns, 1, g)
    p2 = pos2.reshape(ns, 1, g)
    return pl.pallas_call(
        _combine_ln_kernel,
        grid=(ns,),
        in_specs=[first, first, ahead, ahead, pl.BlockSpec(memory_space=pl.ANY),
                  pl.BlockSpec((g, LANES), lambda i: (i, 0)),
                  pl.BlockSpec((g, D_MODEL), lambda i: (i, 0)),
                  _vec_spec(5, g, rows_per_batch), _full_spec((1, D_MODEL)), _full_spec((1, D_MODEL))],
        out_specs=pl.BlockSpec((g, D_MODEL), lambda i: (i, 0)),
        out_shape=jax.ShapeDtypeStruct((r, D_MODEL), F32),
        scratch_shapes=[pltpu.VMEM((2, 2, g * ROW_TILES, LANES), F32), pltpu.SemaphoreType.DMA((2, 2))],
        compiler_params=_cparams("arbitrary"),
        name="combine_ln",
    )(p1, p2, p1, p2, ys, route, x2d, vecs, ln_g.reshape(1, D_MODEL), ln_b.reshape(1, D_MODEL))


def _routing_plan(route, tm):
    n = route.shape[0]
    e = jnp.concatenate([route[:, 0], route[:, 1]]).astype(jnp.int32)
    onehot = (e[:, None] == jnp.arange(N_EXPERTS, dtype=jnp.int32)[None, :]).astype(jnp.int32)
    csum = jnp.cumsum(onehot, axis=0)
    rank = jnp.sum(csum * onehot, axis=1) - 1
    counts = csum[-1]
    padded = ((counts + tm - 1) // tm) * tm
    ends = jnp.cumsum(padded)
    starts = ends - padded
    pos = jnp.sum(starts[None, :] * onehot, axis=1) + rank
    p = 2 * n + N_EXPERTS * tm
    by_expert = jnp.argsort(e, stable=True).astype(jnp.int32)
    tok_sorted = jnp.pad(jnp.where(by_expert >= n, by_expert - n, by_expert), (N_EXPERTS * tm, p - 2 * n))
    shift = starts - (jnp.cumsum(counts) - counts)
    slot = jnp.arange(p, dtype=jnp.int32)
    src = jnp.zeros((p,), jnp.int32)
    for g in range(N_EXPERTS):
        cand = lax.dynamic_slice(tok_sorted, (N_EXPERTS * tm - shift[g],), (p,))
        src = jnp.where((slot >= starts[g]) & (slot < ends[g]), cand, src)
    tile_start = jnp.arange(p // tm, dtype=jnp.int32) * tm
    tile_expert = jnp.minimum(jnp.sum((tile_start[:, None] >= ends[None, :]).astype(jnp.int32), axis=1),
                              N_EXPERTS - 1).astype(jnp.int32)
    n_used = (ends[-1] // tm).astype(jnp.int32).reshape(1)
    group_end = jnp.sum(jnp.where(tile_expert[:, None] == jnp.arange(N_EXPERTS)[None, :], (starts + counts)[None, :], 0),
                        axis=1)
    tile_rows = jnp.clip(group_end - tile_start, 0, tm).astype(jnp.int32)
    return src * ROW_TILES, tile_expert, n_used, tile_rows, pos[:n] * ROW_TILES, pos[n:] * ROW_TILES


def kernel(x, c, ctx, c_ctx, w_mod, b_mod, ln_g, ln_b, ab_w_in, ab_conv_w, ab_conv_g, ab_conv_b, ab_q_g, ab_k_g,
           ab_w_out, ffn_w_gate, ffn_w_up, ffn_w_down, na_w_qkv, na_rpb, na_w_out, moe_w_router, moe_w_gate,
           moe_w_up, moe_w_down):
    bsz, t, d = x.shape
    n_ctx = ctx.shape[1]
    n = bsz * t
    nc = bsz * n_ctx
    x2 = x.reshape(n, d)
    c2 = ctx.reshape(nc, d)

    cc = jnp.concatenate([c, c_ctx[None, :], jnp.zeros((8 - bsz - 1, d), F32)], axis=0)
    mod = _modulation(cc, w_mod, b_mod)
    vec0 = mod[0].reshape(8 * 6, 1, d)
    vec1 = mod[1].reshape(8 * 6, 1, d)

    w_in = ab_w_in[0].astype(BF16)
    w_out = ab_w_out[0].astype(BF16)
    pa, pq = _inproj0(x2, vec0, w_in, t, 1024)
    pac, pqc = _inproj0(c2, vec0, w_in, None, 512)
    a = _conformer_conv(pa, ab_conv_w[0], ab_conv_g[0], ab_conv_b[0], t, 512)
    ac = _conformer_conv(pac, ab_conv_w[0], ab_conv_g[0], ab_conv_b[0], n_ctx, n_ctx)
    cos_t, sin_t = _rope_tables(t)
    qt, k, vt = _prep_qkv(pq, cos_t, sin_t, ab_q_g[0], ab_k_g[0], bsz, t)
    ones = jnp.ones((HEAD_DIM, n_ctx), F32)
    qtc, kc, vtc = _prep_qkv(pqc, ones, jnp.zeros_like(ones), ab_q_g[0], ab_k_g[0], bsz, n_ctx)
    o = _gqa_attention(qt, jnp.concatenate([kc, k], axis=1), jnp.concatenate([vtc, vt], axis=2), t)
    oc = _gqa_attention(qtc, kc, vtc, n_ctx)
    x2 = _outproj_ln([a, o], w_out, x2, vec0, 2, ln_g[0, 0], ln_b[0, 0], t, 1024)
    c2 = _outproj_ln([ac, oc], w_out, c2, vec0, 2, ln_g[0, 0], ln_b[0, 0], None, 512)
    wg = ffn_w_gate[0].astype(BF16)
    wu = ffn_w_up[0].astype(BF16)
    wd = ffn_w_down[0].astype(BF16)
    x2 = _ffn_ln(x2, vec0, wg, wu, wd, ln_g[0, 1], ln_b[0, 1], t, 512)
    c2 = _ffn_ln(c2, vec0, wg, wu, wd, ln_g[0, 1], ln_b[0, 1], None, 512)

    w_qkv = na_w_qkv[0].astype(BF16)
    qt1, k1, vt1 = _inproj1(x2, vec1, w_qkv, t, 1024, True)
    kc1, vct1 = _inproj1(c2, vec1, w_qkv[:, d:], None, 512, False)
    o = _neighbourhood_attention(qt1, k1, vt1, kc1, vct1, _na_bias(na_rpb[0], t // GRID_W, n_ctx), bsz, t)
    x2 = _outproj_ln([o], na_w_out[0].astype(BF16), x2, vec1, 2, ln_g[1, 0], ln_b[1, 0], t, 1024)

    h, route = _router(x2, vec1, moe_w_router[0], t, 1024)
    src, tile_expert, n_used, tile_rows, pos1, pos2 = _routing_plan(route, MOE_TM)
    ys = _moe_ffn(h, src, tile_expert, n_used, tile_rows, moe_w_gate[0].astype(BF16), moe_w_up[0].astype(BF16),
                  moe_w_down[0].astype(BF16))
    x2 = _combine_ln(ys, pos1, pos2, route, x2, vec1, ln_g[1, 1], ln_b[1, 1], t)
    return x2.reshape(bsz, t, d)
```
